```python
import math
import jax
import jax.numpy as jnp
from jax import lax
import numpy as np

D_MODEL = 1024
BATCH = 16
SEQ = 4096
DEPTH = 4

GRID_W = 64
CTX_LEN = 256

LRU_WIDTH = 256
LRU_BLOCKS = 4
LRU_BLOCK = LRU_WIDTH // LRU_BLOCKS
CONV_W = 4
LRU_C = 8.0
DA_HEADS = 6
DA_QK = 32
DA_V = 2 * DA_QK
DA_WIDTH = DA_HEADS * DA_V
MLA_HEADS = 6
MLA_NOPE = 64
MLA_ROPE = 32
MLA_V = 64
MLA_WIDTH = MLA_HEADS * MLA_V
Q_RANK = 256
KV_RANK = 128
MLA_SCALE = (MLA_NOPE + MLA_ROPE) ** -0.5

D_MIX = LRU_WIDTH + DA_WIDTH + MLA_WIDTH
IN_SPLITS = (LRU_WIDTH, LRU_WIDTH, DA_HEADS * 2 * DA_QK, DA_HEADS * 2 * DA_QK, DA_WIDTH, Q_RANK, KV_RANK, MLA_ROPE)
D_IN = sum(IN_SPLITS)

N_EXPERTS = 16
N_GROUPS = 4
EXPERTS_PER_GROUP = N_EXPERTS // N_GROUPS
TOP_K = 2
D_EXPERT = 256

ROPE_THETA = 10000.0
Q_BLOCK = 128
LN_EPS = 1e-5
RMS_EPS = 1e-6
DEEPNORM_ALPHA = (2 * DEPTH) ** 0.25
DEEPNORM_BETA = (8 * DEPTH) ** -0.25

kernel_name = "hybrid_lru_diffattn_mla_moe_dit"


def layer_norm(x, g, b):
    xf = x.astype(jnp.float32)
    mu = jnp.mean(xf, axis=-1, keepdims=True)
    var = jnp.mean(jnp.square(xf - mu), axis=-1, keepdims=True)
    return ((xf - mu) * lax.rsqrt(var + LN_EPS)).astype(x.dtype) * g + b


def rms_norm(x, g):
    xf = x.astype(jnp.float32)
    return (xf * lax.rsqrt(jnp.mean(xf * xf, axis=-1, keepdims=True) + RMS_EPS)).astype(x.dtype) * g


def axial_rotary(n, dim):
    rows = n // GRID_W
    row = jnp.repeat(jnp.arange(rows), GRID_W).astype(jnp.float32)
    col = jnp.tile(jnp.arange(GRID_W), rows).astype(jnp.float32)
    n_freq = dim // 4
    inv = ROPE_THETA ** (-jnp.arange(n_freq, dtype=jnp.float32) / n_freq)
    ang = jnp.concatenate([row[:, None] * inv, col[:, None] * inv], axis=-1)
    return jnp.cos(ang), jnp.sin(ang)


def apply_rotary(t, cos, sin):
    t1, t2 = jnp.split(t, 2, axis=-1)
    cos = cos.astype(t.dtype)
    sin = sin.astype(t.dtype)
    return jnp.concatenate([t1 * cos - t2 * sin, t1 * sin + t2 * cos], axis=-1)


def split_in(p):
    idx = [int(v) for v in np.cumsum(IN_SPLITS)[:-1]]
    return jnp.split(p, idx, axis=-1)


def sweep_query_blocks(fn, *qs):
    bsz, h, s, _ = qs[0].shape
    nb = s // Q_BLOCK
    blocks = tuple(q.reshape(bsz, h, nb, Q_BLOCK, q.shape[-1]).transpose(2, 0, 1, 3, 4) for q in qs)
    out = lax.map(lambda blk: fn(*blk), blocks)
    return out.transpose(1, 2, 0, 3, 4).reshape(bsz, h, s, out.shape[-1])


def short_conv(x, w, b):
    n = x.shape[1]
    left = CONV_W // 2
    xp = jnp.pad(x, ((0, 0), (left, CONV_W - 1 - left), (0, 0)))
    y = b + xp[:, 0:n] * w[0]
    for k in range(1, CONV_W):
        y = y + xp[:, k:k + n] * w[k]
    return y


def lru_coeffs(y, wa, ba, wi, bi, lam):
    bsz, n, _ = y.shape
    yb = y.reshape(bsz, n, LRU_BLOCKS, LRU_BLOCK)
    r = jax.nn.sigmoid(jnp.einsum('bnhi,hij->bnhj', yb, wa).reshape(bsz, n, LRU_WIDTH) + ba)
    i = jax.nn.sigmoid(jnp.einsum('bnhi,hij->bnhj', yb, wi).reshape(bsz, n, LRU_WIDTH) + bi)
    log_a = -LRU_C * jax.nn.softplus(-lam.astype(jnp.float32)) * r.astype(jnp.float32)
    a = jnp.exp(log_a)
    u = jnp.sqrt(-jnp.expm1(2.0 * log_a)) * (i * y).astype(jnp.float32)
    return a, u


def linear_scan(a, u, reverse):
    def combine(lhs, rhs):
        a_l, u_l = lhs
        a_r, u_r = rhs
        return a_l * a_r, a_r * u_l + u_r
    return lax.associative_scan(combine, (a, u), reverse=reverse, axis=1)


def rglru_mixer(x_l, g_l, x_c, g_c, conv_w, conv_b, wa, ba, wi, bi, lam, need_ctx):
    y_l = short_conv(x_l, conv_w, conv_b)
    y_c = short_conv(x_c, conv_w, conv_b)
    h_l, h_c = [], []
    for d, rev in enumerate((False, True)):
        a_c, u_c = lru_coeffs(y_c, wa[d], ba[d], wi[d], bi[d], lam[d])
        _, s_c = linear_scan(a_c, u_c, rev)
        h0 = s_c[:, 0] if rev else s_c[:, -1]
        a_l, u_l = lru_coeffs(y_l, wa[d], ba[d], wi[d], bi[d], lam[d])
        cum_a, s_l = linear_scan(a_l, u_l, rev)
        h_l.append(cum_a * h0[:, None, :] + s_l)
        h_c.append(s_c)
    out_l = (h_l[0] + h_l[1]).astype(x_l.dtype) * jax.nn.gelu(g_l)
    out_c = (h_c[0] + h_c[1]).astype(x_c.dtype) * jax.nn.gelu(g_c) if need_ctx else None
    return out_l, out_c


def diff_core(q1, q2, k1, k2, v, lam):
    scale = DA_QK ** -0.5
    p1 = jax.nn.softmax(jnp.einsum('bhqd,bhkd->bhqk', q1, k1).astype(jnp.float32) * scale, axis=-1)
    p2 = jax.nn.softmax(jnp.einsum('bhqd,bhkd->bhqk', q2, k2).astype(jnp.float32) * scale, axis=-1)
    return jnp.einsum('bhqk,bhkd->bhqd', (p1 - lam * p2).astype(v.dtype), v)


def diff_attention(q_l, k_l, v_l, q_c, k_c, v_c, lam_vecs, norm_g, lam_init, cos, sin, need_ctx):
    lv = lam_vecs.astype(jnp.float32)
    lam = jnp.exp(jnp.sum(lv[0] * lv[1])) - jnp.exp(jnp.sum(lv[2] * lv[3])) + lam_init

    def split_qk(t, rotate):
        bsz, n, _ = t.shape
        t = t.reshape(bsz, n, DA_HEADS, 2, DA_QK)
        t1, t2 = t[..., 0, :], t[..., 1, :]
        if rotate:
            t1 = apply_rotary(t1, cos[:, None], sin[:, None])
            t2 = apply_rotary(t2, cos[:, None], sin[:, None])
        return t1.transpose(0, 2, 1, 3), t2.transpose(0, 2, 1, 3)

    def split_v(t):
        bsz, n, _ = t.shape
        return t.reshape(bsz, n, DA_HEADS, DA_V).transpose(0, 2, 1, 3)

    def finish(o):
        bsz, _, n, _ = o.shape
        o = rms_norm(o, norm_g) * (1.0 - lam_init)
        return o.transpose(0, 2, 1, 3).reshape(bsz, n, DA_WIDTH)

    q1c, q2c = split_qk(q_c, False)
    k1c, k2c = split_qk(k_c, False)
    vc = split_v(v_c)
    q1l, q2l = split_qk(q_l, True)
    k1l, k2l = split_qk(k_l, True)
    vl = split_v(v_l)
    k1 = jnp.concatenate([k1c, k1l], axis=2)
    k2 = jnp.concatenate([k2c, k2l], axis=2)
    v = jnp.concatenate([vc, vl], axis=2)
    o_l = sweep_query_blocks(lambda a, b: diff_core(a, b, k1, k2, v, lam), q1l, q2l)
    o_c = finish(diff_core(q1c, q2c, k1c, k2c, vc, lam)) if need_ctx else None
    return finish(o_l), o_c


def mla_core(q_nope, q_rope, k_nope, k_rope, v):
    s = jnp.einsum('bhqd,bhkd->bhqk', q_nope, k_nope) + jnp.einsum('bhqr,bkr->bhqk', q_rope, k_rope)
    p = jax.nn.softmax(s.astype(jnp.float32) * MLA_SCALE, axis=-1)
    return jnp.einsum('bhqk,bhkd->bhqd', p.astype(v.dtype), v)


def mla_attention(cq_l, ckv_l, kr_l, cq_c, ckv_c, kr_c, q_norm, kv_norm, w_uq, w_ukv, cos, sin, need_ctx):
    def project(cq, ckv):
        bsz, n, _ = cq.shape
        q = (rms_norm(cq, q_norm) @ w_uq).reshape(bsz, n, MLA_HEADS, MLA_NOPE + MLA_ROPE)
        kv = (rms_norm(ckv, kv_norm) @ w_ukv).reshape(bsz, n, MLA_HEADS, MLA_NOPE + MLA_V)
        return q[..., :MLA_NOPE], q[..., MLA_NOPE:], kv[..., :MLA_NOPE], kv[..., MLA_NOPE:]

    def heads_first(t):
        return t.transpose(0, 2, 1, 3)

    bsz, n, _ = cq_l.shape
    qn_l, qr_l, kn_l, v_l = project(cq_l, ckv_l)
    qr_l = apply_rotary(qr_l, cos[:, None], sin[:, None])
    kr_l = apply_rotary(kr_l, cos, sin)
    qn_c, qr_c, kn_c, v_c = project(cq_c, ckv_c)
    kn_all = jnp.concatenate([heads_first(kn_c), heads_first(kn_l)], axis=2)
    v_all = jnp.concatenate([heads_first(v_c), heads_first(v_l)], axis=2)
    kr_all = jnp.concatenate([kr_c, kr_l], axis=1)
    o_l = sweep_query_blocks(lambda a, b: mla_core(a, b, kn_all, kr_all, v_all), heads_first(qn_l), heads_first(qr_l))
    out_l = o_l.transpose(0, 2, 1, 3).reshape(bsz, n, MLA_WIDTH)
    out_c = None
    if need_ctx:
        o_c = mla_core(heads_first(qn_c), heads_first(qr_c), heads_first(kn_c), kr_c, heads_first(v_c))
        out_c = o_c.transpose(0, 2, 1, 3).reshape(bsz, cq_c.shape[1], MLA_WIDTH)
    return out_l, out_c


def moe_ffn(t, router_w, router_b, w1, w3, w2):
    scores = jax.nn.sigmoid((t @ router_w).astype(jnp.float32))
    sel = scores + router_b.astype(jnp.float32)
    grp_score = jnp.sum(lax.top_k(sel.reshape(-1, N_GROUPS, EXPERTS_PER_GROUP), TOP_K)[0], axis=-1)
    best = jnp.argmax(grp_score, axis=-1)
    in_group = (jnp.arange(N_EXPERTS) // EXPERTS_PER_GROUP)[None, :] == best[:, None]
    _, idx = lax.top_k(jnp.where(in_group, sel, -jnp.inf), TOP_K)
    w = jnp.take_along_axis(scores, idx, axis=-1)
    w = w / jnp.sum(w, axis=-1, keepdims=True)
    gates = jnp.sum(jax.nn.one_hot(idx, N_EXPERTS, dtype=jnp.float32) * w[..., None], axis=1).astype(t.dtype)
    y = jnp.zeros_like(t)
    for e in range(N_EXPERTS):
        h = jax.nn.silu(t @ w1[e]) * (t @ w3[e])
        y = y + gates[:, e:e + 1] * (h @ w2[e])
    return y


def setup_inputs(seed: int = 0) -> dict:
    key = jax.random.key(seed)
    ks = jax.random.split(key, 32)

    def nrm(i, shape, scale):
        return jax.random.normal(ks[i], shape, jnp.float32) * scale

    def gain(i, shape):
        return 1.0 + nrm(i, shape, 0.02)

    u = jax.random.uniform(ks[14], (DEPTH, 2, LRU_WIDTH), jnp.float32, 0.9, 0.999)
    s = u ** (1.0 / LRU_C)
    lru_lambda = jnp.log(s) - jnp.log1p(-s)
    return {
        "x": nrm(0, (BATCH, SEQ, D_MODEL), 1.0),
        "c": nrm(1, (BATCH, D_MODEL), 1.0),
        "ctx": nrm(2, (BATCH, CTX_LEN, D_MODEL), 1.0),
        "c_ctx": nrm(3, (D_MODEL,), 1.0),
        "w_mod": nrm(4, (DEPTH, D_MODEL, 6 * D_MODEL), 0.5 * D_MODEL ** -0.5),
        "b_mod": nrm(5, (DEPTH, 6 * D_MODEL), 0.02),
        "w_in": nrm(6, (DEPTH, D_MODEL, D_IN), D_MODEL ** -0.5),
        "w_out": nrm(7, (DEPTH, D_MIX, D_MODEL), DEEPNORM_BETA * D_MIX ** -0.5),
        "conv_w": nrm(8, (DEPTH, CONV_W, LRU_WIDTH), CONV_W ** -0.5),
        "conv_b": nrm(9, (DEPTH, LRU_WIDTH), 0.02),
        "lru_wa": nrm(10, (DEPTH, 2, LRU_BLOCKS, LRU_BLOCK, LRU_BLOCK), LRU_BLOCK ** -0.5),
        "lru_ba": nrm(11, (DEPTH, 2, LRU_WIDTH), 0.02),
        "lru_wi": nrm(12, (DEPTH, 2, LRU_BLOCKS, LRU_BLOCK, LRU_BLOCK), LRU_BLOCK ** -0.5),
        "lru_bi": nrm(13, (DEPTH, 2, LRU_WIDTH), 0.02),
        "lru_lambda": lru_lambda,
        "diff_lambda": nrm(15, (DEPTH, 4, DA_QK), 0.1),
        "diff_norm": gain(16, (DEPTH, DA_V)),
        "mla_q_norm": gain(17, (DEPTH, Q_RANK)),
        "mla_kv_norm": gain(18, (DEPTH, KV_RANK)),
        "mla_w_uq": nrm(19, (DEPTH, Q_RANK, MLA_HEADS * (MLA_NOPE + MLA_ROPE)), Q_RANK ** -0.5),
        "mla_w_ukv": nrm(20, (DEPTH, KV_RANK, MLA_HEADS * (MLA_NOPE + MLA_V)), KV_RANK ** -0.5),
        "ln1_g": gain(21, (DEPTH, D_MODEL)),
        "ln1_b": nrm(22, (DEPTH, D_MODEL), 0.02),
        "ln2_g": gain(23, (DEPTH, D_MODEL)),
        "ln2_b": nrm(24, (DEPTH, D_MODEL), 0.02),
        "router_w": nrm(25, (D_MODEL, N_EXPERTS), D_MODEL ** -0.5),
        "router_b": nrm(26, (N_EXPERTS,), 0.01),
        "exp_w1": nrm(27, (DEPTH, N_EXPERTS, D_MODEL, D_EXPERT), D_MODEL ** -0.5),
        "exp_w3": nrm(28, (DEPTH, N_EXPERTS, D_MODEL, D_EXPERT), D_MODEL ** -0.5),
        "exp_w2": nrm(29, (DEPTH, N_EXPERTS, D_EXPERT, D_MODEL), DEEPNORM_BETA * D_EXPERT ** -0.5),
    }


def reference(x, c, ctx, c_ctx, w_mod, b_mod, w_in, w_out, conv_w, conv_b,
              lru_wa, lru_ba, lru_wi, lru_bi, lru_lambda, diff_lambda, diff_norm,
              mla_q_norm, mla_kv_norm, mla_w_uq, mla_w_ukv,
              ln1_g, ln1_b, ln2_g, ln2_b, router_w, router_b, exp_w1, exp_w3, exp_w2):
    bsz, n, d = x.shape
    ctx_len = ctx.shape[1]
    cos_da, sin_da = axial_rotary(n, DA_QK)
    cos_mla, sin_mla = axial_rotary(n, MLA_ROPE)
    s_lat = jax.nn.silu(c)
    s_ctx = jax.nn.silu(c_ctx)
    x_l, x_c = x, ctx
    for l in range(DEPTH):
        need_ctx = l < DEPTH - 1
        mod_l = jnp.split((s_lat @ w_mod[l] + b_mod[l])[:, None, :], 6, axis=-1)
        mod_c = jnp.split(s_ctx @ w_mod[l] + b_mod[l], 6, axis=-1)
        u_l = x_l * (1.0 + mod_l[1]) + mod_l[0]
        u_c = x_c * (1.0 + mod_c[1]) + mod_c[0]
        p_l = split_in(u_l @ w_in[l])
        p_c = split_in(u_c @ w_in[l])
        a_l, a_c = rglru_mixer(p_l[0], p_l[1], p_c[0], p_c[1], conv_w[l], conv_b[l],
                               lru_wa[l], lru_ba[l], lru_wi[l], lru_bi[l], lru_lambda[l], need_ctx)
        lam_init = 0.8 - 0.6 * math.exp(-0.3 * l)
        b_l, b_c = diff_attention(p_l[2], p_l[3], p_l[4], p_c[2], p_c[3], p_c[4],
                                  diff_lambda[l], diff_norm[l], lam_init, cos_da, sin_da, need_ctx)
        m_l, m_c = mla_attention(p_l[5], p_l[6], p_l[7], p_c[5], p_c[6], p_c[7],
                                 mla_q_norm[l], mla_kv_norm[l], mla_w_uq[l], mla_w_ukv[l],
                                 cos_mla, sin_mla, need_ctx)
        o_l = jnp.concatenate([a_l, b_l, m_l], axis=-1) @ w_out[l]
        x_l = layer_norm(DEEPNORM_ALPHA * x_l + mod_l[2] * o_l, ln1_g[l], ln1_b[l])
        v_l = x_l * (1.0 + mod_l[4]) + mod_l[3]
        if need_ctx:
            o_c = jnp.concatenate([a_c, b_c, m_c], axis=-1) @ w_out[l]
            x_c = layer_norm(DEEPNORM_ALPHA * x_c + mod_c[2] * o_c, ln1_g[l], ln1_b[l])
            v_c = x_c * (1.0 + mod_c[4]) + mod_c[3]
            f = moe_ffn(jnp.concatenate([v_l.reshape(-1, d), v_c.reshape(-1, d)], axis=0),
                        router_w, router_b, exp_w1[l], exp_w3[l], exp_w2[l])
            f_l = f[: bsz * n].reshape(bsz, n, d)
            f_c = f[bsz * n:].reshape(bsz, ctx_len, d)
            x_c = layer_norm(DEEPNORM_ALPHA * x_c + mod_c[5] * f_c, ln2_g[l], ln2_b[l])
        else:
            f_l = moe_ffn(v_l.reshape(-1, d), router_w, router_b, exp_w1[l], exp_w3[l], exp_w2[l]).reshape(bsz, n, d)
        x_l = layer_norm(DEEPNORM_ALPHA * x_l + mod_l[5] * f_l, ln2_g[l], ln2_b[l])
    return x_l
```

```python
import functools
import math

import jax
import jax.numpy as jnp
from jax import lax
from jax.experimental import pallas as pl
from jax.experimental.pallas import tpu as pltpu

F32 = jnp.float32
BF16 = jnp.bfloat16

GRID_W = 64
LRU_WIDTH = 256
LRU_BLOCKS = 4
CONV_W = 4
LRU_C = 8.0
DA_HEADS = 6
DA_QK = 32
DA_V = 2 * DA_QK
MLA_HEADS = 6
MLA_NOPE = 64
MLA_ROPE = 32
MLA_V = 64
Q_RANK = 256
KV_RANK = 128
MLA_SCALE = (MLA_NOPE + MLA_ROPE) ** -0.5
N_EXPERTS = 16
N_GROUPS = 4
EXPERTS_PER_GROUP = N_EXPERTS // N_GROUPS
D_EXPERT = 256
ROPE_THETA = 10000.0
LN_EPS = 1e-5
RMS_EPS = 1e-6

LANES = 128
SUBLANES = 8
TOKEN_TILE = 256
VMEM_LIMIT = 56 * 1024 * 1024

LOG2E = math.log2(math.e)
DA_QSCALE = DA_QK ** -0.5 * LOG2E
MLA_QSCALE = MLA_SCALE * LOG2E

C_LRU = 0
C_DAQ = 2 * LRU_WIDTH
C_DAKV = C_DAQ + DA_HEADS * 2 * DA_QK
C_CQ = C_DAKV + DA_HEADS * LANES
C_CKV = C_CQ + Q_RANK
C_KR = C_CKV + KV_RANK
C_END = C_KR + LANES
D_ATT = LRU_WIDTH + DA_HEADS * LANES + MLA_HEADS * LANES


def _params(sem):
    return pltpu.CompilerParams(dimension_semantics=sem, vmem_limit_bytes=VMEM_LIMIT)


def _const_spec(shape):
    nd = len(shape)
    return pl.BlockSpec(shape, lambda *_: (0,) * nd, pipeline_mode=pl.Buffered(1))


def _mod_kernel(c_ref, w_ref, b_ref, o_ref):
    c = c_ref[...]
    s = c * jax.nn.sigmoid(c)
    o_ref[0] = jnp.dot(s.astype(BF16), w_ref[0].astype(BF16), preferred_element_type=F32) + b_ref[0]


def _modulation(cc, w_mod, b_mod):
    depth, d, d6 = w_mod.shape
    r = cc.shape[0]
    tn = min(d6, 1536)
    return pl.pallas_call(
        _mod_kernel,
        grid=(depth, d6 // tn),
        in_specs=[
            pl.BlockSpec((r, d), lambda l, j: (0, 0)),
            pl.BlockSpec((1, d, tn), lambda l, j: (l, 0, j)),
            pl.BlockSpec((1, 1, tn), lambda l, j: (l, 0, j)),
        ],
        out_specs=pl.BlockSpec((1, r, tn), lambda l, j: (l, 0, j)),
        out_shape=jax.ShapeDtypeStruct((depth, r, d6), F32),
        compiler_params=_params(("parallel", "parallel")),
        name="modulation",
    )(cc, w_mod, b_mod.reshape(depth, 1, d6))


def _rotate(t, cosf, sinf, first_half):
    partner = jnp.where(first_half, pltpu.roll(t, LANES - DA_QK // 2, 1), pltpu.roll(t, DA_QK // 2, 1))
    return t * cosf + partner * sinf


def _in_kernel(x_ref, mod_ref, w1_ref, wuq_ref, wukv_ref, qn_ref, kvn_ref, cos_ref, sin_ref,
               lx_ref, lg_ref, dq_ref, dkv_ref, mq_ref, mkv_ref):
    x = x_ref[0]
    mod = mod_ref[0, 0]
    u = x * (1.0 + mod[1:2]) + mod[0:1]
    y = jnp.dot(u.astype(BF16), w1_ref[...], preferred_element_type=F32)
    lx_ref[0] = y[:, C_LRU:C_LRU + LRU_WIDTH]
    lg_ref[0] = y[:, C_LRU + LRU_WIDTH:C_DAQ]

    cosf = cos_ref[...]
    sinf = sin_ref[...]
    lane = lax.broadcasted_iota(jnp.int32, cosf.shape, 1)
    first_half = (lane & (DA_QK // 2)) == 0
    rot = functools.partial(_rotate, cosf=cosf, sinf=sinf, first_half=first_half)

    for j in range(DA_HEADS // 2):
        t = y[:, C_DAQ + LANES * j:C_DAQ + LANES * (j + 1)]
        dq_ref[0, :, LANES * j:LANES * (j + 1)] = (rot(t) * DA_QSCALE).astype(BF16)
    for h in range(DA_HEADS):
        t = y[:, C_DAKV + LANES * h:C_DAKV + LANES * (h + 1)]
        is_k = (lane < DA_V) if h % 2 == 0 else (lane >= DA_V)
        dkv_ref[0, h] = jnp.where(is_k, rot(t), t).astype(BF16)

    cq = y[:, C_CQ:C_CKV]
    ckv = y[:, C_CKV:C_KR]
    krp = y[:, C_KR:C_END]
    is_rope = (lane >= MLA_NOPE) & (lane < MLA_NOPE + MLA_ROPE)
    kr = jnp.where(is_rope, rot(krp), krp)
    qn = (cq * lax.rsqrt(jnp.mean(cq * cq, axis=-1, keepdims=True) + RMS_EPS)) * qn_ref[...]
    q = jnp.dot(qn.astype(BF16), wuq_ref[...], preferred_element_type=F32)
    kvn = (ckv * lax.rsqrt(jnp.mean(ckv * ckv, axis=-1, keepdims=True) + RMS_EPS)) * kvn_ref[...]
    kv = jnp.dot(kvn.astype(BF16), wukv_ref[...], preferred_element_type=F32)
    for h in range(MLA_HEADS):
        t = q[:, LANES * h:LANES * (h + 1)]
        mq_ref[0, h] = (jnp.where(is_rope, rot(t), t) * MLA_QSCALE).astype(BF16)
        mkv_ref[0, h, :, 0:LANES] = (kv[:, 2 * LANES * h:2 * LANES * h + LANES] + kr).astype(BF16)
        mkv_ref[0, h, :, LANES:2 * LANES] = kv[:, 2 * LANES * h + LANES:2 * LANES * (h + 1)].astype(BF16)


def _in_proj(xa, modt, w1, wuq, wukv, qnorm, kvnorm, cosf, sinf, nctx):
    b, nt, d = xa.shape
    tm = TOKEN_TILE
    nc = nctx // tm
    tok = lambda w: pl.BlockSpec((1, tm, w), lambda i, t: (i, t, 0))
    head = lambda w: pl.BlockSpec((1, DA_HEADS, tm, w), lambda i, t: (i, 0, t, 0))
    return pl.pallas_call(
        _in_kernel,
        grid=(b, nt // tm),
        in_specs=[
            tok(d),
            pl.BlockSpec((1, 1, 8, d), lambda i, t: (i, jnp.where(t >= nc, 1, 0), 0, 0)),
            _const_spec(w1.shape), _const_spec(wuq.shape), _const_spec(wukv.shape),
            _const_spec(qnorm.shape), _const_spec(kvnorm.shape),
            pl.BlockSpec((tm, LANES), lambda i, t: (t, 0)),
            pl.BlockSpec((tm, LANES), lambda i, t: (t, 0)),
        ],
        out_specs=[tok(LRU_WIDTH), tok(LRU_WIDTH), tok(DA_HEADS * 2 * DA_QK), head(LANES), head(LANES), head(2 * LANES)],
        out_shape=[
            jax.ShapeDtypeStruct((b, nt, LRU_WIDTH), F32),
            jax.ShapeDtypeStruct((b, nt, LRU_WIDTH), F32),
            jax.ShapeDtypeStruct((b, nt, DA_HEADS * 2 * DA_QK), BF16),
            jax.ShapeDtypeStruct((b, DA_HEADS, nt, LANES), BF16),
            jax.ShapeDtypeStruct((b, MLA_HEADS, nt, LANES), BF16),
            jax.ShapeDtypeStruct((b, MLA_HEADS, nt, 2 * LANES), BF16),
        ],
        compiler_params=_params(("parallel", "parallel")),
        name="in_proj",
    )(xa, modt, w1, wuq, wukv, qnorm, kvnorm, cosf, sinf)


def _gelu_tanh(x):
    return 0.5 * x * (1.0 + jnp.tanh(math.sqrt(2.0 / math.pi) * (x + 0.044715 * (x * x * x))))


def _lru_kernel(x_ref, g_ref, cw_ref, cb_ref, wa_ref, wi_ref, ba_ref, bi_ref, lam_ref, o_ref,
                y_s, a_s, s_s, h_s, *, nt, nctx, chunk):
    w = LRU_WIDTH
    tiles = chunk // SUBLANES
    n_chunks = nt // chunk
    sub = lax.broadcasted_iota(jnp.int32, (tiles, SUBLANES, w), 1)
    tile_i = lax.broadcasted_iota(jnp.int32, (tiles, SUBLANES, w), 0)

    def conv_chunk(c, carry):
        r0 = pl.multiple_of(c * chunk, chunk)
        lo = pl.multiple_of(jnp.maximum(r0 - SUBLANES, 0), SUBLANES)
        hi = pl.multiple_of(jnp.minimum(r0 + chunk, nt - SUBLANES), SUBLANES)
        x3 = jnp.concatenate([x_ref[0, pl.ds(lo, SUBLANES), :], x_ref[0, pl.ds(r0, chunk), :],
                              x_ref[0, pl.ds(hi, SUBLANES), :]], axis=0).reshape(tiles + 2, SUBLANES, w)
        sh1 = pltpu.roll(x3, 1, 1)
        sh2 = pltpu.roll(x3, 2, 1)
        sh7 = pltpu.roll(x3, SUBLANES - 1, 1)
        pos = r0 + tile_i * SUBLANES + sub
        in_ctx = pos < nctx
        seg_pos = jnp.where(in_ctx, pos, pos - nctx)
        seg_last = jnp.where(in_ctx, nctx - 1, nt - nctx - 1)
        zero = jnp.zeros((tiles, SUBLANES, w), F32)
        xm2 = jnp.where(seg_pos >= 2, jnp.where(sub >= 2, sh2[1:-1], sh2[0:-2]), zero)
        xm1 = jnp.where(seg_pos >= 1, jnp.where(sub >= 1, sh1[1:-1], sh1[0:-2]), zero)
        xp1 = jnp.where(seg_pos < seg_last, jnp.where(sub < SUBLANES - 1, sh7[1:-1], sh7[2:]), zero)
        y = cb_ref[...] + xm2 * cw_ref[0:1] + xm1 * cw_ref[1:2] + x3[1:-1] * cw_ref[2:3] + xp1 * cw_ref[3:4]
        y_s[pl.ds(r0, chunk), :] = y.reshape(chunk, w)
        return carry

    lax.fori_loop(0, n_chunks, conv_chunk, 0)

    nctx_t = nctx // SUBLANES
    nt_t = nt // SUBLANES

    for d in range(2):
        nlam = -lam_ref[d:d + 1]
        softplus = jnp.maximum(nlam, 0.0) + jnp.log1p(jnp.exp(-jnp.abs(nlam)))
        c8 = -LRU_C * softplus

        def gate_chunk(c, carry, d=d, c8=c8):
            r0 = pl.multiple_of(c * chunk, chunk)
            y = y_s[pl.ds(r0, chunk), :]
            yb = y.astype(BF16)
            r = jax.nn.sigmoid(jnp.dot(yb, wa_ref[d], preferred_element_type=F32) + ba_ref[d:d + 1])
            i = jax.nn.sigmoid(jnp.dot(yb, wi_ref[d], preferred_element_type=F32) + bi_ref[d:d + 1])
            log_a = c8 * r
            a = jnp.exp(log_a)
            th = jnp.tanh(log_a)
            u = jnp.sqrt(-2.0 * th / (1.0 - th)) * (i * y)
            a3 = a.reshape(tiles, SUBLANES, w)
            u3 = u.reshape(tiles, SUBLANES, w)
            for sft in (1, 2, 4):
                if d == 0:
                    ok = sub >= sft
                    ash = pltpu.roll(a3, sft, 1)
                    ush = pltpu.roll(u3, sft, 1)
                else:
                    ok = sub < SUBLANES - sft
                    ash = pltpu.roll(a3, SUBLANES - sft, 1)
                    ush = pltpu.roll(u3, SUBLANES - sft, 1)
                u3 = jnp.where(ok, a3 * ush + u3, u3)
                a3 = jnp.where(ok, a3 * ash, a3)
            a_s[pl.ds(r0, chunk), :] = a3.reshape(chunk, w)
            s_s[pl.ds(r0, chunk), :] = u3.reshape(chunk, w)
            return carry

        lax.fori_loop(0, n_chunks, gate_chunk, 0)

        def carry_tile(j, hprev, d=d):
            if d == 0:
                t = j
            else:
                t = jnp.where(j < nctx_t, nctx_t - 1 - j, nt_t - 1 - (j - nctx_t))
            r0 = pl.multiple_of(t * SUBLANES, SUBLANES)
            h = a_s[pl.ds(r0, SUBLANES), :] * hprev + s_s[pl.ds(r0, SUBLANES), :]
            if d == 0:
                h_s[pl.ds(r0, SUBLANES), :] = h
                return h[SUBLANES - 1:SUBLANES]
            h_s[pl.ds(r0, SUBLANES), :] = h_s[pl.ds(r0, SUBLANES), :] + h
            return h[0:1]

        lax.fori_loop(0, nt_t, carry_tile, jnp.zeros((1, w), F32), unroll=4)

    def out_chunk(c, carry):
        r0 = pl.multiple_of(c * chunk, chunk)
        o_ref[0, pl.ds(r0, chunk), :] = (h_s[pl.ds(r0, chunk), :] * _gelu_tanh(g_ref[0, pl.ds(r0, chunk), :])).astype(BF16)
        return carry

    lax.fori_loop(0, n_chunks, out_chunk, 0)


def _lru(lx, lg, conv_w, conv_b, wa, wi, ba, bi, lam, nctx):
    b, nt, w = lx.shape
    chunk = TOKEN_TILE
    seq = pl.BlockSpec((1, nt, w), lambda i: (i, 0, 0))
    return pl.pallas_call(
        functools.partial(_lru_kernel, nt=nt, nctx=nctx, chunk=chunk),
        grid=(b,),
        in_specs=[seq, seq, _const_spec(conv_w.shape), _const_spec(conv_b.shape), _const_spec(wa.shape),
                  _const_spec(wi.shape), _const_spec(ba.shape), _const_spec(bi.shape), _const_spec(lam.shape)],
        out_specs=seq,
        out_shape=jax.ShapeDtypeStruct((b, nt, w), BF16),
        scratch_shapes=[pltpu.VMEM((nt, w), F32)] * 4,
        compiler_params=_params(("parallel",)),
        name="rglru",
    )(lx, lg, conv_w, conv_b, wa, wi, ba, bi, lam)


def _softmax_pv(q, k, v):
    s = lax.dot_general(q, k, (((1,), (1,)), ((), ())), preferred_element_type=F32)
    m = jnp.max(s, axis=-1, keepdims=True)
    p = jnp.exp2(s - m)
    l = jnp.sum(p, axis=-1, keepdims=True)
    return jnp.dot(p.astype(BF16), v, preferred_element_type=F32) / l


def _da_kernel(q_ref, kv_ref, dl_ref, g_ref, li_ref, o_ref, *, nt, nctx, tq):
    h = pl.program_id(1)
    qi = pl.program_id(2)
    q = q_ref[0]
    lane = lax.broadcasted_iota(jnp.int32, q.shape, 1)
    qlo = (h % 2) * DA_V
    vlo = DA_V - qlo
    dl = dl_ref[...]
    lam_init = li_ref[...]
    lam = (jnp.exp(jnp.sum(dl[0:1] * dl[1:2], axis=-1, keepdims=True))
           - jnp.exp(jnp.sum(dl[2:3] * dl[3:4], axis=-1, keepdims=True)) + lam_init)
    zero = jnp.zeros_like(q)
    useful = (lane >= vlo) & (lane < vlo + DA_V)

    def attend(nk):
        kv = kv_ref[0, 0, 0:nk, :]
        q1 = jnp.where((lane >= qlo) & (lane < qlo + DA_QK), q, zero)
        q2 = jnp.where((lane >= qlo + DA_QK) & (lane < qlo + 2 * DA_QK), q, zero)
        o = _softmax_pv(q1, kv, kv) - lam * _softmax_pv(q2, kv, kv)
        o = jnp.where(useful, o, 0.0)
        ms = jnp.sum(o * o, axis=-1, keepdims=True) * (1.0 / DA_V)
        o = (o * lax.rsqrt(ms + RMS_EPS)) * g_ref[...] * (1.0 - lam_init)
        o_ref[0, 0] = o.astype(BF16)

    n_ctx_blocks = nctx // tq

    @pl.when(qi < n_ctx_blocks)
    def _():
        attend(nctx)

    @pl.when(qi >= n_ctx_blocks)
    def _():
        attend(nt)


def _da_attn(dq, dkv, dlam, gpair, lam_init, nctx):
    b, nt, _ = dq.shape
    tq = TOKEN_TILE
    return pl.pallas_call(
        functools.partial(_da_kernel, nt=nt, nctx=nctx, tq=tq),
        grid=(b, DA_HEADS, nt // tq),
        in_specs=[
            pl.BlockSpec((1, tq, LANES), lambda i, h, t: (i, t, h // 2)),
            pl.BlockSpec((1, 1, nt, LANES), lambda i, h, t: (i, h, 0, 0)),
            _const_spec(dlam.shape), _const_spec(gpair.shape), _const_spec(lam_init.shape),
        ],
        out_specs=pl.BlockSpec((1, 1, tq, LANES), lambda i, h, t: (i, h, t, 0)),
        out_shape=jax.ShapeDtypeStruct((b, DA_HEADS, nt, LANES), BF16),
        compiler_params=_params(("parallel", "parallel", "arbitrary")),
        name="diff_attn",
    )(dq, dkv, dlam, gpair, lam_init)


def _mla_kernel(q_ref, kv_ref, o_ref, *, nt, nctx, tq):
    qi = pl.program_id(2)
    q = q_ref[0, 0]

    def attend(nk):
        o_ref[0, 0] = _softmax_pv(q, kv_ref[0, 0, 0:nk, 0:LANES], kv_ref[0, 0, 0:nk, LANES:2 * LANES]).astype(BF16)

    n_ctx_blocks = nctx // tq

    @pl.when(qi < n_ctx_blocks)
    def _():
        attend(nctx)

    @pl.when(qi >= n_ctx_blocks)
    def _():
        attend(nt)


def _mla_attn(mq, mkv, nctx):
    b, nh, nt, _ = mq.shape
    tq = TOKEN_TILE
    return pl.pallas_call(
        functools.partial(_mla_kernel, nt=nt, nctx=nctx, tq=tq),
        grid=(b, nh, nt // tq),
        in_specs=[
            pl.BlockSpec((1, 1, tq, LANES), lambda i, h, t: (i, h, t, 0)),
            pl.BlockSpec((1, 1, nt, 2 * LANES), lambda i, h, t: (i, h, 0, 0)),
        ],
        out_specs=pl.BlockSpec((1, 1, tq, LANES), lambda i, h, t: (i, h, t, 0)),
        out_shape=jax.ShapeDtypeStruct((b, nh, nt, LANES), BF16),
        compiler_params=_params(("parallel", "parallel", "arbitrary")),
        name="mla_attn",
    )(mq, mkv)


def _layer_norm(z, g, b):
    mu = jnp.mean(z, axis=-1, keepdims=True)
    zc = z - mu
    var = jnp.mean(zc * zc, axis=-1, keepdims=True)
    return (zc * lax.rsqrt(var + LN_EPS)) * g + b


def _out_kernel(x_ref, mod_ref, lru_ref, da_ref, mla_ref, wo_ref, g_ref, b_ref, x1_ref, v_ref, *, alpha):
    a = jnp.concatenate([lru_ref[0]] + [da_ref[0, h] for h in range(DA_HEADS)]
                        + [mla_ref[0, h] for h in range(MLA_HEADS)], axis=-1)
    o = jnp.dot(a, wo_ref[...], preferred_element_type=F32)
    mod = mod_ref[0, 0]
    x1 = _layer_norm(alpha * x_ref[0] + mod[2:3] * o, g_ref[...], b_ref[...])
    x1_ref[0] = x1
    v_ref[0] = (x1 * (1.0 + mod[4:5]) + mod[3:4]).astype(BF16)


def _out_proj(xa, modt, lru_o, da_o, mla_o, wo, g, bb, nctx, alpha):
    b, nt, d = xa.shape
    tm = TOKEN_TILE
    nc = nctx // tm
    tok = lambda w: pl.BlockSpec((1, tm, w), lambda i, t: (i, t, 0))
    head = pl.BlockSpec((1, DA_HEADS, tm, LANES), lambda i, t: (i, 0, t, 0))
    return pl.pallas_call(
        functools.partial(_out_kernel, alpha=alpha),
        grid=(b, nt // tm),
        in_specs=[
            tok(d),
            pl.BlockSpec((1, 1, 8, d), lambda i, t: (i, jnp.where(t >= nc, 1, 0), 0, 0)),
            tok(LRU_WIDTH), head, head, _const_spec(wo.shape), _const_spec(g.shape), _const_spec(bb.shape),
        ],
        out_specs=[tok(d), tok(d)],
        out_shape=[jax.ShapeDtypeStruct((b, nt, d), F32), jax.ShapeDtypeStruct((b, nt, d), BF16)],
        compiler_params=_params(("parallel", "parallel")),
        name="out_proj",
    )(xa, modt, lru_o, da_o, mla_o, wo, g, bb)


def _router_gates(logits, rb):
    scores = jax.nn.sigmoid(logits)
    sel = scores + rb
    lane = lax.broadcasted_iota(jnp.int32, logits.shape, 1)
    r = lane & (EXPERTS_PER_GROUP - 1)
    grp = (lane >> 2) & (N_GROUPS - 1)

    def in_group(x, k):
        return jnp.where(r >= k, pltpu.roll(x, k, 1), pltpu.roll(x, LANES - EXPERTS_PER_GROUP + k, 1))

    others = [in_group(sel, k) for k in (1, 2, 3)]
    pair_max = sel + jnp.maximum(jnp.maximum(others[0], others[1]), others[2])
    grp_score = jnp.maximum(jnp.maximum(pair_max, in_group(pair_max, 1)),
                            jnp.maximum(in_group(pair_max, 2), in_group(pair_max, 3)))
    in_best = None
    for k in (1, 2, 3):
        other = pltpu.roll(grp_score, EXPERTS_PER_GROUP * k, 1)
        wins = (grp_score > other) | ((grp_score == other) & (grp < k))
        in_best = wins if in_best is None else (in_best & wins)
    beaten = jnp.zeros(logits.shape, F32)
    for k, o in zip((1, 2, 3), others):
        beats = (o > sel) | ((o == sel) & (r >= k))
        beaten = beaten + jnp.where(beats, 1.0, 0.0)
    chosen = in_best & (beaten < 2.0)
    sc = jnp.where(chosen, scores, 0.0)
    tot = sc + in_group(sc, 1) + in_group(sc, 2) + in_group(sc, 3)
    return jnp.where(chosen, sc / tot, 0.0)


def _moe_kernel(v_ref, x1_ref, mod_ref, rw_ref, rb_ref, w1_ref, w3_ref, w2_ref, g_ref, b_ref, o_ref, *, alpha):
    v = v_ref[0]
    gates = _router_gates(jnp.dot(v, rw_ref[...], preferred_element_type=F32), rb_ref[...])
    per = EXPERTS_PER_GROUP * D_EXPERT
    f = None
    for c in range(N_GROUPS):
        h1 = jnp.dot(v, w1_ref[:, c * per:(c + 1) * per], preferred_element_type=F32)
        h3 = jnp.dot(v, w3_ref[:, c * per:(c + 1) * per], preferred_element_type=F32)
        hh = (h1 * jax.nn.sigmoid(h1)) * h3
        parts = []
        for j in range(EXPERTS_PER_GROUP):
            e = c * EXPERTS_PER_GROUP + j
            parts.append((hh[:, j * D_EXPERT:(j + 1) * D_EXPERT] * gates[:, e:e + 1]).astype(BF16))
        y = jnp.dot(jnp.concatenate(parts, axis=-1), w2_ref[c * per:(c + 1) * per, :], preferred_element_type=F32)
        f = y if f is None else f + y
    mod = mod_ref[0, 0]
    o_ref[0] = _layer_norm(alpha * x1_ref[0] + mod[5:6] * f, g_ref[...], b_ref[...])


def _moe(v, x1, modt, rw, rb, w1c, w3c, w2c, g, bb, nctx, alpha):
    b, nt, d = x1.shape
    tm = TOKEN_TILE
    nc = nctx // tm
    tok = pl.BlockSpec((1, tm, d), lambda i, t: (i, t, 0))
    return pl.pallas_call(
        functools.partial(_moe_kernel, alpha=alpha),
        grid=(b, nt // tm),
        in_specs=[
            tok, tok,
            pl.BlockSpec((1, 1, 8, d), lambda i, t: (i, jnp.where(t >= nc, 1, 0), 0, 0)),
            _const_spec(rw.shape), _const_spec(rb.shape), _const_spec(w1c.shape), _const_spec(w3c.shape),
            _const_spec(w2c.shape), _const_spec(g.shape), _const_spec(bb.shape),
        ],
        out_specs=tok,
        out_shape=jax.ShapeDtypeStruct((b, nt, d), F32),
        compiler_params=_params(("parallel", "parallel")),
        name="moe",
    )(v, x1, modt, rw, rb, w1c, w3c, w2c, g, bb)


def _rotary_tables(n, nctx):
    rows = n // GRID_W
    row = jnp.repeat(jnp.arange(rows), GRID_W).astype(F32)
    col = jnp.tile(jnp.arange(GRID_W), rows).astype(F32)
    n_freq = DA_QK // 4
    inv = ROPE_THETA ** (-jnp.arange(n_freq, dtype=F32) / n_freq)
    ang = jnp.concatenate([row[:, None] * inv, col[:, None] * inv], axis=-1)
    ang = jnp.concatenate([jnp.zeros((nctx, DA_QK // 2), F32), ang], axis=0)
    c, s = jnp.cos(ang), jnp.sin(ang)
    reps = LANES // DA_QK
    return jnp.tile(jnp.concatenate([c, c], axis=-1), (1, reps)), jnp.tile(jnp.concatenate([-s, s], axis=-1), (1, reps))


def _pack_in_weight(w_in):
    d = w_in.shape[0]
    o = 2 * LRU_WIDTH
    wq = w_in[:, o:o + DA_HEADS * 2 * DA_QK]
    o += DA_HEADS * 2 * DA_QK
    wk = w_in[:, o:o + DA_HEADS * 2 * DA_QK].reshape(d, DA_HEADS, 2 * DA_QK)
    o += DA_HEADS * 2 * DA_QK
    wv = w_in[:, o:o + DA_HEADS * DA_V].reshape(d, DA_HEADS, DA_V)
    o += DA_HEADS * DA_V
    wc = w_in[:, o:o + Q_RANK + KV_RANK]
    o += Q_RANK + KV_RANK
    wkr = w_in[:, o:o + MLA_ROPE]
    even = (jnp.arange(DA_HEADS) % 2 == 0)[None, :, None]
    kv = jnp.where(even, jnp.concatenate([wk, wv], axis=-1), jnp.concatenate([wv, wk], axis=-1))
    krp = jnp.concatenate([jnp.zeros((d, MLA_NOPE), F32), wkr, jnp.zeros((d, LANES - MLA_NOPE - MLA_ROPE), F32)], axis=-1)
    return jnp.concatenate([w_in[:, :2 * LRU_WIDTH], wq, kv.reshape(d, DA_HEADS * LANES), wc, krp], axis=-1).astype(BF16)


def _pack_uq(w_uq):
    r = w_uq.shape[0]
    w = w_uq.reshape(r, MLA_HEADS, MLA_NOPE + MLA_ROPE)
    w = jnp.concatenate([w, jnp.zeros((r, MLA_HEADS, LANES - MLA_NOPE - MLA_ROPE), F32)], axis=-1)
    return w.reshape(r, MLA_HEADS * LANES).astype(BF16)


def _pack_ukv(w_ukv):
    r = w_ukv.shape[0]
    w = w_ukv.reshape(r, MLA_HEADS, MLA_NOPE + MLA_V)
    z = jnp.zeros((r, MLA_HEADS, LANES - MLA_NOPE), F32)
    w = jnp.concatenate([w[..., :MLA_NOPE], z, w[..., MLA_NOPE:], z], axis=-1)
    return w.reshape(r, MLA_HEADS * 2 * LANES).astype(BF16)


def _pack_out_weight(w_out):
    d = w_out.shape[1]
    w_lru = w_out[:LRU_WIDTH]
    w_da = w_out[LRU_WIDTH:LRU_WIDTH + DA_HEADS * DA_V].reshape(DA_HEADS, DA_V, d)
    w_mla = w_out[LRU_WIDTH + DA_HEADS * DA_V:].reshape(MLA_HEADS, MLA_V, d)
    z = jnp.zeros_like(w_da)
    even = (jnp.arange(DA_HEADS) % 2 == 0)[:, None, None]
    da = jnp.where(even, jnp.concatenate([z, w_da], axis=1), jnp.concatenate([w_da, z], axis=1))
    mla = jnp.concatenate([w_mla, jnp.zeros_like(w_mla)], axis=1)
    return jnp.concatenate([w_lru, da.reshape(DA_HEADS * LANES, d), mla.reshape(MLA_HEADS * LANES, d)], axis=0).astype(BF16)


def _block_diag(w):
    nd, nb, bs, _ = w.shape
    eye = jnp.eye(nb, dtype=w.dtype)
    return jnp.einsum('dhij,hg->dhigj', w, eye).reshape(nd, nb * bs, nb * bs)


def kernel(x, c, ctx, c_ctx, w_mod, b_mod, w_in, w_out, conv_w, conv_b, lru_wa, lru_ba, lru_wi, lru_bi, lru_lambda, diff_lambda, diff_norm, mla_q_norm, mla_kv_norm, mla_w_uq, mla_w_ukv, ln1_g, ln1_b, ln2_g, ln2_b, router_w, router_b, exp_w1, exp_w3, exp_w2):
    bsz, n, d = x.shape
    nctx = ctx.shape[1]
    depth = w_mod.shape[0]
    alpha = (2 * depth) ** 0.25
    assert nctx % TOKEN_TILE == 0 and n % TOKEN_TILE == 0 and n % GRID_W == 0

    rows = -(-(bsz + 1) // SUBLANES) * SUBLANES
    cc = jnp.concatenate([c, c_ctx[None, :], jnp.zeros((rows - bsz - 1, d), F32)], axis=0)
    mod = _modulation(cc, w_mod, b_mod).reshape(depth, rows, 6, d)
    mod = jnp.pad(mod, ((0, 0), (0, 0), (0, 2), (0, 0)))
    mod_ctx = jnp.broadcast_to(mod[:, bsz][:, None], (depth, bsz, 8, d))
    modt = jnp.stack([mod_ctx, mod[:, :bsz]], axis=2)

    cosf, sinf = _rotary_tables(n, nctx)
    rw = jnp.tile(router_w, (1, LANES // N_EXPERTS)).astype(BF16)
    rb = jnp.tile(router_b, LANES // N_EXPERTS)[None, :].astype(F32)
    gpair = jnp.tile(diff_norm, (1, LANES // DA_V))

    xa = jnp.concatenate([ctx, x], axis=1)
    for l in range(depth):
        lam_init = jnp.full((1, 1), 0.8 - 0.6 * math.exp(-0.3 * l), F32)
        lx, lg, dq, dkv, mq, mkv = _in_proj(
            xa, modt[l], _pack_in_weight(w_in[l]), _pack_uq(mla_w_uq[l]), _pack_ukv(mla_w_ukv[l]),
            mla_q_norm[l][None, :], mla_kv_norm[l][None, :], cosf, sinf, nctx)
        lru_o = _lru(lx, lg, conv_w[l], conv_b[l][None, :], _block_diag(lru_wa[l]).astype(BF16),
                     _block_diag(lru_wi[l]).astype(BF16), lru_ba[l], lru_bi[l], lru_lambda[l], nctx)
        da_o = _da_attn(dq, dkv, diff_lambda[l], gpair[l][None, :], lam_init, nctx)
        mla_o = _mla_attn(mq, mkv, nctx)
        x1, v = _out_proj(xa, modt[l], lru_o, da_o, mla_o, _pack_out_weight(w_out[l]),
                          ln1_g[l][None, :], ln1_b[l][None, :], nctx, alpha)
        w1c = exp_w1[l].transpose(1, 0, 2).reshape(d, N_EXPERTS * D_EXPERT).astype(BF16)
        w3c = exp_w3[l].transpose(1, 0, 2).reshape(d, N_EXPERTS * D_EXPERT).astype(BF16)
        w2c = exp_w2[l].reshape(N_EXPERTS * D_EXPERT, d).astype(BF16)
        xa = _moe(v, x1, modt[l], rw, rb, w1c, w3c, w2c, ln2_g[l][None, :], ln2_b[l][None, :], nctx, alpha)
    return xa[:, nctx:]
```

```python
import functools
import math

import jax
import jax.numpy as jnp
from jax import lax
from jax.experimental import pallas as pl
from jax.experimental.pallas import tpu as pltpu

F32 = jnp.float32
BF16 = jnp.bfloat16

GRID_W = 64
LRU_WIDTH = 256
LRU_BLOCKS = 4
CONV_W = 4
LRU_C = 8.0
DA_HEADS = 6
DA_QK = 32
DA_V = 2 * DA_QK
MLA_HEADS = 6
MLA_NOPE = 64
MLA_ROPE = 32
MLA_V = 64
Q_RANK = 256
KV_RANK = 128
MLA_SCALE = (MLA_NOPE + MLA_ROPE) ** -0.5
N_EXPERTS = 16
N_GROUPS = 4
EXPERTS_PER_GROUP = N_EXPERTS // N_GROUPS
D_EXPERT = 256
ROPE_THETA = 10000.0
LN_EPS = 1e-5
RMS_EPS = 1e-6

LANES = 128
SUBLANES = 8
TOKEN_TILE = 256
VMEM_LIMIT = 56 * 1024 * 1024

LOG2E = math.log2(math.e)
DA_QSCALE = DA_QK ** -0.5 * LOG2E
MLA_QSCALE = MLA_SCALE * LOG2E

DA_WIDTH = DA_HEADS * DA_V
C_LRU = 0
C_DAQ = 2 * LRU_WIDTH
C_DAK = C_DAQ + DA_WIDTH
C_DAV = C_DAK + DA_WIDTH
C_CQ = C_DAV + DA_WIDTH
C_CKV = C_CQ + Q_RANK
C_KR = C_CKV + KV_RANK
C_END = C_KR + LANES
PAIRS = DA_HEADS // 2
VT_ROWS = DA_V + 16


def _params(sem):
    return pltpu.CompilerParams(dimension_semantics=sem, vmem_limit_bytes=VMEM_LIMIT)


def _const_spec(shape):
    nd = len(shape)
    return pl.BlockSpec(shape, lambda *_: (0,) * nd, pipeline_mode=pl.Buffered(1))


def _mod_kernel(c_ref, w_ref, b_ref, o_ref):
    c = c_ref[...]
    s = c * jax.nn.sigmoid(c)
    o_ref[0] = jnp.dot(s.astype(BF16), w_ref[0].astype(BF16), preferred_element_type=F32) + b_ref[0]


def _modulation(cc, w_mod, b_mod):
    depth, d, d6 = w_mod.shape
    r = cc.shape[0]
    tn = min(d6, 1536)
    return pl.pallas_call(
        _mod_kernel,
        grid=(depth, d6 // tn),
        in_specs=[
            pl.BlockSpec((r, d), lambda l, j: (0, 0)),
            pl.BlockSpec((1, d, tn), lambda l, j: (l, 0, j)),
            pl.BlockSpec((1, 1, tn), lambda l, j: (l, 0, j)),
        ],
        out_specs=pl.BlockSpec((1, r, tn), lambda l, j: (l, 0, j)),
        out_shape=jax.ShapeDtypeStruct((depth, r, d6), F32),
        compiler_params=_params(("parallel", "parallel")),
        name="modulation",
    )(cc, w_mod, b_mod.reshape(depth, 1, d6))


def _rotate(t, cosf, sinf, first_half):
    partner = jnp.where(first_half, pltpu.roll(t, LANES - DA_QK // 2, 1), pltpu.roll(t, DA_QK // 2, 1))
    return t * cosf + partner * sinf


def _store_values_t(vt_ref, v):
    rows = v.shape[0]
    ones = jnp.ones((VT_ROWS - DA_V, rows), BF16)
    for j in range(PAIRS):
        t = v[:, LANES * j:LANES * (j + 1)].T.astype(BF16)
        for k in range(2):
            vt_ref[0, 2 * j + k, 0:DA_V, :] = t[DA_V * k:DA_V * (k + 1)]
            vt_ref[0, 2 * j + k, DA_V:VT_ROWS, :] = ones


def _in_kernel(x_ref, mod_ref, w1_ref, wuq_ref, wuk_ref, wuv_ref, qn_ref, kvn_ref, cos_ref, sin_ref,
               lx_ref, lg_ref, dq_ref, dk_ref, dv_ref, mq_ref, mk_ref, mv_ref):
    x = x_ref[0]
    mod = mod_ref[0, 0]
    u = x * (1.0 + mod[1:2]) + mod[0:1]
    y = jnp.dot(u.astype(BF16), w1_ref[...], preferred_element_type=F32)
    lx_ref[0] = y[:, C_LRU:C_LRU + LRU_WIDTH]
    lg_ref[0] = y[:, C_LRU + LRU_WIDTH:C_DAQ]

    cosf = cos_ref[...]
    sinf = sin_ref[...]
    lane = lax.broadcasted_iota(jnp.int32, cosf.shape, 1)
    first_half = (lane & (DA_QK // 2)) == 0
    rot = functools.partial(_rotate, cosf=cosf, sinf=sinf, first_half=first_half)

    for j in range(PAIRS):
        t = y[:, C_DAQ + LANES * j:C_DAQ + LANES * (j + 1)]
        dq_ref[0, LANES * j:LANES * (j + 1), :] = (rot(t) * DA_QSCALE).T.astype(BF16)
        t = y[:, C_DAK + LANES * j:C_DAK + LANES * (j + 1)]
        dk_ref[0, :, LANES * j:LANES * (j + 1)] = rot(t).astype(BF16)
    _store_values_t(dv_ref, y[:, C_DAV:C_CQ])

    cq = y[:, C_CQ:C_CKV]
    ckv = y[:, C_CKV:C_KR]
    krp = y[:, C_KR:C_END]
    is_rope = (lane >= MLA_NOPE) & (lane < MLA_NOPE + MLA_ROPE)
    kr = jnp.where(is_rope, rot(krp), krp)
    qn = (cq * lax.rsqrt(jnp.mean(cq * cq, axis=-1, keepdims=True) + RMS_EPS)) * qn_ref[...]
    q = jnp.dot(qn.astype(BF16), wuq_ref[...], preferred_element_type=F32)
    kvn = ((ckv * lax.rsqrt(jnp.mean(ckv * ckv, axis=-1, keepdims=True) + RMS_EPS)) * kvn_ref[...]).astype(BF16)
    kn = jnp.dot(kvn, wuk_ref[...], preferred_element_type=F32)
    for h in range(MLA_HEADS):
        t = q[:, LANES * h:LANES * (h + 1)]
        mq_ref[0, h] = (jnp.where(is_rope, rot(t), t) * MLA_QSCALE).T.astype(BF16)
        mk_ref[0, h] = (kn[:, LANES * h:LANES * (h + 1)] + kr).astype(BF16)
    _store_values_t(mv_ref, jnp.dot(kvn, wuv_ref[...], preferred_element_type=F32))


def _in_proj(xa, modt, w1, wuq, wuk, wuv, qnorm, kvnorm, cosf, sinf, nctx):
    b, nt, d = xa.shape
    tm = TOKEN_TILE
    nc = nctx // tm
    tok = lambda w: pl.BlockSpec((1, tm, w), lambda i, t: (i, t, 0))
    head_t = lambda r: pl.BlockSpec((1, DA_HEADS, r, tm), lambda i, t: (i, 0, 0, t))
    return pl.pallas_call(
        _in_kernel,
        grid=(b, nt // tm),
        in_specs=[
            tok(d),
            pl.BlockSpec((1, 1, 8, d), lambda i, t: (i, jnp.where(t >= nc, 1, 0), 0, 0)),
            _const_spec(w1.shape), _const_spec(wuq.shape), _const_spec(wuk.shape), _const_spec(wuv.shape),
            _const_spec(qnorm.shape), _const_spec(kvnorm.shape),
            pl.BlockSpec((tm, LANES), lambda i, t: (t, 0)),
            pl.BlockSpec((tm, LANES), lambda i, t: (t, 0)),
        ],
        out_specs=[
            tok(LRU_WIDTH), tok(LRU_WIDTH),
            pl.BlockSpec((1, DA_WIDTH, tm), lambda i, t: (i, 0, t)), tok(DA_WIDTH), head_t(VT_ROWS),
            head_t(LANES), pl.BlockSpec((1, MLA_HEADS, tm, LANES), lambda i, t: (i, 0, t, 0)), head_t(VT_ROWS),
        ],
        out_shape=[
            jax.ShapeDtypeStruct((b, nt, LRU_WIDTH), F32),
            jax.ShapeDtypeStruct((b, nt, LRU_WIDTH), F32),
            jax.ShapeDtypeStruct((b, DA_WIDTH, nt), BF16),
            jax.ShapeDtypeStruct((b, nt, DA_WIDTH), BF16),
            jax.ShapeDtypeStruct((b, DA_HEADS, VT_ROWS, nt), BF16),
            jax.ShapeDtypeStruct((b, MLA_HEADS, LANES, nt), BF16),
            jax.ShapeDtypeStruct((b, MLA_HEADS, nt, LANES), BF16),
            jax.ShapeDtypeStruct((b, MLA_HEADS, VT_ROWS, nt), BF16),
        ],
        compiler_params=_params(("parallel", "parallel")),
        name="in_proj",
    )(xa, modt, w1, wuq, wuk, wuv, qnorm, kvnorm, cosf, sinf)


def _gelu_tanh(x):
    return 0.5 * x * (1.0 + jnp.tanh(math.sqrt(2.0 / math.pi) * (x + 0.044715 * (x * x * x))))


def _lru_kernel(x_ref, g_ref, cw_ref, cb_ref, wa_ref, wi_ref, ba_ref, bi_ref, lam_ref, o_ref,
                y_s, a_s, s_s, h_s, *, nt, nctx, chunk):
    w = LRU_WIDTH
    tiles = chunk // SUBLANES
    n_chunks = nt // chunk
    sub = lax.broadcasted_iota(jnp.int32, (tiles, SUBLANES, w), 1)
    tile_i = lax.broadcasted_iota(jnp.int32, (tiles, SUBLANES, w), 0)

    def conv_chunk(c, carry):
        r0 = pl.multiple_of(c * chunk, chunk)
        lo = pl.multiple_of(jnp.maximum(r0 - SUBLANES, 0), SUBLANES)
        hi = pl.multiple_of(jnp.minimum(r0 + chunk, nt - SUBLANES), SUBLANES)
        x3 = jnp.concatenate([x_ref[0, pl.ds(lo, SUBLANES), :], x_ref[0, pl.ds(r0, chunk), :],
                              x_ref[0, pl.ds(hi, SUBLANES), :]], axis=0).reshape(tiles + 2, SUBLANES, w)
        sh1 = pltpu.roll(x3, 1, 1)
        sh2 = pltpu.roll(x3, 2, 1)
        sh7 = pltpu.roll(x3, SUBLANES - 1, 1)
        pos = r0 + tile_i * SUBLANES + sub
        in_ctx = pos < nctx
        seg_pos = jnp.where(in_ctx, pos, pos - nctx)
        seg_last = jnp.where(in_ctx, nctx - 1, nt - nctx - 1)
        zero = jnp.zeros((tiles, SUBLANES, w), F32)
        xm2 = jnp.where(seg_pos >= 2, jnp.where(sub >= 2, sh2[1:-1], sh2[0:-2]), zero)
        xm1 = jnp.where(seg_pos >= 1, jnp.where(sub >= 1, sh1[1:-1], sh1[0:-2]), zero)
        xp1 = jnp.where(seg_pos < seg_last, jnp.where(sub < SUBLANES - 1, sh7[1:-1], sh7[2:]), zero)
        y = cb_ref[...] + xm2 * cw_ref[0:1] + xm1 * cw_ref[1:2] + x3[1:-1] * cw_ref[2:3] + xp1 * cw_ref[3:4]
        y_s[pl.ds(r0, chunk), :] = y.reshape(chunk, w)
        return carry

    lax.fori_loop(0, n_chunks, conv_chunk, 0)

    nctx_t = nctx // SUBLANES
    nt_t = nt // SUBLANES

    for d in range(2):
        nlam = -lam_ref[d:d + 1]
        softplus = jnp.maximum(nlam, 0.0) + jnp.log1p(jnp.exp(-jnp.abs(nlam)))
        c8 = -LRU_C * softplus

        def gate_chunk(c, carry, d=d, c8=c8):
            r0 = pl.multiple_of(c * chunk, chunk)
            y = y_s[pl.ds(r0, chunk), :]
            yb = y.astype(BF16)
            r = jax.nn.sigmoid(jnp.dot(yb, wa_ref[d], preferred_element_type=F32) + ba_ref[d:d + 1])
            i = jax.nn.sigmoid(jnp.dot(yb, wi_ref[d], preferred_element_type=F32) + bi_ref[d:d + 1])
            log_a = c8 * r
            a = jnp.exp(log_a)
            th = jnp.tanh(log_a)
            u = jnp.sqrt(-2.0 * th / (1.0 - th)) * (i * y)
            a3 = a.reshape(tiles, SUBLANES, w)
            u3 = u.reshape(tiles, SUBLANES, w)
            for sft in (1, 2, 4):
                if d == 0:
                    ok = sub >= sft
                    ash = pltpu.roll(a3, sft, 1)
                    ush = pltpu.roll(u3, sft, 1)
                else:
                    ok = sub < SUBLANES - sft
                    ash = pltpu.roll(a3, SUBLANES - sft, 1)
                    ush = pltpu.roll(u3, SUBLANES - sft, 1)
                u3 = jnp.where(ok, a3 * ush + u3, u3)
                a3 = jnp.where(ok, a3 * ash, a3)
            a_s[pl.ds(r0, chunk), :] = a3.reshape(chunk, w)
            s_s[pl.ds(r0, chunk), :] = u3.reshape(chunk, w)
            return carry

        lax.fori_loop(0, n_chunks, gate_chunk, 0)

        def carry_tile(j, hprev, d=d):
            if d == 0:
                t = j
            else:
                t = jnp.where(j < nctx_t, nctx_t - 1 - j, nt_t - 1 - (j - nctx_t))
            r0 = pl.multiple_of(t * SUBLANES, SUBLANES)
            h = a_s[pl.ds(r0, SUBLANES), :] * hprev + s_s[pl.ds(r0, SUBLANES), :]
            if d == 0:
                h_s[pl.ds(r0, SUBLANES), :] = h
                return h[SUBLANES - 1:SUBLANES]
            h_s[pl.ds(r0, SUBLANES), :] = h_s[pl.ds(r0, SUBLANES), :] + h
            return h[0:1]

        lax.fori_loop(0, nt_t, carry_tile, jnp.zeros((1, w), F32), unroll=4)

    def out_chunk(c, carry):
        r0 = pl.multiple_of(c * chunk, chunk)
        o_ref[0, pl.ds(r0, chunk), :] = (h_s[pl.ds(r0, chunk), :] * _gelu_tanh(g_ref[0, pl.ds(r0, chunk), :])).astype(BF16)
        return carry

    lax.fori_loop(0, n_chunks, out_chunk, 0)


def _lru(lx, lg, conv_w, conv_b, wa, wi, ba, bi, lam, nctx):
    b, nt, w = lx.shape
    chunk = TOKEN_TILE
    seq = pl.BlockSpec((1, nt, w), lambda i: (i, 0, 0))
    return pl.pallas_call(
        functools.partial(_lru_kernel, nt=nt, nctx=nctx, chunk=chunk),
        grid=(b,),
        in_specs=[seq, seq, _const_spec(conv_w.shape), _const_spec(conv_b.shape), _const_spec(wa.shape),
                  _const_spec(wi.shape), _const_spec(ba.shape), _const_spec(bi.shape), _const_spec(lam.shape)],
        out_specs=seq,
        out_shape=jax.ShapeDtypeStruct((b, nt, w), BF16),
        scratch_shapes=[pltpu.VMEM((nt, w), F32)] * 4,
        compiler_params=_params(("parallel",)),
        name="rglru",
    )(lx, lg, conv_w, conv_b, wa, wi, ba, bi, lam)


KEY_CHUNK = 512


def _key_chunks(nk, nctx):
    chunks = [(0, nctx)]
    chunks += [(s, min(KEY_CHUNK, nk - s)) for s in range(nctx, nk, KEY_CHUNK)]
    return chunks


def _attend_t(chains, chunks):
    def scores(n, ci):
        q_t, key, _ = chains[n]
        return jnp.dot(key(*chunks[ci]), q_t, preferred_element_type=F32)

    s = [scores(n, 0) for n in range(len(chains))]
    state = [None] * len(chains)
    for ci in range(len(chunks)):
        for n, (_, _, value_t) in enumerate(chains):
            cm = jnp.max(s[n], axis=0, keepdims=True)
            if ci == 0:
                m_new = cm
            else:
                m_old, acc = state[n]
                m_new = jnp.maximum(m_old, cm)
            p = jnp.exp2(s[n] - m_new).astype(BF16)
            if ci + 1 < len(chunks):
                s[n] = scores(n, ci + 1)
            pv = jnp.dot(value_t(*chunks[ci]), p, preferred_element_type=F32)
            if ci > 0:
                pv = acc * jnp.exp2(m_old - m_new) + pv
            state[n] = (m_new, pv)
    return [acc[0:DA_V] / acc[DA_V:DA_V + 1] for _, acc in state]


def _da_kernel(q_ref, k_ref, vt_ref, dl_ref, g_ref, li_ref, o_ref, *, nt, nctx, tq):
    qi = pl.program_id(2)
    q_t = q_ref[0]
    row = lax.broadcasted_iota(jnp.int32, q_t.shape, 0)
    dl = dl_ref[...]
    lam_init = li_ref[...]
    lam = (jnp.exp(jnp.sum(dl[0:1] * dl[1:2], axis=-1, keepdims=True))
           - jnp.exp(jnp.sum(dl[2:3] * dl[3:4], axis=-1, keepdims=True)) + lam_init)
    zero = jnp.zeros_like(q_t)

    def attend(nk):
        key = lambda start, size: k_ref[0, start:start + size, :]
        chains = []
        for hh in range(2):
            value_t = lambda start, size, hh=hh: vt_ref[0, hh, :, start:start + size]
            for mi in range(2):
                lo = (2 * hh + mi) * DA_QK
                chains.append((jnp.where((row >= lo) & (row < lo + DA_QK), q_t, zero), key, value_t))
        o = _attend_t(chains, _key_chunks(nk, nctx))
        halves = []
        for hh in range(2):
            d = o[2 * hh] - lam * o[2 * hh + 1]
            ms = jnp.mean(d * d, axis=0, keepdims=True)
            halves.append(d * lax.rsqrt(ms + RMS_EPS))
        out = jnp.concatenate(halves, axis=0).T
        o_ref[0] = (out * g_ref[...] * (1.0 - lam_init)).astype(BF16)

    n_ctx_blocks = nctx // tq

    @pl.when(qi < n_ctx_blocks)
    def _():
        attend(nctx)

    @pl.when(qi >= n_ctx_blocks)
    def _():
        attend(nt)


def _da_attn(dq_t, dk, dv_t, dlam, gpair, lam_init, nctx):
    b, nt, _ = dk.shape
    tq = TOKEN_TILE
    return pl.pallas_call(
        functools.partial(_da_kernel, nt=nt, nctx=nctx, tq=tq),
        grid=(b, PAIRS, nt // tq),
        in_specs=[
            pl.BlockSpec((1, LANES, tq), lambda i, j, t: (i, j, t)),
            pl.BlockSpec((1, nt, LANES), lambda i, j, t: (i, 0, j)),
            pl.BlockSpec((1, 2, VT_ROWS, nt), lambda i, j, t: (i, j, 0, 0)),
            _const_spec(dlam.shape), _const_spec(gpair.shape), _const_spec(lam_init.shape),
        ],
        out_specs=pl.BlockSpec((1, tq, LANES), lambda i, j, t: (i, t, j)),
        out_shape=jax.ShapeDtypeStruct((b, nt, DA_WIDTH), BF16),
        compiler_params=_params(("parallel", "parallel", "arbitrary")),
        name="diff_attn",
    )(dq_t, dk, dv_t, dlam, gpair, lam_init)


def _mla_kernel(q_ref, k_ref, vt_ref, o_ref, *, nt, nctx, tq):
    qi = pl.program_id(2)

    def attend(nk):
        chains = []
        for hh in range(2):
            key = lambda start, size, hh=hh: k_ref[0, hh, start:start + size, :]
            value_t = lambda start, size, hh=hh: vt_ref[0, hh, :, start:start + size]
            chains.append((q_ref[0, hh], key, value_t))
        o = _attend_t(chains, _key_chunks(nk, nctx))
        o_ref[0] = jnp.concatenate(o, axis=0).T.astype(BF16)

    n_ctx_blocks = nctx // tq

    @pl.when(qi < n_ctx_blocks)
    def _():
        attend(nctx)

    @pl.when(qi >= n_ctx_blocks)
    def _():
        attend(nt)


def _mla_attn(mq_t, mk, mv_t, nctx):
    b, nh, nt, _ = mk.shape
    tq = TOKEN_TILE
    return pl.pallas_call(
        functools.partial(_mla_kernel, nt=nt, nctx=nctx, tq=tq),
        grid=(b, nh // 2, nt // tq),
        in_specs=[
            pl.BlockSpec((1, 2, LANES, tq), lambda i, j, t: (i, j, 0, t)),
            pl.BlockSpec((1, 2, nt, LANES), lambda i, j, t: (i, j, 0, 0)),
            pl.BlockSpec((1, 2, VT_ROWS, nt), lambda i, j, t: (i, j, 0, 0)),
        ],
        out_specs=pl.BlockSpec((1, tq, LANES), lambda i, j, t: (i, t, j)),
        out_shape=jax.ShapeDtypeStruct((b, nt, nh * MLA_V), BF16),
        compiler_params=_params(("parallel", "parallel", "arbitrary")),
        name="mla_attn",
    )(mq_t, mk, mv_t)


def _layer_norm(z, g, b):
    mu = jnp.mean(z, axis=-1, keepdims=True)
    zc = z - mu
    var = jnp.mean(zc * zc, axis=-1, keepdims=True)
    return (zc * lax.rsqrt(var + LN_EPS)) * g + b


def _out_kernel(x_ref, mod_ref, lru_ref, da_ref, mla_ref, wo_ref, g_ref, b_ref, x1_ref, v_ref, *, alpha):
    a = jnp.concatenate([lru_ref[0], da_ref[0], mla_ref[0]], axis=-1)
    o = jnp.dot(a, wo_ref[...], preferred_element_type=F32)
    mod = mod_ref[0, 0]
    x1 = _layer_norm(alpha * x_ref[0] + mod[2:3] * o, g_ref[...], b_ref[...])
    x1_ref[0] = x1
    v_ref[0] = (x1 * (1.0 + mod[4:5]) + mod[3:4]).astype(BF16)


def _out_proj(xa, modt, lru_o, da_o, mla_o, wo, g, bb, nctx, alpha):
    b, nt, d = xa.shape
    tm = TOKEN_TILE
    nc = nctx // tm
    tok = lambda w: pl.BlockSpec((1, tm, w), lambda i, t: (i, t, 0))
    return pl.pallas_call(
        functools.partial(_out_kernel, alpha=alpha),
        grid=(b, nt // tm),
        in_specs=[
            tok(d),
            pl.BlockSpec((1, 1, 8, d), lambda i, t: (i, jnp.where(t >= nc, 1, 0), 0, 0)),
            tok(LRU_WIDTH), tok(da_o.shape[-1]), tok(mla_o.shape[-1]),
            _const_spec(wo.shape), _const_spec(g.shape), _const_spec(bb.shape),
        ],
        out_specs=[tok(d), tok(d)],
        out_shape=[jax.ShapeDtypeStruct((b, nt, d), F32), jax.ShapeDtypeStruct((b, nt, d), BF16)],
        compiler_params=_params(("parallel", "parallel")),
        name="out_proj",
    )(xa, modt, lru_o, da_o, mla_o, wo, g, bb)


def _router_gates(logits, rb):
    scores = jax.nn.sigmoid(logits)
    sel = scores + rb
    lane = lax.broadcasted_iota(jnp.int32, logits.shape, 1)
    r = lane & (EXPERTS_PER_GROUP - 1)
    grp = (lane >> 2) & (N_GROUPS - 1)

    def in_group(x, k):
        return jnp.where(r >= k, pltpu.roll(x, k, 1), pltpu.roll(x, LANES - EXPERTS_PER_GROUP + k, 1))

    others = [in_group(sel, k) for k in (1, 2, 3)]
    pair_max = sel + jnp.maximum(jnp.maximum(others[0], others[1]), others[2])
    grp_score = jnp.maximum(jnp.maximum(pair_max, in_group(pair_max, 1)),
                            jnp.maximum(in_group(pair_max, 2), in_group(pair_max, 3)))
    in_best = None
    for k in (1, 2, 3):
        other = pltpu.roll(grp_score, EXPERTS_PER_GROUP * k, 1)
        wins = (grp_score > other) | ((grp_score == other) & (grp < k))
        in_best = wins if in_best is None else (in_best & wins)
    beaten = jnp.zeros(logits.shape, F32)
    for k, o in zip((1, 2, 3), others):
        beats = (o > sel) | ((o == sel) & (r >= k))
        beaten = beaten + jnp.where(beats, 1.0, 0.0)
    chosen = in_best & (beaten < 2.0)
    sc = jnp.where(chosen, scores, 0.0)
    tot = sc + in_group(sc, 1) + in_group(sc, 2) + in_group(sc, 3)
    return jnp.where(chosen, sc / tot, 0.0)


def _moe_kernel(v_ref, x1_ref, mod_ref, rw_ref, rb_ref, w1_ref, w3_ref, w2_ref, g_ref, b_ref, o_ref, *, alpha):
    v = v_ref[0]
    gates = _router_gates(jnp.dot(v, rw_ref[...], preferred_element_type=F32), rb_ref[...])
    per = EXPERTS_PER_GROUP * D_EXPERT
    f = None
    for c in range(N_GROUPS):
        h1 = jnp.dot(v, w1_ref[:, c * per:(c + 1) * per], preferred_element_type=F32)
        h3 = jnp.dot(v, w3_ref[:, c * per:(c + 1) * per], preferred_element_type=F32)
        hh = (h1 * jax.nn.sigmoid(h1)) * h3
        parts = []
        for j in range(EXPERTS_PER_GROUP):
            e = c * EXPERTS_PER_GROUP + j
            parts.append((hh[:, j * D_EXPERT:(j + 1) * D_EXPERT] * gates[:, e:e + 1]).astype(BF16))
        y = jnp.dot(jnp.concatenate(parts, axis=-1), w2_ref[c * per:(c + 1) * per, :], preferred_element_type=F32)
        f = y if f is None else f + y
    mod = mod_ref[0, 0]
    o_ref[0] = _layer_norm(alpha * x1_ref[0] + mod[5:6] * f, g_ref[...], b_ref[...])


def _moe(v, x1, modt, rw, rb, w1c, w3c, w2c, g, bb, nctx, alpha):
    b, nt, d = x1.shape
    tm = TOKEN_TILE
    nc = nctx // tm
    tok = pl.BlockSpec((1, tm, d), lambda i, t: (i, t, 0))
    return pl.pallas_call(
        functools.partial(_moe_kernel, alpha=alpha),
        grid=(b, nt // tm),
        in_specs=[
            tok, tok,
            pl.BlockSpec((1, 1, 8, d), lambda i, t: (i, jnp.where(t >= nc, 1, 0), 0, 0)),
            _const_spec(rw.shape), _const_spec(rb.shape), _const_spec(w1c.shape), _const_spec(w3c.shape),
            _const_spec(w2c.shape), _const_spec(g.shape), _const_spec(bb.shape),
        ],
        out_specs=tok,
        out_shape=jax.ShapeDtypeStruct((b, nt, d), F32),
        compiler_params=_params(("parallel", "parallel")),
        name="moe",
    )(v, x1, modt, rw, rb, w1c, w3c, w2c, g, bb)


def _rotary_tables(n, nctx):
    rows = n // GRID_W
    row = jnp.repeat(jnp.arange(rows), GRID_W).astype(F32)
    col = jnp.tile(jnp.arange(GRID_W), rows).astype(F32)
    n_freq = DA_QK // 4
    inv = ROPE_THETA ** (-jnp.arange(n_freq, dtype=F32) / n_freq)
    ang = jnp.concatenate([row[:, None] * inv, col[:, None] * inv], axis=-1)
    ang = jnp.concatenate([jnp.zeros((nctx, DA_QK // 2), F32), ang], axis=0)
    c, s = jnp.cos(ang), jnp.sin(ang)
    reps = LANES // DA_QK
    return jnp.tile(jnp.concatenate([c, c], axis=-1), (1, reps)), jnp.tile(jnp.concatenate([-s, s], axis=-1), (1, reps))


def _pack_in_weight(w_in):
    d = w_in.shape[0]
    wkr = w_in[:, C_KR:C_KR + MLA_ROPE]
    krp = jnp.concatenate([jnp.zeros((d, MLA_NOPE), F32), wkr, jnp.zeros((d, LANES - MLA_NOPE - MLA_ROPE), F32)], axis=-1)
    return jnp.concatenate([w_in[:, :C_KR], krp], axis=-1).astype(BF16)


def _pack_uq(w_uq):
    r = w_uq.shape[0]
    w = w_uq.reshape(r, MLA_HEADS, MLA_NOPE + MLA_ROPE)
    w = jnp.concatenate([w, jnp.zeros((r, MLA_HEADS, LANES - MLA_NOPE - MLA_ROPE), F32)], axis=-1)
    return w.reshape(r, MLA_HEADS * LANES).astype(BF16)


def _pack_ukv(w_ukv):
    r = w_ukv.shape[0]
    w = w_ukv.reshape(r, MLA_HEADS, MLA_NOPE + MLA_V)
    z = jnp.zeros((r, MLA_HEADS, LANES - MLA_NOPE), F32)
    wk = jnp.concatenate([w[..., :MLA_NOPE], z], axis=-1).reshape(r, MLA_HEADS * LANES)
    wv = w[..., MLA_NOPE:].reshape(r, MLA_HEADS * MLA_V)
    return wk.astype(BF16), wv.astype(BF16)


def _block_diag(w):
    nd, nb, bs, _ = w.shape
    eye = jnp.eye(nb, dtype=w.dtype)
    return jnp.einsum('dhij,hg->dhigj', w, eye).reshape(nd, nb * bs, nb * bs)


def kernel(x, c, ctx, c_ctx, w_mod, b_mod, w_in, w_out, conv_w, conv_b, lru_wa, lru_ba, lru_wi, lru_bi, lru_lambda, diff_lambda, diff_norm, mla_q_norm, mla_kv_norm, mla_w_uq, mla_w_ukv, ln1_g, ln1_b, ln2_g, ln2_b, router_w, router_b, exp_w1, exp_w3, exp_w2):
    bsz, n, d = x.shape
    nctx = ctx.shape[1]
    depth = w_mod.shape[0]
    alpha = (2 * depth) ** 0.25
    assert nctx % TOKEN_TILE == 0 and n % TOKEN_TILE == 0 and n % GRID_W == 0

    rows = -(-(bsz + 1) // SUBLANES) * SUBLANES
    cc = jnp.concatenate([c, c_ctx[None, :], jnp.zeros((rows - bsz - 1, d), F32)], axis=0)
    mod = _modulation(cc, w_mod, b_mod).reshape(depth, rows, 6, d)
    mod = jnp.pad(mod, ((0, 0), (0, 0), (0, 2), (0, 0)))
    mod_ctx = jnp.broadcast_to(mod[:, bsz][:, None], (depth, bsz, 8, d))
    modt = jnp.stack([mod_ctx, mod[:, :bsz]], axis=2)

    cosf, sinf = _rotary_tables(n, nctx)
    rw = jnp.tile(router_w, (1, LANES // N_EXPERTS)).astype(BF16)
    rb = jnp.tile(router_b, LANES // N_EXPERTS)[None, :].astype(F32)
    gpair = jnp.tile(diff_norm, (1, LANES // DA_V))

    xa = jnp.concatenate([ctx, x], axis=1)
    for l in range(depth):
        lam_init = jnp.full((1, 1), 0.8 - 0.6 * math.exp(-0.3 * l), F32)
        wuk, wuv = _pack_ukv(mla_w_ukv[l])
        lx, lg, dq_t, dk, dv_t, mq_t, mk, mv_t = _in_proj(
            xa, modt[l], _pack_in_weight(w_in[l]), _pack_uq(mla_w_uq[l]), wuk, wuv,
            mla_q_norm[l][None, :], mla_kv_norm[l][None, :], cosf, sinf, nctx)
        lru_o = _lru(lx, lg, conv_w[l], conv_b[l][None, :], _block_diag(lru_wa[l]).astype(BF16),
                     _block_diag(lru_wi[l]).astype(BF16), lru_ba[l], lru_bi[l], lru_lambda[l], nctx)
        da_o = _da_attn(dq_t, dk, dv_t, diff_lambda[l], gpair[l][None, :], lam_init, nctx)
        mla_o = _mla_attn(mq_t, mk, mv_t, nctx)
        x1, v = _out_proj(xa, modt[l], lru_o, da_o, mla_o, w_out[l].astype(BF16),
                          ln1_g[l][None, :], ln1_b[l][None, :], nctx, alpha)
        w1c = exp_w1[l].transpose(1, 0, 2).reshape(d, N_EXPERTS * D_EXPERT).astype(BF16)
        w3c = exp_w3[l].transpose(1, 0, 2).reshape(d, N_EXPERTS * D_EXPERT).astype(BF16)
        w2c = exp_w2[l].reshape(N_EXPERTS * D_EXPERT, d).astype(BF16)
        xa = _moe(v, x1, modt[l], rw, rb, w1c, w3c, w2c, ln2_g[l][None, :], ln2_b[l][None, :], nctx, alpha)
    return xa[:, nctx:]
```

```python
import functools
import math

import jax
import jax.numpy as jnp
from jax import lax
from jax.experimental import pallas as pl
from jax.experimental.pallas import tpu as pltpu

F32 = jnp.float32
BF16 = jnp.bfloat16

GRID_W = 64
LRU_WIDTH = 256
LRU_BLOCKS = 4
CONV_W = 4
LRU_C = 8.0
DA_HEADS = 6
DA_QK = 32
DA_V = 2 * DA_QK
MLA_HEADS = 6
MLA_NOPE = 64
MLA_ROPE = 32
MLA_V = 64
Q_RANK = 256
KV_RANK = 128
MLA_SCALE = (MLA_NOPE + MLA_ROPE) ** -0.5
N_EXPERTS = 16
N_GROUPS = 4
EXPERTS_PER_GROUP = N_EXPERTS // N_GROUPS
D_EXPERT = 256
ROPE_THETA = 10000.0
LN_EPS = 1e-5
RMS_EPS = 1e-6

LANES = 128
SUBLANES = 8
TOKEN_TILE = 256
VMEM_LIMIT = 56 * 1024 * 1024

LOG2E = math.log2(math.e)
DA_QSCALE = DA_QK ** -0.5 * LOG2E
MLA_QSCALE = MLA_SCALE * LOG2E

DA_WIDTH = DA_HEADS * DA_V
C_LRU = 0
C_DAQ = 2 * LRU_WIDTH
C_DAK = C_DAQ + DA_WIDTH
C_DAV = C_DAK + DA_WIDTH
C_CQ = C_DAV + DA_WIDTH
C_CKV = C_CQ + Q_RANK
C_KR = C_CKV + KV_RANK
C_END = C_KR + LANES
PAIRS = DA_HEADS // 2
VT_ROWS = DA_V + 16


def _params(sem):
    return pltpu.CompilerParams(dimension_semantics=sem, vmem_limit_bytes=VMEM_LIMIT)


def _const_spec(shape):
    nd = len(shape)
    return pl.BlockSpec(shape, lambda *_: (0,) * nd, pipeline_mode=pl.Buffered(1))


def _mod_kernel(c_ref, w_ref, b_ref, o_ref):
    c = c_ref[...]
    s = c * jax.nn.sigmoid(c)
    o_ref[0] = jnp.dot(s.astype(BF16), w_ref[0].astype(BF16), preferred_element_type=F32) + b_ref[0]


def _modulation(cc, w_mod, b_mod):
    depth, d, d6 = w_mod.shape
    r = cc.shape[0]
    tn = min(d6, 1536)
    return pl.pallas_call(
        _mod_kernel,
        grid=(depth, d6 // tn),
        in_specs=[
            pl.BlockSpec((r, d), lambda l, j: (0, 0)),
            pl.BlockSpec((1, d, tn), lambda l, j: (l, 0, j)),
            pl.BlockSpec((1, 1, tn), lambda l, j: (l, 0, j)),
        ],
        out_specs=pl.BlockSpec((1, r, tn), lambda l, j: (l, 0, j)),
        out_shape=jax.ShapeDtypeStruct((depth, r, d6), F32),
        compiler_params=_params(("parallel", "parallel")),
        name="modulation",
    )(cc, w_mod, b_mod.reshape(depth, 1, d6))


def _rotate(t, cosf, sinf, first_half):
    partner = jnp.where(first_half, pltpu.roll(t, LANES - DA_QK // 2, 1), pltpu.roll(t, DA_QK // 2, 1))
    return t * cosf + partner * sinf


def _store_values_t(vt_ref, v):
    rows = v.shape[0]
    ones = jnp.ones((VT_ROWS - DA_V, rows), BF16)
    for j in range(PAIRS):
        t = v[:, LANES * j:LANES * (j + 1)].T.astype(BF16)
        for k in range(2):
            vt_ref[0, 2 * j + k, 0:DA_V, :] = t[DA_V * k:DA_V * (k + 1)]
            vt_ref[0, 2 * j + k, DA_V:VT_ROWS, :] = ones


def _in_kernel(x_ref, mod_ref, w1_ref, wuq_ref, wuk_ref, wuv_ref, qn_ref, kvn_ref, cos_ref, sin_ref,
               lx_ref, lg_ref, dq_ref, dk_ref, dv_ref, mq_ref, mk_ref, mv_ref):
    x = x_ref[0]
    mod = mod_ref[0, 0]
    u = x * (1.0 + mod[1:2]) + mod[0:1]
    y = jnp.dot(u.astype(BF16), w1_ref[...], preferred_element_type=F32)
    lx_ref[0] = y[:, C_LRU:C_LRU + LRU_WIDTH]
    lg_ref[0] = y[:, C_LRU + LRU_WIDTH:C_DAQ]

    cosf = cos_ref[...]
    sinf = sin_ref[...]
    lane = lax.broadcasted_iota(jnp.int32, cosf.shape, 1)
    first_half = (lane & (DA_QK // 2)) == 0
    rot = functools.partial(_rotate, cosf=cosf, sinf=sinf, first_half=first_half)

    for j in range(PAIRS):
        t = y[:, C_DAQ + LANES * j:C_DAQ + LANES * (j + 1)]
        dq_ref[0, LANES * j:LANES * (j + 1), :] = (rot(t) * DA_QSCALE).T.astype(BF16)
        t = y[:, C_DAK + LANES * j:C_DAK + LANES * (j + 1)]
        dk_ref[0, :, LANES * j:LANES * (j + 1)] = rot(t).astype(BF16)
    _store_values_t(dv_ref, y[:, C_DAV:C_CQ])

    cq = y[:, C_CQ:C_CKV]
    ckv = y[:, C_CKV:C_KR]
    krp = y[:, C_KR:C_END]
    is_rope = (lane >= MLA_NOPE) & (lane < MLA_NOPE + MLA_ROPE)
    kr = jnp.where(is_rope, rot(krp), krp)
    qn = (cq * lax.rsqrt(jnp.mean(cq * cq, axis=-1, keepdims=True) + RMS_EPS)) * qn_ref[...]
    q = jnp.dot(qn.astype(BF16), wuq_ref[...], preferred_element_type=F32)
    kvn = ((ckv * lax.rsqrt(jnp.mean(ckv * ckv, axis=-1, keepdims=True) + RMS_EPS)) * kvn_ref[...]).astype(BF16)
    kn = jnp.dot(kvn, wuk_ref[...], preferred_element_type=F32)
    for h in range(MLA_HEADS):
        t = q[:, LANES * h:LANES * (h + 1)]
        mq_ref[0, h] = (jnp.where(is_rope, rot(t), t) * MLA_QSCALE).T.astype(BF16)
        mk_ref[0, h] = (kn[:, LANES * h:LANES * (h + 1)] + kr).astype(BF16)
    _store_values_t(mv_ref, jnp.dot(kvn, wuv_ref[...], preferred_element_type=F32))


def _in_proj(xa, modt, w1, wuq, wuk, wuv, qnorm, kvnorm, cosf, sinf, nctx):
    b, nt, d = xa.shape
    tm = TOKEN_TILE
    nc = nctx // tm
    tok = lambda w: pl.BlockSpec((1, tm, w), lambda i, t: (i, t, 0))
    head_t = lambda r: pl.BlockSpec((1, DA_HEADS, r, tm), lambda i, t: (i, 0, 0, t))
    return pl.pallas_call(
        _in_kernel,
        grid=(b, nt // tm),
        in_specs=[
            tok(d),
            pl.BlockSpec((1, 1, 8, d), lambda i, t: (i, jnp.where(t >= nc, 1, 0), 0, 0)),
            _const_spec(w1.shape), _const_spec(wuq.shape), _const_spec(wuk.shape), _const_spec(wuv.shape),
            _const_spec(qnorm.shape), _const_spec(kvnorm.shape),
            pl.BlockSpec((tm, LANES), lambda i, t: (t, 0)),
            pl.BlockSpec((tm, LANES), lambda i, t: (t, 0)),
        ],
        out_specs=[
            tok(LRU_WIDTH), tok(LRU_WIDTH),
            pl.BlockSpec((1, DA_WIDTH, tm), lambda i, t: (i, 0, t)), tok(DA_WIDTH), head_t(VT_ROWS),
            head_t(LANES), pl.BlockSpec((1, MLA_HEADS, tm, LANES), lambda i, t: (i, 0, t, 0)), head_t(VT_ROWS),
        ],
        out_shape=[
            jax.ShapeDtypeStruct((b, nt, LRU_WIDTH), F32),
            jax.ShapeDtypeStruct((b, nt, LRU_WIDTH), F32),
            jax.ShapeDtypeStruct((b, DA_WIDTH, nt), BF16),
            jax.ShapeDtypeStruct((b, nt, DA_WIDTH), BF16),
            jax.ShapeDtypeStruct((b, DA_HEADS, VT_ROWS, nt), BF16),
            jax.ShapeDtypeStruct((b, MLA_HEADS, LANES, nt), BF16),
            jax.ShapeDtypeStruct((b, MLA_HEADS, nt, LANES), BF16),
            jax.ShapeDtypeStruct((b, MLA_HEADS, VT_ROWS, nt), BF16),
        ],
        compiler_params=_params(("parallel", "parallel")),
        name="in_proj",
    )(xa, modt, w1, wuq, wuk, wuv, qnorm, kvnorm, cosf, sinf)


def _gelu_tanh(x):
    return 0.5 * x * (1.0 + jnp.tanh(math.sqrt(2.0 / math.pi) * (x + 0.044715 * (x * x * x))))


def _lru_kernel(x_ref, g_ref, cw_ref, cb_ref, wa_ref, wi_ref, ba_ref, bi_ref, lam_ref, o_ref,
                y_s, a_s, s_s, h_s, *, nt, nctx, chunk):
    w = LRU_WIDTH
    tiles = chunk // SUBLANES
    n_chunks = nt // chunk
    sub = lax.broadcasted_iota(jnp.int32, (tiles, SUBLANES, w), 1)
    tile_i = lax.broadcasted_iota(jnp.int32, (tiles, SUBLANES, w), 0)

    def conv_chunk(c, carry):
        r0 = pl.multiple_of(c * chunk, chunk)
        lo = pl.multiple_of(jnp.maximum(r0 - SUBLANES, 0), SUBLANES)
        hi = pl.multiple_of(jnp.minimum(r0 + chunk, nt - SUBLANES), SUBLANES)
        x3 = jnp.concatenate([x_ref[0, pl.ds(lo, SUBLANES), :], x_ref[0, pl.ds(r0, chunk), :],
                              x_ref[0, pl.ds(hi, SUBLANES), :]], axis=0).reshape(tiles + 2, SUBLANES, w)
        sh1 = pltpu.roll(x3, 1, 1)
        sh2 = pltpu.roll(x3, 2, 1)
        sh7 = pltpu.roll(x3, SUBLANES - 1, 1)
        pos = r0 + tile_i * SUBLANES + sub
        in_ctx = pos < nctx
        seg_pos = jnp.where(in_ctx, pos, pos - nctx)
        seg_last = jnp.where(in_ctx, nctx - 1, nt - nctx - 1)
        zero = jnp.zeros((tiles, SUBLANES, w), F32)
        xm2 = jnp.where(seg_pos >= 2, jnp.where(sub >= 2, sh2[1:-1], sh2[0:-2]), zero)
        xm1 = jnp.where(seg_pos >= 1, jnp.where(sub >= 1, sh1[1:-1], sh1[0:-2]), zero)
        xp1 = jnp.where(seg_pos < seg_last, jnp.where(sub < SUBLANES - 1, sh7[1:-1], sh7[2:]), zero)
        y = cb_ref[...] + xm2 * cw_ref[0:1] + xm1 * cw_ref[1:2] + x3[1:-1] * cw_ref[2:3] + xp1 * cw_ref[3:4]
        y_s[pl.ds(r0, chunk), :] = y.reshape(chunk, w)
        return carry

    lax.fori_loop(0, n_chunks, conv_chunk, 0)

    nctx_t = nctx // SUBLANES
    nt_t = nt // SUBLANES

    for d in range(2):
        nlam = -lam_ref[d:d + 1]
        softplus = jnp.maximum(nlam, 0.0) + jnp.log1p(jnp.exp(-jnp.abs(nlam)))
        c8 = -LRU_C * softplus

        def gate_chunk(c, carry, d=d, c8=c8):
            r0 = pl.multiple_of(c * chunk, chunk)
            y = y_s[pl.ds(r0, chunk), :]
            yb = y.astype(BF16)
            r = jax.nn.sigmoid(jnp.dot(yb, wa_ref[d], preferred_element_type=F32) + ba_ref[d:d + 1])
            i = jax.nn.sigmoid(jnp.dot(yb, wi_ref[d], preferred_element_type=F32) + bi_ref[d:d + 1])
            log_a = c8 * r
            a = jnp.exp(log_a)
            th = jnp.tanh(log_a)
            u = jnp.sqrt(-2.0 * th / (1.0 - th)) * (i * y)
            a3 = a.reshape(tiles, SUBLANES, w)
            u3 = u.reshape(tiles, SUBLANES, w)
            for sft in (1, 2, 4):
                if d == 0:
                    ok = sub >= sft
                    ash = pltpu.roll(a3, sft, 1)
                    ush = pltpu.roll(u3, sft, 1)
                else:
                    ok = sub < SUBLANES - sft
                    ash = pltpu.roll(a3, SUBLANES - sft, 1)
                    ush = pltpu.roll(u3, SUBLANES - sft, 1)
                u3 = jnp.where(ok, a3 * ush + u3, u3)
                a3 = jnp.where(ok, a3 * ash, a3)
            a_s[pl.ds(r0, chunk), :] = a3.reshape(chunk, w)
            s_s[pl.ds(r0, chunk), :] = u3.reshape(chunk, w)
            return carry

        lax.fori_loop(0, n_chunks, gate_chunk, 0)

        def carry_tile(j, hprev, d=d):
            if d == 0:
                t = j
            else:
                t = jnp.where(j < nctx_t, nctx_t - 1 - j, nt_t - 1 - (j - nctx_t))
            r0 = pl.multiple_of(t * SUBLANES, SUBLANES)
            h = a_s[pl.ds(r0, SUBLANES), :] * hprev + s_s[pl.ds(r0, SUBLANES), :]
            if d == 0:
                h_s[pl.ds(r0, SUBLANES), :] = h
                return h[SUBLANES - 1:SUBLANES]
            h_s[pl.ds(r0, SUBLANES), :] = h_s[pl.ds(r0, SUBLANES), :] + h
            return h[0:1]

        lax.fori_loop(0, nt_t, carry_tile, jnp.zeros((1, w), F32), unroll=4)

    def out_chunk(c, carry):
        r0 = pl.multiple_of(c * chunk, chunk)
        o_ref[0, pl.ds(r0, chunk), :] = (h_s[pl.ds(r0, chunk), :] * _gelu_tanh(g_ref[0, pl.ds(r0, chunk), :])).astype(BF16)
        return carry

    lax.fori_loop(0, n_chunks, out_chunk, 0)


def _lru(lx, lg, conv_w, conv_b, wa, wi, ba, bi, lam, nctx):
    b, nt, w = lx.shape
    chunk = TOKEN_TILE
    seq = pl.BlockSpec((1, nt, w), lambda i: (i, 0, 0))
    return pl.pallas_call(
        functools.partial(_lru_kernel, nt=nt, nctx=nctx, chunk=chunk),
        grid=(b,),
        in_specs=[seq, seq, _const_spec(conv_w.shape), _const_spec(conv_b.shape), _const_spec(wa.shape),
                  _const_spec(wi.shape), _const_spec(ba.shape), _const_spec(bi.shape), _const_spec(lam.shape)],
        out_specs=seq,
        out_shape=jax.ShapeDtypeStruct((b, nt, w), BF16),
        scratch_shapes=[pltpu.VMEM((nt, w), F32)] * 4,
        compiler_params=_params(("parallel",)),
        name="rglru",
    )(lx, lg, conv_w, conv_b, wa, wi, ba, bi, lam)


KEY_CHUNK = 512


def _key_chunks(nk, nctx):
    chunks = [(0, nctx)]
    chunks += [(s, min(KEY_CHUNK, nk - s)) for s in range(nctx, nk, KEY_CHUNK)]
    return chunks


def _attend_t(chains, chunks):
    def scores(n, ci):
        q_t, key, _ = chains[n]
        return jnp.dot(key(*chunks[ci]), q_t, preferred_element_type=F32)

    s = [scores(n, 0) for n in range(len(chains))]
    state = [None] * len(chains)
    for ci in range(len(chunks)):
        for n, (_, _, value_t) in enumerate(chains):
            cm = jnp.max(s[n], axis=0, keepdims=True)
            if ci == 0:
                m_new = cm
            else:
                m_old, acc = state[n]
                m_new = jnp.maximum(m_old, cm)
            p = jnp.exp2(s[n] - m_new).astype(BF16)
            if ci + 1 < len(chunks):
                s[n] = scores(n, ci + 1)
            pv = jnp.dot(value_t(*chunks[ci]), p, preferred_element_type=F32)
            if ci > 0:
                pv = acc * jnp.exp2(m_old - m_new) + pv
            state[n] = (m_new, pv)
    return [acc[0:DA_V] / acc[DA_V:DA_V + 1] for _, acc in state]


DA_PAIRS_PER_STEP = 3


def _da_kernel(q_ref, k_ref, vt_ref, dl_ref, g_ref, li_ref, o_ref, *, nt, nctx, tq):
    qi = pl.program_id(2)
    row = lax.broadcasted_iota(jnp.int32, (LANES, tq), 0)
    dl = dl_ref[...]
    lam_init = li_ref[...]
    lam = (jnp.exp(jnp.sum(dl[0:1] * dl[1:2], axis=-1, keepdims=True))
           - jnp.exp(jnp.sum(dl[2:3] * dl[3:4], axis=-1, keepdims=True)) + lam_init)
    zero = jnp.zeros((LANES, tq), BF16)

    def attend(nk):
        chains = []
        for j in range(DA_PAIRS_PER_STEP):
            q_t = q_ref[0, LANES * j:LANES * (j + 1), :]
            key = lambda start, size, j=j: k_ref[0, start:start + size, LANES * j:LANES * (j + 1)]
            for hh in range(2):
                value_t = lambda start, size, h=2 * j + hh: vt_ref[0, h, :, start:start + size]
                for mi in range(2):
                    lo = (2 * hh + mi) * DA_QK
                    chains.append((jnp.where((row >= lo) & (row < lo + DA_QK), q_t, zero), key, value_t))
        o = _attend_t(chains, _key_chunks(nk, nctx))
        for j in range(DA_PAIRS_PER_STEP):
            halves = []
            for hh in range(2):
                d = o[4 * j + 2 * hh] - lam * o[4 * j + 2 * hh + 1]
                ms = jnp.mean(d * d, axis=0, keepdims=True)
                halves.append(d * lax.rsqrt(ms + RMS_EPS))
            out = jnp.concatenate(halves, axis=0).T
            o_ref[0, :, LANES * j:LANES * (j + 1)] = (out * g_ref[...] * (1.0 - lam_init)).astype(BF16)

    n_ctx_blocks = nctx // tq

    @pl.when(qi < n_ctx_blocks)
    def _():
        attend(nctx)

    @pl.when(qi >= n_ctx_blocks)
    def _():
        attend(nt)


def _da_attn(dq_t, dk, dv_t, dlam, gpair, lam_init, nctx):
    b, nt, _ = dk.shape
    tq = TOKEN_TILE
    pp = DA_PAIRS_PER_STEP
    return pl.pallas_call(
        functools.partial(_da_kernel, nt=nt, nctx=nctx, tq=tq),
        grid=(b, PAIRS // pp, nt // tq),
        in_specs=[
            pl.BlockSpec((1, pp * LANES, tq), lambda i, j, t: (i, j, t)),
            pl.BlockSpec((1, nt, pp * LANES), lambda i, j, t: (i, 0, j)),
            pl.BlockSpec((1, 2 * pp, VT_ROWS, nt), lambda i, j, t: (i, j, 0, 0)),
            _const_spec(dlam.shape), _const_spec(gpair.shape), _const_spec(lam_init.shape),
        ],
        out_specs=pl.BlockSpec((1, tq, pp * LANES), lambda i, j, t: (i, t, j)),
        out_shape=jax.ShapeDtypeStruct((b, nt, DA_WIDTH), BF16),
        compiler_params=_params(("parallel", "parallel", "arbitrary")),
        name="diff_attn",
    )(dq_t, dk, dv_t, dlam, gpair, lam_init)


MLA_HEADS_PER_STEP = 6


def _mla_kernel(q_ref, k_ref, vt_ref, o_ref, *, nt, nctx, tq):
    qi = pl.program_id(2)

    def attend(nk):
        chains = []
        for hh in range(MLA_HEADS_PER_STEP):
            key = lambda start, size, hh=hh: k_ref[0, hh, start:start + size, :]
            value_t = lambda start, size, hh=hh: vt_ref[0, hh, :, start:start + size]
            chains.append((q_ref[0, hh], key, value_t))
        o = _attend_t(chains, _key_chunks(nk, nctx))
        for j in range(MLA_HEADS_PER_STEP // 2):
            o_ref[0, :, LANES * j:LANES * (j + 1)] = jnp.concatenate(o[2 * j:2 * j + 2], axis=0).T.astype(BF16)

    n_ctx_blocks = nctx // tq

    @pl.when(qi < n_ctx_blocks)
    def _():
        attend(nctx)

    @pl.when(qi >= n_ctx_blocks)
    def _():
        attend(nt)


def _mla_attn(mq_t, mk, mv_t, nctx):
    b, nh, nt, _ = mk.shape
    tq = TOKEN_TILE
    hp = MLA_HEADS_PER_STEP
    return pl.pallas_call(
        functools.partial(_mla_kernel, nt=nt, nctx=nctx, tq=tq),
        grid=(b, nh // hp, nt // tq),
        in_specs=[
            pl.BlockSpec((1, hp, LANES, tq), lambda i, j, t: (i, j, 0, t)),
            pl.BlockSpec((1, hp, nt, LANES), lambda i, j, t: (i, j, 0, 0)),
            pl.BlockSpec((1, hp, VT_ROWS, nt), lambda i, j, t: (i, j, 0, 0)),
        ],
        out_specs=pl.BlockSpec((1, tq, hp * MLA_V), lambda i, j, t: (i, t, j)),
        out_shape=jax.ShapeDtypeStruct((b, nt, nh * MLA_V), BF16),
        compiler_params=_params(("parallel", "parallel", "arbitrary")),
        name="mla_attn",
    )(mq_t, mk, mv_t)


def _layer_norm(z, g, b):
    mu = jnp.mean(z, axis=-1, keepdims=True)
    zc = z - mu
    var = jnp.mean(zc * zc, axis=-1, keepdims=True)
    return (zc * lax.rsqrt(var + LN_EPS)) * g + b


def _out_kernel(x_ref, mod_ref, lru_ref, da_ref, mla_ref, wo_ref, g_ref, b_ref, x1_ref, v_ref, *, alpha):
    a = jnp.concatenate([lru_ref[0], da_ref[0], mla_ref[0]], axis=-1)
    o = jnp.dot(a, wo_ref[...], preferred_element_type=F32)
    mod = mod_ref[0, 0]
    x1 = _layer_norm(alpha * x_ref[0] + mod[2:3] * o, g_ref[...], b_ref[...])
    x1_ref[0] = x1
    v_ref[0] = (x1 * (1.0 + mod[4:5]) + mod[3:4]).astype(BF16)


def _out_proj(xa, modt, lru_o, da_o, mla_o, wo, g, bb, nctx, alpha):
    b, nt, d = xa.shape
    tm = TOKEN_TILE
    nc = nctx // tm
    tok = lambda w: pl.BlockSpec((1, tm, w), lambda i, t: (i, t, 0))
    return pl.pallas_call(
        functools.partial(_out_kernel, alpha=alpha),
        grid=(b, nt // tm),
        in_specs=[
            tok(d),
            pl.BlockSpec((1, 1, 8, d), lambda i, t: (i, jnp.where(t >= nc, 1, 0), 0, 0)),
            tok(LRU_WIDTH), tok(da_o.shape[-1]), tok(mla_o.shape[-1]),
            _const_spec(wo.shape), _const_spec(g.shape), _const_spec(bb.shape),
        ],
        out_specs=[tok(d), tok(d)],
        out_shape=[jax.ShapeDtypeStruct((b, nt, d), F32), jax.ShapeDtypeStruct((b, nt, d), BF16)],
        compiler_params=_params(("parallel", "parallel")),
        name="out_proj",
    )(xa, modt, lru_o, da_o, mla_o, wo, g, bb)


def _router_gates(logits, rb):
    scores = jax.nn.sigmoid(logits)
    sel = scores + rb
    lane = lax.broadcasted_iota(jnp.int32, logits.shape, 1)
    r = lane & (EXPERTS_PER_GROUP - 1)
    grp = (lane >> 2) & (N_GROUPS - 1)

    def in_group(x, k):
        return jnp.where(r >= k, pltpu.roll(x, k, 1), pltpu.roll(x, LANES - EXPERTS_PER_GROUP + k, 1))

    others = [in_group(sel, k) for k in (1, 2, 3)]
    pair_max = sel + jnp.maximum(jnp.maximum(others[0], others[1]), others[2])
    grp_score = jnp.maximum(jnp.maximum(pair_max, in_group(pair_max, 1)),
                            jnp.maximum(in_group(pair_max, 2), in_group(pair_max, 3)))
    in_best = None
    for k in (1, 2, 3):
        other = pltpu.roll(grp_score, EXPERTS_PER_GROUP * k, 1)
        wins = (grp_score > other) | ((grp_score == other) & (grp < k))
        in_best = wins if in_best is None else (in_best & wins)
    beaten = jnp.zeros(logits.shape, F32)
    for k, o in zip((1, 2, 3), others):
        beats = (o > sel) | ((o == sel) & (r >= k))
        beaten = beaten + jnp.where(beats, 1.0, 0.0)
    chosen = in_best & (beaten < 2.0)
    sc = jnp.where(chosen, scores, 0.0)
    tot = sc + in_group(sc, 1) + in_group(sc, 2) + in_group(sc, 3)
    return jnp.where(chosen, sc / tot, 0.0)


def _moe_kernel(v_ref, x1_ref, mod_ref, rw_ref, rb_ref, w1_ref, w3_ref, w2_ref, g_ref, b_ref, o_ref, *, alpha):
    v = v_ref[0]
    gates = _router_gates(jnp.dot(v, rw_ref[...], preferred_element_type=F32), rb_ref[...])
    per = EXPERTS_PER_GROUP * D_EXPERT
    f = None
    for c in range(N_GROUPS):
        h1 = jnp.dot(v, w1_ref[:, c * per:(c + 1) * per], preferred_element_type=F32)
        h3 = jnp.dot(v, w3_ref[:, c * per:(c + 1) * per], preferred_element_type=F32)
        hh = (h1 * jax.nn.sigmoid(h1)) * h3
        parts = []
        for j in range(EXPERTS_PER_GROUP):
            e = c * EXPERTS_PER_GROUP + j
            parts.append((hh[:, j * D_EXPERT:(j + 1) * D_EXPERT] * gates[:, e:e + 1]).astype(BF16))
        y = jnp.dot(jnp.concatenate(parts, axis=-1), w2_ref[c * per:(c + 1) * per, :], preferred_element_type=F32)
        f = y if f is None else f + y
    mod = mod_ref[0, 0]
    o_ref[0] = _layer_norm(alpha * x1_ref[0] + mod[5:6] * f, g_ref[...], b_ref[...])


def _moe(v, x1, modt, rw, rb, w1c, w3c, w2c, g, bb, nctx, alpha):
    b, nt, d = x1.shape
    tm = TOKEN_TILE
    nc = nctx // tm
    tok = pl.BlockSpec((1, tm, d), lambda i, t: (i, t, 0))
    return pl.pallas_call(
        functools.partial(_moe_kernel, alpha=alpha),
        grid=(b, nt // tm),
        in_specs=[
            tok, tok,
            pl.BlockSpec((1, 1, 8, d), lambda i, t: (i, jnp.where(t >= nc, 1, 0), 0, 0)),
            _const_spec(rw.shape), _const_spec(rb.shape), _const_spec(w1c.shape), _const_spec(w3c.shape),
            _const_spec(w2c.shape), _const_spec(g.shape), _const_spec(bb.shape),
        ],
        out_specs=tok,
        out_shape=jax.ShapeDtypeStruct((b, nt, d), F32),
        compiler_params=_params(("parallel", "parallel")),
        name="moe",
    )(v, x1, modt, rw, rb, w1c, w3c, w2c, g, bb)


def _rotary_tables(n, nctx):
    rows = n // GRID_W
    row = jnp.repeat(jnp.arange(rows), GRID_W).astype(F32)
    col = jnp.tile(jnp.arange(GRID_W), rows).astype(F32)
    n_freq = DA_QK // 4
    inv = ROPE_THETA ** (-jnp.arange(n_freq, dtype=F32) / n_freq)
    ang = jnp.concatenate([row[:, None] * inv, col[:, None] * inv], axis=-1)
    ang = jnp.concatenate([jnp.zeros((nctx, DA_QK // 2), F32), ang], axis=0)
    c, s = jnp.cos(ang), jnp.sin(ang)
    reps = LANES // DA_QK
    return jnp.tile(jnp.concatenate([c, c], axis=-1), (1, reps)), jnp.tile(jnp.concatenate([-s, s], axis=-1), (1, reps))


def _pack_in_weight(w_in):
    d = w_in.shape[0]
    wkr = w_in[:, C_KR:C_KR + MLA_ROPE]
    krp = jnp.concatenate([jnp.zeros((d, MLA_NOPE), F32), wkr, jnp.zeros((d, LANES - MLA_NOPE - MLA_ROPE), F32)], axis=-1)
    return jnp.concatenate([w_in[:, :C_KR], krp], axis=-1).astype(BF16)


def _pack_uq(w_uq):
    r = w_uq.shape[0]
    w = w_uq.reshape(r, MLA_HEADS, MLA_NOPE + MLA_ROPE)
    w = jnp.concatenate([w, jnp.zeros((r, MLA_HEADS, LANES - MLA_NOPE - MLA_ROPE), F32)], axis=-1)
    return w.reshape(r, MLA_HEADS * LANES).astype(BF16)


def _pack_ukv(w_ukv):
    r = w_ukv.shape[0]
    w = w_ukv.reshape(r, MLA_HEADS, MLA_NOPE + MLA_V)
    z = jnp.zeros((r, MLA_HEADS, LANES - MLA_NOPE), F32)
    wk = jnp.concatenate([w[..., :MLA_NOPE], z], axis=-1).reshape(r, MLA_HEADS * LANES)
    wv = w[..., MLA_NOPE:].reshape(r, MLA_HEADS * MLA_V)
    return wk.astype(BF16), wv.astype(BF16)


def _block_diag(w):
    nd, nb, bs, _ = w.shape
    eye = jnp.eye(nb, dtype=w.dtype)
    return jnp.einsum('dhij,hg->dhigj', w, eye).reshape(nd, nb * bs, nb * bs)


def kernel(x, c, ctx, c_ctx, w_mod, b_mod, w_in, w_out, conv_w, conv_b, lru_wa, lru_ba, lru_wi, lru_bi, lru_lambda, diff_lambda, diff_norm, mla_q_norm, mla_kv_norm, mla_w_uq, mla_w_ukv, ln1_g, ln1_b, ln2_g, ln2_b, router_w, router_b, exp_w1, exp_w3, exp_w2):
    bsz, n, d = x.shape
    nctx = ctx.shape[1]
    depth = w_mod.shape[0]
    alpha = (2 * depth) ** 0.25
    assert nctx % TOKEN_TILE == 0 and n % TOKEN_TILE == 0 and n % GRID_W == 0

    rows = -(-(bsz + 1) // SUBLANES) * SUBLANES
    cc = jnp.concatenate([c, c_ctx[None, :], jnp.zeros((rows - bsz - 1, d), F32)], axis=0)
    mod = _modulation(cc, w_mod, b_mod).reshape(depth, rows, 6, d)
    mod = jnp.pad(mod, ((0, 0), (0, 0), (0, 2), (0, 0)))
    mod_ctx = jnp.broadcast_to(mod[:, bsz][:, None], (depth, bsz, 8, d))
    modt = jnp.stack([mod_ctx, mod[:, :bsz]], axis=2)

    cosf, sinf = _rotary_tables(n, nctx)
    rw = jnp.tile(router_w, (1, LANES // N_EXPERTS)).astype(BF16)
    rb = jnp.tile(router_b, LANES // N_EXPERTS)[None, :].astype(F32)
    gpair = jnp.tile(diff_norm, (1, LANES // DA_V))

    xa = jnp.concatenate([ctx, x], axis=1)
    for l in range(depth):
        lam_init = jnp.full((1, 1), 0.8 - 0.6 * math.exp(-0.3 * l), F32)
        wuk, wuv = _pack_ukv(mla_w_ukv[l])
        lx, lg, dq_t, dk, dv_t, mq_t, mk, mv_t = _in_proj(
            xa, modt[l], _pack_in_weight(w_in[l]), _pack_uq(mla_w_uq[l]), wuk, wuv,
            mla_q_norm[l][None, :], mla_kv_norm[l][None, :], cosf, sinf, nctx)
        lru_o = _lru(lx, lg, conv_w[l], conv_b[l][None, :], _block_diag(lru_wa[l]).astype(BF16),
                     _block_diag(lru_wi[l]).astype(BF16), lru_ba[l], lru_bi[l], lru_lambda[l], nctx)
        da_o = _da_attn(dq_t, dk, dv_t, diff_lambda[l], gpair[l][None, :], lam_init, nctx)
        mla_o = _mla_attn(mq_t, mk, mv_t, nctx)
        x1, v = _out_proj(xa, modt[l], lru_o, da_o, mla_o, w_out[l].astype(BF16),
                          ln1_g[l][None, :], ln1_b[l][None, :], nctx, alpha)
        w1c = exp_w1[l].transpose(1, 0, 2).reshape(d, N_EXPERTS * D_EXPERT).astype(BF16)
        w3c = exp_w3[l].transpose(1, 0, 2).reshape(d, N_EXPERTS * D_EXPERT).astype(BF16)
        w2c = exp_w2[l].reshape(N_EXPERTS * D_EXPERT, d).astype(BF16)
        xa = _moe(v, x1, modt[l], rw, rb, w1c, w3c, w2c, ln2_g[l][None, :], ln2_b[l][None, :], nctx, alpha)
    return xa[:, nctx:]
```

```python
import functools
import math

import jax
import jax.numpy as jnp
from jax import lax
from jax.experimental import pallas as pl
from jax.experimental.pallas import tpu as pltpu

F32 = jnp.float32
BF16 = jnp.bfloat16

GRID_W = 64
LRU_WIDTH = 256
LRU_BLOCKS = 4
CONV_W = 4
LRU_C = 8.0
DA_HEADS = 6
DA_QK = 32
DA_V = 2 * DA_QK
MLA_HEADS = 6
MLA_NOPE = 64
MLA_ROPE = 32
MLA_V = 64
Q_RANK = 256
KV_RANK = 128
MLA_SCALE = (MLA_NOPE + MLA_ROPE) ** -0.5
N_EXPERTS = 16
N_GROUPS = 4
EXPERTS_PER_GROUP = N_EXPERTS // N_GROUPS
D_EXPERT = 256
ROPE_THETA = 10000.0
LN_EPS = 1e-5
RMS_EPS = 1e-6

LANES = 128
SUBLANES = 8
TOKEN_TILE = 256
VMEM_LIMIT = 56 * 1024 * 1024

LOG2E = math.log2(math.e)
DA_QSCALE = DA_QK ** -0.5 * LOG2E
MLA_QSCALE = MLA_SCALE * LOG2E

DA_WIDTH = DA_HEADS * DA_V
C_LRU = 0
C_DAQ = 2 * LRU_WIDTH
C_DAK = C_DAQ + DA_WIDTH
C_DAV = C_DAK + DA_WIDTH
C_CQ = C_DAV + DA_WIDTH
C_CKV = C_CQ + Q_RANK
C_KR = C_CKV + KV_RANK
C_END = C_KR + LANES
PAIRS = DA_HEADS // 2
VT_ROWS = DA_V + 16


def _params(sem):
    return pltpu.CompilerParams(dimension_semantics=sem, vmem_limit_bytes=VMEM_LIMIT)


def _const_spec(shape):
    nd = len(shape)
    return pl.BlockSpec(shape, lambda *_: (0,) * nd, pipeline_mode=pl.Buffered(1))


def _mod_kernel(c_ref, w_ref, b_ref, o_ref):
    c = c_ref[...]
    s = c * jax.nn.sigmoid(c)
    o_ref[0] = jnp.dot(s.astype(BF16), w_ref[0].astype(BF16), preferred_element_type=F32) + b_ref[0]


def _modulation(cc, w_mod, b_mod):
    depth, d, d6 = w_mod.shape
    r = cc.shape[0]
    tn = min(d6, 1536)
    return pl.pallas_call(
        _mod_kernel,
        grid=(depth, d6 // tn),
        in_specs=[
            pl.BlockSpec((r, d), lambda l, j: (0, 0)),
            pl.BlockSpec((1, d, tn), lambda l, j: (l, 0, j)),
            pl.BlockSpec((1, 1, tn), lambda l, j: (l, 0, j)),
        ],
        out_specs=pl.BlockSpec((1, r, tn), lambda l, j: (l, 0, j)),
        out_shape=jax.ShapeDtypeStruct((depth, r, d6), F32),
        compiler_params=_params(("parallel", "parallel")),
        name="modulation",
    )(cc, w_mod, b_mod.reshape(depth, 1, d6))


def _rotate(t, cosf, sinf, first_half):
    partner = jnp.where(first_half, pltpu.roll(t, LANES - DA_QK // 2, 1), pltpu.roll(t, DA_QK // 2, 1))
    return t * cosf + partner * sinf


def _store_values_t(vt_ref, v):
    rows = v.shape[0]
    ones = jnp.ones((VT_ROWS - DA_V, rows), BF16)
    for j in range(PAIRS):
        t = v[:, LANES * j:LANES * (j + 1)].T.astype(BF16)
        for k in range(2):
            vt_ref[0, 2 * j + k, 0:DA_V, :] = t[DA_V * k:DA_V * (k + 1)]
            vt_ref[0, 2 * j + k, DA_V:VT_ROWS, :] = ones


def _in_kernel(x_ref, mod_ref, w1_ref, wuq_ref, wuk_ref, wuv_ref, qn_ref, kvn_ref, cos_ref, sin_ref,
               lx_ref, lg_ref, dq_ref, dk_ref, dv_ref, mq_ref, mk_ref, mv_ref):
    x = x_ref[0]
    mod = mod_ref[0, 0]
    u = x * (1.0 + mod[1:2]) + mod[0:1]
    y = jnp.dot(u.astype(BF16), w1_ref[...], preferred_element_type=F32)
    lx_ref[0] = y[:, C_LRU:C_LRU + LRU_WIDTH]
    lg_ref[0] = y[:, C_LRU + LRU_WIDTH:C_DAQ]

    cosf = cos_ref[...]
    sinf = sin_ref[...]
    lane = lax.broadcasted_iota(jnp.int32, cosf.shape, 1)
    first_half = (lane & (DA_QK // 2)) == 0
    rot = functools.partial(_rotate, cosf=cosf, sinf=sinf, first_half=first_half)

    for j in range(PAIRS):
        t = y[:, C_DAQ + LANES * j:C_DAQ + LANES * (j + 1)]
        dq_ref[0, LANES * j:LANES * (j + 1), :] = (rot(t) * DA_QSCALE).T.astype(BF16)
        t = y[:, C_DAK + LANES * j:C_DAK + LANES * (j + 1)]
        dk_ref[0, :, LANES * j:LANES * (j + 1)] = rot(t).astype(BF16)
    _store_values_t(dv_ref, y[:, C_DAV:C_CQ])

    cq = y[:, C_CQ:C_CKV]
    ckv = y[:, C_CKV:C_KR]
    krp = y[:, C_KR:C_END]
    is_rope = (lane >= MLA_NOPE) & (lane < MLA_NOPE + MLA_ROPE)
    kr = jnp.where(is_rope, rot(krp), krp)
    qn = (cq * lax.rsqrt(jnp.mean(cq * cq, axis=-1, keepdims=True) + RMS_EPS)) * qn_ref[...]
    q = jnp.dot(qn.astype(BF16), wuq_ref[...], preferred_element_type=F32)
    kvn = ((ckv * lax.rsqrt(jnp.mean(ckv * ckv, axis=-1, keepdims=True) + RMS_EPS)) * kvn_ref[...]).astype(BF16)
    kn = jnp.dot(kvn, wuk_ref[...], preferred_element_type=F32)
    for h in range(MLA_HEADS):
        t = q[:, LANES * h:LANES * (h + 1)]
        mq_ref[0, h] = (jnp.where(is_rope, rot(t), t) * MLA_QSCALE).T.astype(BF16)
        mk_ref[0, h] = (kn[:, LANES * h:LANES * (h + 1)] + kr).astype(BF16)
    _store_values_t(mv_ref, jnp.dot(kvn, wuv_ref[...], preferred_element_type=F32))


def _in_proj(xa, modt, w1, wuq, wuk, wuv, qnorm, kvnorm, cosf, sinf, nctx):
    b, nt, d = xa.shape
    tm = TOKEN_TILE
    nc = nctx // tm
    tok = lambda w: pl.BlockSpec((1, tm, w), lambda i, t: (i, t, 0))
    head_t = lambda r: pl.BlockSpec((1, DA_HEADS, r, tm), lambda i, t: (i, 0, 0, t))
    return pl.pallas_call(
        _in_kernel,
        grid=(b, nt // tm),
        in_specs=[
            tok(d),
            pl.BlockSpec((1, 1, 8, d), lambda i, t: (i, jnp.where(t >= nc, 1, 0), 0, 0)),
            _const_spec(w1.shape), _const_spec(wuq.shape), _const_spec(wuk.shape), _const_spec(wuv.shape),
            _const_spec(qnorm.shape), _const_spec(kvnorm.shape),
            pl.BlockSpec((tm, LANES), lambda i, t: (t, 0)),
            pl.BlockSpec((tm, LANES), lambda i, t: (t, 0)),
        ],
        out_specs=[
            tok(LRU_WIDTH), tok(LRU_WIDTH),
            pl.BlockSpec((1, DA_WIDTH, tm), lambda i, t: (i, 0, t)), tok(DA_WIDTH), head_t(VT_ROWS),
            head_t(LANES), pl.BlockSpec((1, MLA_HEADS, tm, LANES), lambda i, t: (i, 0, t, 0)), head_t(VT_ROWS),
        ],
        out_shape=[
            jax.ShapeDtypeStruct((b, nt, LRU_WIDTH), F32),
            jax.ShapeDtypeStruct((b, nt, LRU_WIDTH), F32),
            jax.ShapeDtypeStruct((b, DA_WIDTH, nt), BF16),
            jax.ShapeDtypeStruct((b, nt, DA_WIDTH), BF16),
            jax.ShapeDtypeStruct((b, DA_HEADS, VT_ROWS, nt), BF16),
            jax.ShapeDtypeStruct((b, MLA_HEADS, LANES, nt), BF16),
            jax.ShapeDtypeStruct((b, MLA_HEADS, nt, LANES), BF16),
            jax.ShapeDtypeStruct((b, MLA_HEADS, VT_ROWS, nt), BF16),
        ],
        compiler_params=_params(("parallel", "parallel")),
        name="in_proj",
    )(xa, modt, w1, wuq, wuk, wuv, qnorm, kvnorm, cosf, sinf)


def _gelu_tanh(x):
    return 0.5 * x * (1.0 + jnp.tanh(math.sqrt(2.0 / math.pi) * (x + 0.044715 * (x * x * x))))


def _lru_kernel(x_ref, g_ref, cw_ref, cb_ref, wa_ref, wi_ref, ba_ref, bi_ref, lam_ref, o_ref,
                y_s, a_s, s_s, h_s, *, nt, nctx, chunk):
    w = LRU_WIDTH
    tiles = chunk // SUBLANES
    n_chunks = nt // chunk
    sub = lax.broadcasted_iota(jnp.int32, (tiles, SUBLANES, w), 1)
    tile_i = lax.broadcasted_iota(jnp.int32, (tiles, SUBLANES, w), 0)

    def conv_chunk(c, carry):
        r0 = pl.multiple_of(c * chunk, chunk)
        lo = pl.multiple_of(jnp.maximum(r0 - SUBLANES, 0), SUBLANES)
        hi = pl.multiple_of(jnp.minimum(r0 + chunk, nt - SUBLANES), SUBLANES)
        x3 = jnp.concatenate([x_ref[0, pl.ds(lo, SUBLANES), :], x_ref[0, pl.ds(r0, chunk), :],
                              x_ref[0, pl.ds(hi, SUBLANES), :]], axis=0).reshape(tiles + 2, SUBLANES, w)
        sh1 = pltpu.roll(x3, 1, 1)
        sh2 = pltpu.roll(x3, 2, 1)
        sh7 = pltpu.roll(x3, SUBLANES - 1, 1)
        pos = r0 + tile_i * SUBLANES + sub
        in_ctx = pos < nctx
        seg_pos = jnp.where(in_ctx, pos, pos - nctx)
        seg_last = jnp.where(in_ctx, nctx - 1, nt - nctx - 1)
        zero = jnp.zeros((tiles, SUBLANES, w), F32)
        xm2 = jnp.where(seg_pos >= 2, jnp.where(sub >= 2, sh2[1:-1], sh2[0:-2]), zero)
        xm1 = jnp.where(seg_pos >= 1, jnp.where(sub >= 1, sh1[1:-1], sh1[0:-2]), zero)
        xp1 = jnp.where(seg_pos < seg_last, jnp.where(sub < SUBLANES - 1, sh7[1:-1], sh7[2:]), zero)
        y = cb_ref[...] + xm2 * cw_ref[0:1] + xm1 * cw_ref[1:2] + x3[1:-1] * cw_ref[2:3] + xp1 * cw_ref[3:4]
        y_s[pl.ds(r0, chunk), :] = y.reshape(chunk, w)
        return carry

    lax.fori_loop(0, n_chunks, conv_chunk, 0)

    nctx_t = nctx // SUBLANES
    nt_t = nt // SUBLANES

    for d in range(2):
        nlam = -lam_ref[d:d + 1]
        softplus = jnp.maximum(nlam, 0.0) + jnp.log1p(jnp.exp(-jnp.abs(nlam)))
        c8 = -LRU_C * softplus

        def gate_chunk(c, carry, d=d, c8=c8):
            r0 = pl.multiple_of(c * chunk, chunk)
            y = y_s[pl.ds(r0, chunk), :]
            yb = y.astype(BF16)
            r = jax.nn.sigmoid(jnp.dot(yb, wa_ref[d], preferred_element_type=F32) + ba_ref[d:d + 1])
            i = jax.nn.sigmoid(jnp.dot(yb, wi_ref[d], preferred_element_type=F32) + bi_ref[d:d + 1])
            log_a = c8 * r
            a = jnp.exp(log_a)
            th = jnp.tanh(log_a)
            u = jnp.sqrt(-2.0 * th / (1.0 - th)) * (i * y)
            a3 = a.reshape(tiles, SUBLANES, w)
            u3 = u.reshape(tiles, SUBLANES, w)
            for sft in (1, 2, 4):
                if d == 0:
                    ok = sub >= sft
                    ash = pltpu.roll(a3, sft, 1)
                    ush = pltpu.roll(u3, sft, 1)
                else:
                    ok = sub < SUBLANES - sft
                    ash = pltpu.roll(a3, SUBLANES - sft, 1)
                    ush = pltpu.roll(u3, SUBLANES - sft, 1)
                u3 = jnp.where(ok, a3 * ush + u3, u3)
                a3 = jnp.where(ok, a3 * ash, a3)
            a_s[pl.ds(r0, chunk), :] = a3.reshape(chunk, w)
            s_s[pl.ds(r0, chunk), :] = u3.reshape(chunk, w)
            return carry

        lax.fori_loop(0, n_chunks, gate_chunk, 0)

        def carry_tile(j, hprev, d=d):
            if d == 0:
                t = j
            else:
                t = jnp.where(j < nctx_t, nctx_t - 1 - j, nt_t - 1 - (j - nctx_t))
            r0 = pl.multiple_of(t * SUBLANES, SUBLANES)
            h = a_s[pl.ds(r0, SUBLANES), :] * hprev + s_s[pl.ds(r0, SUBLANES), :]
            if d == 0:
                h_s[pl.ds(r0, SUBLANES), :] = h
                return h[SUBLANES - 1:SUBLANES]
            h_s[pl.ds(r0, SUBLANES), :] = h_s[pl.ds(r0, SUBLANES), :] + h
            return h[0:1]

        lax.fori_loop(0, nt_t, carry_tile, jnp.zeros((1, w), F32), unroll=4)

    def out_chunk(c, carry):
        r0 = pl.multiple_of(c * chunk, chunk)
        o_ref[0, pl.ds(r0, chunk), :] = (h_s[pl.ds(r0, chunk), :] * _gelu_tanh(g_ref[0, pl.ds(r0, chunk), :])).astype(BF16)
        return carry

    lax.fori_loop(0, n_chunks, out_chunk, 0)


def _lru(lx, lg, conv_w, conv_b, wa, wi, ba, bi, lam, nctx):
    b, nt, w = lx.shape
    chunk = TOKEN_TILE
    seq = pl.BlockSpec((1, nt, w), lambda i: (i, 0, 0))
    return pl.pallas_call(
        functools.partial(_lru_kernel, nt=nt, nctx=nctx, chunk=chunk),
        grid=(b,),
        in_specs=[seq, seq, _const_spec(conv_w.shape), _const_spec(conv_b.shape), _const_spec(wa.shape),
                  _const_spec(wi.shape), _const_spec(ba.shape), _const_spec(bi.shape), _const_spec(lam.shape)],
        out_specs=seq,
        out_shape=jax.ShapeDtypeStruct((b, nt, w), BF16),
        scratch_shapes=[pltpu.VMEM((nt, w), F32)] * 4,
        compiler_params=_params(("parallel",)),
        name="rglru",
    )(lx, lg, conv_w, conv_b, wa, wi, ba, bi, lam)


DA_KEY_CHUNK = 512
MLA_KEY_CHUNK = 256


def _key_chunks(nk, nctx, size):
    chunks = [(0, nctx)]
    chunks += [(s, min(size, nk - s)) for s in range(nctx, nk, size)]
    return chunks


def _attend_t(chains, chunks):
    def scores(n, ci):
        q_t, key, _ = chains[n]
        return jnp.dot(key(*chunks[ci]), q_t, preferred_element_type=F32)

    s = [scores(n, 0) for n in range(len(chains))]
    state = [None] * len(chains)
    for ci in range(len(chunks)):
        for n, (_, _, value_t) in enumerate(chains):
            cm = jnp.max(s[n], axis=0, keepdims=True)
            if ci == 0:
                m_new = cm
            else:
                m_old, acc = state[n]
                m_new = jnp.maximum(m_old, cm)
            p = jnp.exp2(s[n] - m_new).astype(BF16)
            if ci + 1 < len(chunks):
                s[n] = scores(n, ci + 1)
            pv = jnp.dot(value_t(*chunks[ci]), p, preferred_element_type=F32)
            if ci > 0:
                pv = acc * jnp.exp2(m_old - m_new) + pv
            state[n] = (m_new, pv)
    return [acc[0:DA_V] / acc[DA_V:DA_V + 1] for _, acc in state]


DA_PAIRS_PER_STEP = 1


def _da_kernel(q_ref, k_ref, vt_ref, dl_ref, g_ref, li_ref, o_ref, *, nt, nctx, tq):
    qi = pl.program_id(2)
    row = lax.broadcasted_iota(jnp.int32, (LANES, tq), 0)
    dl = dl_ref[...]
    lam_init = li_ref[...]
    lam = (jnp.exp(jnp.sum(dl[0:1] * dl[1:2], axis=-1, keepdims=True))
           - jnp.exp(jnp.sum(dl[2:3] * dl[3:4], axis=-1, keepdims=True)) + lam_init)
    zero = jnp.zeros((LANES, tq), BF16)

    def attend(nk):
        chains = []
        for j in range(DA_PAIRS_PER_STEP):
            q_t = q_ref[0, LANES * j:LANES * (j + 1), :]
            key = lambda start, size, j=j: k_ref[0, start:start + size, LANES * j:LANES * (j + 1)]
            for hh in range(2):
                value_t = lambda start, size, h=2 * j + hh: vt_ref[0, h, :, start:start + size]
                for mi in range(2):
                    lo = (2 * hh + mi) * DA_QK
                    chains.append((jnp.where((row >= lo) & (row < lo + DA_QK), q_t, zero), key, value_t))
        o = _attend_t(chains, _key_chunks(nk, nctx, DA_KEY_CHUNK))
        for j in range(DA_PAIRS_PER_STEP):
            halves = []
            for hh in range(2):
                d = o[4 * j + 2 * hh] - lam * o[4 * j + 2 * hh + 1]
                ms = jnp.mean(d * d, axis=0, keepdims=True)
                halves.append(d * lax.rsqrt(ms + RMS_EPS))
            out = jnp.concatenate(halves, axis=0).T
            o_ref[0, :, LANES * j:LANES * (j + 1)] = (out * g_ref[...] * (1.0 - lam_init)).astype(BF16)

    n_ctx_blocks = nctx // tq

    @pl.when(qi < n_ctx_blocks)
    def _():
        attend(nctx)

    @pl.when(qi >= n_ctx_blocks)
    def _():
        attend(nt)


def _da_attn(dq_t, dk, dv_t, dlam, gpair, lam_init, nctx):
    b, nt, _ = dk.shape
    tq = TOKEN_TILE
    pp = DA_PAIRS_PER_STEP
    return pl.pallas_call(
        functools.partial(_da_kernel, nt=nt, nctx=nctx, tq=tq),
        grid=(b, PAIRS // pp, nt // tq),
        in_specs=[
            pl.BlockSpec((1, pp * LANES, tq), lambda i, j, t: (i, j, t)),
            pl.BlockSpec((1, nt, pp * LANES), lambda i, j, t: (i, 0, j)),
            pl.BlockSpec((1, 2 * pp, VT_ROWS, nt), lambda i, j, t: (i, j, 0, 0)),
            _const_spec(dlam.shape), _const_spec(gpair.shape), _const_spec(lam_init.shape),
        ],
        out_specs=pl.BlockSpec((1, tq, pp * LANES), lambda i, j, t: (i, t, j)),
        out_shape=jax.ShapeDtypeStruct((b, nt, DA_WIDTH), BF16),
        compiler_params=_params(("parallel", "parallel", "arbitrary")),
        name="diff_attn",
    )(dq_t, dk, dv_t, dlam, gpair, lam_init)


MLA_HEADS_PER_STEP = 6


def _mla_kernel(q_ref, k_ref, vt_ref, o_ref, *, nt, nctx, tq):
    qi = pl.program_id(2)

    def attend(nk):
        chains = []
        for hh in range(MLA_HEADS_PER_STEP):
            key = lambda start, size, hh=hh: k_ref[0, hh, start:start + size, :]
            value_t = lambda start, size, hh=hh: vt_ref[0, hh, :, start:start + size]
            chains.append((q_ref[0, hh], key, value_t))
        o = _attend_t(chains, _key_chunks(nk, nctx, MLA_KEY_CHUNK))
        for j in range(MLA_HEADS_PER_STEP // 2):
            o_ref[0, :, LANES * j:LANES * (j + 1)] = jnp.concatenate(o[2 * j:2 * j + 2], axis=0).T.astype(BF16)

    n_ctx_blocks = nctx // tq

    @pl.when(qi < n_ctx_blocks)
    def _():
        attend(nctx)

    @pl.when(qi >= n_ctx_blocks)
    def _():
        attend(nt)


def _mla_attn(mq_t, mk, mv_t, nctx):
    b, nh, nt, _ = mk.shape
    tq = TOKEN_TILE
    hp = MLA_HEADS_PER_STEP
    return pl.pallas_call(
        functools.partial(_mla_kernel, nt=nt, nctx=nctx, tq=tq),
        grid=(b, nh // hp, nt // tq),
        in_specs=[
            pl.BlockSpec((1, hp, LANES, tq), lambda i, j, t: (i, j, 0, t)),
            pl.BlockSpec((1, hp, nt, LANES), lambda i, j, t: (i, j, 0, 0)),
            pl.BlockSpec((1, hp, VT_ROWS, nt), lambda i, j, t: (i, j, 0, 0)),
        ],
        out_specs=pl.BlockSpec((1, tq, hp * MLA_V), lambda i, j, t: (i, t, j)),
        out_shape=jax.ShapeDtypeStruct((b, nt, nh * MLA_V), BF16),
        compiler_params=_params(("parallel", "parallel", "arbitrary")),
        name="mla_attn",
    )(mq_t, mk, mv_t)


def _layer_norm(z, g, b):
    mu = jnp.mean(z, axis=-1, keepdims=True)
    zc = z - mu
    var = jnp.mean(zc * zc, axis=-1, keepdims=True)
    return (zc * lax.rsqrt(var + LN_EPS)) * g + b


def _router_gates(logits, rb):
    scores = jax.nn.sigmoid(logits)
    sel = scores + rb
    lane = lax.broadcasted_iota(jnp.int32, logits.shape, 1)
    r = lane & (EXPERTS_PER_GROUP - 1)
    grp = (lane >> 2) & (N_GROUPS - 1)

    def in_group(x, k):
        return jnp.where(r >= k, pltpu.roll(x, k, 1), pltpu.roll(x, LANES - EXPERTS_PER_GROUP + k, 1))

    others = [in_group(sel, k) for k in (1, 2, 3)]
    pair_max = sel + jnp.maximum(jnp.maximum(others[0], others[1]), others[2])
    grp_score = jnp.maximum(jnp.maximum(pair_max, in_group(pair_max, 1)),
                            jnp.maximum(in_group(pair_max, 2), in_group(pair_max, 3)))
    in_best = None
    for k in (1, 2, 3):
        other = pltpu.roll(grp_score, EXPERTS_PER_GROUP * k, 1)
        wins = (grp_score > other) | ((grp_score == other) & (grp < k))
        in_best = wins if in_best is None else (in_best & wins)
    beaten = jnp.zeros(logits.shape, F32)
    for k, o in zip((1, 2, 3), others):
        beats = (o > sel) | ((o == sel) & (r >= k))
        beaten = beaten + jnp.where(beats, 1.0, 0.0)
    chosen = in_best & (beaten < 2.0)
    sc = jnp.where(chosen, scores, 0.0)
    tot = sc + in_group(sc, 1) + in_group(sc, 2) + in_group(sc, 3)
    return jnp.where(chosen, sc / tot, 0.0)


def _post_kernel(x_ref, mod_ref, lru_ref, da_ref, mla_ref, wo_ref, g1_ref, b1_ref,
                 rw_ref, rb_ref, w1_ref, w3_ref, w2_ref, g_ref, b_ref, o_ref, *, alpha):
    mod = mod_ref[0, 0]
    a = jnp.concatenate([lru_ref[0], da_ref[0], mla_ref[0]], axis=-1)
    o = jnp.dot(a, wo_ref[...], preferred_element_type=F32)
    x1 = _layer_norm(alpha * x_ref[0] + mod[2:3] * o, g1_ref[...], b1_ref[...])
    v = (x1 * (1.0 + mod[4:5]) + mod[3:4]).astype(BF16)
    gates = _router_gates(jnp.dot(v, rw_ref[...], preferred_element_type=F32), rb_ref[...])
    per = EXPERTS_PER_GROUP * D_EXPERT
    f = None
    for c in range(N_GROUPS):
        h1 = jnp.dot(v, w1_ref[:, c * per:(c + 1) * per], preferred_element_type=F32)
        h3 = jnp.dot(v, w3_ref[:, c * per:(c + 1) * per], preferred_element_type=F32)
        hh = (h1 * jax.nn.sigmoid(h1)) * h3
        parts = []
        for j in range(EXPERTS_PER_GROUP):
            e = c * EXPERTS_PER_GROUP + j
            parts.append((hh[:, j * D_EXPERT:(j + 1) * D_EXPERT] * gates[:, e:e + 1]).astype(BF16))
        y = jnp.dot(jnp.concatenate(parts, axis=-1), w2_ref[c * per:(c + 1) * per, :], preferred_element_type=F32)
        f = y if f is None else f + y
    o_ref[0] = _layer_norm(alpha * x1 + mod[5:6] * f, g_ref[...], b_ref[...])


def _post(xa, modt, lru_o, da_o, mla_o, wo, g1, b1, rw, rb, w1c, w3c, w2c, g2, b2, nctx, alpha, latent_only):
    b, nt, d = xa.shape
    tm = TOKEN_TILE
    nc = nctx // tm
    skip = nc if latent_only else 0
    tok = lambda w: pl.BlockSpec((1, tm, w), lambda i, t: (i, t + skip, 0))
    consts = (wo, g1, b1, rw, rb, w1c, w3c, w2c, g2, b2)
    return pl.pallas_call(
        functools.partial(_post_kernel, alpha=alpha),
        grid=(b, nt // tm - skip),
        in_specs=[
            tok(d),
            pl.BlockSpec((1, 1, 8, d), lambda i, t: (i, jnp.where(t + skip >= nc, 1, 0), 0, 0)),
            tok(LRU_WIDTH), tok(da_o.shape[-1]), tok(mla_o.shape[-1]),
        ] + [_const_spec(c.shape) for c in consts],
        out_specs=pl.BlockSpec((1, tm, d), lambda i, t: (i, t, 0)),
        out_shape=jax.ShapeDtypeStruct((b, nt - skip * tm, d), F32),
        compiler_params=_params(("parallel", "parallel")),
        name="post",
    )(xa, modt, lru_o, da_o, mla_o, *consts)


def _rotary_tables(n, nctx):
    rows = n // GRID_W
    row = jnp.repeat(jnp.arange(rows), GRID_W).astype(F32)
    col = jnp.tile(jnp.arange(GRID_W), rows).astype(F32)
    n_freq = DA_QK // 4
    inv = ROPE_THETA ** (-jnp.arange(n_freq, dtype=F32) / n_freq)
    ang = jnp.concatenate([row[:, None] * inv, col[:, None] * inv], axis=-1)
    ang = jnp.concatenate([jnp.zeros((nctx, DA_QK // 2), F32), ang], axis=0)
    c, s = jnp.cos(ang), jnp.sin(ang)
    reps = LANES // DA_QK
    return jnp.tile(jnp.concatenate([c, c], axis=-1), (1, reps)), jnp.tile(jnp.concatenate([-s, s], axis=-1), (1, reps))


def _pack_in_weight(w_in):
    d = w_in.shape[0]
    wkr = w_in[:, C_KR:C_KR + MLA_ROPE]
    krp = jnp.concatenate([jnp.zeros((d, MLA_NOPE), F32), wkr, jnp.zeros((d, LANES - MLA_NOPE - MLA_ROPE), F32)], axis=-1)
    return jnp.concatenate([w_in[:, :C_KR], krp], axis=-1).astype(BF16)


def _pack_uq(w_uq):
    r = w_uq.shape[0]
    w = w_uq.reshape(r, MLA_HEADS, MLA_NOPE + MLA_ROPE)
    w = jnp.concatenate([w, jnp.zeros((r, MLA_HEADS, LANES - MLA_NOPE - MLA_ROPE), F32)], axis=-1)
    return w.reshape(r, MLA_HEADS * LANES).astype(BF16)


def _pack_ukv(w_ukv):
    r = w_ukv.shape[0]
    w = w_ukv.reshape(r, MLA_HEADS, MLA_NOPE + MLA_V)
    z = jnp.zeros((r, MLA_HEADS, LANES - MLA_NOPE), F32)
    wk = jnp.concatenate([w[..., :MLA_NOPE], z], axis=-1).reshape(r, MLA_HEADS * LANES)
    wv = w[..., MLA_NOPE:].reshape(r, MLA_HEADS * MLA_V)
    return wk.astype(BF16), wv.astype(BF16)


def _block_diag(w):
    nd, nb, bs, _ = w.shape
    eye = jnp.eye(nb, dtype=w.dtype)
    return jnp.einsum('dhij,hg->dhigj', w, eye).reshape(nd, nb * bs, nb * bs)


def kernel(x, c, ctx, c_ctx, w_mod, b_mod, w_in, w_out, conv_w, conv_b, lru_wa, lru_ba, lru_wi, lru_bi, lru_lambda, diff_lambda, diff_norm, mla_q_norm, mla_kv_norm, mla_w_uq, mla_w_ukv, ln1_g, ln1_b, ln2_g, ln2_b, router_w, router_b, exp_w1, exp_w3, exp_w2):
    bsz, n, d = x.shape
    nctx = ctx.shape[1]
    depth = w_mod.shape[0]
    alpha = (2 * depth) ** 0.25
    assert nctx % TOKEN_TILE == 0 and n % TOKEN_TILE == 0 and n % GRID_W == 0

    rows = -(-(bsz + 1) // SUBLANES) * SUBLANES
    cc = jnp.concatenate([c, c_ctx[None, :], jnp.zeros((rows - bsz - 1, d), F32)], axis=0)
    mod = _modulation(cc, w_mod, b_mod).reshape(depth, rows, 6, d)
    mod = jnp.pad(mod, ((0, 0), (0, 0), (0, 2), (0, 0)))
    mod_ctx = jnp.broadcast_to(mod[:, bsz][:, None], (depth, bsz, 8, d))
    modt = jnp.stack([mod_ctx, mod[:, :bsz]], axis=2)

    cosf, sinf = _rotary_tables(n, nctx)
    rw = jnp.tile(router_w, (1, LANES // N_EXPERTS)).astype(BF16)
    rb = jnp.tile(router_b, LANES // N_EXPERTS)[None, :].astype(F32)
    gpair = jnp.tile(diff_norm, (1, LANES // DA_V))

    xa = jnp.concatenate([ctx, x], axis=1)
    for l in range(depth):
        lam_init = jnp.full((1, 1), 0.8 - 0.6 * math.exp(-0.3 * l), F32)
        wuk, wuv = _pack_ukv(mla_w_ukv[l])
        lx, lg, dq_t, dk, dv_t, mq_t, mk, mv_t = _in_proj(
            xa, modt[l], _pack_in_weight(w_in[l]), _pack_uq(mla_w_uq[l]), wuk, wuv,
            mla_q_norm[l][None, :], mla_kv_norm[l][None, :], cosf, sinf, nctx)
        lru_o = _lru(lx, lg, conv_w[l], conv_b[l][None, :], _block_diag(lru_wa[l]).astype(BF16),
                     _block_diag(lru_wi[l]).astype(BF16), lru_ba[l], lru_bi[l], lru_lambda[l], nctx)
        da_o = _da_attn(dq_t, dk, dv_t, diff_lambda[l], gpair[l][None, :], lam_init, nctx)
        mla_o = _mla_attn(mq_t, mk, mv_t, nctx)
        w1c = exp_w1[l].transpose(1, 0, 2).reshape(d, N_EXPERTS * D_EXPERT).astype(BF16)
        w3c = exp_w3[l].transpose(1, 0, 2).reshape(d, N_EXPERTS * D_EXPERT).astype(BF16)
        w2c = exp_w2[l].reshape(N_EXPERTS * D_EXPERT, d).astype(BF16)
        xa = _post(xa, modt[l], lru_o, da_o, mla_o, w_out[l].astype(BF16), ln1_g[l][None, :], ln1_b[l][None, :],
                   rw, rb, w1c, w3c, w2c, ln2_g[l][None, :], ln2_b[l][None, :], nctx, alpha,
                   latent_only=(l == depth - 1))
    return xa
```

```python
import functools
import math

import jax
import jax.numpy as jnp
from jax import lax
from jax.experimental import pallas as pl
from jax.experimental.pallas import tpu as pltpu

F32 = jnp.float32
BF16 = jnp.bfloat16

GRID_W = 64
LRU_WIDTH = 256
LRU_BLOCKS = 4
CONV_W = 4
LRU_C = 8.0
DA_HEADS = 6
DA_QK = 32
DA_V = 2 * DA_QK
MLA_HEADS = 6
MLA_NOPE = 64
MLA_ROPE = 32
MLA_V = 64
Q_RANK = 256
KV_RANK = 128
MLA_SCALE = (MLA_NOPE + MLA_ROPE) ** -0.5
N_EXPERTS = 16
N_GROUPS = 4
EXPERTS_PER_GROUP = N_EXPERTS // N_GROUPS
D_EXPERT = 256
ROPE_THETA = 10000.0
LN_EPS = 1e-5
RMS_EPS = 1e-6

LANES = 128
SUBLANES = 8
TOKEN_TILE = 256
VMEM_LIMIT = 56 * 1024 * 1024

LOG2E = math.log2(math.e)
DA_QSCALE = DA_QK ** -0.5 * LOG2E
MLA_QSCALE = MLA_SCALE * LOG2E

DA_WIDTH = DA_HEADS * DA_V
C_LRU = 0
C_DAQ = 2 * LRU_WIDTH
C_DAK = C_DAQ + DA_WIDTH
C_DAV = C_DAK + DA_WIDTH
C_CQ = C_DAV + DA_WIDTH
C_CKV = C_CQ + Q_RANK
C_KR = C_CKV + KV_RANK
C_END = C_KR + LANES
PAIRS = DA_HEADS // 2
VT_ROWS = DA_V + 16


def _params(sem):
    return pltpu.CompilerParams(dimension_semantics=sem, vmem_limit_bytes=VMEM_LIMIT)


def _const_spec(shape):
    nd = len(shape)
    return pl.BlockSpec(shape, lambda *_: (0,) * nd, pipeline_mode=pl.Buffered(1))


def _mod_kernel(c_ref, w_ref, b_ref, o_ref):
    c = c_ref[...]
    s = c * jax.nn.sigmoid(c)
    o_ref[0] = jnp.dot(s.astype(BF16), w_ref[0].astype(BF16), preferred_element_type=F32) + b_ref[0]


def _modulation(cc, w_mod, b_mod):
    depth, d, d6 = w_mod.shape
    r = cc.shape[0]
    tn = min(d6, 1536)
    return pl.pallas_call(
        _mod_kernel,
        grid=(depth, d6 // tn),
        in_specs=[
            pl.BlockSpec((r, d), lambda l, j: (0, 0)),
            pl.BlockSpec((1, d, tn), lambda l, j: (l, 0, j)),
            pl.BlockSpec((1, 1, tn), lambda l, j: (l, 0, j)),
        ],
        out_specs=pl.BlockSpec((1, r, tn), lambda l, j: (l, 0, j)),
        out_shape=jax.ShapeDtypeStruct((depth, r, d6), F32),
        compiler_params=_params(("parallel", "parallel")),
        name="modulation",
    )(cc, w_mod, b_mod.reshape(depth, 1, d6))


def _rotate(t, cosf, sinf, first_half):
    partner = jnp.where(first_half, pltpu.roll(t, LANES - DA_QK // 2, 1), pltpu.roll(t, DA_QK // 2, 1))
    return t * cosf + partner * sinf


def _store_values_t(vt_ref, v):
    rows = v.shape[0]
    ones = jnp.ones((VT_ROWS - DA_V, rows), BF16)
    for j in range(PAIRS):
        t = v[:, LANES * j:LANES * (j + 1)].T.astype(BF16)
        for k in range(2):
            vt_ref[0, 2 * j + k, 0:DA_V, :] = t[DA_V * k:DA_V * (k + 1)]
            vt_ref[0, 2 * j + k, DA_V:VT_ROWS, :] = ones


def _in_kernel(x_ref, mod_ref, w1_ref, wuq_ref, wuk_ref, wuv_ref, qn_ref, kvn_ref, cos_ref, sin_ref,
               lx_ref, lg_ref, dq_ref, dk_ref, dv_ref, mq_ref, mk_ref, mv_ref):
    x = x_ref[0]
    mod = mod_ref[0, 0]
    u = x * (1.0 + mod[1:2]) + mod[0:1]
    y = jnp.dot(u.astype(BF16), w1_ref[...], preferred_element_type=F32)
    lx_ref[0] = y[:, C_LRU:C_LRU + LRU_WIDTH]
    lg_ref[0] = y[:, C_LRU + LRU_WIDTH:C_DAQ]

    cosf = cos_ref[...]
    sinf = sin_ref[...]
    lane = lax.broadcasted_iota(jnp.int32, cosf.shape, 1)
    first_half = (lane & (DA_QK // 2)) == 0
    rot = functools.partial(_rotate, cosf=cosf, sinf=sinf, first_half=first_half)

    for j in range(PAIRS):
        t = y[:, C_DAQ + LANES * j:C_DAQ + LANES * (j + 1)]
        dq_ref[0, LANES * j:LANES * (j + 1), :] = (rot(t) * DA_QSCALE).T.astype(BF16)
        t = y[:, C_DAK + LANES * j:C_DAK + LANES * (j + 1)]
        dk_ref[0, :, LANES * j:LANES * (j + 1)] = rot(t).astype(BF16)
    _store_values_t(dv_ref, y[:, C_DAV:C_CQ])

    cq = y[:, C_CQ:C_CKV]
    ckv = y[:, C_CKV:C_KR]
    krp = y[:, C_KR:C_END]
    is_rope = (lane >= MLA_NOPE) & (lane < MLA_NOPE + MLA_ROPE)
    kr = jnp.where(is_rope, rot(krp), krp)
    qn = (cq * lax.rsqrt(jnp.mean(cq * cq, axis=-1, keepdims=True) + RMS_EPS)) * qn_ref[...]
    q = jnp.dot(qn.astype(BF16), wuq_ref[...], preferred_element_type=F32)
    kvn = ((ckv * lax.rsqrt(jnp.mean(ckv * ckv, axis=-1, keepdims=True) + RMS_EPS)) * kvn_ref[...]).astype(BF16)
    kn = jnp.dot(kvn, wuk_ref[...], preferred_element_type=F32)
    for h in range(MLA_HEADS):
        t = q[:, LANES * h:LANES * (h + 1)]
        mq_ref[0, h] = (jnp.where(is_rope, rot(t), t) * MLA_QSCALE).T.astype(BF16)
        mk_ref[0, h] = (kn[:, LANES * h:LANES * (h + 1)] + kr).astype(BF16)
    _store_values_t(mv_ref, jnp.dot(kvn, wuv_ref[...], preferred_element_type=F32))


def _in_proj(xa, modt, w1, wuq, wuk, wuv, qnorm, kvnorm, cosf, sinf, nctx):
    b, nt, d = xa.shape
    tm = TOKEN_TILE
    nc = nctx // tm
    tok = lambda w: pl.BlockSpec((1, tm, w), lambda i, t: (i, t, 0))
    head_t = lambda r: pl.BlockSpec((1, DA_HEADS, r, tm), lambda i, t: (i, 0, 0, t))
    return pl.pallas_call(
        _in_kernel,
        grid=(b, nt // tm),
        in_specs=[
            tok(d),
            pl.BlockSpec((1, 1, 8, d), lambda i, t: (i, jnp.where(t >= nc, 1, 0), 0, 0)),
            _const_spec(w1.shape), _const_spec(wuq.shape), _const_spec(wuk.shape), _const_spec(wuv.shape),
            _const_spec(qnorm.shape), _const_spec(kvnorm.shape),
            pl.BlockSpec((tm, LANES), lambda i, t: (t, 0)),
            pl.BlockSpec((tm, LANES), lambda i, t: (t, 0)),
        ],
        out_specs=[
            tok(LRU_WIDTH), tok(LRU_WIDTH),
            pl.BlockSpec((1, DA_WIDTH, tm), lambda i, t: (i, 0, t)), tok(DA_WIDTH), head_t(VT_ROWS),
            head_t(LANES), pl.BlockSpec((1, MLA_HEADS, tm, LANES), lambda i, t: (i, 0, t, 0)), head_t(VT_ROWS),
        ],
        out_shape=[
            jax.ShapeDtypeStruct((b, nt, LRU_WIDTH), F32),
            jax.ShapeDtypeStruct((b, nt, LRU_WIDTH), F32),
            jax.ShapeDtypeStruct((b, DA_WIDTH, nt), BF16),
            jax.ShapeDtypeStruct((b, nt, DA_WIDTH), BF16),
            jax.ShapeDtypeStruct((b, DA_HEADS, VT_ROWS, nt), BF16),
            jax.ShapeDtypeStruct((b, MLA_HEADS, LANES, nt), BF16),
            jax.ShapeDtypeStruct((b, MLA_HEADS, nt, LANES), BF16),
            jax.ShapeDtypeStruct((b, MLA_HEADS, VT_ROWS, nt), BF16),
        ],
        compiler_params=_params(("parallel", "parallel")),
        name="in_proj",
    )(xa, modt, w1, wuq, wuk, wuv, qnorm, kvnorm, cosf, sinf)


def _gelu_tanh(x):
    return 0.5 * x * (1.0 + jnp.tanh(math.sqrt(2.0 / math.pi) * (x + 0.044715 * (x * x * x))))


def _lru_kernel(x_ref, g_ref, cw_ref, cb_ref, wa_ref, wi_ref, ba_ref, bi_ref, lam_ref, o_ref,
                y_s, a_s, s_s, h_s, *, nt, nctx, chunk):
    w = LRU_WIDTH
    tiles = chunk // SUBLANES
    n_chunks = nt // chunk
    sub = lax.broadcasted_iota(jnp.int32, (tiles, SUBLANES, w), 1)
    tile_i = lax.broadcasted_iota(jnp.int32, (tiles, SUBLANES, w), 0)

    def conv_chunk(c, carry):
        r0 = pl.multiple_of(c * chunk, chunk)
        lo = pl.multiple_of(jnp.maximum(r0 - SUBLANES, 0), SUBLANES)
        hi = pl.multiple_of(jnp.minimum(r0 + chunk, nt - SUBLANES), SUBLANES)
        x3 = jnp.concatenate([x_ref[0, pl.ds(lo, SUBLANES), :], x_ref[0, pl.ds(r0, chunk), :],
                              x_ref[0, pl.ds(hi, SUBLANES), :]], axis=0).reshape(tiles + 2, SUBLANES, w)
        sh1 = pltpu.roll(x3, 1, 1)
        sh2 = pltpu.roll(x3, 2, 1)
        sh7 = pltpu.roll(x3, SUBLANES - 1, 1)
        pos = r0 + tile_i * SUBLANES + sub
        in_ctx = pos < nctx
        seg_pos = jnp.where(in_ctx, pos, pos - nctx)
        seg_last = jnp.where(in_ctx, nctx - 1, nt - nctx - 1)
        zero = jnp.zeros((tiles, SUBLANES, w), F32)
        xm2 = jnp.where(seg_pos >= 2, jnp.where(sub >= 2, sh2[1:-1], sh2[0:-2]), zero)
        xm1 = jnp.where(seg_pos >= 1, jnp.where(sub >= 1, sh1[1:-1], sh1[0:-2]), zero)
        xp1 = jnp.where(seg_pos < seg_last, jnp.where(sub < SUBLANES - 1, sh7[1:-1], sh7[2:]), zero)
        y = cb_ref[...] + xm2 * cw_ref[0:1] + xm1 * cw_ref[1:2] + x3[1:-1] * cw_ref[2:3] + xp1 * cw_ref[3:4]
        y_s[pl.ds(r0, chunk), :] = y.reshape(chunk, w)
        return carry

    lax.fori_loop(0, n_chunks, conv_chunk, 0)

    nctx_t = nctx // SUBLANES
    nt_t = nt // SUBLANES

    for d in range(2):
        nlam = -lam_ref[d:d + 1]
        softplus = jnp.maximum(nlam, 0.0) + jnp.log1p(jnp.exp(-jnp.abs(nlam)))
        c8 = -LRU_C * softplus

        def gate_chunk(c, carry, d=d, c8=c8):
            r0 = pl.multiple_of(c * chunk, chunk)
            y = y_s[pl.ds(r0, chunk), :]
            yb = y.astype(BF16)
            r = jax.nn.sigmoid(jnp.dot(yb, wa_ref[d], preferred_element_type=F32) + ba_ref[d:d + 1])
            i = jax.nn.sigmoid(jnp.dot(yb, wi_ref[d], preferred_element_type=F32) + bi_ref[d:d + 1])
            log_a = c8 * r
            a = jnp.exp(log_a)
            th = jnp.tanh(log_a)
            u = jnp.sqrt(-2.0 * th / (1.0 - th)) * (i * y)
            a3 = a.reshape(tiles, SUBLANES, w)
            u3 = u.reshape(tiles, SUBLANES, w)
            for sft in (1, 2, 4):
                if d == 0:
                    ok = sub >= sft
                    ash = pltpu.roll(a3, sft, 1)
                    ush = pltpu.roll(u3, sft, 1)
                else:
                    ok = sub < SUBLANES - sft
                    ash = pltpu.roll(a3, SUBLANES - sft, 1)
                    ush = pltpu.roll(u3, SUBLANES - sft, 1)
                u3 = jnp.where(ok, a3 * ush + u3, u3)
                a3 = jnp.where(ok, a3 * ash, a3)
            a_s[pl.ds(r0, chunk), :] = a3.reshape(chunk, w)
            s_s[pl.ds(r0, chunk), :] = u3.reshape(chunk, w)
            return carry

        lax.fori_loop(0, n_chunks, gate_chunk, 0)

        def carry_tile(j, hprev, d=d):
            if d == 0:
                t = j
            else:
                t = jnp.where(j < nctx_t, nctx_t - 1 - j, nt_t - 1 - (j - nctx_t))
            r0 = pl.multiple_of(t * SUBLANES, SUBLANES)
            h = a_s[pl.ds(r0, SUBLANES), :] * hprev + s_s[pl.ds(r0, SUBLANES), :]
            if d == 0:
                h_s[pl.ds(r0, SUBLANES), :] = h
                return h[SUBLANES - 1:SUBLANES]
            h_s[pl.ds(r0, SUBLANES), :] = h_s[pl.ds(r0, SUBLANES), :] + h
            return h[0:1]

        lax.fori_loop(0, nt_t, carry_tile, jnp.zeros((1, w), F32), unroll=4)

    def out_chunk(c, carry):
        r0 = pl.multiple_of(c * chunk, chunk)
        o_ref[0, pl.ds(r0, chunk), :] = (h_s[pl.ds(r0, chunk), :] * _gelu_tanh(g_ref[0, pl.ds(r0, chunk), :])).astype(BF16)
        return carry

    lax.fori_loop(0, n_chunks, out_chunk, 0)


def _lru(lx, lg, conv_w, conv_b, wa, wi, ba, bi, lam, nctx):
    b, nt, w = lx.shape
    chunk = TOKEN_TILE
    seq = pl.BlockSpec((1, nt, w), lambda i: (i, 0, 0))
    return pl.pallas_call(
        functools.partial(_lru_kernel, nt=nt, nctx=nctx, chunk=chunk),
        grid=(b,),
        in_specs=[seq, seq, _const_spec(conv_w.shape), _const_spec(conv_b.shape), _const_spec(wa.shape),
                  _const_spec(wi.shape), _const_spec(ba.shape), _const_spec(bi.shape), _const_spec(lam.shape)],
        out_specs=seq,
        out_shape=jax.ShapeDtypeStruct((b, nt, w), BF16),
        scratch_shapes=[pltpu.VMEM((nt, w), F32)] * 4,
        compiler_params=_params(("parallel",)),
        name="rglru",
    )(lx, lg, conv_w, conv_b, wa, wi, ba, bi, lam)


DA_KEY_CHUNK = 256
MLA_KEY_CHUNK = 256


def _key_chunks(nk, nctx, size):
    chunks = [(0, nctx)]
    chunks += [(s, min(size, nk - s)) for s in range(nctx, nk, size)]
    return chunks


def _attend_t(chains, chunks):
    def scores(n, ci):
        q_t, key, _ = chains[n]
        return jnp.dot(key(*chunks[ci]), q_t, preferred_element_type=F32)

    s = [scores(n, 0) for n in range(len(chains))]
    state = [None] * len(chains)
    for ci in range(len(chunks)):
        for n, (_, _, value_t) in enumerate(chains):
            cm = jnp.max(s[n], axis=0, keepdims=True)
            if ci == 0:
                m_new = cm
            else:
                m_old, acc = state[n]
                m_new = jnp.maximum(m_old, cm)
            p = jnp.exp2(s[n] - m_new).astype(BF16)
            if ci + 1 < len(chunks):
                s[n] = scores(n, ci + 1)
            pv = jnp.dot(value_t(*chunks[ci]), p, preferred_element_type=F32)
            if ci > 0:
                pv = acc * jnp.exp2(m_old - m_new) + pv
            state[n] = (m_new, pv)
    return [acc[0:DA_V] / acc[DA_V:DA_V + 1] for _, acc in state]


def _da_kernel(q_ref, k_ref, vt_ref, dl_ref, g_ref, li_ref, o_ref, *, nt, nctx, tq):
    qi = pl.program_id(1)
    half = pl.program_id(2)
    row = lax.broadcasted_iota(jnp.int32, (LANES, tq), 0)
    lane = lax.broadcasted_iota(jnp.int32, (tq, LANES), 1)
    dl = dl_ref[...]
    lam_init = li_ref[...]
    lam = (jnp.exp(jnp.sum(dl[0:1] * dl[1:2], axis=-1, keepdims=True))
           - jnp.exp(jnp.sum(dl[2:3] * dl[3:4], axis=-1, keepdims=True)) + lam_init)
    zero = jnp.zeros((LANES, tq), BF16)

    def attend(nk):
        chains = []
        for j in range(PAIRS):
            q_t = q_ref[0, LANES * j:LANES * (j + 1), :]
            key = lambda start, size, j=j: k_ref[0, start:start + size, LANES * j:LANES * (j + 1)]
            value_t = lambda start, size, j=j: vt_ref[0, 2 * j + half, :, start:start + size]
            for mi in range(2):
                lo = half * DA_V + mi * DA_QK
                chains.append((jnp.where((row >= lo) & (row < lo + DA_QK), q_t, zero), key, value_t))
        o = _attend_t(chains, _key_chunks(nk, nctx, DA_KEY_CHUNK))
        for j in range(PAIRS):
            d = o[2 * j] - lam * o[2 * j + 1]
            d = d * lax.rsqrt(jnp.mean(d * d, axis=0, keepdims=True) + RMS_EPS)
            both = jnp.concatenate([d, d], axis=0).T
            new = (both * g_ref[...] * (1.0 - lam_init)).astype(BF16)
            slab = (0, slice(None), slice(LANES * j, LANES * (j + 1)))

            @pl.when(half == 0)
            def _():
                o_ref[slab] = new

            @pl.when(half == 1)
            def _():
                o_ref[slab] = jnp.where(lane >= DA_V, new, o_ref[slab])

    n_ctx_blocks = nctx // tq

    @pl.when(qi < n_ctx_blocks)
    def _():
        attend(nctx)

    @pl.when(qi >= n_ctx_blocks)
    def _():
        attend(nt)


def _da_attn(dq_t, dk, dv_t, dlam, gpair, lam_init, nctx):
    b, nt, _ = dk.shape
    tq = TOKEN_TILE
    return pl.pallas_call(
        functools.partial(_da_kernel, nt=nt, nctx=nctx, tq=tq),
        grid=(b, nt // tq, 2),
        in_specs=[
            pl.BlockSpec((1, DA_WIDTH, tq), lambda i, t, h: (i, 0, t)),
            pl.BlockSpec((1, nt, DA_WIDTH), lambda i, t, h: (i, 0, 0)),
            pl.BlockSpec((1, DA_HEADS, VT_ROWS, nt), lambda i, t, h: (i, 0, 0, 0)),
            _const_spec(dlam.shape), _const_spec(gpair.shape), _const_spec(lam_init.shape),
        ],
        out_specs=pl.BlockSpec((1, tq, DA_WIDTH), lambda i, t, h: (i, t, 0)),
        out_shape=jax.ShapeDtypeStruct((b, nt, DA_WIDTH), BF16),
        compiler_params=_params(("parallel", "arbitrary", "arbitrary")),
        name="diff_attn",
    )(dq_t, dk, dv_t, dlam, gpair, lam_init)


MLA_HEADS_PER_STEP = 6


def _mla_kernel(q_ref, k_ref, vt_ref, o_ref, *, nt, nctx, tq):
    qi = pl.program_id(2)

    def attend(nk):
        chains = []
        for hh in range(MLA_HEADS_PER_STEP):
            key = lambda start, size, hh=hh: k_ref[0, hh, start:start + size, :]
            value_t = lambda start, size, hh=hh: vt_ref[0, hh, :, start:start + size]
            chains.append((q_ref[0, hh], key, value_t))
        o = _attend_t(chains, _key_chunks(nk, nctx, MLA_KEY_CHUNK))
        for j in range(MLA_HEADS_PER_STEP // 2):
            o_ref[0, :, LANES * j:LANES * (j + 1)] = jnp.concatenate(o[2 * j:2 * j + 2], axis=0).T.astype(BF16)

    n_ctx_blocks = nctx // tq

    @pl.when(qi < n_ctx_blocks)
    def _():
        attend(nctx)

    @pl.when(qi >= n_ctx_blocks)
    def _():
        attend(nt)


def _mla_attn(mq_t, mk, mv_t, nctx):
    b, nh, nt, _ = mk.shape
    tq = TOKEN_TILE
    hp = MLA_HEADS_PER_STEP
    return pl.pallas_call(
        functools.partial(_mla_kernel, nt=nt, nctx=nctx, tq=tq),
        grid=(b, nh // hp, nt // tq),
        in_specs=[
            pl.BlockSpec((1, hp, LANES, tq), lambda i, j, t: (i, j, 0, t)),
            pl.BlockSpec((1, hp, nt, LANES), lambda i, j, t: (i, j, 0, 0)),
            pl.BlockSpec((1, hp, VT_ROWS, nt), lambda i, j, t: (i, j, 0, 0)),
        ],
        out_specs=pl.BlockSpec((1, tq, hp * MLA_V), lambda i, j, t: (i, t, j)),
        out_shape=jax.ShapeDtypeStruct((b, nt, nh * MLA_V), BF16),
        compiler_params=_params(("parallel", "parallel", "arbitrary")),
        name="mla_attn",
    )(mq_t, mk, mv_t)


def _layer_norm(z, g, b):
    mu = jnp.mean(z, axis=-1, keepdims=True)
    zc = z - mu
    var = jnp.mean(zc * zc, axis=-1, keepdims=True)
    return (zc * lax.rsqrt(var + LN_EPS)) * g + b


def _router_gates(logits, rb):
    scores = jax.nn.sigmoid(logits)
    sel = scores + rb
    lane = lax.broadcasted_iota(jnp.int32, logits.shape, 1)
    r = lane & (EXPERTS_PER_GROUP - 1)
    grp = (lane >> 2) & (N_GROUPS - 1)

    def in_group(x, k):
        return jnp.where(r >= k, pltpu.roll(x, k, 1), pltpu.roll(x, LANES - EXPERTS_PER_GROUP + k, 1))

    others = [in_group(sel, k) for k in (1, 2, 3)]
    pair_max = sel + jnp.maximum(jnp.maximum(others[0], others[1]), others[2])
    grp_score = jnp.maximum(jnp.maximum(pair_max, in_group(pair_max, 1)),
                            jnp.maximum(in_group(pair_max, 2), in_group(pair_max, 3)))
    in_best = None
    for k in (1, 2, 3):
        other = pltpu.roll(grp_score, EXPERTS_PER_GROUP * k, 1)
        wins = (grp_score > other) | ((grp_score == other) & (grp < k))
        in_best = wins if in_best is None else (in_best & wins)
    beaten = jnp.zeros(logits.shape, F32)
    for k, o in zip((1, 2, 3), others):
        beats = (o > sel) | ((o == sel) & (r >= k))
        beaten = beaten + jnp.where(beats, 1.0, 0.0)
    chosen = in_best & (beaten < 2.0)
    sc = jnp.where(chosen, scores, 0.0)
    tot = sc + in_group(sc, 1) + in_group(sc, 2) + in_group(sc, 3)
    return jnp.where(chosen, sc / tot, 0.0)


def _post_kernel(x_ref, mod_ref, lru_ref, da_ref, mla_ref, wo_ref, g1_ref, b1_ref,
                 rw_ref, rb_ref, w1_ref, w3_ref, w2_ref, g_ref, b_ref, o_ref, *, alpha):
    mod = mod_ref[0, 0]
    a = jnp.concatenate([lru_ref[0], da_ref[0], mla_ref[0]], axis=-1)
    o = jnp.dot(a, wo_ref[...], preferred_element_type=F32)
    x1 = _layer_norm(alpha * x_ref[0] + mod[2:3] * o, g1_ref[...], b1_ref[...])
    v = (x1 * (1.0 + mod[4:5]) + mod[3:4]).astype(BF16)
    gates = _router_gates(jnp.dot(v, rw_ref[...], preferred_element_type=F32), rb_ref[...])
    per = EXPERTS_PER_GROUP * D_EXPERT
    f = None
    for c in range(N_GROUPS):
        h1 = jnp.dot(v, w1_ref[:, c * per:(c + 1) * per], preferred_element_type=F32)
        h3 = jnp.dot(v, w3_ref[:, c * per:(c + 1) * per], preferred_element_type=F32)
        hh = (h1 * jax.nn.sigmoid(h1)) * h3
        parts = []
        for j in range(EXPERTS_PER_GROUP):
            e = c * EXPERTS_PER_GROUP + j
            parts.append((hh[:, j * D_EXPERT:(j + 1) * D_EXPERT] * gates[:, e:e + 1]).astype(BF16))
        y = jnp.dot(jnp.concatenate(parts, axis=-1), w2_ref[c * per:(c + 1) * per, :], preferred_element_type=F32)
        f = y if f is None else f + y
    o_ref[0] = _layer_norm(alpha * x1 + mod[5:6] * f, g_ref[...], b_ref[...])


def _post(xa, modt, lru_o, da_o, mla_o, wo, g1, b1, rw, rb, w1c, w3c, w2c, g2, b2, nctx, alpha, latent_only):
    b, nt, d = xa.shape
    tm = TOKEN_TILE
    nc = nctx // tm
    skip = nc if latent_only else 0
    tok = lambda w: pl.BlockSpec((1, tm, w), lambda i, t: (i, t + skip, 0))
    consts = (wo, g1, b1, rw, rb, w1c, w3c, w2c, g2, b2)
    return pl.pallas_call(
        functools.partial(_post_kernel, alpha=alpha),
        grid=(b, nt // tm - skip),
        in_specs=[
            tok(d),
            pl.BlockSpec((1, 1, 8, d), lambda i, t: (i, jnp.where(t + skip >= nc, 1, 0), 0, 0)),
            tok(LRU_WIDTH), tok(da_o.shape[-1]), tok(mla_o.shape[-1]),
        ] + [_const_spec(c.shape) for c in consts],
        out_specs=pl.BlockSpec((1, tm, d), lambda i, t: (i, t, 0)),
        out_shape=jax.ShapeDtypeStruct((b, nt - skip * tm, d), F32),
        compiler_params=_params(("parallel", "parallel")),
        name="post",
    )(xa, modt, lru_o, da_o, mla_o, *consts)


def _rotary_tables(n, nctx):
    rows = n // GRID_W
    row = jnp.repeat(jnp.arange(rows), GRID_W).astype(F32)
    col = jnp.tile(jnp.arange(GRID_W), rows).astype(F32)
    n_freq = DA_QK // 4
    inv = ROPE_THETA ** (-jnp.arange(n_freq, dtype=F32) / n_freq)
    ang = jnp.concatenate([row[:, None] * inv, col[:, None] * inv], axis=-1)
    ang = jnp.concatenate([jnp.zeros((nctx, DA_QK // 2), F32), ang], axis=0)
    c, s = jnp.cos(ang), jnp.sin(ang)
    reps = LANES // DA_QK
    return jnp.tile(jnp.concatenate([c, c], axis=-1), (1, reps)), jnp.tile(jnp.concatenate([-s, s], axis=-1), (1, reps))


def _pack_in_weight(w_in):
    d = w_in.shape[0]
    wkr = w_in[:, C_KR:C_KR + MLA_ROPE]
    krp = jnp.concatenate([jnp.zeros((d, MLA_NOPE), F32), wkr, jnp.zeros((d, LANES - MLA_NOPE - MLA_ROPE), F32)], axis=-1)
    return jnp.concatenate([w_in[:, :C_KR], krp], axis=-1).astype(BF16)


def _pack_uq(w_uq):
    r = w_uq.shape[0]
    w = w_uq.reshape(r, MLA_HEADS, MLA_NOPE + MLA_ROPE)
    w = jnp.concatenate([w, jnp.zeros((r, MLA_HEADS, LANES - MLA_NOPE - MLA_ROPE), F32)], axis=-1)
    return w.reshape(r, MLA_HEADS * LANES).astype(BF16)


def _pack_ukv(w_ukv):
    r = w_ukv.shape[0]
    w = w_ukv.reshape(r, MLA_HEADS, MLA_NOPE + MLA_V)
    z = jnp.zeros((r, MLA_HEADS, LANES - MLA_NOPE), F32)
    wk = jnp.concatenate([w[..., :MLA_NOPE], z], axis=-1).reshape(r, MLA_HEADS * LANES)
    wv = w[..., MLA_NOPE:].reshape(r, MLA_HEADS * MLA_V)
    return wk.astype(BF16), wv.astype(BF16)


def _block_diag(w):
    nd, nb, bs, _ = w.shape
    eye = jnp.eye(nb, dtype=w.dtype)
    return jnp.einsum('dhij,hg->dhigj', w, eye).reshape(nd, nb * bs, nb * bs)


def kernel(x, c, ctx, c_ctx, w_mod, b_mod, w_in, w_out, conv_w, conv_b, lru_wa, lru_ba, lru_wi, lru_bi, lru_lambda, diff_lambda, diff_norm, mla_q_norm, mla_kv_norm, mla_w_uq, mla_w_ukv, ln1_g, ln1_b, ln2_g, ln2_b, router_w, router_b, exp_w1, exp_w3, exp_w2):
    bsz, n, d = x.shape
    nctx = ctx.shape[1]
    depth = w_mod.shape[0]
    alpha = (2 * depth) ** 0.25
    assert nctx % TOKEN_TILE == 0 and n % TOKEN_TILE == 0 and n % GRID_W == 0

    rows = -(-(bsz + 1) // SUBLANES) * SUBLANES
    cc = jnp.concatenate([c, c_ctx[None, :], jnp.zeros((rows - bsz - 1, d), F32)], axis=0)
    mod = _modulation(cc, w_mod, b_mod).reshape(depth, rows, 6, d)
    mod = jnp.pad(mod, ((0, 0), (0, 0), (0, 2), (0, 0)))
    mod_ctx = jnp.broadcast_to(mod[:, bsz][:, None], (depth, bsz, 8, d))
    modt = jnp.stack([mod_ctx, mod[:, :bsz]], axis=2)

    cosf, sinf = _rotary_tables(n, nctx)
    rw = jnp.tile(router_w, (1, LANES // N_EXPERTS)).astype(BF16)
    rb = jnp.tile(router_b, LANES // N_EXPERTS)[None, :].astype(F32)
    gpair = jnp.tile(diff_norm, (1, LANES // DA_V))

    xa = jnp.concatenate([ctx, x], axis=1)
    for l in range(depth):
        lam_init = jnp.full((1, 1), 0.8 - 0.6 * math.exp(-0.3 * l), F32)
        wuk, wuv = _pack_ukv(mla_w_ukv[l])
        lx, lg, dq_t, dk, dv_t, mq_t, mk, mv_t = _in_proj(
            xa, modt[l], _pack_in_weight(w_in[l]), _pack_uq(mla_w_uq[l]), wuk, wuv,
            mla_q_norm[l][None, :], mla_kv_norm[l][None, :], cosf, sinf, nctx)
        lru_o = _lru(lx, lg, conv_w[l], conv_b[l][None, :], _block_diag(lru_wa[l]).astype(BF16),
                     _block_diag(lru_wi[l]).astype(BF16), lru_ba[l], lru_bi[l], lru_lambda[l], nctx)
        da_o = _da_attn(dq_t, dk, dv_t, diff_lambda[l], gpair[l][None, :], lam_init, nctx)
        mla_o = _mla_attn(mq_t, mk, mv_t, nctx)
        w1c = exp_w1[l].transpose(1, 0, 2).reshape(d, N_EXPERTS * D_EXPERT).astype(BF16)
        w3c = exp_w3[l].transpose(1, 0, 2).reshape(d, N_EXPERTS * D_EXPERT).astype(BF16)
        w2c = exp_w2[l].reshape(N_EXPERTS * D_EXPERT, d).astype(BF16)
        xa = _post(xa, modt[l], lru_o, da_o, mla_o, w_out[l].astype(BF16), ln1_g[l][None, :], ln1_b[l][None, :],
                   rw, rb, w1c, w3c, w2c, ln2_g[l][None, :], ln2_b[l][None, :], nctx, alpha,
                   latent_only=(l == depth - 1))
    return xa
```

```python
import functools
import math

import jax
import jax.numpy as jnp
from jax import lax
from jax.experimental import pallas as pl
from jax.experimental.pallas import tpu as pltpu

F32 = jnp.float32
BF16 = jnp.bfloat16

GRID_W = 64
LRU_WIDTH = 256
LRU_BLOCKS = 4
CONV_W = 4
LRU_C = 8.0
DA_HEADS = 6
DA_QK = 32
DA_V = 2 * DA_QK
MLA_HEADS = 6
MLA_NOPE = 64
MLA_ROPE = 32
MLA_V = 64
Q_RANK = 256
KV_RANK = 128
MLA_SCALE = (MLA_NOPE + MLA_ROPE) ** -0.5
N_EXPERTS = 16
N_GROUPS = 4
EXPERTS_PER_GROUP = N_EXPERTS // N_GROUPS
D_EXPERT = 256
ROPE_THETA = 10000.0
LN_EPS = 1e-5
RMS_EPS = 1e-6

LANES = 128
SUBLANES = 8
TOKEN_TILE = 256
VMEM_LIMIT = 56 * 1024 * 1024

LOG2E = math.log2(math.e)
DA_QSCALE = DA_QK ** -0.5 * LOG2E
MLA_QSCALE = MLA_SCALE * LOG2E

DA_WIDTH = DA_HEADS * DA_V
C_LRU = 0
C_DAQ = 2 * LRU_WIDTH
C_DAK = C_DAQ + DA_WIDTH
C_DAV = C_DAK + DA_WIDTH
C_CQ = C_DAV + DA_WIDTH
C_CKV = C_CQ + Q_RANK
C_KR = C_CKV + KV_RANK
C_END = C_KR + LANES
PAIRS = DA_HEADS // 2
VT_ROWS = DA_V + 16


def _params(sem):
    return pltpu.CompilerParams(dimension_semantics=sem, vmem_limit_bytes=VMEM_LIMIT)


def _const_spec(shape):
    nd = len(shape)
    return pl.BlockSpec(shape, lambda *_: (0,) * nd, pipeline_mode=pl.Buffered(1))


def _mod_kernel(c_ref, w_ref, b_ref, o_ref):
    c = c_ref[...]
    s = c * jax.nn.sigmoid(c)
    o_ref[0] = jnp.dot(s.astype(BF16), w_ref[0].astype(BF16), preferred_element_type=F32) + b_ref[0]


def _modulation(cc, w_mod, b_mod):
    depth, d, d6 = w_mod.shape
    r = cc.shape[0]
    tn = min(d6, 1536)
    return pl.pallas_call(
        _mod_kernel,
        grid=(depth, d6 // tn),
        in_specs=[
            pl.BlockSpec((r, d), lambda l, j: (0, 0)),
            pl.BlockSpec((1, d, tn), lambda l, j: (l, 0, j)),
            pl.BlockSpec((1, 1, tn), lambda l, j: (l, 0, j)),
        ],
        out_specs=pl.BlockSpec((1, r, tn), lambda l, j: (l, 0, j)),
        out_shape=jax.ShapeDtypeStruct((depth, r, d6), F32),
        compiler_params=_params(("parallel", "parallel")),
        name="modulation",
    )(cc, w_mod, b_mod.reshape(depth, 1, d6))


def _rotate(t, cosf, sinf, first_half):
    partner = jnp.where(first_half, pltpu.roll(t, LANES - DA_QK // 2, 1), pltpu.roll(t, DA_QK // 2, 1))
    return t * cosf + partner * sinf


def _store_values_t(vt_ref, v):
    rows = v.shape[0]
    ones = jnp.ones((VT_ROWS - DA_V, rows), BF16)
    for j in range(PAIRS):
        t = v[:, LANES * j:LANES * (j + 1)].T.astype(BF16)
        for k in range(2):
            vt_ref[0, 2 * j + k, 0:DA_V, :] = t[DA_V * k:DA_V * (k + 1)]
            vt_ref[0, 2 * j + k, DA_V:VT_ROWS, :] = ones


def _in_kernel(x_ref, mod_ref, w1_ref, wuq_ref, wuk_ref, wuv_ref, qn_ref, kvn_ref, cos_ref, sin_ref,
               lx_ref, lg_ref, dq_ref, dk_ref, dv_ref, mq_ref, mk_ref, mv_ref):
    x = x_ref[0]
    mod = mod_ref[0, 0]
    u = x * (1.0 + mod[1:2]) + mod[0:1]
    y = jnp.dot(u.astype(BF16), w1_ref[...], preferred_element_type=F32)
    lx_ref[0] = y[:, C_LRU:C_LRU + LRU_WIDTH]
    lg_ref[0] = y[:, C_LRU + LRU_WIDTH:C_DAQ]

    cosf = cos_ref[...]
    sinf = sin_ref[...]
    lane = lax.broadcasted_iota(jnp.int32, cosf.shape, 1)
    first_half = (lane & (DA_QK // 2)) == 0
    rot = functools.partial(_rotate, cosf=cosf, sinf=sinf, first_half=first_half)

    for j in range(PAIRS):
        t = y[:, C_DAQ + LANES * j:C_DAQ + LANES * (j + 1)]
        dq_ref[0, LANES * j:LANES * (j + 1), :] = (rot(t) * DA_QSCALE).T.astype(BF16)
        t = y[:, C_DAK + LANES * j:C_DAK + LANES * (j + 1)]
        dk_ref[0, :, LANES * j:LANES * (j + 1)] = rot(t).astype(BF16)
    _store_values_t(dv_ref, y[:, C_DAV:C_CQ])

    cq = y[:, C_CQ:C_CKV]
    ckv = y[:, C_CKV:C_KR]
    krp = y[:, C_KR:C_END]
    is_rope = (lane >= MLA_NOPE) & (lane < MLA_NOPE + MLA_ROPE)
    kr = jnp.where(is_rope, rot(krp), krp)
    qn = (cq * lax.rsqrt(jnp.mean(cq * cq, axis=-1, keepdims=True) + RMS_EPS)) * qn_ref[...]
    q = jnp.dot(qn.astype(BF16), wuq_ref[...], preferred_element_type=F32)
    kvn = ((ckv * lax.rsqrt(jnp.mean(ckv * ckv, axis=-1, keepdims=True) + RMS_EPS)) * kvn_ref[...]).astype(BF16)
    kn = jnp.dot(kvn, wuk_ref[...], preferred_element_type=F32)
    for h in range(MLA_HEADS):
        t = q[:, LANES * h:LANES * (h + 1)]
        mq_ref[0, h] = (jnp.where(is_rope, rot(t), t) * MLA_QSCALE).T.astype(BF16)
        mk_ref[0, h] = (kn[:, LANES * h:LANES * (h + 1)] + kr).astype(BF16)
    _store_values_t(mv_ref, jnp.dot(kvn, wuv_ref[...], preferred_element_type=F32))


def _in_proj(xa, modt, w1, wuq, wuk, wuv, qnorm, kvnorm, cosf, sinf, nctx):
    b, nt, d = xa.shape
    tm = TOKEN_TILE
    nc = nctx // tm
    tok = lambda w: pl.BlockSpec((1, tm, w), lambda i, t: (i, t, 0))
    head_t = lambda r: pl.BlockSpec((1, DA_HEADS, r, tm), lambda i, t: (i, 0, 0, t))
    return pl.pallas_call(
        _in_kernel,
        grid=(b, nt // tm),
        in_specs=[
            tok(d),
            pl.BlockSpec((1, 1, 8, d), lambda i, t: (i, jnp.where(t >= nc, 1, 0), 0, 0)),
            _const_spec(w1.shape), _const_spec(wuq.shape), _const_spec(wuk.shape), _const_spec(wuv.shape),
            _const_spec(qnorm.shape), _const_spec(kvnorm.shape),
            pl.BlockSpec((tm, LANES), lambda i, t: (t, 0)),
            pl.BlockSpec((tm, LANES), lambda i, t: (t, 0)),
        ],
        out_specs=[
            tok(LRU_WIDTH), tok(LRU_WIDTH),
            pl.BlockSpec((1, DA_WIDTH, tm), lambda i, t: (i, 0, t)), tok(DA_WIDTH), head_t(VT_ROWS),
            head_t(LANES), pl.BlockSpec((1, MLA_HEADS, tm, LANES), lambda i, t: (i, 0, t, 0)), head_t(VT_ROWS),
        ],
        out_shape=[
            jax.ShapeDtypeStruct((b, nt, LRU_WIDTH), F32),
            jax.ShapeDtypeStruct((b, nt, LRU_WIDTH), F32),
            jax.ShapeDtypeStruct((b, DA_WIDTH, nt), BF16),
            jax.ShapeDtypeStruct((b, nt, DA_WIDTH), BF16),
            jax.ShapeDtypeStruct((b, DA_HEADS, VT_ROWS, nt), BF16),
            jax.ShapeDtypeStruct((b, MLA_HEADS, LANES, nt), BF16),
            jax.ShapeDtypeStruct((b, MLA_HEADS, nt, LANES), BF16),
            jax.ShapeDtypeStruct((b, MLA_HEADS, VT_ROWS, nt), BF16),
        ],
        compiler_params=_params(("parallel", "parallel")),
        name="in_proj",
    )(xa, modt, w1, wuq, wuk, wuv, qnorm, kvnorm, cosf, sinf)


def _gelu_tanh(x):
    return 0.5 * x * (1.0 + jnp.tanh(math.sqrt(2.0 / math.pi) * (x + 0.044715 * (x * x * x))))


def _lru_kernel(x_ref, g_ref, cw_ref, cb_ref, wa_ref, wi_ref, ba_ref, bi_ref, lam_ref, o_ref,
                y_s, a_s, s_s, h_s, *, nt, nctx, chunk):
    w = LRU_WIDTH
    tiles = chunk // SUBLANES
    n_chunks = nt // chunk
    sub = lax.broadcasted_iota(jnp.int32, (tiles, SUBLANES, w), 1)
    tile_i = lax.broadcasted_iota(jnp.int32, (tiles, SUBLANES, w), 0)

    def conv_chunk(c, carry):
        r0 = pl.multiple_of(c * chunk, chunk)
        lo = pl.multiple_of(jnp.maximum(r0 - SUBLANES, 0), SUBLANES)
        hi = pl.multiple_of(jnp.minimum(r0 + chunk, nt - SUBLANES), SUBLANES)
        x3 = jnp.concatenate([x_ref[0, pl.ds(lo, SUBLANES), :], x_ref[0, pl.ds(r0, chunk), :],
                              x_ref[0, pl.ds(hi, SUBLANES), :]], axis=0).reshape(tiles + 2, SUBLANES, w)
        sh1 = pltpu.roll(x3, 1, 1)
        sh2 = pltpu.roll(x3, 2, 1)
        sh7 = pltpu.roll(x3, SUBLANES - 1, 1)
        pos = r0 + tile_i * SUBLANES + sub
        in_ctx = pos < nctx
        seg_pos = jnp.where(in_ctx, pos, pos - nctx)
        seg_last = jnp.where(in_ctx, nctx - 1, nt - nctx - 1)
        zero = jnp.zeros((tiles, SUBLANES, w), F32)
        xm2 = jnp.where(seg_pos >= 2, jnp.where(sub >= 2, sh2[1:-1], sh2[0:-2]), zero)
        xm1 = jnp.where(seg_pos >= 1, jnp.where(sub >= 1, sh1[1:-1], sh1[0:-2]), zero)
        xp1 = jnp.where(seg_pos < seg_last, jnp.where(sub < SUBLANES - 1, sh7[1:-1], sh7[2:]), zero)
        y = cb_ref[...] + xm2 * cw_ref[0:1] + xm1 * cw_ref[1:2] + x3[1:-1] * cw_ref[2:3] + xp1 * cw_ref[3:4]
        y_s[pl.ds(r0, chunk), :] = y.reshape(chunk, w)
        return carry

    lax.fori_loop(0, n_chunks, conv_chunk, 0)

    nctx_t = nctx // SUBLANES
    nt_t = nt // SUBLANES

    for d in range(2):
        nlam = -lam_ref[d:d + 1]
        softplus = jnp.maximum(nlam, 0.0) + jnp.log1p(jnp.exp(-jnp.abs(nlam)))
        c8 = -LRU_C * softplus

        def gate_chunk(c, carry, d=d, c8=c8):
            r0 = pl.multiple_of(c * chunk, chunk)
            y = y_s[pl.ds(r0, chunk), :]
            yb = y.astype(BF16)
            r = jax.nn.sigmoid(jnp.dot(yb, wa_ref[d], preferred_element_type=F32) + ba_ref[d:d + 1])
            i = jax.nn.sigmoid(jnp.dot(yb, wi_ref[d], preferred_element_type=F32) + bi_ref[d:d + 1])
            log_a = c8 * r
            a = jnp.exp(log_a)
            th = jnp.tanh(log_a)
            u = jnp.sqrt(-2.0 * th / (1.0 - th)) * (i * y)
            a3 = a.reshape(tiles, SUBLANES, w)
            u3 = u.reshape(tiles, SUBLANES, w)
            for sft in (1, 2, 4):
                if d == 0:
                    ok = sub >= sft
                    ash = pltpu.roll(a3, sft, 1)
                    ush = pltpu.roll(u3, sft, 1)
                else:
                    ok = sub < SUBLANES - sft
                    ash = pltpu.roll(a3, SUBLANES - sft, 1)
                    ush = pltpu.roll(u3, SUBLANES - sft, 1)
                u3 = jnp.where(ok, a3 * ush + u3, u3)
                a3 = jnp.where(ok, a3 * ash, a3)
            a_s[pl.ds(r0, chunk), :] = a3.reshape(chunk, w)
            s_s[pl.ds(r0, chunk), :] = u3.reshape(chunk, w)
            return carry

        lax.fori_loop(0, n_chunks, gate_chunk, 0)

        def carry_tile(j, hprev, d=d):
            if d == 0:
                t = j
            else:
                t = jnp.where(j < nctx_t, nctx_t - 1 - j, nt_t - 1 - (j - nctx_t))
            r0 = pl.multiple_of(t * SUBLANES, SUBLANES)
            h = a_s[pl.ds(r0, SUBLANES), :] * hprev + s_s[pl.ds(r0, SUBLANES), :]
            if d == 0:
                h_s[pl.ds(r0, SUBLANES), :] = h
                return h[SUBLANES - 1:SUBLANES]
            h_s[pl.ds(r0, SUBLANES), :] = h_s[pl.ds(r0, SUBLANES), :] + h
            return h[0:1]

        lax.fori_loop(0, nt_t, carry_tile, jnp.zeros((1, w), F32), unroll=4)

    def out_chunk(c, carry):
        r0 = pl.multiple_of(c * chunk, chunk)
        o_ref[0, pl.ds(r0, chunk), :] = (h_s[pl.ds(r0, chunk), :] * _gelu_tanh(g_ref[0, pl.ds(r0, chunk), :])).astype(BF16)
        return carry

    lax.fori_loop(0, n_chunks, out_chunk, 0)


def _lru(lx, lg, conv_w, conv_b, wa, wi, ba, bi, lam, nctx):
    b, nt, w = lx.shape
    chunk = TOKEN_TILE
    seq = pl.BlockSpec((1, nt, w), lambda i: (i, 0, 0))
    return pl.pallas_call(
        functools.partial(_lru_kernel, nt=nt, nctx=nctx, chunk=chunk),
        grid=(b,),
        in_specs=[seq, seq, _const_spec(conv_w.shape), _const_spec(conv_b.shape), _const_spec(wa.shape),
                  _const_spec(wi.shape), _const_spec(ba.shape), _const_spec(bi.shape), _const_spec(lam.shape)],
        out_specs=seq,
        out_shape=jax.ShapeDtypeStruct((b, nt, w), BF16),
        scratch_shapes=[pltpu.VMEM((nt, w), F32)] * 4,
        compiler_params=_params(("parallel",)),
        name="rglru",
    )(lx, lg, conv_w, conv_b, wa, wi, ba, bi, lam)


DA_KEY_CHUNK = 256
MLA_KEY_CHUNK = 256


def _key_chunks(nk, nctx, size):
    chunks = [(0, nctx)]
    chunks += [(s, min(size, nk - s)) for s in range(nctx, nk, size)]
    return chunks


def _attend_t(chains, chunks):
    def scores(n, ci):
        q_t, key, _ = chains[n]
        return jnp.dot(key(*chunks[ci]), q_t, preferred_element_type=F32)

    s = [scores(n, 0) for n in range(len(chains))]
    state = [None] * len(chains)
    for ci in range(len(chunks)):
        for n, (_, _, value_t) in enumerate(chains):
            cm = jnp.max(s[n], axis=0, keepdims=True)
            if ci == 0:
                m_new = cm
            else:
                m_old, acc = state[n]
                m_new = jnp.maximum(m_old, cm)
            p = jnp.exp2(s[n] - m_new).astype(BF16)
            if ci + 1 < len(chunks):
                s[n] = scores(n, ci + 1)
            pv = jnp.dot(value_t(*chunks[ci]), p, preferred_element_type=F32)
            if ci > 0:
                pv = acc * jnp.exp2(m_old - m_new) + pv
            state[n] = (m_new, pv)
    return [acc[0:DA_V] / acc[DA_V:DA_V + 1] for _, acc in state]


def _da_kernel(q_ref, k_ref, vt_ref, dl_ref, g_ref, li_ref, o_ref, *, nt, nctx, tq):
    half = pl.program_id(1)
    row = lax.broadcasted_iota(jnp.int32, (LANES, tq), 0)
    lane = lax.broadcasted_iota(jnp.int32, (tq, LANES), 1)
    dl = dl_ref[...]
    lam_init = li_ref[...]
    lam = (jnp.exp(jnp.sum(dl[0:1] * dl[1:2], axis=-1, keepdims=True))
           - jnp.exp(jnp.sum(dl[2:3] * dl[3:4], axis=-1, keepdims=True)) + lam_init)
    zero = jnp.zeros((LANES, tq), BF16)

    def attend(q0, nk):
        chains = []
        for j in range(PAIRS):
            q_t = q_ref[0, LANES * j:LANES * (j + 1), pl.ds(q0, tq)]
            key = lambda start, size, j=j: k_ref[0, start:start + size, LANES * j:LANES * (j + 1)]
            value_t = lambda start, size, j=j: vt_ref[0, 2 * j + half, :, start:start + size]
            for mi in range(2):
                lo = half * DA_V + mi * DA_QK
                chains.append((jnp.where((row >= lo) & (row < lo + DA_QK), q_t, zero), key, value_t))
        o = _attend_t(chains, _key_chunks(nk, nctx, DA_KEY_CHUNK))
        for j in range(PAIRS):
            d = o[2 * j] - lam * o[2 * j + 1]
            d = d * lax.rsqrt(jnp.mean(d * d, axis=0, keepdims=True) + RMS_EPS)
            both = jnp.concatenate([d, d], axis=0).T
            new = (both * g_ref[...] * (1.0 - lam_init)).astype(BF16)
            slab = (0, pl.ds(q0, tq), slice(LANES * j, LANES * (j + 1)))

            @pl.when(half == 0)
            def _():
                o_ref[slab] = new

            @pl.when(half == 1)
            def _():
                o_ref[slab] = jnp.where(lane >= DA_V, new, o_ref[slab])

    _for_query_blocks(attend, nt, nctx, tq)


def _for_query_blocks(attend, nt, nctx, tq):
    for t in range(nctx // tq):
        attend(t * tq, nctx)

    def latent_block(t, carry):
        attend(pl.multiple_of(nctx + t * tq, tq), nt)
        return carry

    lax.fori_loop(0, (nt - nctx) // tq, latent_block, 0)


def _da_attn(dq_t, dk, dv_t, dlam, gpair, lam_init, nctx):
    b, nt, _ = dk.shape
    tq = TOKEN_TILE
    return pl.pallas_call(
        functools.partial(_da_kernel, nt=nt, nctx=nctx, tq=tq),
        grid=(b, 2),
        in_specs=[
            pl.BlockSpec((1, DA_WIDTH, nt), lambda i, h: (i, 0, 0)),
            pl.BlockSpec((1, nt, DA_WIDTH), lambda i, h: (i, 0, 0)),
            pl.BlockSpec((1, DA_HEADS, VT_ROWS, nt), lambda i, h: (i, 0, 0, 0)),
            _const_spec(dlam.shape), _const_spec(gpair.shape), _const_spec(lam_init.shape),
        ],
        out_specs=pl.BlockSpec((1, nt, DA_WIDTH), lambda i, h: (i, 0, 0)),
        out_shape=jax.ShapeDtypeStruct((b, nt, DA_WIDTH), BF16),
        compiler_params=_params(("parallel", "arbitrary")),
        name="diff_attn",
    )(dq_t, dk, dv_t, dlam, gpair, lam_init)


def _mla_kernel(q_ref, k_ref, vt_ref, o_ref, *, nt, nctx, tq):
    def attend(q0, nk):
        chains = []
        for hh in range(MLA_HEADS):
            key = lambda start, size, hh=hh: k_ref[0, hh, start:start + size, :]
            value_t = lambda start, size, hh=hh: vt_ref[0, hh, :, start:start + size]
            chains.append((q_ref[0, hh, :, pl.ds(q0, tq)], key, value_t))
        o = _attend_t(chains, _key_chunks(nk, nctx, MLA_KEY_CHUNK))
        for j in range(MLA_HEADS // 2):
            o_ref[0, pl.ds(q0, tq), LANES * j:LANES * (j + 1)] = jnp.concatenate(o[2 * j:2 * j + 2], axis=0).T.astype(BF16)

    _for_query_blocks(attend, nt, nctx, tq)


def _mla_attn(mq_t, mk, mv_t, nctx):
    b, nh, nt, _ = mk.shape
    tq = TOKEN_TILE
    return pl.pallas_call(
        functools.partial(_mla_kernel, nt=nt, nctx=nctx, tq=tq),
        grid=(b,),
        in_specs=[
            pl.BlockSpec((1, nh, LANES, nt), lambda i: (i, 0, 0, 0)),
            pl.BlockSpec((1, nh, nt, LANES), lambda i: (i, 0, 0, 0)),
            pl.BlockSpec((1, nh, VT_ROWS, nt), lambda i: (i, 0, 0, 0)),
        ],
        out_specs=pl.BlockSpec((1, nt, nh * MLA_V), lambda i: (i, 0, 0)),
        out_shape=jax.ShapeDtypeStruct((b, nt, nh * MLA_V), BF16),
        compiler_params=_params(("parallel",)),
        name="mla_attn",
    )(mq_t, mk, mv_t)


def _layer_norm(z, g, b):
    mu = jnp.mean(z, axis=-1, keepdims=True)
    zc = z - mu
    var = jnp.mean(zc * zc, axis=-1, keepdims=True)
    return (zc * lax.rsqrt(var + LN_EPS)) * g + b


def _router_gates(logits, rb):
    scores = jax.nn.sigmoid(logits)
    sel = scores + rb
    lane = lax.broadcasted_iota(jnp.int32, logits.shape, 1)
    r = lane & (EXPERTS_PER_GROUP - 1)
    grp = (lane >> 2) & (N_GROUPS - 1)

    def in_group(x, k):
        return jnp.where(r >= k, pltpu.roll(x, k, 1), pltpu.roll(x, LANES - EXPERTS_PER_GROUP + k, 1))

    others = [in_group(sel, k) for k in (1, 2, 3)]
    pair_max = sel + jnp.maximum(jnp.maximum(others[0], others[1]), others[2])
    grp_score = jnp.maximum(jnp.maximum(pair_max, in_group(pair_max, 1)),
                            jnp.maximum(in_group(pair_max, 2), in_group(pair_max, 3)))
    in_best = None
    for k in (1, 2, 3):
        other = pltpu.roll(grp_score, EXPERTS_PER_GROUP * k, 1)
        wins = (grp_score > other) | ((grp_score == other) & (grp < k))
        in_best = wins if in_best is None else (in_best & wins)
    beaten = jnp.zeros(logits.shape, F32)
    for k, o in zip((1, 2, 3), others):
        beats = (o > sel) | ((o == sel) & (r >= k))
        beaten = beaten + jnp.where(beats, 1.0, 0.0)
    chosen = in_best & (beaten < 2.0)
    sc = jnp.where(chosen, scores, 0.0)
    tot = sc + in_group(sc, 1) + in_group(sc, 2) + in_group(sc, 3)
    return jnp.where(chosen, sc / tot, 0.0)


def _post_kernel(x_ref, mod_ref, lru_ref, da_ref, mla_ref, wo_ref, g1_ref, b1_ref,
                 rw_ref, rb_ref, w1_ref, w3_ref, w2_ref, g_ref, b_ref, o_ref, *, alpha):
    mod = mod_ref[0, 0]
    a = jnp.concatenate([lru_ref[0], da_ref[0], mla_ref[0]], axis=-1)
    o = jnp.dot(a, wo_ref[...], preferred_element_type=F32)
    x1 = _layer_norm(alpha * x_ref[0] + mod[2:3] * o, g1_ref[...], b1_ref[...])
    v = (x1 * (1.0 + mod[4:5]) + mod[3:4]).astype(BF16)
    gates = _router_gates(jnp.dot(v, rw_ref[...], preferred_element_type=F32), rb_ref[...])
    per = EXPERTS_PER_GROUP * D_EXPERT
    f = None
    for c in range(N_GROUPS):
        h1 = jnp.dot(v, w1_ref[:, c * per:(c + 1) * per], preferred_element_type=F32)
        h3 = jnp.dot(v, w3_ref[:, c * per:(c + 1) * per], preferred_element_type=F32)
        hh = (h1 * jax.nn.sigmoid(h1)) * h3
        parts = []
        for j in range(EXPERTS_PER_GROUP):
            e = c * EXPERTS_PER_GROUP + j
            parts.append((hh[:, j * D_EXPERT:(j + 1) * D_EXPERT] * gates[:, e:e + 1]).astype(BF16))
        y = jnp.dot(jnp.concatenate(parts, axis=-1), w2_ref[c * per:(c + 1) * per, :], preferred_element_type=F32)
        f = y if f is None else f + y
    o_ref[0] = _layer_norm(alpha * x1 + mod[5:6] * f, g_ref[...], b_ref[...])


def _post(xa, modt, lru_o, da_o, mla_o, wo, g1, b1, rw, rb, w1c, w3c, w2c, g2, b2, nctx, alpha, latent_only):
    b, nt, d = xa.shape
    tm = TOKEN_TILE
    nc = nctx // tm
    skip = nc if latent_only else 0
    tok = lambda w: pl.BlockSpec((1, tm, w), lambda i, t: (i, t + skip, 0))
    consts = (wo, g1, b1, rw, rb, w1c, w3c, w2c, g2, b2)
    return pl.pallas_call(
        functools.partial(_post_kernel, alpha=alpha),
        grid=(b, nt // tm - skip),
        in_specs=[
            tok(d),
            pl.BlockSpec((1, 1, 8, d), lambda i, t: (i, jnp.where(t + skip >= nc, 1, 0), 0, 0)),
            tok(LRU_WIDTH), tok(da_o.shape[-1]), tok(mla_o.shape[-1]),
        ] + [_const_spec(c.shape) for c in consts],
        out_specs=pl.BlockSpec((1, tm, d), lambda i, t: (i, t, 0)),
        out_shape=jax.ShapeDtypeStruct((b, nt - skip * tm, d), F32),
        compiler_params=_params(("parallel", "parallel")),
        name="post",
    )(xa, modt, lru_o, da_o, mla_o, *consts)


def _rotary_tables(n, nctx):
    rows = n // GRID_W
    row = jnp.repeat(jnp.arange(rows), GRID_W).astype(F32)
    col = jnp.tile(jnp.arange(GRID_W), rows).astype(F32)
    n_freq = DA_QK // 4
    inv = ROPE_THETA ** (-jnp.arange(n_freq, dtype=F32) / n_freq)
    ang = jnp.concatenate([row[:, None] * inv, col[:, None] * inv], axis=-1)
    ang = jnp.concatenate([jnp.zeros((nctx, DA_QK // 2), F32), ang], axis=0)
    c, s = jnp.cos(ang), jnp.sin(ang)
    reps = LANES // DA_QK
    return jnp.tile(jnp.concatenate([c, c], axis=-1), (1, reps)), jnp.tile(jnp.concatenate([-s, s], axis=-1), (1, reps))


def _pack_in_weight(w_in):
    d = w_in.shape[0]
    wkr = w_in[:, C_KR:C_KR + MLA_ROPE]
    krp = jnp.concatenate([jnp.zeros((d, MLA_NOPE), F32), wkr, jnp.zeros((d, LANES - MLA_NOPE - MLA_ROPE), F32)], axis=-1)
    return jnp.concatenate([w_in[:, :C_KR], krp], axis=-1).astype(BF16)


def _pack_uq(w_uq):
    r = w_uq.shape[0]
    w = w_uq.reshape(r, MLA_HEADS, MLA_NOPE + MLA_ROPE)
    w = jnp.concatenate([w, jnp.zeros((r, MLA_HEADS, LANES - MLA_NOPE - MLA_ROPE), F32)], axis=-1)
    return w.reshape(r, MLA_HEADS * LANES).astype(BF16)


def _pack_ukv(w_ukv):
    r = w_ukv.shape[0]
    w = w_ukv.reshape(r, MLA_HEADS, MLA_NOPE + MLA_V)
    z = jnp.zeros((r, MLA_HEADS, LANES - MLA_NOPE), F32)
    wk = jnp.concatenate([w[..., :MLA_NOPE], z], axis=-1).reshape(r, MLA_HEADS * LANES)
    wv = w[..., MLA_NOPE:].reshape(r, MLA_HEADS * MLA_V)
    return wk.astype(BF16), wv.astype(BF16)


def _block_diag(w):
    nd, nb, bs, _ = w.shape
    eye = jnp.eye(nb, dtype=w.dtype)
    return jnp.einsum('dhij,hg->dhigj', w, eye).reshape(nd, nb * bs, nb * bs)


def kernel(x, c, ctx, c_ctx, w_mod, b_mod, w_in, w_out, conv_w, conv_b, lru_wa, lru_ba, lru_wi, lru_bi, lru_lambda, diff_lambda, diff_norm, mla_q_norm, mla_kv_norm, mla_w_uq, mla_w_ukv, ln1_g, ln1_b, ln2_g, ln2_b, router_w, router_b, exp_w1, exp_w3, exp_w2):
    bsz, n, d = x.shape
    nctx = ctx.shape[1]
    depth = w_mod.shape[0]
    alpha = (2 * depth) ** 0.25
    assert nctx % TOKEN_TILE == 0 and n % TOKEN_TILE == 0 and n % GRID_W == 0

    rows = -(-(bsz + 1) // SUBLANES) * SUBLANES
    cc = jnp.concatenate([c, c_ctx[None, :], jnp.zeros((rows - bsz - 1, d), F32)], axis=0)
    mod = _modulation(cc, w_mod, b_mod).reshape(depth, rows, 6, d)
    mod = jnp.pad(mod, ((0, 0), (0, 0), (0, 2), (0, 0)))
    mod_ctx = jnp.broadcast_to(mod[:, bsz][:, None], (depth, bsz, 8, d))
    modt = jnp.stack([mod_ctx, mod[:, :bsz]], axis=2)

    cosf, sinf = _rotary_tables(n, nctx)
    rw = jnp.tile(router_w, (1, LANES // N_EXPERTS)).astype(BF16)
    rb = jnp.tile(router_b, LANES // N_EXPERTS)[None, :].astype(F32)
    gpair = jnp.tile(diff_norm, (1, LANES // DA_V))

    xa = jnp.concatenate([ctx, x], axis=1)
    for l in range(depth):
        lam_init = jnp.full((1, 1), 0.8 - 0.6 * math.exp(-0.3 * l), F32)
        wuk, wuv = _pack_ukv(mla_w_ukv[l])
        lx, lg, dq_t, dk, dv_t, mq_t, mk, mv_t = _in_proj(
            xa, modt[l], _pack_in_weight(w_in[l]), _pack_uq(mla_w_uq[l]), wuk, wuv,
            mla_q_norm[l][None, :], mla_kv_norm[l][None, :], cosf, sinf, nctx)
        lru_o = _lru(lx, lg, conv_w[l], conv_b[l][None, :], _block_diag(lru_wa[l]).astype(BF16),
                     _block_diag(lru_wi[l]).astype(BF16), lru_ba[l], lru_bi[l], lru_lambda[l], nctx)
        da_o = _da_attn(dq_t, dk, dv_t, diff_lambda[l], gpair[l][None, :], lam_init, nctx)
        mla_o = _mla_attn(mq_t, mk, mv_t, nctx)
        w1c = exp_w1[l].transpose(1, 0, 2).reshape(d, N_EXPERTS * D_EXPERT).astype(BF16)
        w3c = exp_w3[l].transpose(1, 0, 2).reshape(d, N_EXPERTS * D_EXPERT).astype(BF16)
        w2c = exp_w2[l].reshape(N_EXPERTS * D_EXPERT, d).astype(BF16)
        xa = _post(xa, modt[l], lru_o, da_o, mla_o, w_out[l].astype(BF16), ln1_g[l][None, :], ln1_b[l][None, :],
                   rw, rb, w1c, w3c, w2c, ln2_g[l][None, :], ln2_b[l][None, :], nctx, alpha,
                   latent_only=(l == depth - 1))
    return xa
```

```python
import functools
import math

import jax
import jax.numpy as jnp
from jax import lax
from jax.experimental import pallas as pl
from jax.experimental.pallas import tpu as pltpu

F32 = jnp.float32
BF16 = jnp.bfloat16

GRID_W = 64
LRU_WIDTH = 256
LRU_BLOCKS = 4
CONV_W = 4
LRU_C = 8.0
DA_HEADS = 6
DA_QK = 32
DA_V = 2 * DA_QK
MLA_HEADS = 6
MLA_NOPE = 64
MLA_ROPE = 32
MLA_V = 64
Q_RANK = 256
KV_RANK = 128
MLA_SCALE = (MLA_NOPE + MLA_ROPE) ** -0.5
N_EXPERTS = 16
N_GROUPS = 4
EXPERTS_PER_GROUP = N_EXPERTS // N_GROUPS
D_EXPERT = 256
ROPE_THETA = 10000.0
LN_EPS = 1e-5
RMS_EPS = 1e-6

LANES = 128
SUBLANES = 8
TOKEN_TILE = 256
VMEM_LIMIT = 56 * 1024 * 1024

LOG2E = math.log2(math.e)
DA_QSCALE = DA_QK ** -0.5 * LOG2E
MLA_QSCALE = MLA_SCALE * LOG2E

DA_WIDTH = DA_HEADS * DA_V
C_LRU = 0
C_DAQ = 2 * LRU_WIDTH
C_DAK = C_DAQ + DA_WIDTH
C_DAV = C_DAK + DA_WIDTH
C_CQ = C_DAV + DA_WIDTH
C_CKV = C_CQ + Q_RANK
C_KR = C_CKV + KV_RANK
C_END = C_KR + LANES
PAIRS = DA_HEADS // 2
VT_ROWS = DA_V + 16


def _params(sem):
    return pltpu.CompilerParams(dimension_semantics=sem, vmem_limit_bytes=VMEM_LIMIT)


def _const_spec(shape):
    nd = len(shape)
    return pl.BlockSpec(shape, lambda *_: (0,) * nd, pipeline_mode=pl.Buffered(1))


def _mod_kernel(c_ref, w_ref, b_ref, o_ref):
    c = c_ref[...]
    s = c * jax.nn.sigmoid(c)
    o_ref[0] = jnp.dot(s.astype(BF16), w_ref[0].astype(BF16), preferred_element_type=F32) + b_ref[0]


def _modulation(cc, w_mod, b_mod):
    depth, d, d6 = w_mod.shape
    r = cc.shape[0]
    tn = min(d6, 1536)
    return pl.pallas_call(
        _mod_kernel,
        grid=(depth, d6 // tn),
        in_specs=[
            pl.BlockSpec((r, d), lambda l, j: (0, 0)),
            pl.BlockSpec((1, d, tn), lambda l, j: (l, 0, j)),
            pl.BlockSpec((1, 1, tn), lambda l, j: (l, 0, j)),
        ],
        out_specs=pl.BlockSpec((1, r, tn), lambda l, j: (l, 0, j)),
        out_shape=jax.ShapeDtypeStruct((depth, r, d6), F32),
        compiler_params=_params(("parallel", "parallel")),
        name="modulation",
    )(cc, w_mod, b_mod.reshape(depth, 1, d6))


def _rotate(t, cosf, sinf, first_half):
    partner = jnp.where(first_half, pltpu.roll(t, LANES - DA_QK // 2, 1), pltpu.roll(t, DA_QK // 2, 1))
    return t * cosf + partner * sinf


def _store_values_t(vt_ref, v):
    rows = v.shape[0]
    ones = jnp.ones((VT_ROWS - DA_V, rows), BF16)
    for j in range(PAIRS):
        t = v[:, LANES * j:LANES * (j + 1)].T.astype(BF16)
        for k in range(2):
            vt_ref[0, 2 * j + k, 0:DA_V, :] = t[DA_V * k:DA_V * (k + 1)]
            vt_ref[0, 2 * j + k, DA_V:VT_ROWS, :] = ones


def _in_kernel(x_ref, mod_ref, w1_ref, wuq_ref, wuk_ref, wuv_ref, qn_ref, kvn_ref, cos_ref, sin_ref,
               lx_ref, lg_ref, dq_ref, dk_ref, dv_ref, mq_ref, mk_ref, mv_ref):
    x = x_ref[0]
    mod = mod_ref[0, 0]
    u = x * (1.0 + mod[1:2]) + mod[0:1]
    y = jnp.dot(u.astype(BF16), w1_ref[...], preferred_element_type=F32)
    lx_ref[0] = y[:, C_LRU:C_LRU + LRU_WIDTH]
    lg_ref[0] = y[:, C_LRU + LRU_WIDTH:C_DAQ]

    cosf = cos_ref[...]
    sinf = sin_ref[...]
    lane = lax.broadcasted_iota(jnp.int32, cosf.shape, 1)
    first_half = (lane & (DA_QK // 2)) == 0
    rot = functools.partial(_rotate, cosf=cosf, sinf=sinf, first_half=first_half)

    for j in range(PAIRS):
        t = y[:, C_DAQ + LANES * j:C_DAQ + LANES * (j + 1)]
        dq_ref[0, LANES * j:LANES * (j + 1), :] = (rot(t) * DA_QSCALE).T.astype(BF16)
        t = y[:, C_DAK + LANES * j:C_DAK + LANES * (j + 1)]
        dk_ref[0, :, LANES * j:LANES * (j + 1)] = rot(t).astype(BF16)
    _store_values_t(dv_ref, y[:, C_DAV:C_CQ])

    cq = y[:, C_CQ:C_CKV]
    ckv = y[:, C_CKV:C_KR]
    krp = y[:, C_KR:C_END]
    is_rope = (lane >= MLA_NOPE) & (lane < MLA_NOPE + MLA_ROPE)
    kr = jnp.where(is_rope, rot(krp), krp)
    qn = (cq * lax.rsqrt(jnp.mean(cq * cq, axis=-1, keepdims=True) + RMS_EPS)) * qn_ref[...]
    q = jnp.dot(qn.astype(BF16), wuq_ref[...], preferred_element_type=F32)
    kvn = ((ckv * lax.rsqrt(jnp.mean(ckv * ckv, axis=-1, keepdims=True) + RMS_EPS)) * kvn_ref[...]).astype(BF16)
    kn = jnp.dot(kvn, wuk_ref[...], preferred_element_type=F32)
    for h in range(MLA_HEADS):
        t = q[:, LANES * h:LANES * (h + 1)]
        mq_ref[0, h] = (jnp.where(is_rope, rot(t), t) * MLA_QSCALE).T.astype(BF16)
        mk_ref[0, h] = (kn[:, LANES * h:LANES * (h + 1)] + kr).astype(BF16)
    _store_values_t(mv_ref, jnp.dot(kvn, wuv_ref[...], preferred_element_type=F32))


def _in_proj(xa, modt, w1, wuq, wuk, wuv, qnorm, kvnorm, cosf, sinf, nctx):
    b, nt, d = xa.shape
    tm = TOKEN_TILE
    nc = nctx // tm
    tok = lambda w: pl.BlockSpec((1, tm, w), lambda i, t: (i, t, 0))
    head_t = lambda r: pl.BlockSpec((1, DA_HEADS, r, tm), lambda i, t: (i, 0, 0, t))
    return pl.pallas_call(
        _in_kernel,
        grid=(b, nt // tm),
        in_specs=[
            tok(d),
            pl.BlockSpec((1, 1, 8, d), lambda i, t: (i, jnp.where(t >= nc, 1, 0), 0, 0)),
            _const_spec(w1.shape), _const_spec(wuq.shape), _const_spec(wuk.shape), _const_spec(wuv.shape),
            _const_spec(qnorm.shape), _const_spec(kvnorm.shape),
            pl.BlockSpec((tm, LANES), lambda i, t: (t, 0)),
            pl.BlockSpec((tm, LANES), lambda i, t: (t, 0)),
        ],
        out_specs=[
            tok(LRU_WIDTH), tok(LRU_WIDTH),
            pl.BlockSpec((1, DA_WIDTH, tm), lambda i, t: (i, 0, t)), tok(DA_WIDTH), head_t(VT_ROWS),
            head_t(LANES), pl.BlockSpec((1, MLA_HEADS, tm, LANES), lambda i, t: (i, 0, t, 0)), head_t(VT_ROWS),
        ],
        out_shape=[
            jax.ShapeDtypeStruct((b, nt, LRU_WIDTH), F32),
            jax.ShapeDtypeStruct((b, nt, LRU_WIDTH), F32),
            jax.ShapeDtypeStruct((b, DA_WIDTH, nt), BF16),
            jax.ShapeDtypeStruct((b, nt, DA_WIDTH), BF16),
            jax.ShapeDtypeStruct((b, DA_HEADS, VT_ROWS, nt), BF16),
            jax.ShapeDtypeStruct((b, MLA_HEADS, LANES, nt), BF16),
            jax.ShapeDtypeStruct((b, MLA_HEADS, nt, LANES), BF16),
            jax.ShapeDtypeStruct((b, MLA_HEADS, VT_ROWS, nt), BF16),
        ],
        compiler_params=_params(("parallel", "parallel")),
        name="in_proj",
    )(xa, modt, w1, wuq, wuk, wuv, qnorm, kvnorm, cosf, sinf)


def _gelu_tanh(x):
    return 0.5 * x * (1.0 + jnp.tanh(math.sqrt(2.0 / math.pi) * (x + 0.044715 * (x * x * x))))


def _lru_kernel(x_ref, g_ref, cw_ref, cb_ref, wa_ref, wi_ref, ba_ref, bi_ref, lam_ref, o_ref,
                y_s, a_s, s_s, h_s, *, nt, nctx, chunk):
    w = LRU_WIDTH
    tiles = chunk // SUBLANES
    n_chunks = nt // chunk
    sub = lax.broadcasted_iota(jnp.int32, (tiles, SUBLANES, w), 1)
    tile_i = lax.broadcasted_iota(jnp.int32, (tiles, SUBLANES, w), 0)

    def conv_chunk(c, carry):
        r0 = pl.multiple_of(c * chunk, chunk)
        lo = pl.multiple_of(jnp.maximum(r0 - SUBLANES, 0), SUBLANES)
        hi = pl.multiple_of(jnp.minimum(r0 + chunk, nt - SUBLANES), SUBLANES)
        x3 = jnp.concatenate([x_ref[0, pl.ds(lo, SUBLANES), :], x_ref[0, pl.ds(r0, chunk), :],
                              x_ref[0, pl.ds(hi, SUBLANES), :]], axis=0).reshape(tiles + 2, SUBLANES, w)
        sh1 = pltpu.roll(x3, 1, 1)
        sh2 = pltpu.roll(x3, 2, 1)
        sh7 = pltpu.roll(x3, SUBLANES - 1, 1)
        pos = r0 + tile_i * SUBLANES + sub
        in_ctx = pos < nctx
        seg_pos = jnp.where(in_ctx, pos, pos - nctx)
        seg_last = jnp.where(in_ctx, nctx - 1, nt - nctx - 1)
        zero = jnp.zeros((tiles, SUBLANES, w), F32)
        xm2 = jnp.where(seg_pos >= 2, jnp.where(sub >= 2, sh2[1:-1], sh2[0:-2]), zero)
        xm1 = jnp.where(seg_pos >= 1, jnp.where(sub >= 1, sh1[1:-1], sh1[0:-2]), zero)
        xp1 = jnp.where(seg_pos < seg_last, jnp.where(sub < SUBLANES - 1, sh7[1:-1], sh7[2:]), zero)
        y = cb_ref[...] + xm2 * cw_ref[0:1] + xm1 * cw_ref[1:2] + x3[1:-1] * cw_ref[2:3] + xp1 * cw_ref[3:4]
        y_s[pl.ds(r0, chunk), :] = y.reshape(chunk, w)
        return carry

    lax.fori_loop(0, n_chunks, conv_chunk, 0)

    nctx_t = nctx // SUBLANES
    nt_t = nt // SUBLANES

    for d in range(2):
        nlam = -lam_ref[d:d + 1]
        softplus = jnp.maximum(nlam, 0.0) + jnp.log1p(jnp.exp(-jnp.abs(nlam)))
        c8 = -LRU_C * softplus

        def gate_chunk(c, carry, d=d, c8=c8):
            r0 = pl.multiple_of(c * chunk, chunk)
            y = y_s[pl.ds(r0, chunk), :]
            yb = y.astype(BF16)
            r = jax.nn.sigmoid(jnp.dot(yb, wa_ref[d], preferred_element_type=F32) + ba_ref[d:d + 1])
            i = jax.nn.sigmoid(jnp.dot(yb, wi_ref[d], preferred_element_type=F32) + bi_ref[d:d + 1])
            log_a = c8 * r
            a = jnp.exp(log_a)
            th = jnp.tanh(log_a)
            u = jnp.sqrt(-2.0 * th / (1.0 - th)) * (i * y)
            a3 = a.reshape(tiles, SUBLANES, w)
            u3 = u.reshape(tiles, SUBLANES, w)
            for sft in (1, 2, 4):
                if d == 0:
                    ok = sub >= sft
                    ash = pltpu.roll(a3, sft, 1)
                    ush = pltpu.roll(u3, sft, 1)
                else:
                    ok = sub < SUBLANES - sft
                    ash = pltpu.roll(a3, SUBLANES - sft, 1)
                    ush = pltpu.roll(u3, SUBLANES - sft, 1)
                u3 = jnp.where(ok, a3 * ush + u3, u3)
                a3 = jnp.where(ok, a3 * ash, a3)
            a_s[pl.ds(r0, chunk), :] = a3.reshape(chunk, w)
            s_s[pl.ds(r0, chunk), :] = u3.reshape(chunk, w)
            return carry

        lax.fori_loop(0, n_chunks, gate_chunk, 0)

        def carry_tile(j, hprev, d=d):
            if d == 0:
                t = j
            else:
                t = jnp.where(j < nctx_t, nctx_t - 1 - j, nt_t - 1 - (j - nctx_t))
            r0 = pl.multiple_of(t * SUBLANES, SUBLANES)
            h = a_s[pl.ds(r0, SUBLANES), :] * hprev + s_s[pl.ds(r0, SUBLANES), :]
            if d == 0:
                h_s[pl.ds(r0, SUBLANES), :] = h
                return h[SUBLANES - 1:SUBLANES]
            h_s[pl.ds(r0, SUBLANES), :] = h_s[pl.ds(r0, SUBLANES), :] + h
            return h[0:1]

        lax.fori_loop(0, nt_t, carry_tile, jnp.zeros((1, w), F32), unroll=4)

    def out_chunk(c, carry):
        r0 = pl.multiple_of(c * chunk, chunk)
        o_ref[0, pl.ds(r0, chunk), :] = (h_s[pl.ds(r0, chunk), :] * _gelu_tanh(g_ref[0, pl.ds(r0, chunk), :])).astype(BF16)
        return carry

    lax.fori_loop(0, n_chunks, out_chunk, 0)


def _lru(lx, lg, conv_w, conv_b, wa, wi, ba, bi, lam, nctx):
    b, nt, w = lx.shape
    chunk = TOKEN_TILE
    seq = pl.BlockSpec((1, nt, w), lambda i: (i, 0, 0))
    return pl.pallas_call(
        functools.partial(_lru_kernel, nt=nt, nctx=nctx, chunk=chunk),
        grid=(b,),
        in_specs=[seq, seq, _const_spec(conv_w.shape), _const_spec(conv_b.shape), _const_spec(wa.shape),
                  _const_spec(wi.shape), _const_spec(ba.shape), _const_spec(bi.shape), _const_spec(lam.shape)],
        out_specs=seq,
        out_shape=jax.ShapeDtypeStruct((b, nt, w), BF16),
        scratch_shapes=[pltpu.VMEM((nt, w), F32)] * 4,
        compiler_params=_params(("parallel",)),
        name="rglru",
    )(lx, lg, conv_w, conv_b, wa, wi, ba, bi, lam)


DA_KEY_CHUNK = 256
MLA_KEY_CHUNK = 256


def _key_chunks(nk, nctx, size):
    chunks = [(0, nctx)]
    chunks += [(s, min(size, nk - s)) for s in range(nctx, nk, size)]
    return chunks


def _attend_t(chains, chunks):
    def scores(n, ci):
        q_t, key, _ = chains[n]
        return jnp.dot(key(*chunks[ci]), q_t, preferred_element_type=F32)

    s = [scores(n, 0) for n in range(len(chains))]
    state = [None] * len(chains)
    for ci in range(len(chunks)):
        for n, (_, _, value_t) in enumerate(chains):
            cm = jnp.max(s[n], axis=0, keepdims=True)
            if ci == 0:
                m_new = cm
            else:
                m_old, acc = state[n]
                m_new = jnp.maximum(m_old, cm)
            p = jnp.exp2(s[n] - m_new).astype(BF16)
            if ci + 1 < len(chunks):
                s[n] = scores(n, ci + 1)
            pv = jnp.dot(value_t(*chunks[ci]), p, preferred_element_type=F32)
            if ci > 0:
                pv = acc * jnp.exp2(m_old - m_new) + pv
            state[n] = (m_new, pv)
    return [acc[0:DA_V] / acc[DA_V:DA_V + 1] for _, acc in state]


def _da_kernel(q_ref, k_ref, vt_ref, dl_ref, g_ref, li_ref, o_ref, *, nt, nctx, tq):
    half = pl.program_id(1)
    row = lax.broadcasted_iota(jnp.int32, (LANES, tq), 0)
    lane = lax.broadcasted_iota(jnp.int32, (tq, LANES), 1)
    dl = dl_ref[...]
    lam_init = li_ref[...]
    lam = (jnp.exp(jnp.sum(dl[0:1] * dl[1:2], axis=-1, keepdims=True))
           - jnp.exp(jnp.sum(dl[2:3] * dl[3:4], axis=-1, keepdims=True)) + lam_init)
    zero = jnp.zeros((LANES, tq), BF16)

    def attend(q0, nk):
        chains = []
        for j in range(PAIRS):
            q_t = q_ref[0, LANES * j:LANES * (j + 1), pl.ds(q0, tq)]
            key = lambda start, size, j=j: k_ref[0, start:start + size, LANES * j:LANES * (j + 1)]
            value_t = lambda start, size, j=j: vt_ref[0, 2 * j + half, :, start:start + size]
            for mi in range(2):
                lo = half * DA_V + mi * DA_QK
                chains.append((jnp.where((row >= lo) & (row < lo + DA_QK), q_t, zero), key, value_t))
        o = _attend_t(chains, _key_chunks(nk, nctx, DA_KEY_CHUNK))
        for j in range(PAIRS):
            d = o[2 * j] - lam * o[2 * j + 1]
            d = d * lax.rsqrt(jnp.mean(d * d, axis=0, keepdims=True) + RMS_EPS)
            both = jnp.concatenate([d, d], axis=0).T
            new = (both * g_ref[...] * (1.0 - lam_init)).astype(BF16)
            slab = (0, pl.ds(q0, tq), slice(LANES * j, LANES * (j + 1)))

            @pl.when(half == 0)
            def _():
                o_ref[slab] = new

            @pl.when(half == 1)
            def _():
                o_ref[slab] = jnp.where(lane >= DA_V, new, o_ref[slab])

    _for_query_blocks(attend, nt, nctx, tq)


def _for_query_blocks(attend, nt, nctx, tq):
    for t in range(nctx // tq):
        attend(t * tq, nctx)

    def latent_block(t, carry):
        attend(pl.multiple_of(nctx + t * tq, tq), nt)
        return carry

    lax.fori_loop(0, (nt - nctx) // tq, latent_block, 0)


def _da_attn(dq_t, dk, dv_t, dlam, gpair, lam_init, nctx):
    b, nt, _ = dk.shape
    tq = TOKEN_TILE
    return pl.pallas_call(
        functools.partial(_da_kernel, nt=nt, nctx=nctx, tq=tq),
        grid=(b, 2),
        in_specs=[
            pl.BlockSpec((1, DA_WIDTH, nt), lambda i, h: (i, 0, 0)),
            pl.BlockSpec((1, nt, DA_WIDTH), lambda i, h: (i, 0, 0)),
            pl.BlockSpec((1, DA_HEADS, VT_ROWS, nt), lambda i, h: (i, 0, 0, 0)),
            _const_spec(dlam.shape), _const_spec(gpair.shape), _const_spec(lam_init.shape),
        ],
        out_specs=pl.BlockSpec((1, nt, DA_WIDTH), lambda i, h: (i, 0, 0)),
        out_shape=jax.ShapeDtypeStruct((b, nt, DA_WIDTH), BF16),
        compiler_params=_params(("parallel", "arbitrary")),
        name="diff_attn",
    )(dq_t, dk, dv_t, dlam, gpair, lam_init)


def _mla_kernel(q_ref, k_ref, vt_ref, o_ref, *, nt, nctx, tq):
    def attend(q0, nk):
        chains = []
        for hh in range(MLA_HEADS):
            key = lambda start, size, hh=hh: k_ref[0, hh, start:start + size, :]
            value_t = lambda start, size, hh=hh: vt_ref[0, hh, :, start:start + size]
            chains.append((q_ref[0, hh, :, pl.ds(q0, tq)], key, value_t))
        o = _attend_t(chains, _key_chunks(nk, nctx, MLA_KEY_CHUNK))
        for j in range(MLA_HEADS // 2):
            o_ref[0, pl.ds(q0, tq), LANES * j:LANES * (j + 1)] = jnp.concatenate(o[2 * j:2 * j + 2], axis=0).T.astype(BF16)

    _for_query_blocks(attend, nt, nctx, tq)


def _mla_attn(mq_t, mk, mv_t, nctx):
    b, nh, nt, _ = mk.shape
    tq = TOKEN_TILE
    return pl.pallas_call(
        functools.partial(_mla_kernel, nt=nt, nctx=nctx, tq=tq),
        grid=(b,),
        in_specs=[
            pl.BlockSpec((1, nh, LANES, nt), lambda i: (i, 0, 0, 0)),
            pl.BlockSpec((1, nh, nt, LANES), lambda i: (i, 0, 0, 0)),
            pl.BlockSpec((1, nh, VT_ROWS, nt), lambda i: (i, 0, 0, 0)),
        ],
        out_specs=pl.BlockSpec((1, nt, nh * MLA_V), lambda i: (i, 0, 0)),
        out_shape=jax.ShapeDtypeStruct((b, nt, nh * MLA_V), BF16),
        compiler_params=_params(("parallel",)),
        name="mla_attn",
    )(mq_t, mk, mv_t)


def _layer_norm(z, g, b):
    mu = jnp.mean(z, axis=-1, keepdims=True)
    zc = z - mu
    var = jnp.mean(zc * zc, axis=-1, keepdims=True)
    return (zc * lax.rsqrt(var + LN_EPS)) * g + b


def _router_gates(logits, rb):
    scores = jax.nn.sigmoid(logits)
    sel = scores + rb
    lane = lax.broadcasted_iota(jnp.int32, logits.shape, 1)
    r = lane & (EXPERTS_PER_GROUP - 1)
    grp = (lane >> 2) & (N_GROUPS - 1)

    def in_group(x, k):
        return jnp.where(r >= k, pltpu.roll(x, k, 1), pltpu.roll(x, LANES - EXPERTS_PER_GROUP + k, 1))

    others = [in_group(sel, k) for k in (1, 2, 3)]
    pair_max = sel + jnp.maximum(jnp.maximum(others[0], others[1]), others[2])
    grp_score = jnp.maximum(jnp.maximum(pair_max, in_group(pair_max, 1)),
                            jnp.maximum(in_group(pair_max, 2), in_group(pair_max, 3)))
    in_best = None
    for k in (1, 2, 3):
        other = pltpu.roll(grp_score, EXPERTS_PER_GROUP * k, 1)
        wins = (grp_score > other) | ((grp_score == other) & (grp < k))
        in_best = wins if in_best is None else (in_best & wins)
    beaten = jnp.zeros(logits.shape, F32)
    for k, o in zip((1, 2, 3), others):
        beats = (o > sel) | ((o == sel) & (r >= k))
        beaten = beaten + jnp.where(beats, 1.0, 0.0)
    chosen = in_best & (beaten < 2.0)
    sc = jnp.where(chosen, scores, 0.0)
    tot = sc + in_group(sc, 1) + in_group(sc, 2) + in_group(sc, 3)
    return jnp.where(chosen, sc / tot, 0.0)


def _post_kernel(x_ref, mod_ref, lru_ref, da_ref, mla_ref, wo_ref, g1_ref, b1_ref,
                 rw_ref, rb_ref, w1_ref, w3_ref, w2_ref, g_ref, b_ref, o_ref, *, alpha):
    nb, tm, d = x_ref.shape
    rows = nb * tm
    mod = mod_ref[:, 0]
    per_row = lambda k: jnp.broadcast_to(mod[:, k:k + 1], (nb, tm, d)).reshape(rows, d)
    a = jnp.concatenate([lru_ref[...], da_ref[...], mla_ref[...]], axis=-1).reshape(rows, -1)
    o = jnp.dot(a, wo_ref[...], preferred_element_type=F32)
    x1 = _layer_norm(alpha * x_ref[...].reshape(rows, d) + per_row(2) * o, g1_ref[...], b1_ref[...])
    v = (x1 * (1.0 + per_row(4)) + per_row(3)).astype(BF16)
    gates = _router_gates(jnp.dot(v, rw_ref[...], preferred_element_type=F32), rb_ref[...])
    per = EXPERTS_PER_GROUP * D_EXPERT
    f = None
    for c in range(N_GROUPS):
        h1 = jnp.dot(v, w1_ref[:, c * per:(c + 1) * per], preferred_element_type=F32)
        h3 = jnp.dot(v, w3_ref[:, c * per:(c + 1) * per], preferred_element_type=F32)
        hh = (h1 * jax.nn.sigmoid(h1)) * h3
        parts = []
        for j in range(EXPERTS_PER_GROUP):
            e = c * EXPERTS_PER_GROUP + j
            parts.append((hh[:, j * D_EXPERT:(j + 1) * D_EXPERT] * gates[:, e:e + 1]).astype(BF16))
        y = jnp.dot(jnp.concatenate(parts, axis=-1), w2_ref[c * per:(c + 1) * per, :], preferred_element_type=F32)
        f = y if f is None else f + y
    o_ref[...] = _layer_norm(alpha * x1 + per_row(5) * f, g_ref[...], b_ref[...]).reshape(nb, tm, d)


POST_BATCH = 2


def _post(xa, modt, lru_o, da_o, mla_o, wo, g1, b1, rw, rb, w1c, w3c, w2c, g2, b2, nctx, alpha, latent_only):
    b, nt, d = xa.shape
    tm = TOKEN_TILE
    nc = nctx // tm
    skip = nc if latent_only else 0
    nb = POST_BATCH if b % POST_BATCH == 0 else 1
    tok = lambda w: pl.BlockSpec((nb, tm, w), lambda i, t: (i, t + skip, 0))
    consts = (wo, g1, b1, rw, rb, w1c, w3c, w2c, g2, b2)
    return pl.pallas_call(
        functools.partial(_post_kernel, alpha=alpha),
        grid=(b // nb, nt // tm - skip),
        in_specs=[
            tok(d),
            pl.BlockSpec((nb, 1, 8, d), lambda i, t: (i, jnp.where(t + skip >= nc, 1, 0), 0, 0)),
            tok(LRU_WIDTH), tok(da_o.shape[-1]), tok(mla_o.shape[-1]),
        ] + [_const_spec(c.shape) for c in consts],
        out_specs=pl.BlockSpec((nb, tm, d), lambda i, t: (i, t, 0)),
        out_shape=jax.ShapeDtypeStruct((b, nt - skip * tm, d), F32),
        compiler_params=_params(("parallel", "parallel")),
        name="post",
    )(xa, modt, lru_o, da_o, mla_o, *consts)


def _rotary_tables(n, nctx):
    rows = n // GRID_W
    row = jnp.repeat(jnp.arange(rows), GRID_W).astype(F32)
    col = jnp.tile(jnp.arange(GRID_W), rows).astype(F32)
    n_freq = DA_QK // 4
    inv = ROPE_THETA ** (-jnp.arange(n_freq, dtype=F32) / n_freq)
    ang = jnp.concatenate([row[:, None] * inv, col[:, None] * inv], axis=-1)
    ang = jnp.concatenate([jnp.zeros((nctx, DA_QK // 2), F32), ang], axis=0)
    c, s = jnp.cos(ang), jnp.sin(ang)
    reps = LANES // DA_QK
    return jnp.tile(jnp.concatenate([c, c], axis=-1), (1, reps)), jnp.tile(jnp.concatenate([-s, s], axis=-1), (1, reps))


def _pack_in_weight(w_in):
    d = w_in.shape[0]
    wkr = w_in[:, C_KR:C_KR + MLA_ROPE]
    krp = jnp.concatenate([jnp.zeros((d, MLA_NOPE), F32), wkr, jnp.zeros((d, LANES - MLA_NOPE - MLA_ROPE), F32)], axis=-1)
    return jnp.concatenate([w_in[:, :C_KR], krp], axis=-1).astype(BF16)


def _pack_uq(w_uq):
    r = w_uq.shape[0]
    w = w_uq.reshape(r, MLA_HEADS, MLA_NOPE + MLA_ROPE)
    w = jnp.concatenate([w, jnp.zeros((r, MLA_HEADS, LANES - MLA_NOPE - MLA_ROPE), F32)], axis=-1)
    return w.reshape(r, MLA_HEADS * LANES).astype(BF16)


def _pack_ukv(w_ukv):
    r = w_ukv.shape[0]
    w = w_ukv.reshape(r, MLA_HEADS, MLA_NOPE + MLA_V)
    z = jnp.zeros((r, MLA_HEADS, LANES - MLA_NOPE), F32)
    wk = jnp.concatenate([w[..., :MLA_NOPE], z], axis=-1).reshape(r, MLA_HEADS * LANES)
    wv = w[..., MLA_NOPE:].reshape(r, MLA_HEADS * MLA_V)
    return wk.astype(BF16), wv.astype(BF16)


def _block_diag(w):
    nd, nb, bs, _ = w.shape
    eye = jnp.eye(nb, dtype=w.dtype)
    return jnp.einsum('dhij,hg->dhigj', w, eye).reshape(nd, nb * bs, nb * bs)


def kernel(x, c, ctx, c_ctx, w_mod, b_mod, w_in, w_out, conv_w, conv_b, lru_wa, lru_ba, lru_wi, lru_bi, lru_lambda, diff_lambda, diff_norm, mla_q_norm, mla_kv_norm, mla_w_uq, mla_w_ukv, ln1_g, ln1_b, ln2_g, ln2_b, router_w, router_b, exp_w1, exp_w3, exp_w2):
    bsz, n, d = x.shape
    nctx = ctx.shape[1]
    depth = w_mod.shape[0]
    alpha = (2 * depth) ** 0.25
    assert nctx % TOKEN_TILE == 0 and n % TOKEN_TILE == 0 and n % GRID_W == 0

    rows = -(-(bsz + 1) // SUBLANES) * SUBLANES
    cc = jnp.concatenate([c, c_ctx[None, :], jnp.zeros((rows - bsz - 1, d), F32)], axis=0)
    mod = _modulation(cc, w_mod, b_mod).reshape(depth, rows, 6, d)
    mod = jnp.pad(mod, ((0, 0), (0, 0), (0, 2), (0, 0)))
    mod_ctx = jnp.broadcast_to(mod[:, bsz][:, None], (depth, bsz, 8, d))
    modt = jnp.stack([mod_ctx, mod[:, :bsz]], axis=2)

    cosf, sinf = _rotary_tables(n, nctx)
    rw = jnp.tile(router_w, (1, LANES // N_EXPERTS)).astype(BF16)
    rb = jnp.tile(router_b, LANES // N_EXPERTS)[None, :].astype(F32)
    gpair = jnp.tile(diff_norm, (1, LANES // DA_V))

    xa = jnp.concatenate([ctx, x], axis=1)
    for l in range(depth):
        lam_init = jnp.full((1, 1), 0.8 - 0.6 * math.exp(-0.3 * l), F32)
        wuk, wuv = _pack_ukv(mla_w_ukv[l])
        lx, lg, dq_t, dk, dv_t, mq_t, mk, mv_t = _in_proj(
            xa, modt[l], _pack_in_weight(w_in[l]), _pack_uq(mla_w_uq[l]), wuk, wuv,
            mla_q_norm[l][None, :], mla_kv_norm[l][None, :], cosf, sinf, nctx)
        lru_o = _lru(lx, lg, conv_w[l], conv_b[l][None, :], _block_diag(lru_wa[l]).astype(BF16),
                     _block_diag(lru_wi[l]).astype(BF16), lru_ba[l], lru_bi[l], lru_lambda[l], nctx)
        da_o = _da_attn(dq_t, dk, dv_t, diff_lambda[l], gpair[l][None, :], lam_init, nctx)
        mla_o = _mla_attn(mq_t, mk, mv_t, nctx)
        w1c = exp_w1[l].transpose(1, 0, 2).reshape(d, N_EXPERTS * D_EXPERT).astype(BF16)
        w3c = exp_w3[l].transpose(1, 0, 2).reshape(d, N_EXPERTS * D_EXPERT).astype(BF16)
        w2c = exp_w2[l].reshape(N_EXPERTS * D_EXPERT, d).astype(BF16)
        xa = _post(xa, modt[l], lru_o, da_o, mla_o, w_out[l].astype(BF16), ln1_g[l][None, :], ln1_b[l][None, :],
                   rw, rb, w1c, w3c, w2c, ln2_g[l][None, :], ln2_b[l][None, :], nctx, alpha,
                   latent_only=(l == depth - 1))
    return xa
```

```python
import functools
import math

import jax
import jax.numpy as jnp
from jax import lax
from jax.experimental import pallas as pl
from jax.experimental.pallas import tpu as pltpu

F32 = jnp.float32
BF16 = jnp.bfloat16

GRID_W = 64
LRU_WIDTH = 256
LRU_BLOCKS = 4
CONV_W = 4
LRU_C = 8.0
DA_HEADS = 6
DA_QK = 32
DA_V = 2 * DA_QK
MLA_HEADS = 6
MLA_NOPE = 64
MLA_ROPE = 32
MLA_V = 64
Q_RANK = 256
KV_RANK = 128
MLA_SCALE = (MLA_NOPE + MLA_ROPE) ** -0.5
N_EXPERTS = 16
N_GROUPS = 4
EXPERTS_PER_GROUP = N_EXPERTS // N_GROUPS
D_EXPERT = 256
ROPE_THETA = 10000.0
LN_EPS = 1e-5
RMS_EPS = 1e-6

LANES = 128
SUBLANES = 8
TOKEN_TILE = 256
VMEM_LIMIT = 56 * 1024 * 1024

LOG2E = math.log2(math.e)
DA_QSCALE = DA_QK ** -0.5 * LOG2E
MLA_QSCALE = MLA_SCALE * LOG2E

DA_WIDTH = DA_HEADS * DA_V
C_LRU = 0
C_DAQ = 2 * LRU_WIDTH
C_DAK = C_DAQ + DA_WIDTH
C_DAV = C_DAK + DA_WIDTH
C_CQ = C_DAV + DA_WIDTH
C_CKV = C_CQ + Q_RANK
C_KR = C_CKV + KV_RANK
C_END = C_KR + LANES
PAIRS = DA_HEADS // 2
VT_ROWS = DA_V + 16


def _params(sem):
    return pltpu.CompilerParams(dimension_semantics=sem, vmem_limit_bytes=VMEM_LIMIT)


def _const_spec(shape):
    nd = len(shape)
    return pl.BlockSpec(shape, lambda *_: (0,) * nd, pipeline_mode=pl.Buffered(1))


def _mod_kernel(c_ref, w_ref, b_ref, o_ref):
    c = c_ref[...]
    s = c * jax.nn.sigmoid(c)
    o_ref[0] = jnp.dot(s.astype(BF16), w_ref[0].astype(BF16), preferred_element_type=F32) + b_ref[0]


def _modulation(cc, w_mod, b_mod):
    depth, d, d6 = w_mod.shape
    r = cc.shape[0]
    tn = min(d6, 1536)
    return pl.pallas_call(
        _mod_kernel,
        grid=(depth, d6 // tn),
        in_specs=[
            pl.BlockSpec((r, d), lambda l, j: (0, 0)),
            pl.BlockSpec((1, d, tn), lambda l, j: (l, 0, j)),
            pl.BlockSpec((1, 1, tn), lambda l, j: (l, 0, j)),
        ],
        out_specs=pl.BlockSpec((1, r, tn), lambda l, j: (l, 0, j)),
        out_shape=jax.ShapeDtypeStruct((depth, r, d6), F32),
        compiler_params=_params(("parallel", "parallel")),
        name="modulation",
    )(cc, w_mod, b_mod.reshape(depth, 1, d6))


def _rotate(t, cosf, sinf, first_half):
    partner = jnp.where(first_half, pltpu.roll(t, LANES - DA_QK // 2, 1), pltpu.roll(t, DA_QK // 2, 1))
    return t * cosf + partner * sinf


def _store_values_t(vt_ref, v):
    rows = v.shape[0]
    ones = jnp.ones((VT_ROWS - DA_V, rows), BF16)
    for j in range(PAIRS):
        t = v[:, LANES * j:LANES * (j + 1)].T.astype(BF16)
        for k in range(2):
            vt_ref[0, 2 * j + k, 0:DA_V, :] = t[DA_V * k:DA_V * (k + 1)]
            vt_ref[0, 2 * j + k, DA_V:VT_ROWS, :] = ones


def _in_kernel(x_ref, mod_ref, w1_ref, wuq_ref, wuk_ref, wuv_ref, qn_ref, kvn_ref, cos_ref, sin_ref,
               lx_ref, lg_ref, dq_ref, dk_ref, dv_ref, mq_ref, mk_ref, mv_ref):
    x = x_ref[0]
    mod = mod_ref[0, 0]
    u = x * (1.0 + mod[1:2]) + mod[0:1]
    y = jnp.dot(u.astype(BF16), w1_ref[...], preferred_element_type=F32)
    lx_ref[0] = y[:, C_LRU:C_LRU + LRU_WIDTH]
    lg_ref[0] = y[:, C_LRU + LRU_WIDTH:C_DAQ]

    cosf = cos_ref[...]
    sinf = sin_ref[...]
    lane = lax.broadcasted_iota(jnp.int32, cosf.shape, 1)
    first_half = (lane & (DA_QK // 2)) == 0
    rot = functools.partial(_rotate, cosf=cosf, sinf=sinf, first_half=first_half)

    for j in range(PAIRS):
        t = y[:, C_DAQ + LANES * j:C_DAQ + LANES * (j + 1)]
        dq_ref[0, LANES * j:LANES * (j + 1), :] = (rot(t) * DA_QSCALE).T.astype(BF16)
        t = y[:, C_DAK + LANES * j:C_DAK + LANES * (j + 1)]
        dk_ref[0, :, LANES * j:LANES * (j + 1)] = rot(t).astype(BF16)
    _store_values_t(dv_ref, y[:, C_DAV:C_CQ])

    cq = y[:, C_CQ:C_CKV]
    ckv = y[:, C_CKV:C_KR]
    krp = y[:, C_KR:C_END]
    is_rope = (lane >= MLA_NOPE) & (lane < MLA_NOPE + MLA_ROPE)
    kr = jnp.where(is_rope, rot(krp), krp)
    qn = (cq * lax.rsqrt(jnp.mean(cq * cq, axis=-1, keepdims=True) + RMS_EPS)) * qn_ref[...]
    q = jnp.dot(qn.astype(BF16), wuq_ref[...], preferred_element_type=F32)
    kvn = ((ckv * lax.rsqrt(jnp.mean(ckv * ckv, axis=-1, keepdims=True) + RMS_EPS)) * kvn_ref[...]).astype(BF16)
    kn = jnp.dot(kvn, wuk_ref[...], preferred_element_type=F32)
    for h in range(MLA_HEADS):
        t = q[:, LANES * h:LANES * (h + 1)]
        mq_ref[0, h] = (jnp.where(is_rope, rot(t), t) * MLA_QSCALE).T.astype(BF16)
        mk_ref[0, h] = (kn[:, LANES * h:LANES * (h + 1)] + kr).astype(BF16)
    _store_values_t(mv_ref, jnp.dot(kvn, wuv_ref[...], preferred_element_type=F32))


def _in_proj(xa, modt, w1, wuq, wuk, wuv, qnorm, kvnorm, cosf, sinf, nctx):
    b, nt, d = xa.shape
    tm = TOKEN_TILE
    nc = nctx // tm
    tok = lambda w: pl.BlockSpec((1, tm, w), lambda i, t: (i, t, 0))
    head_t = lambda r: pl.BlockSpec((1, DA_HEADS, r, tm), lambda i, t: (i, 0, 0, t))
    return pl.pallas_call(
        _in_kernel,
        grid=(b, nt // tm),
        in_specs=[
            tok(d),
            pl.BlockSpec((1, 1, 8, d), lambda i, t: (i, jnp.where(t >= nc, 1, 0), 0, 0)),
            _const_spec(w1.shape), _const_spec(wuq.shape), _const_spec(wuk.shape), _const_spec(wuv.shape),
            _const_spec(qnorm.shape), _const_spec(kvnorm.shape),
            pl.BlockSpec((tm, LANES), lambda i, t: (t, 0)),
            pl.BlockSpec((tm, LANES), lambda i, t: (t, 0)),
        ],
        out_specs=[
            tok(LRU_WIDTH), tok(LRU_WIDTH),
            pl.BlockSpec((1, DA_WIDTH, tm), lambda i, t: (i, 0, t)), tok(DA_WIDTH), head_t(VT_ROWS),
            head_t(LANES), pl.BlockSpec((1, MLA_HEADS, tm, LANES), lambda i, t: (i, 0, t, 0)), head_t(VT_ROWS),
        ],
        out_shape=[
            jax.ShapeDtypeStruct((b, nt, LRU_WIDTH), F32),
            jax.ShapeDtypeStruct((b, nt, LRU_WIDTH), F32),
            jax.ShapeDtypeStruct((b, DA_WIDTH, nt), BF16),
            jax.ShapeDtypeStruct((b, nt, DA_WIDTH), BF16),
            jax.ShapeDtypeStruct((b, DA_HEADS, VT_ROWS, nt), BF16),
            jax.ShapeDtypeStruct((b, MLA_HEADS, LANES, nt), BF16),
            jax.ShapeDtypeStruct((b, MLA_HEADS, nt, LANES), BF16),
            jax.ShapeDtypeStruct((b, MLA_HEADS, VT_ROWS, nt), BF16),
        ],
        compiler_params=_params(("parallel", "parallel")),
        name="in_proj",
    )(xa, modt, w1, wuq, wuk, wuv, qnorm, kvnorm, cosf, sinf)


def _gelu_tanh(x):
    return 0.5 * x * (1.0 + jnp.tanh(math.sqrt(2.0 / math.pi) * (x + 0.044715 * (x * x * x))))


def _lru_kernel(x_ref, g_ref, cw_ref, cb_ref, wa_ref, wi_ref, ba_ref, bi_ref, lam_ref, o_ref,
                y_s, a_s, s_s, h_s, *, nt, nctx, chunk):
    w = LRU_WIDTH
    tiles = chunk // SUBLANES
    n_chunks = nt // chunk
    sub = lax.broadcasted_iota(jnp.int32, (tiles, SUBLANES, w), 1)
    tile_i = lax.broadcasted_iota(jnp.int32, (tiles, SUBLANES, w), 0)

    def conv_chunk(c, carry):
        r0 = pl.multiple_of(c * chunk, chunk)
        lo = pl.multiple_of(jnp.maximum(r0 - SUBLANES, 0), SUBLANES)
        hi = pl.multiple_of(jnp.minimum(r0 + chunk, nt - SUBLANES), SUBLANES)
        x3 = jnp.concatenate([x_ref[0, pl.ds(lo, SUBLANES), :], x_ref[0, pl.ds(r0, chunk), :],
                              x_ref[0, pl.ds(hi, SUBLANES), :]], axis=0).reshape(tiles + 2, SUBLANES, w)
        sh1 = pltpu.roll(x3, 1, 1)
        sh2 = pltpu.roll(x3, 2, 1)
        sh7 = pltpu.roll(x3, SUBLANES - 1, 1)
        pos = r0 + tile_i * SUBLANES + sub
        in_ctx = pos < nctx
        seg_pos = jnp.where(in_ctx, pos, pos - nctx)
        seg_last = jnp.where(in_ctx, nctx - 1, nt - nctx - 1)
        zero = jnp.zeros((tiles, SUBLANES, w), F32)
        xm2 = jnp.where(seg_pos >= 2, jnp.where(sub >= 2, sh2[1:-1], sh2[0:-2]), zero)
        xm1 = jnp.where(seg_pos >= 1, jnp.where(sub >= 1, sh1[1:-1], sh1[0:-2]), zero)
        xp1 = jnp.where(seg_pos < seg_last, jnp.where(sub < SUBLANES - 1, sh7[1:-1], sh7[2:]), zero)
        y = cb_ref[...] + xm2 * cw_ref[0:1] + xm1 * cw_ref[1:2] + x3[1:-1] * cw_ref[2:3] + xp1 * cw_ref[3:4]
        y_s[pl.ds(r0, chunk), :] = y.reshape(chunk, w)
        return carry

    lax.fori_loop(0, n_chunks, conv_chunk, 0)

    nctx_t = nctx // SUBLANES
    nt_t = nt // SUBLANES

    for d in range(2):
        nlam = -lam_ref[d:d + 1]
        softplus = jnp.maximum(nlam, 0.0) + jnp.log1p(jnp.exp(-jnp.abs(nlam)))
        c8 = -LRU_C * softplus

        def gate_chunk(c, carry, d=d, c8=c8):
            r0 = pl.multiple_of(c * chunk, chunk)
            y = y_s[pl.ds(r0, chunk), :]
            yb = y.astype(BF16)
            r = jax.nn.sigmoid(jnp.dot(yb, wa_ref[d], preferred_element_type=F32) + ba_ref[d:d + 1])
            i = jax.nn.sigmoid(jnp.dot(yb, wi_ref[d], preferred_element_type=F32) + bi_ref[d:d + 1])
            log_a = c8 * r
            a = jnp.exp(log_a)
            th = jnp.tanh(log_a)
            u = jnp.sqrt(-2.0 * th / (1.0 - th)) * (i * y)
            a3 = a.reshape(tiles, SUBLANES, w)
            u3 = u.reshape(tiles, SUBLANES, w)
            for sft in (1, 2, 4):
                if d == 0:
                    ok = sub >= sft
                    ash = pltpu.roll(a3, sft, 1)
                    ush = pltpu.roll(u3, sft, 1)
                else:
                    ok = sub < SUBLANES - sft
                    ash = pltpu.roll(a3, SUBLANES - sft, 1)
                    ush = pltpu.roll(u3, SUBLANES - sft, 1)
                u3 = jnp.where(ok, a3 * ush + u3, u3)
                a3 = jnp.where(ok, a3 * ash, a3)
            a_s[pl.ds(r0, chunk), :] = a3.reshape(chunk, w)
            s_s[pl.ds(r0, chunk), :] = u3.reshape(chunk, w)
            return carry

        lax.fori_loop(0, n_chunks, gate_chunk, 0)

        def carry_tile(j, hprev, d=d):
            if d == 0:
                t = j
            else:
                t = jnp.where(j < nctx_t, nctx_t - 1 - j, nt_t - 1 - (j - nctx_t))
            r0 = pl.multiple_of(t * SUBLANES, SUBLANES)
            h = a_s[pl.ds(r0, SUBLANES), :] * hprev + s_s[pl.ds(r0, SUBLANES), :]
            if d == 0:
                h_s[pl.ds(r0, SUBLANES), :] = h
                return h[SUBLANES - 1:SUBLANES]
            h_s[pl.ds(r0, SUBLANES), :] = h_s[pl.ds(r0, SUBLANES), :] + h
            return h[0:1]

        lax.fori_loop(0, nt_t, carry_tile, jnp.zeros((1, w), F32), unroll=4)

    def out_chunk(c, carry):
        r0 = pl.multiple_of(c * chunk, chunk)
        o_ref[0, pl.ds(r0, chunk), :] = (h_s[pl.ds(r0, chunk), :] * _gelu_tanh(g_ref[0, pl.ds(r0, chunk), :])).astype(BF16)
        return carry

    lax.fori_loop(0, n_chunks, out_chunk, 0)


def _lru(lx, lg, conv_w, conv_b, wa, wi, ba, bi, lam, nctx):
    b, nt, w = lx.shape
    chunk = TOKEN_TILE
    seq = pl.BlockSpec((1, nt, w), lambda i: (i, 0, 0))
    return pl.pallas_call(
        functools.partial(_lru_kernel, nt=nt, nctx=nctx, chunk=chunk),
        grid=(b,),
        in_specs=[seq, seq, _const_spec(conv_w.shape), _const_spec(conv_b.shape), _const_spec(wa.shape),
                  _const_spec(wi.shape), _const_spec(ba.shape), _const_spec(bi.shape), _const_spec(lam.shape)],
        out_specs=seq,
        out_shape=jax.ShapeDtypeStruct((b, nt, w), BF16),
        scratch_shapes=[pltpu.VMEM((nt, w), F32)] * 4,
        compiler_params=_params(("parallel",)),
        name="rglru",
    )(lx, lg, conv_w, conv_b, wa, wi, ba, bi, lam)


DA_KEY_CHUNK = 256
MLA_KEY_CHUNK = 256


def _key_chunks(nk, nctx, size):
    chunks = [(0, nctx)]
    chunks += [(s, min(size, nk - s)) for s in range(nctx, nk, size)]
    return chunks


def _attend_t(chains, chunks):
    def scores(n, ci):
        q_t, key, _ = chains[n]
        return jnp.dot(key(*chunks[ci]), q_t, preferred_element_type=F32)

    s = [scores(n, 0) for n in range(len(chains))]
    state = [None] * len(chains)
    for ci in range(len(chunks)):
        for n, (_, _, value_t) in enumerate(chains):
            cm = jnp.max(s[n], axis=0, keepdims=True)
            if ci == 0:
                m_new = cm
            else:
                m_old, acc = state[n]
                m_new = jnp.maximum(m_old, cm)
            p = jnp.exp2(s[n] - m_new).astype(BF16)
            if ci + 1 < len(chunks):
                s[n] = scores(n, ci + 1)
            pv = jnp.dot(value_t(*chunks[ci]), p, preferred_element_type=F32)
            if ci > 0:
                pv = acc * jnp.exp2(m_old - m_new) + pv
            state[n] = (m_new, pv)
    return [acc[0:DA_V] / acc[DA_V:DA_V + 1] for _, acc in state]


def _da_kernel(q_ref, k_ref, vt_ref, dl_ref, g_ref, li_ref, o_ref, *, nt, nctx, tq):
    half = pl.program_id(1)
    row = lax.broadcasted_iota(jnp.int32, (LANES, tq), 0)
    lane = lax.broadcasted_iota(jnp.int32, (tq, LANES), 1)
    dl = dl_ref[...]
    lam_init = li_ref[...]
    lam = (jnp.exp(jnp.sum(dl[0:1] * dl[1:2], axis=-1, keepdims=True))
           - jnp.exp(jnp.sum(dl[2:3] * dl[3:4], axis=-1, keepdims=True)) + lam_init)
    zero = jnp.zeros((LANES, tq), BF16)

    def attend(q0, nk):
        chains = []
        for j in range(PAIRS):
            q_t = q_ref[0, LANES * j:LANES * (j + 1), pl.ds(q0, tq)]
            key = lambda start, size, j=j: k_ref[0, start:start + size, LANES * j:LANES * (j + 1)]
            value_t = lambda start, size, j=j: vt_ref[0, 2 * j + half, :, start:start + size]
            for mi in range(2):
                lo = half * DA_V + mi * DA_QK
                chains.append((jnp.where((row >= lo) & (row < lo + DA_QK), q_t, zero), key, value_t))
        o = _attend_t(chains, _key_chunks(nk, nctx, DA_KEY_CHUNK))
        for j in range(PAIRS):
            d = o[2 * j] - lam * o[2 * j + 1]
            d = d * lax.rsqrt(jnp.mean(d * d, axis=0, keepdims=True) + RMS_EPS)
            both = jnp.concatenate([d, d], axis=0).T
            new = (both * g_ref[...] * (1.0 - lam_init)).astype(BF16)
            slab = (0, pl.ds(q0, tq), slice(LANES * j, LANES * (j + 1)))

            @pl.when(half == 0)
            def _():
                o_ref[slab] = new

            @pl.when(half == 1)
            def _():
                o_ref[slab] = jnp.where(lane >= DA_V, new, o_ref[slab])

    _for_query_blocks(attend, nt, nctx, tq)


def _for_query_blocks(attend, nt, nctx, tq):
    for t in range(nctx // tq):
        attend(t * tq, nctx)

    def latent_block(t, carry):
        attend(pl.multiple_of(nctx + t * tq, tq), nt)
        return carry

    lax.fori_loop(0, (nt - nctx) // tq, latent_block, 0)


def _da_attn(dq_t, dk, dv_t, dlam, gpair, lam_init, nctx):
    b, nt, _ = dk.shape
    tq = TOKEN_TILE
    return pl.pallas_call(
        functools.partial(_da_kernel, nt=nt, nctx=nctx, tq=tq),
        grid=(b, 2),
        in_specs=[
            pl.BlockSpec((1, DA_WIDTH, nt), lambda i, h: (i, 0, 0)),
            pl.BlockSpec((1, nt, DA_WIDTH), lambda i, h: (i, 0, 0)),
            pl.BlockSpec((1, DA_HEADS, VT_ROWS, nt), lambda i, h: (i, 0, 0, 0)),
            _const_spec(dlam.shape), _const_spec(gpair.shape), _const_spec(lam_init.shape),
        ],
        out_specs=pl.BlockSpec((1, nt, DA_WIDTH), lambda i, h: (i, 0, 0)),
        out_shape=jax.ShapeDtypeStruct((b, nt, DA_WIDTH), BF16),
        compiler_params=_params(("parallel", "arbitrary")),
        name="diff_attn",
    )(dq_t, dk, dv_t, dlam, gpair, lam_init)


def _mla_kernel(q_ref, k_ref, vt_ref, o_ref, *, nt, nctx, tq):
    def attend(q0, nk):
        chains = []
        for hh in range(MLA_HEADS):
            key = lambda start, size, hh=hh: k_ref[0, hh, start:start + size, :]
            value_t = lambda start, size, hh=hh: vt_ref[0, hh, :, start:start + size]
            chains.append((q_ref[0, hh, :, pl.ds(q0, tq)], key, value_t))
        o = _attend_t(chains, _key_chunks(nk, nctx, MLA_KEY_CHUNK))
        for j in range(MLA_HEADS // 2):
            o_ref[0, pl.ds(q0, tq), LANES * j:LANES * (j + 1)] = jnp.concatenate(o[2 * j:2 * j + 2], axis=0).T.astype(BF16)

    _for_query_blocks(attend, nt, nctx, tq)


def _mla_attn(mq_t, mk, mv_t, nctx):
    b, nh, nt, _ = mk.shape
    tq = TOKEN_TILE
    return pl.pallas_call(
        functools.partial(_mla_kernel, nt=nt, nctx=nctx, tq=tq),
        grid=(b,),
        in_specs=[
            pl.BlockSpec((1, nh, LANES, nt), lambda i: (i, 0, 0, 0)),
            pl.BlockSpec((1, nh, nt, LANES), lambda i: (i, 0, 0, 0)),
            pl.BlockSpec((1, nh, VT_ROWS, nt), lambda i: (i, 0, 0, 0)),
        ],
        out_specs=pl.BlockSpec((1, nt, nh * MLA_V), lambda i: (i, 0, 0)),
        out_shape=jax.ShapeDtypeStruct((b, nt, nh * MLA_V), BF16),
        compiler_params=_params(("parallel",)),
        name="mla_attn",
    )(mq_t, mk, mv_t)


def _layer_norm(z, g, b):
    mu = jnp.mean(z, axis=-1, keepdims=True)
    zc = z - mu
    var = jnp.mean(zc * zc, axis=-1, keepdims=True)
    return (zc * lax.rsqrt(var + LN_EPS)) * g + b


def _router_gates(logits, rb):
    scores = jax.nn.sigmoid(logits)
    sel = scores + rb
    lane = lax.broadcasted_iota(jnp.int32, logits.shape, 1)
    r = lane & (EXPERTS_PER_GROUP - 1)
    grp = (lane >> 2) & (N_GROUPS - 1)

    def in_group(x, k):
        return jnp.where(r >= k, pltpu.roll(x, k, 1), pltpu.roll(x, LANES - EXPERTS_PER_GROUP + k, 1))

    others = [in_group(sel, k) for k in (1, 2, 3)]
    pair_max = sel + jnp.maximum(jnp.maximum(others[0], others[1]), others[2])
    grp_score = jnp.maximum(jnp.maximum(pair_max, in_group(pair_max, 1)),
                            jnp.maximum(in_group(pair_max, 2), in_group(pair_max, 3)))
    in_best = None
    for k in (1, 2, 3):
        other = pltpu.roll(grp_score, EXPERTS_PER_GROUP * k, 1)
        wins = (grp_score > other) | ((grp_score == other) & (grp < k))
        in_best = wins if in_best is None else (in_best & wins)
    beaten = jnp.zeros(logits.shape, F32)
    for k, o in zip((1, 2, 3), others):
        beats = (o > sel) | ((o == sel) & (r >= k))
        beaten = beaten + jnp.where(beats, 1.0, 0.0)
    chosen = in_best & (beaten < 2.0)
    sc = jnp.where(chosen, scores, 0.0)
    tot = sc + in_group(sc, 1) + in_group(sc, 2) + in_group(sc, 3)
    return jnp.where(chosen, sc / tot, 0.0), in_best


ROW_BLOCK = 128


def _grouped_experts(v, gates, in_best, w1_ref, w3_ref, w2_ref, xs_ref, gs_ref, ys_ref):
    rows, d = v.shape
    per = EXPERTS_PER_GROUP * D_EXPERT
    lane = lax.broadcasted_iota(jnp.int32, (rows, LANES), 1)
    gsel = jnp.where(in_best & ((lane & (EXPERTS_PER_GROUP - 1)) == 0) & (lane < N_EXPERTS), 1.0, 0.0)
    gsel_b = gsel.astype(BF16)
    ri = lax.broadcasted_iota(jnp.int32, (rows, rows), 0)
    ci = lax.broadcasted_iota(jnp.int32, (rows, rows), 1)
    onehot = lambda cond: jnp.where(cond, 1.0, 0.0).astype(BF16)

    tot = jnp.sum(gsel, axis=0, keepdims=True)
    lane1 = lax.broadcasted_iota(jnp.int32, (1, LANES), 1)
    cnt = [jnp.sum(jnp.where(lane1 == EXPERTS_PER_GROUP * g, tot, 0.0)).astype(jnp.int32) for g in range(N_GROUPS)]
    off = [jnp.int32(0)]
    for g in range(1, N_GROUPS):
        off.append(off[-1] + cnt[g - 1])

    off_lane = sum(jnp.where(lane1 == EXPERTS_PER_GROUP * g, off[g].astype(F32), 0.0) for g in range(N_GROUPS))
    before = jnp.dot(onehot(ci < ri), gsel_b, preferred_element_type=F32)
    pos_col = jnp.sum(gsel * (before + off_lane), axis=1, keepdims=True)
    sub8 = lax.broadcasted_iota(jnp.int32, (SUBLANES, LANES), 0)
    lane8 = lax.broadcasted_iota(jnp.int32, (SUBLANES, LANES), 1)
    pick = jnp.where((lane8 == EXPERTS_PER_GROUP * sub8) & (sub8 < N_GROUPS), 1.0, 0.0).astype(BF16)
    gsel_t = lax.dot_general(pick, gsel_b, (((1,), (1,)), ((), ())), preferred_element_type=F32)
    before_t = jnp.dot(gsel_t.astype(BF16), onehot(ri < ci), preferred_element_type=F32)
    sub_col = lax.broadcasted_iota(jnp.int32, (SUBLANES, 1), 0)
    off_sub = sum(jnp.where(sub_col == g, off[g].astype(F32), 0.0) for g in range(N_GROUPS))
    pos_row = jnp.sum(gsel_t * (before_t + off_sub), axis=0, keepdims=True)
    perm = onehot(pos_row == ri.astype(F32))
    perm_t = onehot(pos_col == ci.astype(F32))

    xs_ref[...] = jnp.dot(perm, v, preferred_element_type=F32).astype(BF16)
    g_hi = gates.astype(BF16)
    g_lo = (gates - g_hi.astype(F32)).astype(BF16)
    gs_ref[...] = jnp.dot(perm, g_hi, preferred_element_type=F32) + jnp.dot(perm, g_lo, preferred_element_type=F32)
    ys_ref[...] = jnp.zeros(ys_ref.shape, F32)

    for b in range(rows // ROW_BLOCK):
        lo, hi = b * ROW_BLOCK, (b + 1) * ROW_BLOCK
        for g in range(N_GROUPS):
            @pl.when((off[g] < hi) & (off[g] + cnt[g] > lo))
            def _(lo=lo, hi=hi, g=g):
                xb = xs_ref[lo:hi, :]
                gb = gs_ref[lo:hi, :]
                h1 = jnp.dot(xb, w1_ref[:, g * per:(g + 1) * per], preferred_element_type=F32)
                h3 = jnp.dot(xb, w3_ref[:, g * per:(g + 1) * per], preferred_element_type=F32)
                hh = (h1 * jax.nn.sigmoid(h1)) * h3
                parts = []
                for j in range(EXPERTS_PER_GROUP):
                    e = g * EXPERTS_PER_GROUP + j
                    parts.append((hh[:, j * D_EXPERT:(j + 1) * D_EXPERT] * gb[:, e:e + 1]).astype(BF16))
                ys_ref[lo:hi, :] += jnp.dot(jnp.concatenate(parts, axis=-1), w2_ref[g * per:(g + 1) * per, :],
                                            preferred_element_type=F32)

    return jnp.dot(perm_t, ys_ref[...].astype(BF16), preferred_element_type=F32)


def _post_kernel(x_ref, mod_ref, lru_ref, da_ref, mla_ref, wo_ref, g1_ref, b1_ref,
                 rw_ref, rb_ref, w1_ref, w3_ref, w2_ref, g_ref, b_ref, o_ref, xs_ref, gs_ref, ys_ref, *, alpha):
    nb, tm, d = x_ref.shape
    rows = nb * tm
    mod = mod_ref[:, 0]
    per_row = lambda k: jnp.broadcast_to(mod[:, k:k + 1], (nb, tm, d)).reshape(rows, d)
    a = jnp.concatenate([lru_ref[...], da_ref[...], mla_ref[...]], axis=-1).reshape(rows, -1)
    o = jnp.dot(a, wo_ref[...], preferred_element_type=F32)
    x1 = _layer_norm(alpha * x_ref[...].reshape(rows, d) + per_row(2) * o, g1_ref[...], b1_ref[...])
    v = (x1 * (1.0 + per_row(4)) + per_row(3)).astype(BF16)
    gates, in_best = _router_gates(jnp.dot(v, rw_ref[...], preferred_element_type=F32), rb_ref[...])
    f = _grouped_experts(v, gates, in_best, w1_ref, w3_ref, w2_ref, xs_ref, gs_ref, ys_ref)
    o_ref[...] = _layer_norm(alpha * x1 + per_row(5) * f, g_ref[...], b_ref[...]).reshape(nb, tm, d)


POST_BATCH = 2


def _post(xa, modt, lru_o, da_o, mla_o, wo, g1, b1, rw, rb, w1c, w3c, w2c, g2, b2, nctx, alpha, latent_only):
    b, nt, d = xa.shape
    tm = TOKEN_TILE
    nc = nctx // tm
    skip = nc if latent_only else 0
    nb = POST_BATCH if b % POST_BATCH == 0 else 1
    tok = lambda w: pl.BlockSpec((nb, tm, w), lambda i, t: (i, t + skip, 0))
    consts = (wo, g1, b1, rw, rb, w1c, w3c, w2c, g2, b2)
    return pl.pallas_call(
        functools.partial(_post_kernel, alpha=alpha),
        grid=(b // nb, nt // tm - skip),
        in_specs=[
            tok(d),
            pl.BlockSpec((nb, 1, 8, d), lambda i, t: (i, jnp.where(t + skip >= nc, 1, 0), 0, 0)),
            tok(LRU_WIDTH), tok(da_o.shape[-1]), tok(mla_o.shape[-1]),
        ] + [_const_spec(c.shape) for c in consts],
        out_specs=pl.BlockSpec((nb, tm, d), lambda i, t: (i, t, 0)),
        out_shape=jax.ShapeDtypeStruct((b, nt - skip * tm, d), F32),
        scratch_shapes=[pltpu.VMEM((nb * tm, d), BF16), pltpu.VMEM((nb * tm, LANES), F32),
                        pltpu.VMEM((nb * tm, d), F32)],
        compiler_params=_params(("parallel", "parallel")),
        name="post",
    )(xa, modt, lru_o, da_o, mla_o, *consts)


def _rotary_tables(n, nctx):
    rows = n // GRID_W
    row = jnp.repeat(jnp.arange(rows), GRID_W).astype(F32)
    col = jnp.tile(jnp.arange(GRID_W), rows).astype(F32)
    n_freq = DA_QK // 4
    inv = ROPE_THETA ** (-jnp.arange(n_freq, dtype=F32) / n_freq)
    ang = jnp.concatenate([row[:, None] * inv, col[:, None] * inv], axis=-1)
    ang = jnp.concatenate([jnp.zeros((nctx, DA_QK // 2), F32), ang], axis=0)
    c, s = jnp.cos(ang), jnp.sin(ang)
    reps = LANES // DA_QK
    return jnp.tile(jnp.concatenate([c, c], axis=-1), (1, reps)), jnp.tile(jnp.concatenate([-s, s], axis=-1), (1, reps))


def _pack_in_weight(w_in):
    d = w_in.shape[0]
    wkr = w_in[:, C_KR:C_KR + MLA_ROPE]
    krp = jnp.concatenate([jnp.zeros((d, MLA_NOPE), F32), wkr, jnp.zeros((d, LANES - MLA_NOPE - MLA_ROPE), F32)], axis=-1)
    return jnp.concatenate([w_in[:, :C_KR], krp], axis=-1).astype(BF16)


def _pack_uq(w_uq):
    r = w_uq.shape[0]
    w = w_uq.reshape(r, MLA_HEADS, MLA_NOPE + MLA_ROPE)
    w = jnp.concatenate([w, jnp.zeros((r, MLA_HEADS, LANES - MLA_NOPE - MLA_ROPE), F32)], axis=-1)
    return w.reshape(r, MLA_HEADS * LANES).astype(BF16)


def _pack_ukv(w_ukv):
    r = w_ukv.shape[0]
    w = w_ukv.reshape(r, MLA_HEADS, MLA_NOPE + MLA_V)
    z = jnp.zeros((r, MLA_HEADS, LANES - MLA_NOPE), F32)
    wk = jnp.concatenate([w[..., :MLA_NOPE], z], axis=-1).reshape(r, MLA_HEADS * LANES)
    wv = w[..., MLA_NOPE:].reshape(r, MLA_HEADS * MLA_V)
    return wk.astype(BF16), wv.astype(BF16)


def _block_diag(w):
    nd, nb, bs, _ = w.shape
    eye = jnp.eye(nb, dtype=w.dtype)
    return jnp.einsum('dhij,hg->dhigj', w, eye).reshape(nd, nb * bs, nb * bs)


def kernel(x, c, ctx, c_ctx, w_mod, b_mod, w_in, w_out, conv_w, conv_b, lru_wa, lru_ba, lru_wi, lru_bi, lru_lambda, diff_lambda, diff_norm, mla_q_norm, mla_kv_norm, mla_w_uq, mla_w_ukv, ln1_g, ln1_b, ln2_g, ln2_b, router_w, router_b, exp_w1, exp_w3, exp_w2):
    bsz, n, d = x.shape
    nctx = ctx.shape[1]
    depth = w_mod.shape[0]
    alpha = (2 * depth) ** 0.25
    assert nctx % TOKEN_TILE == 0 and n % TOKEN_TILE == 0 and n % GRID_W == 0

    rows = -(-(bsz + 1) // SUBLANES) * SUBLANES
    cc = jnp.concatenate([c, c_ctx[None, :], jnp.zeros((rows - bsz - 1, d), F32)], axis=0)
    mod = _modulation(cc, w_mod, b_mod).reshape(depth, rows, 6, d)
    mod = jnp.pad(mod, ((0, 0), (0, 0), (0, 2), (0, 0)))
    mod_ctx = jnp.broadcast_to(mod[:, bsz][:, None], (depth, bsz, 8, d))
    modt = jnp.stack([mod_ctx, mod[:, :bsz]], axis=2)

    cosf, sinf = _rotary_tables(n, nctx)
    rw = jnp.tile(router_w, (1, LANES // N_EXPERTS)).astype(BF16)
    rb = jnp.tile(router_b, LANES // N_EXPERTS)[None, :].astype(F32)
    gpair = jnp.tile(diff_norm, (1, LANES // DA_V))

    xa = jnp.concatenate([ctx, x], axis=1)
    for l in range(depth):
        lam_init = jnp.full((1, 1), 0.8 - 0.6 * math.exp(-0.3 * l), F32)
        wuk, wuv = _pack_ukv(mla_w_ukv[l])
        lx, lg, dq_t, dk, dv_t, mq_t, mk, mv_t = _in_proj(
            xa, modt[l], _pack_in_weight(w_in[l]), _pack_uq(mla_w_uq[l]), wuk, wuv,
            mla_q_norm[l][None, :], mla_kv_norm[l][None, :], cosf, sinf, nctx)
        lru_o = _lru(lx, lg, conv_w[l], conv_b[l][None, :], _block_diag(lru_wa[l]).astype(BF16),
                     _block_diag(lru_wi[l]).astype(BF16), lru_ba[l], lru_bi[l], lru_lambda[l], nctx)
        da_o = _da_attn(dq_t, dk, dv_t, diff_lambda[l], gpair[l][None, :], lam_init, nctx)
        mla_o = _mla_attn(mq_t, mk, mv_t, nctx)
        w1c = exp_w1[l].transpose(1, 0, 2).reshape(d, N_EXPERTS * D_EXPERT).astype(BF16)
        w3c = exp_w3[l].transpose(1, 0, 2).reshape(d, N_EXPERTS * D_EXPERT).astype(BF16)
        w2c = exp_w2[l].reshape(N_EXPERTS * D_EXPERT, d).astype(BF16)
        xa = _post(xa, modt[l], lru_o, da_o, mla_o, w_out[l].astype(BF16), ln1_g[l][None, :], ln1_b[l][None, :],
                   rw, rb, w1c, w3c, w2c, ln2_g[l][None, :], ln2_b[l][None, :], nctx, alpha,
                   latent_only=(l == depth - 1))
    return xa
```

```python
import functools
import math

import jax
import jax.numpy as jnp
from jax import lax
from jax.experimental import pallas as pl
from jax.experimental.pallas import tpu as pltpu

F32 = jnp.float32
BF16 = jnp.bfloat16

GRID_W = 64
LRU_WIDTH = 256
LRU_BLOCKS = 4
CONV_W = 4
LRU_C = 8.0
DA_HEADS = 6
DA_QK = 32
DA_V = 2 * DA_QK
MLA_HEADS = 6
MLA_NOPE = 64
MLA_ROPE = 32
MLA_V = 64
Q_RANK = 256
KV_RANK = 128
MLA_SCALE = (MLA_NOPE + MLA_ROPE) ** -0.5
N_EXPERTS = 16
N_GROUPS = 4
EXPERTS_PER_GROUP = N_EXPERTS // N_GROUPS
D_EXPERT = 256
ROPE_THETA = 10000.0
LN_EPS = 1e-5
RMS_EPS = 1e-6

LANES = 128
SUBLANES = 8
TOKEN_TILE = 256
VMEM_LIMIT = 56 * 1024 * 1024

LOG2E = math.log2(math.e)
DA_QSCALE = DA_QK ** -0.5 * LOG2E
MLA_QSCALE = MLA_SCALE * LOG2E

DA_WIDTH = DA_HEADS * DA_V
C_CQ = 0
C_CKV = C_CQ + Q_RANK
C_KR = C_CKV + KV_RANK
C_DAQ = C_KR + LANES
C_DAK = C_DAQ + DA_WIDTH
C_DAV = C_DAK + DA_WIDTH
C_LRU = C_DAV + DA_WIDTH
C_END = C_LRU + 2 * LRU_WIDTH
PAIRS = DA_HEADS // 2
VT_ROWS = DA_V + 16


def _params(sem):
    return pltpu.CompilerParams(dimension_semantics=sem, vmem_limit_bytes=VMEM_LIMIT)


def _const_spec(shape):
    nd = len(shape)
    return pl.BlockSpec(shape, lambda *_: (0,) * nd, pipeline_mode=pl.Buffered(1))


def _mod_kernel(c_ref, w_ref, b_ref, o_ref):
    c = c_ref[...]
    s = c * jax.nn.sigmoid(c)
    o_ref[0] = jnp.dot(s.astype(BF16), w_ref[0].astype(BF16), preferred_element_type=F32) + b_ref[0]


def _modulation(cc, w_mod, b_mod):
    depth, d, d6 = w_mod.shape
    r = cc.shape[0]
    tn = min(d6, 1536)
    return pl.pallas_call(
        _mod_kernel,
        grid=(depth, d6 // tn),
        in_specs=[
            pl.BlockSpec((r, d), lambda l, j: (0, 0)),
            pl.BlockSpec((1, d, tn), lambda l, j: (l, 0, j)),
            pl.BlockSpec((1, 1, tn), lambda l, j: (l, 0, j)),
        ],
        out_specs=pl.BlockSpec((1, r, tn), lambda l, j: (l, 0, j)),
        out_shape=jax.ShapeDtypeStruct((depth, r, d6), F32),
        compiler_params=_params(("parallel", "parallel")),
        name="modulation",
    )(cc, w_mod, b_mod.reshape(depth, 1, d6))


def _rotate(t, cosf, sinf, first_half):
    partner = jnp.where(first_half, pltpu.roll(t, LANES - DA_QK // 2, 1), pltpu.roll(t, DA_QK // 2, 1))
    return t * cosf + partner * sinf


def _store_values_t(vt_ref, v):
    rows = v.shape[0]
    ones = jnp.ones((VT_ROWS - DA_V, rows), BF16)
    for j in range(PAIRS):
        t = v[:, LANES * j:LANES * (j + 1)].T.astype(BF16)
        for k in range(2):
            vt_ref[0, 2 * j + k, 0:DA_V, :] = t[DA_V * k:DA_V * (k + 1)]
            vt_ref[0, 2 * j + k, DA_V:VT_ROWS, :] = ones


def _in_kernel(x_ref, mod_ref, w1_ref, wuq_ref, wuk_ref, wuv_ref, qn_ref, kvn_ref, cos_ref, sin_ref,
               lx_ref, lg_ref, dq_ref, dk_ref, dv_ref, mq_ref, mk_ref, mv_ref):
    x = x_ref[0]
    mod = mod_ref[0, 0]
    u = (x * (1.0 + mod[1:2]) + mod[0:1]).astype(BF16)
    proj = lambda lo, hi: jnp.dot(u, w1_ref[:, lo:hi], preferred_element_type=F32)

    cosf = cos_ref[...]
    sinf = sin_ref[...]
    lane = lax.broadcasted_iota(jnp.int32, cosf.shape, 1)
    first_half = (lane & (DA_QK // 2)) == 0
    rot = functools.partial(_rotate, cosf=cosf, sinf=sinf, first_half=first_half)
    is_rope = (lane >= MLA_NOPE) & (lane < MLA_NOPE + MLA_ROPE)

    y_mla = proj(C_CQ, C_DAQ)
    y_da = proj(C_DAQ, C_DAV)

    cq = y_mla[:, C_CQ:C_CKV]
    ckv = y_mla[:, C_CKV:C_KR]
    krp = y_mla[:, C_KR:C_DAQ]
    qn = (cq * lax.rsqrt(jnp.mean(cq * cq, axis=-1, keepdims=True) + RMS_EPS)) * qn_ref[...]
    kvn = ((ckv * lax.rsqrt(jnp.mean(ckv * ckv, axis=-1, keepdims=True) + RMS_EPS)) * kvn_ref[...]).astype(BF16)
    q = jnp.dot(qn.astype(BF16), wuq_ref[...], preferred_element_type=F32)
    kn = jnp.dot(kvn, wuk_ref[...], preferred_element_type=F32)

    for j in range(PAIRS):
        t = y_da[:, LANES * j:LANES * (j + 1)]
        dq_ref[0, LANES * j:LANES * (j + 1), :] = (rot(t) * DA_QSCALE).T.astype(BF16)
        t = y_da[:, DA_WIDTH + LANES * j:DA_WIDTH + LANES * (j + 1)]
        dk_ref[0, :, LANES * j:LANES * (j + 1)] = rot(t).astype(BF16)

    y_rest = proj(C_DAV, C_END)
    mv = jnp.dot(kvn, wuv_ref[...], preferred_element_type=F32)

    kr = jnp.where(is_rope, rot(krp), krp)
    for h in range(MLA_HEADS):
        t = q[:, LANES * h:LANES * (h + 1)]
        mq_ref[0, h] = (jnp.where(is_rope, rot(t), t) * MLA_QSCALE).T.astype(BF16)
        mk_ref[0, h] = (kn[:, LANES * h:LANES * (h + 1)] + kr).astype(BF16)

    _store_values_t(dv_ref, y_rest[:, 0:DA_WIDTH])
    _store_values_t(mv_ref, mv)
    lx_ref[0] = y_rest[:, DA_WIDTH:DA_WIDTH + LRU_WIDTH]
    lg_ref[0] = y_rest[:, DA_WIDTH + LRU_WIDTH:DA_WIDTH + 2 * LRU_WIDTH]


def _in_proj(xa, modt, w1, wuq, wuk, wuv, qnorm, kvnorm, cosf, sinf, nctx):
    b, nt, d = xa.shape
    tm = TOKEN_TILE
    nc = nctx // tm
    tok = lambda w: pl.BlockSpec((1, tm, w), lambda i, t: (i, t, 0))
    head_t = lambda r: pl.BlockSpec((1, DA_HEADS, r, tm), lambda i, t: (i, 0, 0, t))
    return pl.pallas_call(
        _in_kernel,
        grid=(b, nt // tm),
        in_specs=[
            tok(d),
            pl.BlockSpec((1, 1, 8, d), lambda i, t: (i, jnp.where(t >= nc, 1, 0), 0, 0)),
            _const_spec(w1.shape), _const_spec(wuq.shape), _const_spec(wuk.shape), _const_spec(wuv.shape),
            _const_spec(qnorm.shape), _const_spec(kvnorm.shape),
            pl.BlockSpec((tm, LANES), lambda i, t: (t, 0)),
            pl.BlockSpec((tm, LANES), lambda i, t: (t, 0)),
        ],
        out_specs=[
            tok(LRU_WIDTH), tok(LRU_WIDTH),
            pl.BlockSpec((1, DA_WIDTH, tm), lambda i, t: (i, 0, t)), tok(DA_WIDTH), head_t(VT_ROWS),
            head_t(LANES), pl.BlockSpec((1, MLA_HEADS, tm, LANES), lambda i, t: (i, 0, t, 0)), head_t(VT_ROWS),
        ],
        out_shape=[
            jax.ShapeDtypeStruct((b, nt, LRU_WIDTH), F32),
            jax.ShapeDtypeStruct((b, nt, LRU_WIDTH), F32),
            jax.ShapeDtypeStruct((b, DA_WIDTH, nt), BF16),
            jax.ShapeDtypeStruct((b, nt, DA_WIDTH), BF16),
            jax.ShapeDtypeStruct((b, DA_HEADS, VT_ROWS, nt), BF16),
            jax.ShapeDtypeStruct((b, MLA_HEADS, LANES, nt), BF16),
            jax.ShapeDtypeStruct((b, MLA_HEADS, nt, LANES), BF16),
            jax.ShapeDtypeStruct((b, MLA_HEADS, VT_ROWS, nt), BF16),
        ],
        compiler_params=_params(("parallel", "parallel")),
        name="in_proj",
    )(xa, modt, w1, wuq, wuk, wuv, qnorm, kvnorm, cosf, sinf)


def _gelu_tanh(x):
    return 0.5 * x * (1.0 + jnp.tanh(math.sqrt(2.0 / math.pi) * (x + 0.044715 * (x * x * x))))


def _lru_kernel(x_ref, g_ref, cw_ref, cb_ref, wa_ref, wi_ref, ba_ref, bi_ref, lam_ref, o_ref,
                y_s, a_s, s_s, h_s, *, nt, nctx, chunk):
    w = LRU_WIDTH
    tiles = chunk // SUBLANES
    n_chunks = nt // chunk
    sub = lax.broadcasted_iota(jnp.int32, (tiles, SUBLANES, w), 1)
    tile_i = lax.broadcasted_iota(jnp.int32, (tiles, SUBLANES, w), 0)

    def conv_chunk(c, carry):
        r0 = pl.multiple_of(c * chunk, chunk)
        lo = pl.multiple_of(jnp.maximum(r0 - SUBLANES, 0), SUBLANES)
        hi = pl.multiple_of(jnp.minimum(r0 + chunk, nt - SUBLANES), SUBLANES)
        x3 = jnp.concatenate([x_ref[0, pl.ds(lo, SUBLANES), :], x_ref[0, pl.ds(r0, chunk), :],
                              x_ref[0, pl.ds(hi, SUBLANES), :]], axis=0).reshape(tiles + 2, SUBLANES, w)
        sh1 = pltpu.roll(x3, 1, 1)
        sh2 = pltpu.roll(x3, 2, 1)
        sh7 = pltpu.roll(x3, SUBLANES - 1, 1)
        pos = r0 + tile_i * SUBLANES + sub
        in_ctx = pos < nctx
        seg_pos = jnp.where(in_ctx, pos, pos - nctx)
        seg_last = jnp.where(in_ctx, nctx - 1, nt - nctx - 1)
        zero = jnp.zeros((tiles, SUBLANES, w), F32)
        xm2 = jnp.where(seg_pos >= 2, jnp.where(sub >= 2, sh2[1:-1], sh2[0:-2]), zero)
        xm1 = jnp.where(seg_pos >= 1, jnp.where(sub >= 1, sh1[1:-1], sh1[0:-2]), zero)
        xp1 = jnp.where(seg_pos < seg_last, jnp.where(sub < SUBLANES - 1, sh7[1:-1], sh7[2:]), zero)
        y = cb_ref[...] + xm2 * cw_ref[0:1] + xm1 * cw_ref[1:2] + x3[1:-1] * cw_ref[2:3] + xp1 * cw_ref[3:4]
        y_s[pl.ds(r0, chunk), :] = y.reshape(chunk, w)
        return carry

    lax.fori_loop(0, n_chunks, conv_chunk, 0)

    nctx_t = nctx // SUBLANES
    nt_t = nt // SUBLANES

    for d in range(2):
        nlam = -lam_ref[d:d + 1]
        softplus = jnp.maximum(nlam, 0.0) + jnp.log1p(jnp.exp(-jnp.abs(nlam)))
        c8 = -LRU_C * softplus

        def gate_chunk(c, carry, d=d, c8=c8):
            r0 = pl.multiple_of(c * chunk, chunk)
            y = y_s[pl.ds(r0, chunk), :]
            yb = y.astype(BF16)
            r = jax.nn.sigmoid(jnp.dot(yb, wa_ref[d], preferred_element_type=F32) + ba_ref[d:d + 1])
            i = jax.nn.sigmoid(jnp.dot(yb, wi_ref[d], preferred_element_type=F32) + bi_ref[d:d + 1])
            log_a = c8 * r
            a = jnp.exp(log_a)
            th = jnp.tanh(log_a)
            u = jnp.sqrt(-2.0 * th / (1.0 - th)) * (i * y)
            a3 = a.reshape(tiles, SUBLANES, w)
            u3 = u.reshape(tiles, SUBLANES, w)
            for sft in (1, 2, 4):
                if d == 0:
                    ok = sub >= sft
                    ash = pltpu.roll(a3, sft, 1)
                    ush = pltpu.roll(u3, sft, 1)
                else:
                    ok = sub < SUBLANES - sft
                    ash = pltpu.roll(a3, SUBLANES - sft, 1)
                    ush = pltpu.roll(u3, SUBLANES - sft, 1)
                u3 = jnp.where(ok, a3 * ush + u3, u3)
                a3 = jnp.where(ok, a3 * ash, a3)
            a_s[pl.ds(r0, chunk), :] = a3.reshape(chunk, w)
            s_s[pl.ds(r0, chunk), :] = u3.reshape(chunk, w)
            return carry

        lax.fori_loop(0, n_chunks, gate_chunk, 0)

        def carry_tile(j, hprev, d=d):
            if d == 0:
                t = j
            else:
                t = jnp.where(j < nctx_t, nctx_t - 1 - j, nt_t - 1 - (j - nctx_t))
            r0 = pl.multiple_of(t * SUBLANES, SUBLANES)
            h = a_s[pl.ds(r0, SUBLANES), :] * hprev + s_s[pl.ds(r0, SUBLANES), :]
            if d == 0:
                h_s[pl.ds(r0, SUBLANES), :] = h
                return h[SUBLANES - 1:SUBLANES]
            h_s[pl.ds(r0, SUBLANES), :] = h_s[pl.ds(r0, SUBLANES), :] + h
            return h[0:1]

        lax.fori_loop(0, nt_t, carry_tile, jnp.zeros((1, w), F32), unroll=4)

    def out_chunk(c, carry):
        r0 = pl.multiple_of(c * chunk, chunk)
        o_ref[0, pl.ds(r0, chunk), :] = (h_s[pl.ds(r0, chunk), :] * _gelu_tanh(g_ref[0, pl.ds(r0, chunk), :])).astype(BF16)
        return carry

    lax.fori_loop(0, n_chunks, out_chunk, 0)


def _lru(lx, lg, conv_w, conv_b, wa, wi, ba, bi, lam, nctx):
    b, nt, w = lx.shape
    chunk = TOKEN_TILE
    seq = pl.BlockSpec((1, nt, w), lambda i: (i, 0, 0))
    return pl.pallas_call(
        functools.partial(_lru_kernel, nt=nt, nctx=nctx, chunk=chunk),
        grid=(b,),
        in_specs=[seq, seq, _const_spec(conv_w.shape), _const_spec(conv_b.shape), _const_spec(wa.shape),
                  _const_spec(wi.shape), _const_spec(ba.shape), _const_spec(bi.shape), _const_spec(lam.shape)],
        out_specs=seq,
        out_shape=jax.ShapeDtypeStruct((b, nt, w), BF16),
        scratch_shapes=[pltpu.VMEM((nt, w), F32)] * 4,
        compiler_params=_params(("parallel",)),
        name="rglru",
    )(lx, lg, conv_w, conv_b, wa, wi, ba, bi, lam)


DA_KEY_CHUNK = 256
MLA_KEY_CHUNK = 256


def _key_chunks(nk, nctx, size):
    chunks = [(0, nctx)]
    chunks += [(s, min(size, nk - s)) for s in range(nctx, nk, size)]
    return chunks


def _attend_t(chains, chunks):
    def scores(n, ci):
        q_t, key, _ = chains[n]
        return jnp.dot(key(*chunks[ci]), q_t, preferred_element_type=F32).astype(BF16)

    s = [scores(n, 0) for n in range(len(chains))]
    state = [None] * len(chains)
    for ci in range(len(chunks)):
        for n, (_, _, value_t) in enumerate(chains):
            cm = jnp.max(s[n], axis=0, keepdims=True)
            if ci == 0:
                m_new = cm
            else:
                m_old, acc = state[n]
                m_new = jnp.maximum(m_old, cm)
            p = jnp.exp2(s[n] - m_new)
            if ci + 1 < len(chunks):
                s[n] = scores(n, ci + 1)
            pv = jnp.dot(value_t(*chunks[ci]), p, preferred_element_type=F32)
            if ci > 0:
                pv = acc * jnp.exp2(m_old.astype(F32) - m_new.astype(F32)) + pv
            state[n] = (m_new, pv)
    return [acc[0:DA_V] / acc[DA_V:DA_V + 1] for _, acc in state]


def _da_kernel(q_ref, k_ref, vt_ref, dl_ref, g_ref, li_ref, o_ref, *, nt, nctx, tq):
    half = pl.program_id(1)
    row = lax.broadcasted_iota(jnp.int32, (LANES, tq), 0)
    lane = lax.broadcasted_iota(jnp.int32, (tq, LANES), 1)
    dl = dl_ref[...]
    lam_init = li_ref[...]
    lam = (jnp.exp(jnp.sum(dl[0:1] * dl[1:2], axis=-1, keepdims=True))
           - jnp.exp(jnp.sum(dl[2:3] * dl[3:4], axis=-1, keepdims=True)) + lam_init)
    zero = jnp.zeros((LANES, tq), BF16)

    def attend(q0, nk):
        chains = []
        for j in range(PAIRS):
            q_t = q_ref[0, LANES * j:LANES * (j + 1), pl.ds(q0, tq)]
            key = lambda start, size, j=j: k_ref[0, start:start + size, LANES * j:LANES * (j + 1)]
            value_t = lambda start, size, j=j: vt_ref[0, 2 * j + half, :, start:start + size]
            for mi in range(2):
                lo = half * DA_V + mi * DA_QK
                chains.append((jnp.where((row >= lo) & (row < lo + DA_QK), q_t, zero), key, value_t))
        o = _attend_t(chains, _key_chunks(nk, nctx, DA_KEY_CHUNK))
        for j in range(PAIRS):
            d = o[2 * j] - lam * o[2 * j + 1]
            d = d * lax.rsqrt(jnp.mean(d * d, axis=0, keepdims=True) + RMS_EPS)
            both = jnp.concatenate([d, d], axis=0).T
            new = (both * g_ref[...] * (1.0 - lam_init)).astype(BF16)
            slab = (0, pl.ds(q0, tq), slice(LANES * j, LANES * (j + 1)))

            @pl.when(half == 0)
            def _():
                o_ref[slab] = new

            @pl.when(half == 1)
            def _():
                o_ref[slab] = jnp.where(lane >= DA_V, new, o_ref[slab])

    _for_query_blocks(attend, nt, nctx, tq)


def _for_query_blocks(attend, nt, nctx, tq):
    for t in range(nctx // tq):
        attend(t * tq, nctx)

    def latent_block(t, carry):
        attend(pl.multiple_of(nctx + t * tq, tq), nt)
        return carry

    lax.fori_loop(0, (nt - nctx) // tq, latent_block, 0)


def _da_attn(dq_t, dk, dv_t, dlam, gpair, lam_init, nctx):
    b, nt, _ = dk.shape
    tq = TOKEN_TILE
    return pl.pallas_call(
        functools.partial(_da_kernel, nt=nt, nctx=nctx, tq=tq),
        grid=(b, 2),
        in_specs=[
            pl.BlockSpec((1, DA_WIDTH, nt), lambda i, h: (i, 0, 0)),
            pl.BlockSpec((1, nt, DA_WIDTH), lambda i, h: (i, 0, 0)),
            pl.BlockSpec((1, DA_HEADS, VT_ROWS, nt), lambda i, h: (i, 0, 0, 0)),
            _const_spec(dlam.shape), _const_spec(gpair.shape), _const_spec(lam_init.shape),
        ],
        out_specs=pl.BlockSpec((1, nt, DA_WIDTH), lambda i, h: (i, 0, 0)),
        out_shape=jax.ShapeDtypeStruct((b, nt, DA_WIDTH), BF16),
        compiler_params=_params(("parallel", "arbitrary")),
        name="diff_attn",
    )(dq_t, dk, dv_t, dlam, gpair, lam_init)


def _mla_kernel(q_ref, k_ref, vt_ref, o_ref, *, nt, nctx, tq):
    def attend(q0, nk):
        chains = []
        for hh in range(MLA_HEADS):
            key = lambda start, size, hh=hh: k_ref[0, hh, start:start + size, :]
            value_t = lambda start, size, hh=hh: vt_ref[0, hh, :, start:start + size]
            chains.append((q_ref[0, hh, :, pl.ds(q0, tq)], key, value_t))
        o = _attend_t(chains, _key_chunks(nk, nctx, MLA_KEY_CHUNK))
        for j in range(MLA_HEADS // 2):
            o_ref[0, pl.ds(q0, tq), LANES * j:LANES * (j + 1)] = jnp.concatenate(o[2 * j:2 * j + 2], axis=0).T.astype(BF16)

    _for_query_blocks(attend, nt, nctx, tq)


def _mla_attn(mq_t, mk, mv_t, nctx):
    b, nh, nt, _ = mk.shape
    tq = TOKEN_TILE
    return pl.pallas_call(
        functools.partial(_mla_kernel, nt=nt, nctx=nctx, tq=tq),
        grid=(b,),
        in_specs=[
            pl.BlockSpec((1, nh, LANES, nt), lambda i: (i, 0, 0, 0)),
            pl.BlockSpec((1, nh, nt, LANES), lambda i: (i, 0, 0, 0)),
            pl.BlockSpec((1, nh, VT_ROWS, nt), lambda i: (i, 0, 0, 0)),
        ],
        out_specs=pl.BlockSpec((1, nt, nh * MLA_V), lambda i: (i, 0, 0)),
        out_shape=jax.ShapeDtypeStruct((b, nt, nh * MLA_V), BF16),
        compiler_params=_params(("parallel",)),
        name="mla_attn",
    )(mq_t, mk, mv_t)


def _layer_norm(z, g, b):
    mu = jnp.mean(z, axis=-1, keepdims=True)
    zc = z - mu
    var = jnp.mean(zc * zc, axis=-1, keepdims=True)
    return (zc * lax.rsqrt(var + LN_EPS)) * g + b


def _router_gates(logits, rb):
    scores = jax.nn.sigmoid(logits)
    sel = scores + rb
    lane = lax.broadcasted_iota(jnp.int32, logits.shape, 1)
    r = lane & (EXPERTS_PER_GROUP - 1)
    grp = (lane >> 2) & (N_GROUPS - 1)

    def in_group(x, k):
        return jnp.where(r >= k, pltpu.roll(x, k, 1), pltpu.roll(x, LANES - EXPERTS_PER_GROUP + k, 1))

    others = [in_group(sel, k) for k in (1, 2, 3)]
    pair_max = sel + jnp.maximum(jnp.maximum(others[0], others[1]), others[2])
    grp_score = jnp.maximum(jnp.maximum(pair_max, in_group(pair_max, 1)),
                            jnp.maximum(in_group(pair_max, 2), in_group(pair_max, 3)))
    in_best = None
    for k in (1, 2, 3):
        other = pltpu.roll(grp_score, EXPERTS_PER_GROUP * k, 1)
        wins = (grp_score > other) | ((grp_score == other) & (grp < k))
        in_best = wins if in_best is None else (in_best & wins)
    beaten = jnp.zeros(logits.shape, F32)
    for k, o in zip((1, 2, 3), others):
        beats = (o > sel) | ((o == sel) & (r >= k))
        beaten = beaten + jnp.where(beats, 1.0, 0.0)
    chosen = in_best & (beaten < 2.0)
    sc = jnp.where(chosen, scores, 0.0)
    tot = sc + in_group(sc, 1) + in_group(sc, 2) + in_group(sc, 3)
    return jnp.where(chosen, sc / tot, 0.0), in_best


ROW_BLOCK = 128


def _grouped_experts(v, gates, in_best, w1_ref, w3_ref, w2_ref, xs_ref, gs_ref, ys_ref):
    rows, d = v.shape
    per = EXPERTS_PER_GROUP * D_EXPERT
    lane = lax.broadcasted_iota(jnp.int32, (rows, LANES), 1)
    gsel = jnp.where(in_best & ((lane & (EXPERTS_PER_GROUP - 1)) == 0) & (lane < N_EXPERTS), 1.0, 0.0)
    gsel_b = gsel.astype(BF16)
    ri = lax.broadcasted_iota(jnp.int32, (rows, rows), 0)
    ci = lax.broadcasted_iota(jnp.int32, (rows, rows), 1)
    onehot = lambda cond: jnp.where(cond, 1.0, 0.0).astype(BF16)

    tot = jnp.sum(gsel, axis=0, keepdims=True)
    lane1 = lax.broadcasted_iota(jnp.int32, (1, LANES), 1)
    cnt = [jnp.sum(jnp.where(lane1 == EXPERTS_PER_GROUP * g, tot, 0.0)).astype(jnp.int32) for g in range(N_GROUPS)]
    off = [jnp.int32(0)]
    for g in range(1, N_GROUPS):
        off.append(off[-1] + cnt[g - 1])

    off_lane = sum(jnp.where(lane1 == EXPERTS_PER_GROUP * g, off[g].astype(F32), 0.0) for g in range(N_GROUPS))
    before = jnp.dot(onehot(ci < ri), gsel_b, preferred_element_type=F32)
    pos_col = jnp.sum(gsel * (before + off_lane), axis=1, keepdims=True)
    sub8 = lax.broadcasted_iota(jnp.int32, (SUBLANES, LANES), 0)
    lane8 = lax.broadcasted_iota(jnp.int32, (SUBLANES, LANES), 1)
    pick = jnp.where((lane8 == EXPERTS_PER_GROUP * sub8) & (sub8 < N_GROUPS), 1.0, 0.0).astype(BF16)
    gsel_t = lax.dot_general(pick, gsel_b, (((1,), (1,)), ((), ())), preferred_element_type=F32)
    before_t = jnp.dot(gsel_t.astype(BF16), onehot(ri < ci), preferred_element_type=F32)
    sub_col = lax.broadcasted_iota(jnp.int32, (SUBLANES, 1), 0)
    off_sub = sum(jnp.where(sub_col == g, off[g].astype(F32), 0.0) for g in range(N_GROUPS))
    pos_row = jnp.sum(gsel_t * (before_t + off_sub), axis=0, keepdims=True)
    perm = onehot(pos_row == ri.astype(F32))
    perm_t = onehot(pos_col == ci.astype(F32))

    xs_ref[...] = jnp.dot(perm, v, preferred_element_type=F32).astype(BF16)
    g_hi = gates.astype(BF16)
    g_lo = (gates - g_hi.astype(F32)).astype(BF16)
    gs_ref[...] = jnp.dot(perm, g_hi, preferred_element_type=F32) + jnp.dot(perm, g_lo, preferred_element_type=F32)
    ys_ref[...] = jnp.zeros(ys_ref.shape, F32)

    for b in range(rows // ROW_BLOCK):
        lo, hi = b * ROW_BLOCK, (b + 1) * ROW_BLOCK
        for g in range(N_GROUPS):
            @pl.when((off[g] < hi) & (off[g] + cnt[g] > lo))
            def _(lo=lo, hi=hi, g=g):
                xb = xs_ref[lo:hi, :]
                gb = gs_ref[lo:hi, :]
                h1 = jnp.dot(xb, w1_ref[:, g * per:(g + 1) * per], preferred_element_type=F32)
                h3 = jnp.dot(xb, w3_ref[:, g * per:(g + 1) * per], preferred_element_type=F32)
                hh = (h1 * jax.nn.sigmoid(h1)) * h3
                parts = []
                for j in range(EXPERTS_PER_GROUP):
                    e = g * EXPERTS_PER_GROUP + j
                    parts.append((hh[:, j * D_EXPERT:(j + 1) * D_EXPERT] * gb[:, e:e + 1]).astype(BF16))
                ys_ref[lo:hi, :] += jnp.dot(jnp.concatenate(parts, axis=-1), w2_ref[g * per:(g + 1) * per, :],
                                            preferred_element_type=F32)

    return jnp.dot(perm_t, ys_ref[...].astype(BF16), preferred_element_type=F32)


def _post_kernel(x_ref, mod_ref, lru_ref, da_ref, mla_ref, wo_ref, g1_ref, b1_ref,
                 rw_ref, rb_ref, w1_ref, w3_ref, w2_ref, g_ref, b_ref, o_ref, xs_ref, gs_ref, ys_ref, *, alpha):
    nb, tm, d = x_ref.shape
    rows = nb * tm
    mod = mod_ref[:, 0]
    per_row = lambda k: jnp.broadcast_to(mod[:, k:k + 1], (nb, tm, d)).reshape(rows, d)
    a = jnp.concatenate([lru_ref[...], da_ref[...], mla_ref[...]], axis=-1).reshape(rows, -1)
    o = jnp.dot(a, wo_ref[...], preferred_element_type=F32)
    x1 = _layer_norm(alpha * x_ref[...].reshape(rows, d) + per_row(2) * o, g1_ref[...], b1_ref[...])
    v = (x1 * (1.0 + per_row(4)) + per_row(3)).astype(BF16)
    gates, in_best = _router_gates(jnp.dot(v, rw_ref[...], preferred_element_type=F32), rb_ref[...])
    f = _grouped_experts(v, gates, in_best, w1_ref, w3_ref, w2_ref, xs_ref, gs_ref, ys_ref)
    o_ref[...] = _layer_norm(alpha * x1 + per_row(5) * f, g_ref[...], b_ref[...]).reshape(nb, tm, d)


POST_BATCH = 2


def _post(xa, modt, lru_o, da_o, mla_o, wo, g1, b1, rw, rb, w1c, w3c, w2c, g2, b2, nctx, alpha, latent_only):
    b, nt, d = xa.shape
    tm = TOKEN_TILE
    nc = nctx // tm
    skip = nc if latent_only else 0
    nb = POST_BATCH if b % POST_BATCH == 0 else 1
    tok = lambda w: pl.BlockSpec((nb, tm, w), lambda i, t: (i, t + skip, 0))
    consts = (wo, g1, b1, rw, rb, w1c, w3c, w2c, g2, b2)
    return pl.pallas_call(
        functools.partial(_post_kernel, alpha=alpha),
        grid=(b // nb, nt // tm - skip),
        in_specs=[
            tok(d),
            pl.BlockSpec((nb, 1, 8, d), lambda i, t: (i, jnp.where(t + skip >= nc, 1, 0), 0, 0)),
            tok(LRU_WIDTH), tok(da_o.shape[-1]), tok(mla_o.shape[-1]),
        ] + [_const_spec(c.shape) for c in consts],
        out_specs=pl.BlockSpec((nb, tm, d), lambda i, t: (i, t, 0)),
        out_shape=jax.ShapeDtypeStruct((b, nt - skip * tm, d), F32),
        scratch_shapes=[pltpu.VMEM((nb * tm, d), BF16), pltpu.VMEM((nb * tm, LANES), F32),
                        pltpu.VMEM((nb * tm, d), F32)],
        compiler_params=_params(("parallel", "parallel")),
        name="post",
    )(xa, modt, lru_o, da_o, mla_o, *consts)


def _rotary_tables(n, nctx):
    rows = n // GRID_W
    row = jnp.repeat(jnp.arange(rows), GRID_W).astype(F32)
    col = jnp.tile(jnp.arange(GRID_W), rows).astype(F32)
    n_freq = DA_QK // 4
    inv = ROPE_THETA ** (-jnp.arange(n_freq, dtype=F32) / n_freq)
    ang = jnp.concatenate([row[:, None] * inv, col[:, None] * inv], axis=-1)
    ang = jnp.concatenate([jnp.zeros((nctx, DA_QK // 2), F32), ang], axis=0)
    c, s = jnp.cos(ang), jnp.sin(ang)
    reps = LANES // DA_QK
    return jnp.tile(jnp.concatenate([c, c], axis=-1), (1, reps)), jnp.tile(jnp.concatenate([-s, s], axis=-1), (1, reps))


def _pack_in_weight(w_in):
    d = w_in.shape[0]
    n_lru_da = 2 * LRU_WIDTH + 3 * DA_WIDTH
    n_rank = Q_RANK + KV_RANK
    wkr = w_in[:, n_lru_da + n_rank:n_lru_da + n_rank + MLA_ROPE]
    krp = jnp.concatenate([jnp.zeros((d, MLA_NOPE), F32), wkr, jnp.zeros((d, LANES - MLA_NOPE - MLA_ROPE), F32)], axis=-1)
    return jnp.concatenate([w_in[:, n_lru_da:n_lru_da + n_rank], krp, w_in[:, 2 * LRU_WIDTH:n_lru_da],
                            w_in[:, :2 * LRU_WIDTH]], axis=-1).astype(BF16)


def _pack_uq(w_uq):
    r = w_uq.shape[0]
    w = w_uq.reshape(r, MLA_HEADS, MLA_NOPE + MLA_ROPE)
    w = jnp.concatenate([w, jnp.zeros((r, MLA_HEADS, LANES - MLA_NOPE - MLA_ROPE), F32)], axis=-1)
    return w.reshape(r, MLA_HEADS * LANES).astype(BF16)


def _pack_ukv(w_ukv):
    r = w_ukv.shape[0]
    w = w_ukv.reshape(r, MLA_HEADS, MLA_NOPE + MLA_V)
    z = jnp.zeros((r, MLA_HEADS, LANES - MLA_NOPE), F32)
    wk = jnp.concatenate([w[..., :MLA_NOPE], z], axis=-1).reshape(r, MLA_HEADS * LANES)
    wv = w[..., MLA_NOPE:].reshape(r, MLA_HEADS * MLA_V)
    return wk.astype(BF16), wv.astype(BF16)


def _block_diag(w):
    nd, nb, bs, _ = w.shape
    eye = jnp.eye(nb, dtype=w.dtype)
    return jnp.einsum('dhij,hg->dhigj', w, eye).reshape(nd, nb * bs, nb * bs)


def kernel(x, c, ctx, c_ctx, w_mod, b_mod, w_in, w_out, conv_w, conv_b, lru_wa, lru_ba, lru_wi, lru_bi, lru_lambda, diff_lambda, diff_norm, mla_q_norm, mla_kv_norm, mla_w_uq, mla_w_ukv, ln1_g, ln1_b, ln2_g, ln2_b, router_w, router_b, exp_w1, exp_w3, exp_w2):
    bsz, n, d = x.shape
    nctx = ctx.shape[1]
    depth = w_mod.shape[0]
    alpha = (2 * depth) ** 0.25
    assert nctx % TOKEN_TILE == 0 and n % TOKEN_TILE == 0 and n % GRID_W == 0

    rows = -(-(bsz + 1) // SUBLANES) * SUBLANES
    cc = jnp.concatenate([c, c_ctx[None, :], jnp.zeros((rows - bsz - 1, d), F32)], axis=0)
    mod = _modulation(cc, w_mod, b_mod).reshape(depth, rows, 6, d)
    mod = jnp.pad(mod, ((0, 0), (0, 0), (0, 2), (0, 0)))
    mod_ctx = jnp.broadcast_to(mod[:, bsz][:, None], (depth, bsz, 8, d))
    modt = jnp.stack([mod_ctx, mod[:, :bsz]], axis=2)

    cosf, sinf = _rotary_tables(n, nctx)
    rw = jnp.tile(router_w, (1, LANES // N_EXPERTS)).astype(BF16)
    rb = jnp.tile(router_b, LANES // N_EXPERTS)[None, :].astype(F32)
    gpair = jnp.tile(diff_norm, (1, LANES // DA_V))

    xa = jnp.concatenate([ctx, x], axis=1)
    for l in range(depth):
        lam_init = jnp.full((1, 1), 0.8 - 0.6 * math.exp(-0.3 * l), F32)
        wuk, wuv = _pack_ukv(mla_w_ukv[l])
        lx, lg, dq_t, dk, dv_t, mq_t, mk, mv_t = _in_proj(
            xa, modt[l], _pack_in_weight(w_in[l]), _pack_uq(mla_w_uq[l]), wuk, wuv,
            mla_q_norm[l][None, :], mla_kv_norm[l][None, :], cosf, sinf, nctx)
        lru_o = _lru(lx, lg, conv_w[l], conv_b[l][None, :], _block_diag(lru_wa[l]).astype(BF16),
                     _block_diag(lru_wi[l]).astype(BF16), lru_ba[l], lru_bi[l], lru_lambda[l], nctx)
        da_o = _da_attn(dq_t, dk, dv_t, diff_lambda[l], gpair[l][None, :], lam_init, nctx)
        mla_o = _mla_attn(mq_t, mk, mv_t, nctx)
        w1c = exp_w1[l].transpose(1, 0, 2).reshape(d, N_EXPERTS * D_EXPERT).astype(BF16)
        w3c = exp_w3[l].transpose(1, 0, 2).reshape(d, N_EXPERTS * D_EXPERT).astype(BF16)
        w2c = exp_w2[l].reshape(N_EXPERTS * D_EXPERT, d).astype(BF16)
        xa = _post(xa, modt[l], lru_o, da_o, mla_o, w_out[l].astype(BF16), ln1_g[l][None, :], ln1_b[l][None, :],
                   rw, rb, w1c, w3c, w2c, ln2_g[l][None, :], ln2_b[l][None, :], nctx, alpha,
                   latent_only=(l == depth - 1))
    return xa
```

```python
import functools
import math

import jax
import jax.numpy as jnp
from jax import lax
from jax.experimental import pallas as pl
from jax.experimental.pallas import tpu as pltpu

F32 = jnp.float32
BF16 = jnp.bfloat16

GRID_W = 64
LRU_WIDTH = 256
LRU_BLOCKS = 4
CONV_W = 4
LRU_C = 8.0
DA_HEADS = 6
DA_QK = 32
DA_V = 2 * DA_QK
MLA_HEADS = 6
MLA_NOPE = 64
MLA_ROPE = 32
MLA_V = 64
Q_RANK = 256
KV_RANK = 128
MLA_SCALE = (MLA_NOPE + MLA_ROPE) ** -0.5
N_EXPERTS = 16
N_GROUPS = 4
EXPERTS_PER_GROUP = N_EXPERTS // N_GROUPS
D_EXPERT = 256
ROPE_THETA = 10000.0
LN_EPS = 1e-5
RMS_EPS = 1e-6

LANES = 128
SUBLANES = 8
TOKEN_TILE = 256
VMEM_LIMIT = 56 * 1024 * 1024

LOG2E = math.log2(math.e)
DA_QSCALE = DA_QK ** -0.5 * LOG2E
MLA_QSCALE = MLA_SCALE * LOG2E

DA_WIDTH = DA_HEADS * DA_V
C_CQ = 0
C_CKV = C_CQ + Q_RANK
C_KR = C_CKV + KV_RANK
C_DAQ = C_KR + LANES
C_DAK = C_DAQ + DA_WIDTH
C_DAV = C_DAK + DA_WIDTH
C_LRU = C_DAV + DA_WIDTH
C_END = C_LRU + 2 * LRU_WIDTH
PAIRS = DA_HEADS // 2
VT_ROWS = DA_V + 16


def _params(sem):
    return pltpu.CompilerParams(dimension_semantics=sem, vmem_limit_bytes=VMEM_LIMIT)


def _const_spec(shape):
    nd = len(shape)
    return pl.BlockSpec(shape, lambda *_: (0,) * nd, pipeline_mode=pl.Buffered(1))


def _mod_kernel(c_ref, w_ref, b_ref, o_ref):
    c = c_ref[...]
    s = c * jax.nn.sigmoid(c)
    o_ref[0] = jnp.dot(s.astype(BF16), w_ref[0].astype(BF16), preferred_element_type=F32) + b_ref[0]


def _modulation(cc, w_mod, b_mod):
    depth, d, d6 = w_mod.shape
    r = cc.shape[0]
    tn = min(d6, 1536)
    return pl.pallas_call(
        _mod_kernel,
        grid=(depth, d6 // tn),
        in_specs=[
            pl.BlockSpec((r, d), lambda l, j: (0, 0)),
            pl.BlockSpec((1, d, tn), lambda l, j: (l, 0, j)),
            pl.BlockSpec((1, 1, tn), lambda l, j: (l, 0, j)),
        ],
        out_specs=pl.BlockSpec((1, r, tn), lambda l, j: (l, 0, j)),
        out_shape=jax.ShapeDtypeStruct((depth, r, d6), F32),
        compiler_params=_params(("parallel", "parallel")),
        name="modulation",
    )(cc, w_mod, b_mod.reshape(depth, 1, d6))


def _rotate(t, cosf, sinf, first_half):
    partner = jnp.where(first_half, pltpu.roll(t, LANES - DA_QK // 2, 1), pltpu.roll(t, DA_QK // 2, 1))
    return t * cosf + partner * sinf


def _store_values_t(vt_ref, v):
    rows = v.shape[0]
    ones = jnp.ones((VT_ROWS - DA_V, rows), BF16)
    for j in range(PAIRS):
        t = v[:, LANES * j:LANES * (j + 1)].T.astype(BF16)
        for k in range(2):
            vt_ref[0, 2 * j + k, 0:DA_V, :] = t[DA_V * k:DA_V * (k + 1)]
            vt_ref[0, 2 * j + k, DA_V:VT_ROWS, :] = ones


def _in_kernel(x_ref, mod_ref, w1_ref, wuq_ref, wuk_ref, wuv_ref, qn_ref, kvn_ref, cos_ref, sin_ref,
               lx_ref, lg_ref, dq_ref, dk_ref, dv_ref, mq_ref, mk_ref, mv_ref):
    x = x_ref[0]
    mod = mod_ref[0, 0]
    u = (x * (1.0 + mod[1:2]) + mod[0:1]).astype(BF16)
    proj = lambda lo, hi: jnp.dot(u, w1_ref[:, lo:hi], preferred_element_type=F32)

    cosf = cos_ref[...]
    sinf = sin_ref[...]
    lane = lax.broadcasted_iota(jnp.int32, cosf.shape, 1)
    first_half = (lane & (DA_QK // 2)) == 0
    rot = functools.partial(_rotate, cosf=cosf, sinf=sinf, first_half=first_half)
    is_rope = (lane >= MLA_NOPE) & (lane < MLA_NOPE + MLA_ROPE)

    y_mla = proj(C_CQ, C_DAQ)
    y_da = proj(C_DAQ, C_DAV)

    cq = y_mla[:, C_CQ:C_CKV]
    ckv = y_mla[:, C_CKV:C_KR]
    krp = y_mla[:, C_KR:C_DAQ]
    qn = (cq * lax.rsqrt(jnp.mean(cq * cq, axis=-1, keepdims=True) + RMS_EPS)) * qn_ref[...]
    kvn = ((ckv * lax.rsqrt(jnp.mean(ckv * ckv, axis=-1, keepdims=True) + RMS_EPS)) * kvn_ref[...]).astype(BF16)
    q = jnp.dot(qn.astype(BF16), wuq_ref[...], preferred_element_type=F32)
    kn = jnp.dot(kvn, wuk_ref[...], preferred_element_type=F32)

    for j in range(PAIRS):
        t = y_da[:, LANES * j:LANES * (j + 1)]
        dq_ref[0, LANES * j:LANES * (j + 1), :] = (rot(t) * DA_QSCALE).T.astype(BF16)
        t = y_da[:, DA_WIDTH + LANES * j:DA_WIDTH + LANES * (j + 1)]
        dk_ref[0, :, LANES * j:LANES * (j + 1)] = rot(t).astype(BF16)

    y_rest = proj(C_DAV, C_END)
    mv = jnp.dot(kvn, wuv_ref[...], preferred_element_type=F32)

    kr = jnp.where(is_rope, rot(krp), krp)
    for h in range(MLA_HEADS):
        t = q[:, LANES * h:LANES * (h + 1)]
        mq_ref[0, h] = (jnp.where(is_rope, rot(t), t) * MLA_QSCALE).T.astype(BF16)
        mk_ref[0, h] = (kn[:, LANES * h:LANES * (h + 1)] + kr).astype(BF16)

    _store_values_t(dv_ref, y_rest[:, 0:DA_WIDTH])
    _store_values_t(mv_ref, mv)
    lx_ref[0] = y_rest[:, DA_WIDTH:DA_WIDTH + LRU_WIDTH]
    lg_ref[0] = y_rest[:, DA_WIDTH + LRU_WIDTH:DA_WIDTH + 2 * LRU_WIDTH]


def _in_proj(xa, modt, w1, wuq, wuk, wuv, qnorm, kvnorm, cosf, sinf, nctx):
    b, nt, d = xa.shape
    tm = TOKEN_TILE
    nc = nctx // tm
    tok = lambda w: pl.BlockSpec((1, tm, w), lambda i, t: (i, t, 0))
    head_t = lambda r: pl.BlockSpec((1, DA_HEADS, r, tm), lambda i, t: (i, 0, 0, t))
    return pl.pallas_call(
        _in_kernel,
        grid=(b, nt // tm),
        in_specs=[
            tok(d),
            pl.BlockSpec((1, 1, 8, d), lambda i, t: (i, jnp.where(t >= nc, 1, 0), 0, 0)),
            _const_spec(w1.shape), _const_spec(wuq.shape), _const_spec(wuk.shape), _const_spec(wuv.shape),
            _const_spec(qnorm.shape), _const_spec(kvnorm.shape),
            pl.BlockSpec((tm, LANES), lambda i, t: (t, 0)),
            pl.BlockSpec((tm, LANES), lambda i, t: (t, 0)),
        ],
        out_specs=[
            tok(LRU_WIDTH), tok(LRU_WIDTH),
            pl.BlockSpec((1, DA_WIDTH, tm), lambda i, t: (i, 0, t)), tok(DA_WIDTH), head_t(VT_ROWS),
            head_t(LANES), pl.BlockSpec((1, MLA_HEADS, tm, LANES), lambda i, t: (i, 0, t, 0)), head_t(VT_ROWS),
        ],
        out_shape=[
            jax.ShapeDtypeStruct((b, nt, LRU_WIDTH), F32),
            jax.ShapeDtypeStruct((b, nt, LRU_WIDTH), F32),
            jax.ShapeDtypeStruct((b, DA_WIDTH, nt), BF16),
            jax.ShapeDtypeStruct((b, nt, DA_WIDTH), BF16),
            jax.ShapeDtypeStruct((b, DA_HEADS, VT_ROWS, nt), BF16),
            jax.ShapeDtypeStruct((b, MLA_HEADS, LANES, nt), BF16),
            jax.ShapeDtypeStruct((b, MLA_HEADS, nt, LANES), BF16),
            jax.ShapeDtypeStruct((b, MLA_HEADS, VT_ROWS, nt), BF16),
        ],
        compiler_params=_params(("parallel", "parallel")),
        name="in_proj",
    )(xa, modt, w1, wuq, wuk, wuv, qnorm, kvnorm, cosf, sinf)


def _gelu_tanh(x):
    return 0.5 * x * (1.0 + jnp.tanh(math.sqrt(2.0 / math.pi) * (x + 0.044715 * (x * x * x))))


def _lru_kernel(x_ref, g_ref, cw_ref, cb_ref, wa_ref, wi_ref, ba_ref, bi_ref, lam_ref, o_ref,
                y_s, a_s, s_s, h_s, *, nt, nctx, chunk):
    w = LRU_WIDTH
    tiles = chunk // SUBLANES
    n_chunks = nt // chunk
    sub = lax.broadcasted_iota(jnp.int32, (tiles, SUBLANES, w), 1)
    tile_i = lax.broadcasted_iota(jnp.int32, (tiles, SUBLANES, w), 0)

    def conv_chunk(c, carry):
        r0 = pl.multiple_of(c * chunk, chunk)
        lo = pl.multiple_of(jnp.maximum(r0 - SUBLANES, 0), SUBLANES)
        hi = pl.multiple_of(jnp.minimum(r0 + chunk, nt - SUBLANES), SUBLANES)
        x3 = jnp.concatenate([x_ref[0, pl.ds(lo, SUBLANES), :], x_ref[0, pl.ds(r0, chunk), :],
                              x_ref[0, pl.ds(hi, SUBLANES), :]], axis=0).reshape(tiles + 2, SUBLANES, w)
        sh1 = pltpu.roll(x3, 1, 1)
        sh2 = pltpu.roll(x3, 2, 1)
        sh7 = pltpu.roll(x3, SUBLANES - 1, 1)
        pos = r0 + tile_i * SUBLANES + sub
        in_ctx = pos < nctx
        seg_pos = jnp.where(in_ctx, pos, pos - nctx)
        seg_last = jnp.where(in_ctx, nctx - 1, nt - nctx - 1)
        zero = jnp.zeros((tiles, SUBLANES, w), F32)
        xm2 = jnp.where(seg_pos >= 2, jnp.where(sub >= 2, sh2[1:-1], sh2[0:-2]), zero)
        xm1 = jnp.where(seg_pos >= 1, jnp.where(sub >= 1, sh1[1:-1], sh1[0:-2]), zero)
        xp1 = jnp.where(seg_pos < seg_last, jnp.where(sub < SUBLANES - 1, sh7[1:-1], sh7[2:]), zero)
        y = cb_ref[...] + xm2 * cw_ref[0:1] + xm1 * cw_ref[1:2] + x3[1:-1] * cw_ref[2:3] + xp1 * cw_ref[3:4]
        y_s[pl.ds(r0, chunk), :] = y.reshape(chunk, w)
        return carry

    lax.fori_loop(0, n_chunks, conv_chunk, 0)

    nctx_t = nctx // SUBLANES
    nt_t = nt // SUBLANES

    for d in range(2):
        nlam = -lam_ref[d:d + 1]
        softplus = jnp.maximum(nlam, 0.0) + jnp.log1p(jnp.exp(-jnp.abs(nlam)))
        c8 = -LRU_C * softplus

        def gate_chunk(c, carry, d=d, c8=c8):
            r0 = pl.multiple_of(c * chunk, chunk)
            y = y_s[pl.ds(r0, chunk), :]
            yb = y.astype(BF16)
            r = jax.nn.sigmoid(jnp.dot(yb, wa_ref[d], preferred_element_type=F32) + ba_ref[d:d + 1])
            i = jax.nn.sigmoid(jnp.dot(yb, wi_ref[d], preferred_element_type=F32) + bi_ref[d:d + 1])
            log_a = c8 * r
            a = jnp.exp(log_a)
            th = jnp.tanh(log_a)
            u = jnp.sqrt(-2.0 * th / (1.0 - th)) * (i * y)
            a3 = a.reshape(tiles, SUBLANES, w)
            u3 = u.reshape(tiles, SUBLANES, w)
            for sft in (1, 2, 4):
                if d == 0:
                    ok = sub >= sft
                    ash = pltpu.roll(a3, sft, 1)
                    ush = pltpu.roll(u3, sft, 1)
                else:
                    ok = sub < SUBLANES - sft
                    ash = pltpu.roll(a3, SUBLANES - sft, 1)
                    ush = pltpu.roll(u3, SUBLANES - sft, 1)
                u3 = jnp.where(ok, a3 * ush + u3, u3)
                a3 = jnp.where(ok, a3 * ash, a3)
            a_s[pl.ds(r0, chunk), :] = a3.reshape(chunk, w)
            s_s[pl.ds(r0, chunk), :] = u3.reshape(chunk, w)
            return carry

        lax.fori_loop(0, n_chunks, gate_chunk, 0)

        def carry_tile(j, hprev, d=d):
            if d == 0:
                t = j
            else:
                t = jnp.where(j < nctx_t, nctx_t - 1 - j, nt_t - 1 - (j - nctx_t))
            r0 = pl.multiple_of(t * SUBLANES, SUBLANES)
            h = a_s[pl.ds(r0, SUBLANES), :] * hprev + s_s[pl.ds(r0, SUBLANES), :]
            if d == 0:
                h_s[pl.ds(r0, SUBLANES), :] = h
                return h[SUBLANES - 1:SUBLANES]
            h_s[pl.ds(r0, SUBLANES), :] = h_s[pl.ds(r0, SUBLANES), :] + h
            return h[0:1]

        lax.fori_loop(0, nt_t, carry_tile, jnp.zeros((1, w), F32), unroll=4)

    def out_chunk(c, carry):
        r0 = pl.multiple_of(c * chunk, chunk)
        o_ref[0, pl.ds(r0, chunk), :] = (h_s[pl.ds(r0, chunk), :] * _gelu_tanh(g_ref[0, pl.ds(r0, chunk), :])).astype(BF16)
        return carry

    lax.fori_loop(0, n_chunks, out_chunk, 0)


def _lru(lx, lg, conv_w, conv_b, wa, wi, ba, bi, lam, nctx):
    b, nt, w = lx.shape
    chunk = TOKEN_TILE
    seq = pl.BlockSpec((1, nt, w), lambda i: (i, 0, 0))
    return pl.pallas_call(
        functools.partial(_lru_kernel, nt=nt, nctx=nctx, chunk=chunk),
        grid=(b,),
        in_specs=[seq, seq, _const_spec(conv_w.shape), _const_spec(conv_b.shape), _const_spec(wa.shape),
                  _const_spec(wi.shape), _const_spec(ba.shape), _const_spec(bi.shape), _const_spec(lam.shape)],
        out_specs=seq,
        out_shape=jax.ShapeDtypeStruct((b, nt, w), BF16),
        scratch_shapes=[pltpu.VMEM((nt, w), F32)] * 4,
        compiler_params=_params(("parallel",)),
        name="rglru",
    )(lx, lg, conv_w, conv_b, wa, wi, ba, bi, lam)


DA_KEY_CHUNK = 256
MLA_KEY_CHUNK = 256


def _key_chunks(nk, nctx, size):
    chunks = [(0, nctx)]
    chunks += [(s, min(size, nk - s)) for s in range(nctx, nk, size)]
    return chunks


def _attend_t(chains, chunks):
    def scores(n, ci):
        q_t, key, _ = chains[n]
        return jnp.dot(key(*chunks[ci]), q_t, preferred_element_type=F32).astype(BF16)

    s = [scores(n, 0) for n in range(len(chains))]
    state = [None] * len(chains)
    for ci in range(len(chunks)):
        for n, (_, _, value_t) in enumerate(chains):
            cm = jnp.max(s[n], axis=0, keepdims=True)
            if ci == 0:
                m_new = cm
            else:
                m_old, acc = state[n]
                m_new = jnp.maximum(m_old, cm)
            p = jnp.exp2(s[n] - m_new)
            if ci + 1 < len(chunks):
                s[n] = scores(n, ci + 1)
            pv = jnp.dot(value_t(*chunks[ci]), p, preferred_element_type=F32)
            if ci > 0:
                pv = acc * jnp.exp2(m_old.astype(F32) - m_new.astype(F32)) + pv
            state[n] = (m_new, pv)
    return [acc[0:DA_V] / acc[DA_V:DA_V + 1] for _, acc in state]


def _da_kernel(q_ref, k_ref, vt_ref, dl_ref, g_ref, li_ref, o_ref, *, nt, nctx, tq):
    half = pl.program_id(1)
    row = lax.broadcasted_iota(jnp.int32, (LANES, tq), 0)
    lane = lax.broadcasted_iota(jnp.int32, (tq, LANES), 1)
    dl = dl_ref[...]
    lam_init = li_ref[...]
    lam = (jnp.exp(jnp.sum(dl[0:1] * dl[1:2], axis=-1, keepdims=True))
           - jnp.exp(jnp.sum(dl[2:3] * dl[3:4], axis=-1, keepdims=True)) + lam_init)
    zero = jnp.zeros((LANES, tq), BF16)

    def attend(q0, nk):
        chains = []
        for j in range(PAIRS):
            q_t = q_ref[0, LANES * j:LANES * (j + 1), pl.ds(q0, tq)]
            key = lambda start, size, j=j: k_ref[0, start:start + size, LANES * j:LANES * (j + 1)]
            value_t = lambda start, size, j=j: vt_ref[0, 2 * j + half, :, start:start + size]
            for mi in range(2):
                lo = half * DA_V + mi * DA_QK
                chains.append((jnp.where((row >= lo) & (row < lo + DA_QK), q_t, zero), key, value_t))
        o = _attend_t(chains, _key_chunks(nk, nctx, DA_KEY_CHUNK))
        for j in range(PAIRS):
            d = o[2 * j] - lam * o[2 * j + 1]
            d = d * lax.rsqrt(jnp.mean(d * d, axis=0, keepdims=True) + RMS_EPS)
            both = jnp.concatenate([d, d], axis=0).T
            new = (both * g_ref[...] * (1.0 - lam_init)).astype(BF16)
            slab = (0, pl.ds(q0, tq), slice(LANES * j, LANES * (j + 1)))

            @pl.when(half == 0)
            def _():
                o_ref[slab] = new

            @pl.when(half == 1)
            def _():
                o_ref[slab] = jnp.where(lane >= DA_V, new, o_ref[slab])

    _for_query_blocks(attend, nt, nctx, tq)


def _for_query_blocks(attend, nt, nctx, tq):
    for t in range(nctx // tq):
        attend(t * tq, nctx)

    def latent_block(t, carry):
        attend(pl.multiple_of(nctx + t * tq, tq), nt)
        return carry

    lax.fori_loop(0, (nt - nctx) // tq, latent_block, 0)


def _da_attn(dq_t, dk, dv_t, dlam, gpair, lam_init, nctx):
    b, nt, _ = dk.shape
    tq = TOKEN_TILE
    return pl.pallas_call(
        functools.partial(_da_kernel, nt=nt, nctx=nctx, tq=tq),
        grid=(b, 2),
        in_specs=[
            pl.BlockSpec((1, DA_WIDTH, nt), lambda i, h: (i, 0, 0)),
            pl.BlockSpec((1, nt, DA_WIDTH), lambda i, h: (i, 0, 0)),
            pl.BlockSpec((1, DA_HEADS, VT_ROWS, nt), lambda i, h: (i, 0, 0, 0)),
            _const_spec(dlam.shape), _const_spec(gpair.shape), _const_spec(lam_init.shape),
        ],
        out_specs=pl.BlockSpec((1, nt, DA_WIDTH), lambda i, h: (i, 0, 0)),
        out_shape=jax.ShapeDtypeStruct((b, nt, DA_WIDTH), BF16),
        compiler_params=_params(("parallel", "arbitrary")),
        name="diff_attn",
    )(dq_t, dk, dv_t, dlam, gpair, lam_init)


def _mla_kernel(q_ref, k_ref, vt_ref, o_ref, *, nt, nctx, tq):
    def attend(q0, nk):
        chains = []
        for hh in range(MLA_HEADS):
            key = lambda start, size, hh=hh: k_ref[0, hh, start:start + size, :]
            value_t = lambda start, size, hh=hh: vt_ref[0, hh, :, start:start + size]
            chains.append((q_ref[0, hh, :, pl.ds(q0, tq)], key, value_t))
        o = _attend_t(chains, _key_chunks(nk, nctx, MLA_KEY_CHUNK))
        for j in range(MLA_HEADS // 2):
            o_ref[0, pl.ds(q0, tq), LANES * j:LANES * (j + 1)] = jnp.concatenate(o[2 * j:2 * j + 2], axis=0).T.astype(BF16)

    _for_query_blocks(attend, nt, nctx, tq)


def _mla_attn(mq_t, mk, mv_t, nctx):
    b, nh, nt, _ = mk.shape
    tq = TOKEN_TILE
    return pl.pallas_call(
        functools.partial(_mla_kernel, nt=nt, nctx=nctx, tq=tq),
        grid=(b,),
        in_specs=[
            pl.BlockSpec((1, nh, LANES, nt), lambda i: (i, 0, 0, 0)),
            pl.BlockSpec((1, nh, nt, LANES), lambda i: (i, 0, 0, 0)),
            pl.BlockSpec((1, nh, VT_ROWS, nt), lambda i: (i, 0, 0, 0)),
        ],
        out_specs=pl.BlockSpec((1, nt, nh * MLA_V), lambda i: (i, 0, 0)),
        out_shape=jax.ShapeDtypeStruct((b, nt, nh * MLA_V), BF16),
        compiler_params=_params(("parallel",)),
        name="mla_attn",
    )(mq_t, mk, mv_t)


def _layer_norm(z, g, b):
    mu = jnp.mean(z, axis=-1, keepdims=True)
    zc = z - mu
    var = jnp.mean(zc * zc, axis=-1, keepdims=True)
    return (zc * lax.rsqrt(var + LN_EPS)) * g + b


def _router_gates(logits, rb):
    scores = jax.nn.sigmoid(logits)
    sel = scores + rb
    lane = lax.broadcasted_iota(jnp.int32, logits.shape, 1)
    r = lane & (EXPERTS_PER_GROUP - 1)
    grp = (lane >> 2) & (N_GROUPS - 1)

    def in_group(x, k):
        return jnp.where(r >= k, pltpu.roll(x, k, 1), pltpu.roll(x, LANES - EXPERTS_PER_GROUP + k, 1))

    others = [in_group(sel, k) for k in (1, 2, 3)]
    pair_max = sel + jnp.maximum(jnp.maximum(others[0], others[1]), others[2])
    grp_score = jnp.maximum(jnp.maximum(pair_max, in_group(pair_max, 1)),
                            jnp.maximum(in_group(pair_max, 2), in_group(pair_max, 3)))
    in_best = None
    for k in (1, 2, 3):
        other = pltpu.roll(grp_score, EXPERTS_PER_GROUP * k, 1)
        wins = (grp_score > other) | ((grp_score == other) & (grp < k))
        in_best = wins if in_best is None else (in_best & wins)
    beaten = jnp.zeros(logits.shape, F32)
    for k, o in zip((1, 2, 3), others):
        beats = (o > sel) | ((o == sel) & (r >= k))
        beaten = beaten + jnp.where(beats, 1.0, 0.0)
    chosen = in_best & (beaten < 2.0)
    sc = jnp.where(chosen, scores, 0.0)
    tot = sc + in_group(sc, 1) + in_group(sc, 2) + in_group(sc, 3)
    return jnp.where(chosen, sc / tot, 0.0), in_best


ROW_BLOCK = 160
BF16_ROWS = 16


def _grouped_experts(v, gates, in_best, w1_ref, w3_ref, w2_ref, xs_ref, gs_ref, ys_ref):
    rows, d = v.shape
    per = EXPERTS_PER_GROUP * D_EXPERT
    lane = lax.broadcasted_iota(jnp.int32, (rows, LANES), 1)
    gsel = jnp.where(in_best & ((lane & (EXPERTS_PER_GROUP - 1)) == 0) & (lane < N_EXPERTS), 1.0, 0.0)
    gsel_b = gsel.astype(BF16)
    ri = lax.broadcasted_iota(jnp.int32, (rows, rows), 0)
    ci = lax.broadcasted_iota(jnp.int32, (rows, rows), 1)
    onehot = lambda cond: jnp.where(cond, 1.0, 0.0).astype(BF16)

    tot = jnp.sum(gsel, axis=0, keepdims=True)
    lane1 = lax.broadcasted_iota(jnp.int32, (1, LANES), 1)
    cnt = [jnp.sum(jnp.where(lane1 == EXPERTS_PER_GROUP * g, tot, 0.0)).astype(jnp.int32) for g in range(N_GROUPS)]
    off = [jnp.int32(0)]
    for g in range(1, N_GROUPS):
        off.append(off[-1] + cnt[g - 1])

    off_lane = sum(jnp.where(lane1 == EXPERTS_PER_GROUP * g, off[g].astype(F32), 0.0) for g in range(N_GROUPS))
    before = jnp.dot(onehot(ci < ri), gsel_b, preferred_element_type=F32)
    pos_col = jnp.sum(gsel * (before + off_lane), axis=1, keepdims=True)
    sub8 = lax.broadcasted_iota(jnp.int32, (SUBLANES, LANES), 0)
    lane8 = lax.broadcasted_iota(jnp.int32, (SUBLANES, LANES), 1)
    pick = jnp.where((lane8 == EXPERTS_PER_GROUP * sub8) & (sub8 < N_GROUPS), 1.0, 0.0).astype(BF16)
    gsel_t = lax.dot_general(pick, gsel_b, (((1,), (1,)), ((), ())), preferred_element_type=F32)
    before_t = jnp.dot(gsel_t.astype(BF16), onehot(ri < ci), preferred_element_type=F32)
    sub_col = lax.broadcasted_iota(jnp.int32, (SUBLANES, 1), 0)
    off_sub = sum(jnp.where(sub_col == g, off[g].astype(F32), 0.0) for g in range(N_GROUPS))
    pos_row = jnp.sum(gsel_t * (before_t + off_sub), axis=0, keepdims=True)
    perm = onehot(pos_row == ri.astype(F32))
    perm_t = onehot(pos_col == ci.astype(F32))

    xs_ref[...] = jnp.dot(perm, v, preferred_element_type=F32).astype(BF16)
    g_hi = gates.astype(BF16)
    g_lo = (gates - g_hi.astype(F32)).astype(BF16)
    gs_ref[...] = jnp.dot(perm, g_hi, preferred_element_type=F32) + jnp.dot(perm, g_lo, preferred_element_type=F32)
    ys_ref[...] = jnp.zeros(ys_ref.shape, F32)

    row_in_block = lax.broadcasted_iota(jnp.int32, (ROW_BLOCK, LANES), 0)
    for g in range(N_GROUPS):
        first = (off[g] // BF16_ROWS) * BF16_ROWS
        end = off[g] + cnt[g]
        for k in range(-(-rows // ROW_BLOCK)):
            lo = first + k * ROW_BLOCK

            @pl.when((lo < end) & (cnt[g] > 0))
            def _(lo=lo, g=g):
                st = pl.multiple_of(jnp.minimum(lo, rows - ROW_BLOCK), BF16_ROWS)
                xb = xs_ref[pl.ds(st, ROW_BLOCK), :]
                gb = jnp.where(row_in_block + st >= lo, gs_ref[pl.ds(st, ROW_BLOCK), :], 0.0)
                h1 = jnp.dot(xb, w1_ref[:, g * per:(g + 1) * per], preferred_element_type=F32)
                h3 = jnp.dot(xb, w3_ref[:, g * per:(g + 1) * per], preferred_element_type=F32)
                hh = (h1 * jax.nn.sigmoid(h1)) * h3
                parts = []
                for j in range(EXPERTS_PER_GROUP):
                    e = g * EXPERTS_PER_GROUP + j
                    parts.append((hh[:, j * D_EXPERT:(j + 1) * D_EXPERT] * gb[:, e:e + 1]).astype(BF16))
                ys_ref[pl.ds(st, ROW_BLOCK), :] += jnp.dot(jnp.concatenate(parts, axis=-1),
                                                           w2_ref[g * per:(g + 1) * per, :],
                                                           preferred_element_type=F32)

    return jnp.dot(perm_t, ys_ref[...].astype(BF16), preferred_element_type=F32)


def _post_kernel(x_ref, mod_ref, lru_ref, da_ref, mla_ref, wo_ref, g1_ref, b1_ref,
                 rw_ref, rb_ref, w1_ref, w3_ref, w2_ref, g_ref, b_ref, o_ref, xs_ref, gs_ref, ys_ref, *, alpha):
    nb, tm, d = x_ref.shape
    rows = nb * tm
    mod = mod_ref[:, 0]
    per_row = lambda k: jnp.broadcast_to(mod[:, k:k + 1], (nb, tm, d)).reshape(rows, d)
    a = jnp.concatenate([lru_ref[...], da_ref[...], mla_ref[...]], axis=-1).reshape(rows, -1)
    o = jnp.dot(a, wo_ref[...], preferred_element_type=F32)
    x1 = _layer_norm(alpha * x_ref[...].reshape(rows, d) + per_row(2) * o, g1_ref[...], b1_ref[...])
    v = (x1 * (1.0 + per_row(4)) + per_row(3)).astype(BF16)
    gates, in_best = _router_gates(jnp.dot(v, rw_ref[...], preferred_element_type=F32), rb_ref[...])
    f = _grouped_experts(v, gates, in_best, w1_ref, w3_ref, w2_ref, xs_ref, gs_ref, ys_ref)
    o_ref[...] = _layer_norm(alpha * x1 + per_row(5) * f, g_ref[...], b_ref[...]).reshape(nb, tm, d)


POST_BATCH = 2


def _post(xa, modt, lru_o, da_o, mla_o, wo, g1, b1, rw, rb, w1c, w3c, w2c, g2, b2, nctx, alpha, latent_only):
    b, nt, d = xa.shape
    tm = TOKEN_TILE
    nc = nctx // tm
    skip = nc if latent_only else 0
    nb = POST_BATCH if b % POST_BATCH == 0 else 1
    tok = lambda w: pl.BlockSpec((nb, tm, w), lambda i, t: (i, t + skip, 0))
    consts = (wo, g1, b1, rw, rb, w1c, w3c, w2c, g2, b2)
    return pl.pallas_call(
        functools.partial(_post_kernel, alpha=alpha),
        grid=(b // nb, nt // tm - skip),
        in_specs=[
            tok(d),
            pl.BlockSpec((nb, 1, 8, d), lambda i, t: (i, jnp.where(t + skip >= nc, 1, 0), 0, 0)),
            tok(LRU_WIDTH), tok(da_o.shape[-1]), tok(mla_o.shape[-1]),
        ] + [_const_spec(c.shape) for c in consts],
        out_specs=pl.BlockSpec((nb, tm, d), lambda i, t: (i, t, 0)),
        out_shape=jax.ShapeDtypeStruct((b, nt - skip * tm, d), F32),
        scratch_shapes=[pltpu.VMEM((nb * tm, d), BF16), pltpu.VMEM((nb * tm, LANES), F32),
                        pltpu.VMEM((nb * tm, d), F32)],
        compiler_params=_params(("parallel", "parallel")),
        name="post",
    )(xa, modt, lru_o, da_o, mla_o, *consts)


def _rotary_tables(n, nctx):
    rows = n // GRID_W
    row = jnp.repeat(jnp.arange(rows), GRID_W).astype(F32)
    col = jnp.tile(jnp.arange(GRID_W), rows).astype(F32)
    n_freq = DA_QK // 4
    inv = ROPE_THETA ** (-jnp.arange(n_freq, dtype=F32) / n_freq)
    ang = jnp.concatenate([row[:, None] * inv, col[:, None] * inv], axis=-1)
    ang = jnp.concatenate([jnp.zeros((nctx, DA_QK // 2), F32), ang], axis=0)
    c, s = jnp.cos(ang), jnp.sin(ang)
    reps = LANES // DA_QK
    return jnp.tile(jnp.concatenate([c, c], axis=-1), (1, reps)), jnp.tile(jnp.concatenate([-s, s], axis=-1), (1, reps))


def _pack_in_weight(w_in):
    d = w_in.shape[0]
    n_lru_da = 2 * LRU_WIDTH + 3 * DA_WIDTH
    n_rank = Q_RANK + KV_RANK
    wkr = w_in[:, n_lru_da + n_rank:n_lru_da + n_rank + MLA_ROPE]
    krp = jnp.concatenate([jnp.zeros((d, MLA_NOPE), F32), wkr, jnp.zeros((d, LANES - MLA_NOPE - MLA_ROPE), F32)], axis=-1)
    return jnp.concatenate([w_in[:, n_lru_da:n_lru_da + n_rank], krp, w_in[:, 2 * LRU_WIDTH:n_lru_da],
                            w_in[:, :2 * LRU_WIDTH]], axis=-1).astype(BF16)


def _pack_uq(w_uq):
    r = w_uq.shape[0]
    w = w_uq.reshape(r, MLA_HEADS, MLA_NOPE + MLA_ROPE)
    w = jnp.concatenate([w, jnp.zeros((r, MLA_HEADS, LANES - MLA_NOPE - MLA_ROPE), F32)], axis=-1)
    return w.reshape(r, MLA_HEADS * LANES).astype(BF16)


def _pack_ukv(w_ukv):
    r = w_ukv.shape[0]
    w = w_ukv.reshape(r, MLA_HEADS, MLA_NOPE + MLA_V)
    z = jnp.zeros((r, MLA_HEADS, LANES - MLA_NOPE), F32)
    wk = jnp.concatenate([w[..., :MLA_NOPE], z], axis=-1).reshape(r, MLA_HEADS * LANES)
    wv = w[..., MLA_NOPE:].reshape(r, MLA_HEADS * MLA_V)
    return wk.astype(BF16), wv.astype(BF16)


def _block_diag(w):
    nd, nb, bs, _ = w.shape
    eye = jnp.eye(nb, dtype=w.dtype)
    return jnp.einsum('dhij,hg->dhigj', w, eye).reshape(nd, nb * bs, nb * bs)


def kernel(x, c, ctx, c_ctx, w_mod, b_mod, w_in, w_out, conv_w, conv_b, lru_wa, lru_ba, lru_wi, lru_bi, lru_lambda, diff_lambda, diff_norm, mla_q_norm, mla_kv_norm, mla_w_uq, mla_w_ukv, ln1_g, ln1_b, ln2_g, ln2_b, router_w, router_b, exp_w1, exp_w3, exp_w2):
    bsz, n, d = x.shape
    nctx = ctx.shape[1]
    depth = w_mod.shape[0]
    alpha = (2 * depth) ** 0.25
    assert nctx % TOKEN_TILE == 0 and n % TOKEN_TILE == 0 and n % GRID_W == 0

    rows = -(-(bsz + 1) // SUBLANES) * SUBLANES
    cc = jnp.concatenate([c, c_ctx[None, :], jnp.zeros((rows - bsz - 1, d), F32)], axis=0)
    mod = _modulation(cc, w_mod, b_mod).reshape(depth, rows, 6, d)
    mod = jnp.pad(mod, ((0, 0), (0, 0), (0, 2), (0, 0)))
    mod_ctx = jnp.broadcast_to(mod[:, bsz][:, None], (depth, bsz, 8, d))
    modt = jnp.stack([mod_ctx, mod[:, :bsz]], axis=2)

    cosf, sinf = _rotary_tables(n, nctx)
    rw = jnp.tile(router_w, (1, LANES // N_EXPERTS)).astype(BF16)
    rb = jnp.tile(router_b, LANES // N_EXPERTS)[None, :].astype(F32)
    gpair = jnp.tile(diff_norm, (1, LANES // DA_V))

    xa = jnp.concatenate([ctx, x], axis=1)
    for l in range(depth):
        lam_init = jnp.full((1, 1), 0.8 - 0.6 * math.exp(-0.3 * l), F32)
        wuk, wuv = _pack_ukv(mla_w_ukv[l])
        lx, lg, dq_t, dk, dv_t, mq_t, mk, mv_t = _in_proj(
            xa, modt[l], _pack_in_weight(w_in[l]), _pack_uq(mla_w_uq[l]), wuk, wuv,
            mla_q_norm[l][None, :], mla_kv_norm[l][None, :], cosf, sinf, nctx)
        lru_o = _lru(lx, lg, conv_w[l], conv_b[l][None, :], _block_diag(lru_wa[l]).astype(BF16),
                     _block_diag(lru_wi[l]).astype(BF16), lru_ba[l], lru_bi[l], lru_lambda[l], nctx)
        da_o = _da_attn(dq_t, dk, dv_t, diff_lambda[l], gpair[l][None, :], lam_init, nctx)
        mla_o = _mla_attn(mq_t, mk, mv_t, nctx)
        w1c = exp_w1[l].transpose(1, 0, 2).reshape(d, N_EXPERTS * D_EXPERT).astype(BF16)
        w3c = exp_w3[l].transpose(1, 0, 2).reshape(d, N_EXPERTS * D_EXPERT).astype(BF16)
        w2c = exp_w2[l].reshape(N_EXPERTS * D_EXPERT, d).astype(BF16)
        xa = _post(xa, modt[l], lru_o, da_o, mla_o, w_out[l].astype(BF16), ln1_g[l][None, :], ln1_b[l][None, :],
                   rw, rb, w1c, w3c, w2c, ln2_g[l][None, :], ln2_b[l][None, :], nctx, alpha,
                   latent_only=(l == depth - 1))
    return xa
```

```python
import functools
import math

import jax
import jax.numpy as jnp
from jax import lax
from jax.experimental import pallas as pl
from jax.experimental.pallas import tpu as pltpu

F32 = jnp.float32
BF16 = jnp.bfloat16

GRID_W = 64
LRU_WIDTH = 256
LRU_BLOCKS = 4
CONV_W = 4
LRU_C = 8.0
DA_HEADS = 6
DA_QK = 32
DA_V = 2 * DA_QK
MLA_HEADS = 6
MLA_NOPE = 64
MLA_ROPE = 32
MLA_V = 64
Q_RANK = 256
KV_RANK = 128
MLA_SCALE = (MLA_NOPE + MLA_ROPE) ** -0.5
N_EXPERTS = 16
N_GROUPS = 4
EXPERTS_PER_GROUP = N_EXPERTS // N_GROUPS
D_EXPERT = 256
ROPE_THETA = 10000.0
LN_EPS = 1e-5
RMS_EPS = 1e-6

LANES = 128
SUBLANES = 8
TOKEN_TILE = 256
VMEM_LIMIT = 56 * 1024 * 1024

LOG2E = math.log2(math.e)
DA_QSCALE = DA_QK ** -0.5 * LOG2E
MLA_QSCALE = MLA_SCALE * LOG2E

DA_WIDTH = DA_HEADS * DA_V
C_CQ = 0
C_CKV = C_CQ + Q_RANK
C_KR = C_CKV + KV_RANK
C_DAQ = C_KR + LANES
C_DAK = C_DAQ + DA_WIDTH
C_DAV = C_DAK + DA_WIDTH
C_LRU = C_DAV + DA_WIDTH
C_END = C_LRU + 2 * LRU_WIDTH
PAIRS = DA_HEADS // 2
ROT_HALF = DA_QK // 2
VT_ROWS = DA_V + 16


def _params(sem):
    return pltpu.CompilerParams(dimension_semantics=sem, vmem_limit_bytes=VMEM_LIMIT)


def _const_spec(shape):
    nd = len(shape)
    return pl.BlockSpec(shape, lambda *_: (0,) * nd, pipeline_mode=pl.Buffered(1))


def _mod_kernel(c_ref, w_ref, b_ref, o_ref):
    c = c_ref[...]
    s = c * jax.nn.sigmoid(c)
    o_ref[0] = jnp.dot(s.astype(BF16), w_ref[0].astype(BF16), preferred_element_type=F32) + b_ref[0]


def _modulation(cc, w_mod, b_mod):
    depth, d, d6 = w_mod.shape
    r = cc.shape[0]
    tn = min(d6, 1536)
    return pl.pallas_call(
        _mod_kernel,
        grid=(depth, d6 // tn),
        in_specs=[
            pl.BlockSpec((r, d), lambda l, j: (0, 0)),
            pl.BlockSpec((1, d, tn), lambda l, j: (l, 0, j)),
            pl.BlockSpec((1, 1, tn), lambda l, j: (l, 0, j)),
        ],
        out_specs=pl.BlockSpec((1, r, tn), lambda l, j: (l, 0, j)),
        out_shape=jax.ShapeDtypeStruct((depth, r, d6), F32),
        compiler_params=_params(("parallel", "parallel")),
        name="modulation",
    )(cc, w_mod, b_mod.reshape(depth, 1, d6))


def _rotate(t, cosf, sinf):
    return t * cosf + pltpu.roll(t, LANES // 2, 1) * sinf


def _store_values_t(vt_ref, v):
    rows = v.shape[0]
    ones = jnp.ones((VT_ROWS - DA_V, rows), BF16)
    for j in range(PAIRS):
        t = v[:, LANES * j:LANES * (j + 1)].T.astype(BF16)
        for k in range(2):
            vt_ref[0, 2 * j + k, 0:DA_V, :] = t[DA_V * k:DA_V * (k + 1)]
            vt_ref[0, 2 * j + k, DA_V:VT_ROWS, :] = ones


def _in_kernel(xc_ref, xl_ref, mod_ref, w1_ref, wuq_ref, wuk_ref, wuv_ref, qn_ref, kvn_ref, cd_ref, sd_ref, cm_ref,
               sm_ref, lx_ref, lg_ref, dq_ref, dk_ref, dv_ref, mq_ref, mk_ref, mv_ref, *, n_ctx_tiles):
    x = jnp.where(pl.program_id(1) < n_ctx_tiles, xc_ref[0], xl_ref[0])
    mod = mod_ref[0, 0]
    u = (x * (1.0 + mod[1:2]) + mod[0:1]).astype(BF16)
    proj = lambda lo, hi: jnp.dot(u, w1_ref[:, lo:hi], preferred_element_type=F32)
    rot = functools.partial(_rotate, cosf=cd_ref[...], sinf=sd_ref[...])
    rot_mla = functools.partial(_rotate, cosf=cm_ref[...], sinf=sm_ref[...])

    y_mla = proj(C_CQ, C_DAQ)
    y_da = proj(C_DAQ, C_DAV)

    cq = y_mla[:, C_CQ:C_CKV]
    ckv = y_mla[:, C_CKV:C_KR]
    krp = y_mla[:, C_KR:C_DAQ]
    qn = (cq * lax.rsqrt(jnp.mean(cq * cq, axis=-1, keepdims=True) + RMS_EPS)) * qn_ref[...]
    kvn = ((ckv * lax.rsqrt(jnp.mean(ckv * ckv, axis=-1, keepdims=True) + RMS_EPS)) * kvn_ref[...]).astype(BF16)
    q = jnp.dot(qn.astype(BF16), wuq_ref[...], preferred_element_type=F32)
    kn = jnp.dot(kvn, wuk_ref[...], preferred_element_type=F32)

    for j in range(PAIRS):
        t = y_da[:, LANES * j:LANES * (j + 1)]
        dq_ref[0, LANES * j:LANES * (j + 1), :] = (rot(t) * DA_QSCALE).T.astype(BF16)
        t = y_da[:, DA_WIDTH + LANES * j:DA_WIDTH + LANES * (j + 1)]
        dk_ref[0, :, LANES * j:LANES * (j + 1)] = rot(t).astype(BF16)

    y_rest = proj(C_DAV, C_END)
    mv = jnp.dot(kvn, wuv_ref[...], preferred_element_type=F32)

    kr = rot_mla(krp)
    for h in range(MLA_HEADS):
        t = q[:, LANES * h:LANES * (h + 1)]
        mq_ref[0, h] = (rot_mla(t) * MLA_QSCALE).T.astype(BF16)
        mk_ref[0, h] = (kn[:, LANES * h:LANES * (h + 1)] + kr).astype(BF16)

    _store_values_t(dv_ref, y_rest[:, 0:DA_WIDTH])
    _store_values_t(mv_ref, mv)
    lx_ref[0] = y_rest[:, DA_WIDTH:DA_WIDTH + LRU_WIDTH]
    lg_ref[0] = y_rest[:, DA_WIDTH + LRU_WIDTH:DA_WIDTH + 2 * LRU_WIDTH]


def _stream_specs(rows, tm, d, nc, lat_shift, skip=0):
    return [pl.BlockSpec((rows, tm, d), lambda i, t: (i, jnp.minimum(t + skip, nc - 1), 0)),
            pl.BlockSpec((rows, tm, d), lambda i, t: (i, jnp.maximum(t + skip - lat_shift, 0), 0))]


def _in_proj(xc, xl, lat_shift, nt, modt, w1, wuq, wuk, wuv, qnorm, kvnorm, rot_tables, nctx):
    b, _, d = xl.shape
    tm = TOKEN_TILE
    nc = nctx // tm
    tok = lambda w: pl.BlockSpec((1, tm, w), lambda i, t: (i, t, 0))
    head_t = lambda r: pl.BlockSpec((1, DA_HEADS, r, tm), lambda i, t: (i, 0, 0, t))
    return pl.pallas_call(
        functools.partial(_in_kernel, n_ctx_tiles=nc),
        grid=(b, nt // tm),
        in_specs=_stream_specs(1, tm, d, nc, lat_shift) + [
            pl.BlockSpec((1, 1, 8, d), lambda i, t: (i, jnp.where(t >= nc, 1, 0), 0, 0)),
            _const_spec(w1.shape), _const_spec(wuq.shape), _const_spec(wuk.shape), _const_spec(wuv.shape),
            _const_spec(qnorm.shape), _const_spec(kvnorm.shape),
        ] + [pl.BlockSpec((tm, LANES), lambda i, t: (t, 0))] * len(rot_tables),
        out_specs=[
            tok(LRU_WIDTH), tok(LRU_WIDTH),
            pl.BlockSpec((1, DA_WIDTH, tm), lambda i, t: (i, 0, t)), tok(DA_WIDTH), head_t(VT_ROWS),
            head_t(LANES), pl.BlockSpec((1, MLA_HEADS, tm, LANES), lambda i, t: (i, 0, t, 0)), head_t(VT_ROWS),
        ],
        out_shape=[
            jax.ShapeDtypeStruct((b, nt, LRU_WIDTH), F32),
            jax.ShapeDtypeStruct((b, nt, LRU_WIDTH), F32),
            jax.ShapeDtypeStruct((b, DA_WIDTH, nt), BF16),
            jax.ShapeDtypeStruct((b, nt, DA_WIDTH), BF16),
            jax.ShapeDtypeStruct((b, DA_HEADS, VT_ROWS, nt), BF16),
            jax.ShapeDtypeStruct((b, MLA_HEADS, LANES, nt), BF16),
            jax.ShapeDtypeStruct((b, MLA_HEADS, nt, LANES), BF16),
            jax.ShapeDtypeStruct((b, MLA_HEADS, VT_ROWS, nt), BF16),
        ],
        compiler_params=_params(("parallel", "parallel")),
        name="in_proj",
    )(xc, xl, modt, w1, wuq, wuk, wuv, qnorm, kvnorm, *rot_tables)


def _gelu_tanh(x):
    return 0.5 * x * (1.0 + jnp.tanh(math.sqrt(2.0 / math.pi) * (x + 0.044715 * (x * x * x))))


def _lru_kernel(x_ref, g_ref, cw_ref, cb_ref, wa_ref, wi_ref, ba_ref, bi_ref, lam_ref, o_ref,
                y_s, a_s, s_s, h_s, *, nt, nctx, chunk):
    w = LRU_WIDTH
    tiles = chunk // SUBLANES
    n_chunks = nt // chunk
    sub = lax.broadcasted_iota(jnp.int32, (tiles, SUBLANES, w), 1)
    tile_i = lax.broadcasted_iota(jnp.int32, (tiles, SUBLANES, w), 0)

    def conv_chunk(c, carry):
        r0 = pl.multiple_of(c * chunk, chunk)
        lo = pl.multiple_of(jnp.maximum(r0 - SUBLANES, 0), SUBLANES)
        hi = pl.multiple_of(jnp.minimum(r0 + chunk, nt - SUBLANES), SUBLANES)
        x3 = jnp.concatenate([x_ref[0, pl.ds(lo, SUBLANES), :], x_ref[0, pl.ds(r0, chunk), :],
                              x_ref[0, pl.ds(hi, SUBLANES), :]], axis=0).reshape(tiles + 2, SUBLANES, w)
        sh1 = pltpu.roll(x3, 1, 1)
        sh2 = pltpu.roll(x3, 2, 1)
        sh7 = pltpu.roll(x3, SUBLANES - 1, 1)
        pos = r0 + tile_i * SUBLANES + sub
        in_ctx = pos < nctx
        seg_pos = jnp.where(in_ctx, pos, pos - nctx)
        seg_last = jnp.where(in_ctx, nctx - 1, nt - nctx - 1)
        zero = jnp.zeros((tiles, SUBLANES, w), F32)
        xm2 = jnp.where(seg_pos >= 2, jnp.where(sub >= 2, sh2[1:-1], sh2[0:-2]), zero)
        xm1 = jnp.where(seg_pos >= 1, jnp.where(sub >= 1, sh1[1:-1], sh1[0:-2]), zero)
        xp1 = jnp.where(seg_pos < seg_last, jnp.where(sub < SUBLANES - 1, sh7[1:-1], sh7[2:]), zero)
        y = cb_ref[...] + xm2 * cw_ref[0:1] + xm1 * cw_ref[1:2] + x3[1:-1] * cw_ref[2:3] + xp1 * cw_ref[3:4]
        y_s[pl.ds(r0, chunk), :] = y.reshape(chunk, w)
        return carry

    lax.fori_loop(0, n_chunks, conv_chunk, 0)

    nctx_t = nctx // SUBLANES
    nt_t = nt // SUBLANES

    for d in range(2):
        nlam = -lam_ref[d:d + 1]
        softplus = jnp.maximum(nlam, 0.0) + jnp.log1p(jnp.exp(-jnp.abs(nlam)))
        c8 = -LRU_C * softplus

        def gate_chunk(c, carry, d=d, c8=c8):
            r0 = pl.multiple_of(c * chunk, chunk)
            y = y_s[pl.ds(r0, chunk), :]
            yb = y.astype(BF16)
            r = jax.nn.sigmoid(jnp.dot(yb, wa_ref[d], preferred_element_type=F32) + ba_ref[d:d + 1])
            i = jax.nn.sigmoid(jnp.dot(yb, wi_ref[d], preferred_element_type=F32) + bi_ref[d:d + 1])
            log_a = c8 * r
            a = jnp.exp(log_a)
            th = jnp.tanh(log_a)
            u = jnp.sqrt(-2.0 * th / (1.0 - th)) * (i * y)
            a3 = a.reshape(tiles, SUBLANES, w)
            u3 = u.reshape(tiles, SUBLANES, w)
            for sft in (1, 2, 4):
                if d == 0:
                    ok = sub >= sft
                    ash = pltpu.roll(a3, sft, 1)
                    ush = pltpu.roll(u3, sft, 1)
                else:
                    ok = sub < SUBLANES - sft
                    ash = pltpu.roll(a3, SUBLANES - sft, 1)
                    ush = pltpu.roll(u3, SUBLANES - sft, 1)
                u3 = jnp.where(ok, a3 * ush + u3, u3)
                a3 = jnp.where(ok, a3 * ash, a3)
            a_s[pl.ds(r0, chunk), :] = a3.reshape(chunk, w)
            s_s[pl.ds(r0, chunk), :] = u3.reshape(chunk, w)
            return carry

        lax.fori_loop(0, n_chunks, gate_chunk, 0)

        def carry_tile(j, hprev, d=d):
            if d == 0:
                t = j
            else:
                t = jnp.where(j < nctx_t, nctx_t - 1 - j, nt_t - 1 - (j - nctx_t))
            r0 = pl.multiple_of(t * SUBLANES, SUBLANES)
            h = a_s[pl.ds(r0, SUBLANES), :] * hprev + s_s[pl.ds(r0, SUBLANES), :]
            if d == 0:
                h_s[pl.ds(r0, SUBLANES), :] = h
                return h[SUBLANES - 1:SUBLANES]
            h_s[pl.ds(r0, SUBLANES), :] = h_s[pl.ds(r0, SUBLANES), :] + h
            return h[0:1]

        lax.fori_loop(0, nt_t, carry_tile, jnp.zeros((1, w), F32), unroll=4)

    def out_chunk(c, carry):
        r0 = pl.multiple_of(c * chunk, chunk)
        o_ref[0, pl.ds(r0, chunk), :] = (h_s[pl.ds(r0, chunk), :] * _gelu_tanh(g_ref[0, pl.ds(r0, chunk), :])).astype(BF16)
        return carry

    lax.fori_loop(0, n_chunks, out_chunk, 0)


def _lru(lx, lg, conv_w, conv_b, wa, wi, ba, bi, lam, nctx):
    b, nt, w = lx.shape
    chunk = TOKEN_TILE
    seq = pl.BlockSpec((1, nt, w), lambda i: (i, 0, 0))
    return pl.pallas_call(
        functools.partial(_lru_kernel, nt=nt, nctx=nctx, chunk=chunk),
        grid=(b,),
        in_specs=[seq, seq, _const_spec(conv_w.shape), _const_spec(conv_b.shape), _const_spec(wa.shape),
                  _const_spec(wi.shape), _const_spec(ba.shape), _const_spec(bi.shape), _const_spec(lam.shape)],
        out_specs=seq,
        out_shape=jax.ShapeDtypeStruct((b, nt, w), BF16),
        scratch_shapes=[pltpu.VMEM((nt, w), F32)] * 4,
        compiler_params=_params(("parallel",)),
        name="rglru",
    )(lx, lg, conv_w, conv_b, wa, wi, ba, bi, lam)


DA_KEY_CHUNK = 256
MLA_KEY_CHUNK = 256


def _key_chunks(nk, nctx, size):
    chunks = [(0, nctx)]
    chunks += [(s, min(size, nk - s)) for s in range(nctx, nk, size)]
    return chunks


def _attend_t(chains, chunks):
    def scores(n, ci):
        q_t, key, _ = chains[n]
        return jnp.dot(key(*chunks[ci]), q_t, preferred_element_type=F32).astype(BF16)

    s = [scores(n, 0) for n in range(len(chains))]
    state = [None] * len(chains)
    for ci in range(len(chunks)):
        for n, (_, _, value_t) in enumerate(chains):
            cm = jnp.max(s[n], axis=0, keepdims=True)
            if ci == 0:
                m_new = cm
            else:
                m_old, acc = state[n]
                m_new = jnp.maximum(m_old, cm)
            p = jnp.exp2(s[n] - m_new)
            if ci + 1 < len(chunks):
                s[n] = scores(n, ci + 1)
            pv = jnp.dot(value_t(*chunks[ci]), p, preferred_element_type=F32)
            if ci > 0:
                pv = acc * jnp.exp2(m_old.astype(F32) - m_new.astype(F32)) + pv
            state[n] = (m_new, pv)
    return [acc[0:DA_V] / acc[DA_V:DA_V + 1] for _, acc in state]


def _da_kernel(q_ref, k_ref, vt_ref, dl_ref, g_ref, li_ref, o_ref, *, nt, nctx, tq):
    half = pl.program_id(1)
    row = lax.broadcasted_iota(jnp.int32, (LANES, tq), 0)
    lane = lax.broadcasted_iota(jnp.int32, (tq, LANES), 1)
    dl = dl_ref[...]
    lam_init = li_ref[...]
    lam = (jnp.exp(jnp.sum(dl[0:1] * dl[1:2], axis=-1, keepdims=True))
           - jnp.exp(jnp.sum(dl[2:3] * dl[3:4], axis=-1, keepdims=True)) + lam_init)
    zero = jnp.zeros((LANES, tq), BF16)

    def attend(q0, nk):
        chains = []
        for j in range(PAIRS):
            q_t = q_ref[0, LANES * j:LANES * (j + 1), pl.ds(q0, tq)]
            key = lambda start, size, j=j: k_ref[0, start:start + size, LANES * j:LANES * (j + 1)]
            value_t = lambda start, size, j=j: vt_ref[0, 2 * j + half, :, start:start + size]
            for mi in range(2):
                mine = (row & (LANES // 2 - ROT_HALF)) == (2 * half + mi) * ROT_HALF
                chains.append((jnp.where(mine, q_t, zero), key, value_t))
        o = _attend_t(chains, _key_chunks(nk, nctx, DA_KEY_CHUNK))
        for j in range(PAIRS):
            d = o[2 * j] - lam * o[2 * j + 1]
            d = d * lax.rsqrt(jnp.mean(d * d, axis=0, keepdims=True) + RMS_EPS)
            both = jnp.concatenate([d, d], axis=0).T
            new = (both * g_ref[...] * (1.0 - lam_init)).astype(BF16)
            slab = (0, pl.ds(q0, tq), slice(LANES * j, LANES * (j + 1)))

            @pl.when(half == 0)
            def _():
                o_ref[slab] = new

            @pl.when(half == 1)
            def _():
                o_ref[slab] = jnp.where(lane >= DA_V, new, o_ref[slab])

    _for_query_blocks(attend, nt, nctx, tq)


def _for_query_blocks(attend, nt, nctx, tq):
    for t in range(nctx // tq):
        attend(t * tq, nctx)

    def latent_block(t, carry):
        attend(pl.multiple_of(nctx + t * tq, tq), nt)
        return carry

    lax.fori_loop(0, (nt - nctx) // tq, latent_block, 0)


def _da_attn(dq_t, dk, dv_t, dlam, gpair, lam_init, nctx):
    b, nt, _ = dk.shape
    tq = TOKEN_TILE
    return pl.pallas_call(
        functools.partial(_da_kernel, nt=nt, nctx=nctx, tq=tq),
        grid=(b, 2),
        in_specs=[
            pl.BlockSpec((1, DA_WIDTH, nt), lambda i, h: (i, 0, 0)),
            pl.BlockSpec((1, nt, DA_WIDTH), lambda i, h: (i, 0, 0)),
            pl.BlockSpec((1, DA_HEADS, VT_ROWS, nt), lambda i, h: (i, 0, 0, 0)),
            _const_spec(dlam.shape), _const_spec(gpair.shape), _const_spec(lam_init.shape),
        ],
        out_specs=pl.BlockSpec((1, nt, DA_WIDTH), lambda i, h: (i, 0, 0)),
        out_shape=jax.ShapeDtypeStruct((b, nt, DA_WIDTH), BF16),
        compiler_params=_params(("parallel", "arbitrary")),
        name="diff_attn",
    )(dq_t, dk, dv_t, dlam, gpair, lam_init)


def _mla_kernel(q_ref, k_ref, vt_ref, o_ref, *, nt, nctx, tq):
    def attend(q0, nk):
        chains = []
        for hh in range(MLA_HEADS):
            key = lambda start, size, hh=hh: k_ref[0, hh, start:start + size, :]
            value_t = lambda start, size, hh=hh: vt_ref[0, hh, :, start:start + size]
            chains.append((q_ref[0, hh, :, pl.ds(q0, tq)], key, value_t))
        o = _attend_t(chains, _key_chunks(nk, nctx, MLA_KEY_CHUNK))
        for j in range(MLA_HEADS // 2):
            o_ref[0, pl.ds(q0, tq), LANES * j:LANES * (j + 1)] = jnp.concatenate(o[2 * j:2 * j + 2], axis=0).T.astype(BF16)

    _for_query_blocks(attend, nt, nctx, tq)


def _mla_attn(mq_t, mk, mv_t, nctx):
    b, nh, nt, _ = mk.shape
    tq = TOKEN_TILE
    return pl.pallas_call(
        functools.partial(_mla_kernel, nt=nt, nctx=nctx, tq=tq),
        grid=(b,),
        in_specs=[
            pl.BlockSpec((1, nh, LANES, nt), lambda i: (i, 0, 0, 0)),
            pl.BlockSpec((1, nh, nt, LANES), lambda i: (i, 0, 0, 0)),
            pl.BlockSpec((1, nh, VT_ROWS, nt), lambda i: (i, 0, 0, 0)),
        ],
        out_specs=pl.BlockSpec((1, nt, nh * MLA_V), lambda i: (i, 0, 0)),
        out_shape=jax.ShapeDtypeStruct((b, nt, nh * MLA_V), BF16),
        compiler_params=_params(("parallel",)),
        name="mla_attn",
    )(mq_t, mk, mv_t)


def _layer_norm(z, g, b):
    mu = jnp.mean(z, axis=-1, keepdims=True)
    zc = z - mu
    var = jnp.mean(zc * zc, axis=-1, keepdims=True)
    return (zc * lax.rsqrt(var + LN_EPS)) * g + b


def _router_gates(logits, rb):
    scores = jax.nn.sigmoid(logits)
    sel = scores + rb
    lane = lax.broadcasted_iota(jnp.int32, logits.shape, 1)
    r = lane & (EXPERTS_PER_GROUP - 1)
    grp = (lane >> 2) & (N_GROUPS - 1)

    def in_group(x, k):
        return jnp.where(r >= k, pltpu.roll(x, k, 1), pltpu.roll(x, LANES - EXPERTS_PER_GROUP + k, 1))

    others = [in_group(sel, k) for k in (1, 2, 3)]
    pair_max = sel + jnp.maximum(jnp.maximum(others[0], others[1]), others[2])
    grp_score = jnp.maximum(jnp.maximum(pair_max, in_group(pair_max, 1)),
                            jnp.maximum(in_group(pair_max, 2), in_group(pair_max, 3)))
    in_best = None
    for k in (1, 2, 3):
        other = pltpu.roll(grp_score, EXPERTS_PER_GROUP * k, 1)
        wins = (grp_score > other) | ((grp_score == other) & (grp < k))
        in_best = wins if in_best is None else (in_best & wins)
    beaten = jnp.zeros(logits.shape, F32)
    for k, o in zip((1, 2, 3), others):
        beats = (o > sel) | ((o == sel) & (r >= k))
        beaten = beaten + jnp.where(beats, 1.0, 0.0)
    chosen = in_best & (beaten < 2.0)
    sc = jnp.where(chosen, scores, 0.0)
    tot = sc + in_group(sc, 1) + in_group(sc, 2) + in_group(sc, 3)
    return jnp.where(chosen, sc / tot, 0.0), in_best


ROW_BLOCK = 160
BF16_ROWS = 16


def _grouped_experts(v, gates, in_best, w1_ref, w3_ref, w2_ref, xs_ref, gs_ref, ys_ref):
    rows, d = v.shape
    per = EXPERTS_PER_GROUP * D_EXPERT
    lane = lax.broadcasted_iota(jnp.int32, (rows, LANES), 1)
    gsel = jnp.where(in_best & ((lane & (EXPERTS_PER_GROUP - 1)) == 0) & (lane < N_EXPERTS), 1.0, 0.0)
    gsel_b = gsel.astype(BF16)
    ri = lax.broadcasted_iota(jnp.int32, (rows, rows), 0)
    ci = lax.broadcasted_iota(jnp.int32, (rows, rows), 1)
    onehot = lambda cond: jnp.where(cond, 1.0, 0.0).astype(BF16)

    tot = jnp.sum(gsel, axis=0, keepdims=True)
    lane1 = lax.broadcasted_iota(jnp.int32, (1, LANES), 1)
    cnt = [jnp.sum(jnp.where(lane1 == EXPERTS_PER_GROUP * g, tot, 0.0)).astype(jnp.int32) for g in range(N_GROUPS)]
    off = [jnp.int32(0)]
    for g in range(1, N_GROUPS):
        off.append(off[-1] + cnt[g - 1])

    off_lane = sum(jnp.where(lane1 == EXPERTS_PER_GROUP * g, off[g].astype(F32), 0.0) for g in range(N_GROUPS))
    before = jnp.dot(onehot(ci < ri), gsel_b, preferred_element_type=F32)
    pos_col = jnp.sum(gsel * (before + off_lane), axis=1, keepdims=True)
    sub8 = lax.broadcasted_iota(jnp.int32, (SUBLANES, LANES), 0)
    lane8 = lax.broadcasted_iota(jnp.int32, (SUBLANES, LANES), 1)
    pick = jnp.where((lane8 == EXPERTS_PER_GROUP * sub8) & (sub8 < N_GROUPS), 1.0, 0.0).astype(BF16)
    gsel_t = lax.dot_general(pick, gsel_b, (((1,), (1,)), ((), ())), preferred_element_type=F32)
    before_t = jnp.dot(gsel_t.astype(BF16), onehot(ri < ci), preferred_element_type=F32)
    sub_col = lax.broadcasted_iota(jnp.int32, (SUBLANES, 1), 0)
    off_sub = sum(jnp.where(sub_col == g, off[g].astype(F32), 0.0) for g in range(N_GROUPS))
    pos_row = jnp.sum(gsel_t * (before_t + off_sub), axis=0, keepdims=True)
    perm = onehot(pos_row == ri.astype(F32))
    perm_t = onehot(pos_col == ci.astype(F32))

    xs_ref[...] = jnp.dot(perm, v, preferred_element_type=F32).astype(BF16)
    g_hi = gates.astype(BF16)
    g_lo = (gates - g_hi.astype(F32)).astype(BF16)
    gs_ref[...] = jnp.dot(perm, g_hi, preferred_element_type=F32) + jnp.dot(perm, g_lo, preferred_element_type=F32)
    ys_ref[...] = jnp.zeros(ys_ref.shape, F32)

    row_in_block = lax.broadcasted_iota(jnp.int32, (ROW_BLOCK, LANES), 0)
    for g in range(N_GROUPS):
        first = (off[g] // BF16_ROWS) * BF16_ROWS
        end = off[g] + cnt[g]
        for k in range(-(-rows // ROW_BLOCK)):
            lo = first + k * ROW_BLOCK

            @pl.when((lo < end) & (cnt[g] > 0))
            def _(lo=lo, g=g):
                st = pl.multiple_of(jnp.minimum(lo, rows - ROW_BLOCK), BF16_ROWS)
                xb = xs_ref[pl.ds(st, ROW_BLOCK), :]
                gb = jnp.where(row_in_block + st >= lo, gs_ref[pl.ds(st, ROW_BLOCK), :], 0.0)
                h1 = jnp.dot(xb, w1_ref[:, g * per:(g + 1) * per], preferred_element_type=F32)
                h3 = jnp.dot(xb, w3_ref[:, g * per:(g + 1) * per], preferred_element_type=F32)
                hh = (h1 * jax.nn.sigmoid(h1)) * h3
                parts = []
                for j in range(EXPERTS_PER_GROUP):
                    e = g * EXPERTS_PER_GROUP + j
                    parts.append((hh[:, j * D_EXPERT:(j + 1) * D_EXPERT] * gb[:, e:e + 1]).astype(BF16))
                ys_ref[pl.ds(st, ROW_BLOCK), :] += jnp.dot(jnp.concatenate(parts, axis=-1),
                                                           w2_ref[g * per:(g + 1) * per, :],
                                                           preferred_element_type=F32)

    return jnp.dot(perm_t, ys_ref[...].astype(BF16), preferred_element_type=F32)


def _post_kernel(xc_ref, xl_ref, mod_ref, lru_ref, da_ref, mla_ref, wo_ref, g1_ref, b1_ref, rw_ref, rb_ref, w1_ref,
                 w3_ref, w2_ref, g_ref, b_ref, o_ref, xs_ref, gs_ref, ys_ref, *, alpha, n_ctx_tiles):
    nb, tm, d = xl_ref.shape
    rows = nb * tm
    x = jnp.where(pl.program_id(1) < n_ctx_tiles, xc_ref[...], xl_ref[...]).reshape(rows, d)
    mod = mod_ref[:, 0]
    per_row = lambda k: jnp.broadcast_to(mod[:, k:k + 1], (nb, tm, d)).reshape(rows, d)
    a = jnp.concatenate([lru_ref[...], da_ref[...], mla_ref[...]], axis=-1).reshape(rows, -1)
    o = jnp.dot(a, wo_ref[...], preferred_element_type=F32)
    x1 = _layer_norm(alpha * x + per_row(2) * o, g1_ref[...], b1_ref[...])
    v = (x1 * (1.0 + per_row(4)) + per_row(3)).astype(BF16)
    gates, in_best = _router_gates(jnp.dot(v, rw_ref[...], preferred_element_type=F32), rb_ref[...])
    f = _grouped_experts(v, gates, in_best, w1_ref, w3_ref, w2_ref, xs_ref, gs_ref, ys_ref)
    o_ref[...] = _layer_norm(alpha * x1 + per_row(5) * f, g_ref[...], b_ref[...]).reshape(nb, tm, d)


POST_BATCH = 2


def _post(xc, xl, lat_shift, modt, lru_o, da_o, mla_o, wo, g1, b1, rw, rb, w1c, w3c, w2c, g2, b2, nctx, alpha,
          latent_only):
    b, nt, _ = lru_o.shape
    d = xl.shape[-1]
    tm = TOKEN_TILE
    nc = nctx // tm
    skip = nc if latent_only else 0
    nb = POST_BATCH if b % POST_BATCH == 0 else 1
    tok = lambda w: pl.BlockSpec((nb, tm, w), lambda i, t: (i, t + skip, 0))
    consts = (wo, g1, b1, rw, rb, w1c, w3c, w2c, g2, b2)
    return pl.pallas_call(
        functools.partial(_post_kernel, alpha=alpha, n_ctx_tiles=nc - skip),
        grid=(b // nb, nt // tm - skip),
        in_specs=_stream_specs(nb, tm, d, nc, lat_shift, skip) + [
            pl.BlockSpec((nb, 1, 8, d), lambda i, t: (i, jnp.where(t + skip >= nc, 1, 0), 0, 0)),
            tok(LRU_WIDTH), tok(da_o.shape[-1]), tok(mla_o.shape[-1]),
        ] + [_const_spec(c.shape) for c in consts],
        out_specs=pl.BlockSpec((nb, tm, d), lambda i, t: (i, t, 0)),
        out_shape=jax.ShapeDtypeStruct((b, nt - skip * tm, d), F32),
        scratch_shapes=[pltpu.VMEM((nb * tm, d), BF16), pltpu.VMEM((nb * tm, LANES), F32),
                        pltpu.VMEM((nb * tm, d), F32)],
        compiler_params=_params(("parallel", "parallel")),
        name="post",
    )(xc, xl, modt, lru_o, da_o, mla_o, *consts)


def _rotary_tables(n, nctx):
    rows = n // GRID_W
    row = jnp.repeat(jnp.arange(rows), GRID_W).astype(F32)
    col = jnp.tile(jnp.arange(GRID_W), rows).astype(F32)
    n_freq = DA_QK // 4
    inv = ROPE_THETA ** (-jnp.arange(n_freq, dtype=F32) / n_freq)
    ang = jnp.concatenate([row[:, None] * inv, col[:, None] * inv], axis=-1)
    ang = jnp.concatenate([jnp.zeros((nctx, DA_QK // 2), F32), ang], axis=0)
    c, s = jnp.cos(ang), jnp.sin(ang)
    cos_d = jnp.tile(c, (1, LANES // ROT_HALF))
    sin_d = jnp.concatenate([jnp.tile(-s, (1, 4)), jnp.tile(s, (1, 4))], axis=-1)
    one = jnp.ones_like(c)
    zero = jnp.zeros_like(c)
    cos_m = jnp.concatenate([one, one, c, one, one, one, c, one], axis=-1)
    sin_m = jnp.concatenate([zero, zero, -s, zero, zero, zero, s, zero], axis=-1)
    return cos_d, sin_d, cos_m, sin_m


def _rot_lanes(w, groups):
    lead = w.shape[:-1]
    w = w.reshape(lead + (groups, 2, ROT_HALF))
    return jnp.swapaxes(w, -3, -2).reshape(lead + (groups * 2 * ROT_HALF,))


def _mla_lanes(nope, rot):
    z = jnp.zeros(nope.shape[:-1] + (ROT_HALF,), nope.dtype)
    return jnp.concatenate([nope[..., :MLA_NOPE // 2], rot[..., :ROT_HALF], z,
                            nope[..., MLA_NOPE // 2:], rot[..., ROT_HALF:], z], axis=-1)


def _pack_in_weight(w_in):
    d = w_in.shape[0]
    n_lru = 2 * LRU_WIDTH
    n_lru_da = n_lru + 3 * DA_WIDTH
    n_rank = Q_RANK + KV_RANK
    wkr = w_in[:, n_lru_da + n_rank:n_lru_da + n_rank + MLA_ROPE]
    krp = _mla_lanes(jnp.zeros((d, MLA_NOPE), F32), wkr)
    wqk = w_in[:, n_lru:n_lru + 2 * DA_WIDTH].reshape(d, 2 * PAIRS, LANES)
    wqk = _rot_lanes(wqk, LANES // DA_QK).reshape(d, 2 * DA_WIDTH)
    return jnp.concatenate([w_in[:, n_lru_da:n_lru_da + n_rank], krp, wqk, w_in[:, n_lru + 2 * DA_WIDTH:n_lru_da],
                            w_in[:, :n_lru]], axis=-1).astype(BF16)


def _pack_uq(w_uq):
    r = w_uq.shape[0]
    w = w_uq.reshape(r, MLA_HEADS, MLA_NOPE + MLA_ROPE)
    return _mla_lanes(w[..., :MLA_NOPE], w[..., MLA_NOPE:]).reshape(r, MLA_HEADS * LANES).astype(BF16)


def _pack_ukv(w_ukv):
    r = w_ukv.shape[0]
    w = w_ukv.reshape(r, MLA_HEADS, MLA_NOPE + MLA_V)
    wk = _mla_lanes(w[..., :MLA_NOPE], jnp.zeros((r, MLA_HEADS, MLA_ROPE), F32)).reshape(r, MLA_HEADS * LANES)
    wv = w[..., MLA_NOPE:].reshape(r, MLA_HEADS * MLA_V)
    return wk.astype(BF16), wv.astype(BF16)


def _block_diag(w):
    nd, nb, bs, _ = w.shape
    eye = jnp.eye(nb, dtype=w.dtype)
    return jnp.einsum('dhij,hg->dhigj', w, eye).reshape(nd, nb * bs, nb * bs)


def kernel(x, c, ctx, c_ctx, w_mod, b_mod, w_in, w_out, conv_w, conv_b, lru_wa, lru_ba, lru_wi, lru_bi, lru_lambda, diff_lambda, diff_norm, mla_q_norm, mla_kv_norm, mla_w_uq, mla_w_ukv, ln1_g, ln1_b, ln2_g, ln2_b, router_w, router_b, exp_w1, exp_w3, exp_w2):
    bsz, n, d = x.shape
    nctx = ctx.shape[1]
    depth = w_mod.shape[0]
    alpha = (2 * depth) ** 0.25
    assert nctx % TOKEN_TILE == 0 and n % TOKEN_TILE == 0 and n % GRID_W == 0

    rows = -(-(bsz + 1) // SUBLANES) * SUBLANES
    cc = jnp.concatenate([c, c_ctx[None, :], jnp.zeros((rows - bsz - 1, d), F32)], axis=0)
    mod = _modulation(cc, w_mod, b_mod).reshape(depth, rows, 6, d)
    mod = jnp.pad(mod, ((0, 0), (0, 0), (0, 2), (0, 0)))
    mod_ctx = jnp.broadcast_to(mod[:, bsz][:, None], (depth, bsz, 8, d))
    modt = jnp.stack([mod_ctx, mod[:, :bsz]], axis=2)

    rot_tables = _rotary_tables(n, nctx)
    rw = jnp.tile(router_w, (1, LANES // N_EXPERTS)).astype(BF16)
    rb = jnp.tile(router_b, LANES // N_EXPERTS)[None, :].astype(F32)
    gpair = jnp.tile(diff_norm, (1, LANES // DA_V))

    xc, xl, lat_shift = ctx, x, nctx // TOKEN_TILE
    for l in range(depth):
        lam_init = jnp.full((1, 1), 0.8 - 0.6 * math.exp(-0.3 * l), F32)
        wuk, wuv = _pack_ukv(mla_w_ukv[l])
        lx, lg, dq_t, dk, dv_t, mq_t, mk, mv_t = _in_proj(
            xc, xl, lat_shift, nctx + n, modt[l], _pack_in_weight(w_in[l]), _pack_uq(mla_w_uq[l]), wuk, wuv,
            mla_q_norm[l][None, :], mla_kv_norm[l][None, :], rot_tables, nctx)
        lru_o = _lru(lx, lg, conv_w[l], conv_b[l][None, :], _block_diag(lru_wa[l]).astype(BF16),
                     _block_diag(lru_wi[l]).astype(BF16), lru_ba[l], lru_bi[l], lru_lambda[l], nctx)
        da_o = _da_attn(dq_t, dk, dv_t, diff_lambda[l], gpair[l][None, :], lam_init, nctx)
        mla_o = _mla_attn(mq_t, mk, mv_t, nctx)
        w1c = exp_w1[l].transpose(1, 0, 2).reshape(d, N_EXPERTS * D_EXPERT).astype(BF16)
        w3c = exp_w3[l].transpose(1, 0, 2).reshape(d, N_EXPERTS * D_EXPERT).astype(BF16)
        w2c = exp_w2[l].reshape(N_EXPERTS * D_EXPERT, d).astype(BF16)
        xa = _post(xc, xl, lat_shift, modt[l], lru_o, da_o, mla_o, w_out[l].astype(BF16), ln1_g[l][None, :],
                   ln1_b[l][None, :], rw, rb, w1c, w3c, w2c, ln2_g[l][None, :], ln2_b[l][None, :], nctx, alpha,
                   latent_only=(l == depth - 1))
        xc, xl, lat_shift = xa, xa, 0
    return xa
```

```python
import functools
import math

import jax
import jax.numpy as jnp
from jax import lax
from jax.experimental import pallas as pl
from jax.experimental.pallas import tpu as pltpu

F32 = jnp.float32
BF16 = jnp.bfloat16

GRID_W = 64
LRU_WIDTH = 256
LRU_BLOCKS = 4
CONV_W = 4
LRU_C = 8.0
DA_HEADS = 6
DA_QK = 32
DA_V = 2 * DA_QK
MLA_HEADS = 6
MLA_NOPE = 64
MLA_ROPE = 32
MLA_V = 64
Q_RANK = 256
KV_RANK = 128
MLA_SCALE = (MLA_NOPE + MLA_ROPE) ** -0.5
N_EXPERTS = 16
N_GROUPS = 4
EXPERTS_PER_GROUP = N_EXPERTS // N_GROUPS
D_EXPERT = 256
ROPE_THETA = 10000.0
LN_EPS = 1e-5
RMS_EPS = 1e-6

LANES = 128
SUBLANES = 8
TOKEN_TILE = 256
VMEM_LIMIT = 56 * 1024 * 1024

LOG2E = math.log2(math.e)
DA_QSCALE = DA_QK ** -0.5 * LOG2E
MLA_QSCALE = MLA_SCALE * LOG2E

DA_WIDTH = DA_HEADS * DA_V
C_CQ = 0
C_CKV = C_CQ + Q_RANK
C_KR = C_CKV + KV_RANK
C_DAQ = C_KR + LANES
C_DAK = C_DAQ + DA_WIDTH
C_DAV = C_DAK + DA_WIDTH
C_LRU = C_DAV + DA_WIDTH
C_END = C_LRU + 2 * LRU_WIDTH
PAIRS = DA_HEADS // 2
ROT_HALF = DA_QK // 2
VT_ROWS = DA_V + 16


def _params(sem):
    return pltpu.CompilerParams(dimension_semantics=sem, vmem_limit_bytes=VMEM_LIMIT)


def _const_spec(shape):
    nd = len(shape)
    return pl.BlockSpec(shape, lambda *_: (0,) * nd, pipeline_mode=pl.Buffered(1))


def _mod_kernel(c_ref, w_ref, b_ref, o_ref):
    c = c_ref[...]
    s = c * jax.nn.sigmoid(c)
    o_ref[0] = jnp.dot(s.astype(BF16), w_ref[0].astype(BF16), preferred_element_type=F32) + b_ref[0]


def _modulation(cc, w_mod, b_mod):
    depth, d, d6 = w_mod.shape
    r = cc.shape[0]
    tn = min(d6, 1536)
    return pl.pallas_call(
        _mod_kernel,
        grid=(depth, d6 // tn),
        in_specs=[
            pl.BlockSpec((r, d), lambda l, j: (0, 0)),
            pl.BlockSpec((1, d, tn), lambda l, j: (l, 0, j)),
            pl.BlockSpec((1, 1, tn), lambda l, j: (l, 0, j)),
        ],
        out_specs=pl.BlockSpec((1, r, tn), lambda l, j: (l, 0, j)),
        out_shape=jax.ShapeDtypeStruct((depth, r, d6), F32),
        compiler_params=_params(("parallel", "parallel")),
        name="modulation",
    )(cc, w_mod, b_mod.reshape(depth, 1, d6))


def _rotate(t, cosf, sinf):
    return t * cosf + pltpu.roll(t, LANES // 2, 1) * sinf


def _store_values_t(vt_ref, v):
    rows = v.shape[0]
    ones = jnp.ones((VT_ROWS - DA_V, rows), BF16)
    for j in range(PAIRS):
        t = v[:, LANES * j:LANES * (j + 1)].T.astype(BF16)
        for k in range(2):
            vt_ref[0, 2 * j + k, 0:DA_V, :] = t[DA_V * k:DA_V * (k + 1)]
            vt_ref[0, 2 * j + k, DA_V:VT_ROWS, :] = ones


def _in_kernel(xc_ref, xl_ref, mod_ref, w1_ref, wuq_ref, wuk_ref, wuv_ref, qn_ref, kvn_ref, cd_ref, sd_ref, cm_ref,
               sm_ref, lx_ref, lg_ref, dq_ref, dk_ref, dv_ref, mq_ref, mk_ref, mv_ref, *, n_ctx_tiles):
    x = jnp.where(pl.program_id(1) < n_ctx_tiles, xc_ref[0], xl_ref[0])
    mod = mod_ref[0, 0]
    u = (x * (1.0 + mod[1:2]) + mod[0:1]).astype(BF16)
    proj = lambda lo, hi: jnp.dot(u, w1_ref[:, lo:hi], preferred_element_type=F32)
    rot = functools.partial(_rotate, cosf=cd_ref[...], sinf=sd_ref[...])
    rot_mla = functools.partial(_rotate, cosf=cm_ref[...], sinf=sm_ref[...])

    y_mla = proj(C_CQ, C_DAQ)
    y_da = proj(C_DAQ, C_DAV)

    cq = y_mla[:, C_CQ:C_CKV]
    ckv = y_mla[:, C_CKV:C_KR]
    krp = y_mla[:, C_KR:C_DAQ]
    qn = (cq * lax.rsqrt(jnp.mean(cq * cq, axis=-1, keepdims=True) + RMS_EPS)) * qn_ref[...]
    kvn = ((ckv * lax.rsqrt(jnp.mean(ckv * ckv, axis=-1, keepdims=True) + RMS_EPS)) * kvn_ref[...]).astype(BF16)
    q = jnp.dot(qn.astype(BF16), wuq_ref[...], preferred_element_type=F32)
    kn = jnp.dot(kvn, wuk_ref[...], preferred_element_type=F32)

    for j in range(PAIRS):
        t = y_da[:, LANES * j:LANES * (j + 1)]
        dq_ref[0, LANES * j:LANES * (j + 1), :] = (rot(t) * DA_QSCALE).T.astype(BF16)
        t = y_da[:, DA_WIDTH + LANES * j:DA_WIDTH + LANES * (j + 1)]
        dk_ref[0, :, LANES * j:LANES * (j + 1)] = rot(t).astype(BF16)

    y_rest = proj(C_DAV, C_END)
    mv = jnp.dot(kvn, wuv_ref[...], preferred_element_type=F32)

    kr = rot_mla(krp)
    for h in range(MLA_HEADS):
        t = q[:, LANES * h:LANES * (h + 1)]
        mq_ref[0, h] = (rot_mla(t) * MLA_QSCALE).T.astype(BF16)
        mk_ref[0, h] = (kn[:, LANES * h:LANES * (h + 1)] + kr).astype(BF16)

    _store_values_t(dv_ref, y_rest[:, 0:DA_WIDTH])
    _store_values_t(mv_ref, mv)
    lx_ref[0] = y_rest[:, DA_WIDTH:DA_WIDTH + LRU_WIDTH]
    lg_ref[0] = y_rest[:, DA_WIDTH + LRU_WIDTH:DA_WIDTH + 2 * LRU_WIDTH]


def _stream_specs(rows, tm, d, nc, lat_shift, skip=0):
    return [pl.BlockSpec((rows, tm, d), lambda i, t: (i, jnp.minimum(t + skip, nc - 1), 0)),
            pl.BlockSpec((rows, tm, d), lambda i, t: (i, jnp.maximum(t + skip - lat_shift, 0), 0))]


def _in_proj(xc, xl, lat_shift, nt, modt, w1, wuq, wuk, wuv, qnorm, kvnorm, rot_tables, nctx):
    b, _, d = xl.shape
    tm = TOKEN_TILE
    nc = nctx // tm
    tok = lambda w: pl.BlockSpec((1, tm, w), lambda i, t: (i, t, 0))
    head_t = lambda r: pl.BlockSpec((1, DA_HEADS, r, tm), lambda i, t: (i, 0, 0, t))
    return pl.pallas_call(
        functools.partial(_in_kernel, n_ctx_tiles=nc),
        grid=(b, nt // tm),
        in_specs=_stream_specs(1, tm, d, nc, lat_shift) + [
            pl.BlockSpec((1, 1, 8, d), lambda i, t: (i, jnp.where(t >= nc, 1, 0), 0, 0)),
            _const_spec(w1.shape), _const_spec(wuq.shape), _const_spec(wuk.shape), _const_spec(wuv.shape),
            _const_spec(qnorm.shape), _const_spec(kvnorm.shape),
        ] + [pl.BlockSpec((tm, LANES), lambda i, t: (t, 0))] * len(rot_tables),
        out_specs=[
            tok(LRU_WIDTH), tok(LRU_WIDTH),
            pl.BlockSpec((1, DA_WIDTH, tm), lambda i, t: (i, 0, t)), tok(DA_WIDTH), head_t(VT_ROWS),
            head_t(LANES), pl.BlockSpec((1, MLA_HEADS, tm, LANES), lambda i, t: (i, 0, t, 0)), head_t(VT_ROWS),
        ],
        out_shape=[
            jax.ShapeDtypeStruct((b, nt, LRU_WIDTH), F32),
            jax.ShapeDtypeStruct((b, nt, LRU_WIDTH), F32),
            jax.ShapeDtypeStruct((b, DA_WIDTH, nt), BF16),
            jax.ShapeDtypeStruct((b, nt, DA_WIDTH), BF16),
            jax.ShapeDtypeStruct((b, DA_HEADS, VT_ROWS, nt), BF16),
            jax.ShapeDtypeStruct((b, MLA_HEADS, LANES, nt), BF16),
            jax.ShapeDtypeStruct((b, MLA_HEADS, nt, LANES), BF16),
            jax.ShapeDtypeStruct((b, MLA_HEADS, VT_ROWS, nt), BF16),
        ],
        compiler_params=_params(("parallel", "parallel")),
        name="in_proj",
    )(xc, xl, modt, w1, wuq, wuk, wuv, qnorm, kvnorm, *rot_tables)


def _gelu_tanh(x):
    return 0.5 * x * (1.0 + jnp.tanh(math.sqrt(2.0 / math.pi) * (x + 0.044715 * (x * x * x))))


def _lru_kernel(x_ref, g_ref, cw_ref, cb_ref, wa_ref, wi_ref, ba_ref, bi_ref, lam_ref, o_ref,
                y_s, a_s, s_s, h_s, *, nt, nctx, chunk):
    w = LRU_WIDTH
    tiles = chunk // SUBLANES
    n_chunks = nt // chunk
    sub = lax.broadcasted_iota(jnp.int32, (tiles, SUBLANES, w), 1)
    tile_i = lax.broadcasted_iota(jnp.int32, (tiles, SUBLANES, w), 0)

    def conv_chunk(c, carry):
        r0 = pl.multiple_of(c * chunk, chunk)
        lo = pl.multiple_of(jnp.maximum(r0 - SUBLANES, 0), SUBLANES)
        hi = pl.multiple_of(jnp.minimum(r0 + chunk, nt - SUBLANES), SUBLANES)
        x3 = jnp.concatenate([x_ref[0, pl.ds(lo, SUBLANES), :], x_ref[0, pl.ds(r0, chunk), :],
                              x_ref[0, pl.ds(hi, SUBLANES), :]], axis=0).reshape(tiles + 2, SUBLANES, w)
        sh1 = pltpu.roll(x3, 1, 1)
        sh2 = pltpu.roll(x3, 2, 1)
        sh7 = pltpu.roll(x3, SUBLANES - 1, 1)
        pos = r0 + tile_i * SUBLANES + sub
        in_ctx = pos < nctx
        seg_pos = jnp.where(in_ctx, pos, pos - nctx)
        seg_last = jnp.where(in_ctx, nctx - 1, nt - nctx - 1)
        zero = jnp.zeros((tiles, SUBLANES, w), F32)
        xm2 = jnp.where(seg_pos >= 2, jnp.where(sub >= 2, sh2[1:-1], sh2[0:-2]), zero)
        xm1 = jnp.where(seg_pos >= 1, jnp.where(sub >= 1, sh1[1:-1], sh1[0:-2]), zero)
        xp1 = jnp.where(seg_pos < seg_last, jnp.where(sub < SUBLANES - 1, sh7[1:-1], sh7[2:]), zero)
        y = cb_ref[...] + xm2 * cw_ref[0:1] + xm1 * cw_ref[1:2] + x3[1:-1] * cw_ref[2:3] + xp1 * cw_ref[3:4]
        y_s[pl.ds(r0, chunk), :] = y.reshape(chunk, w)
        return carry

    lax.fori_loop(0, n_chunks, conv_chunk, 0)

    nctx_t = nctx // SUBLANES
    nt_t = nt // SUBLANES

    for d in range(2):
        nlam = -lam_ref[d:d + 1]
        softplus = jnp.maximum(nlam, 0.0) + jnp.log1p(jnp.exp(-jnp.abs(nlam)))
        c8 = -LRU_C * softplus

        def gate_chunk(c, carry, d=d, c8=c8):
            r0 = pl.multiple_of(c * chunk, chunk)
            y = y_s[pl.ds(r0, chunk), :]
            yb = y.astype(BF16)
            r = jax.nn.sigmoid(jnp.dot(yb, wa_ref[d], preferred_element_type=F32) + ba_ref[d:d + 1])
            i = jax.nn.sigmoid(jnp.dot(yb, wi_ref[d], preferred_element_type=F32) + bi_ref[d:d + 1])
            log_a = c8 * r
            a = jnp.exp(log_a)
            th = jnp.tanh(log_a)
            u = jnp.sqrt(-2.0 * th / (1.0 - th)) * (i * y)
            a3 = a.reshape(tiles, SUBLANES, w)
            u3 = u.reshape(tiles, SUBLANES, w)
            for sft in (1, 2, 4):
                if d == 0:
                    ok = sub >= sft
                    ash = pltpu.roll(a3, sft, 1)
                    ush = pltpu.roll(u3, sft, 1)
                else:
                    ok = sub < SUBLANES - sft
                    ash = pltpu.roll(a3, SUBLANES - sft, 1)
                    ush = pltpu.roll(u3, SUBLANES - sft, 1)
                u3 = jnp.where(ok, a3 * ush + u3, u3)
                a3 = jnp.where(ok, a3 * ash, a3)
            a_s[pl.ds(r0, chunk), :] = a3.reshape(chunk, w)
            s_s[pl.ds(r0, chunk), :] = u3.reshape(chunk, w)
            return carry

        lax.fori_loop(0, n_chunks, gate_chunk, 0, unroll=4)

        def carry_tile(j, hprev, d=d):
            if d == 0:
                t = j
            else:
                t = jnp.where(j < nctx_t, nctx_t - 1 - j, nt_t - 1 - (j - nctx_t))
            r0 = pl.multiple_of(t * SUBLANES, SUBLANES)
            h = a_s[pl.ds(r0, SUBLANES), :] * hprev + s_s[pl.ds(r0, SUBLANES), :]
            if d == 0:
                h_s[pl.ds(r0, SUBLANES), :] = h
                return h[SUBLANES - 1:SUBLANES]
            h_s[pl.ds(r0, SUBLANES), :] = h_s[pl.ds(r0, SUBLANES), :] + h
            return h[0:1]

        lax.fori_loop(0, nt_t, carry_tile, jnp.zeros((1, w), F32), unroll=4)

    def out_chunk(c, carry):
        r0 = pl.multiple_of(c * chunk, chunk)
        o_ref[0, pl.ds(r0, chunk), :] = (h_s[pl.ds(r0, chunk), :] * _gelu_tanh(g_ref[0, pl.ds(r0, chunk), :])).astype(BF16)
        return carry

    lax.fori_loop(0, n_chunks, out_chunk, 0)


def _lru(lx, lg, conv_w, conv_b, wa, wi, ba, bi, lam, nctx):
    b, nt, w = lx.shape
    chunk = TOKEN_TILE
    seq = pl.BlockSpec((1, nt, w), lambda i: (i, 0, 0))
    return pl.pallas_call(
        functools.partial(_lru_kernel, nt=nt, nctx=nctx, chunk=chunk),
        grid=(b,),
        in_specs=[seq, seq, _const_spec(conv_w.shape), _const_spec(conv_b.shape), _const_spec(wa.shape),
                  _const_spec(wi.shape), _const_spec(ba.shape), _const_spec(bi.shape), _const_spec(lam.shape)],
        out_specs=seq,
        out_shape=jax.ShapeDtypeStruct((b, nt, w), BF16),
        scratch_shapes=[pltpu.VMEM((nt, w), F32)] * 4,
        compiler_params=_params(("parallel",)),
        name="rglru",
    )(lx, lg, conv_w, conv_b, wa, wi, ba, bi, lam)


DA_KEY_CHUNK = 256
MLA_KEY_CHUNK = 256


def _key_chunks(nk, nctx, size):
    chunks = [(0, nctx)]
    chunks += [(s, min(size, nk - s)) for s in range(nctx, nk, size)]
    return chunks


def _attend_t(chains, chunks):
    def scores(n, ci):
        q_t, key, _ = chains[n]
        return jnp.dot(key(*chunks[ci]), q_t, preferred_element_type=F32).astype(BF16)

    s = [scores(n, 0) for n in range(len(chains))]
    state = [None] * len(chains)
    for ci in range(len(chunks)):
        for n, (_, _, value_t) in enumerate(chains):
            cm = jnp.max(s[n], axis=0, keepdims=True)
            if ci == 0:
                m_new = cm
            else:
                m_old, acc = state[n]
                m_new = jnp.maximum(m_old, cm)
            p = jnp.exp2(s[n] - m_new)
            if ci + 1 < len(chunks):
                s[n] = scores(n, ci + 1)
            pv = jnp.dot(value_t(*chunks[ci]), p, preferred_element_type=F32)
            if ci > 0:
                pv = acc * jnp.exp2(m_old.astype(F32) - m_new.astype(F32)) + pv
            state[n] = (m_new, pv)
    return [acc[0:DA_V] / acc[DA_V:DA_V + 1] for _, acc in state]


def _da_kernel(q_ref, k_ref, vt_ref, dl_ref, g_ref, li_ref, o_ref, *, nt, nctx, tq):
    half = pl.program_id(1)
    row = lax.broadcasted_iota(jnp.int32, (LANES, tq), 0)
    lane = lax.broadcasted_iota(jnp.int32, (tq, LANES), 1)
    dl = dl_ref[...]
    lam_init = li_ref[...]
    lam = (jnp.exp(jnp.sum(dl[0:1] * dl[1:2], axis=-1, keepdims=True))
           - jnp.exp(jnp.sum(dl[2:3] * dl[3:4], axis=-1, keepdims=True)) + lam_init)
    zero = jnp.zeros((LANES, tq), BF16)

    def attend(q0, nk):
        chains = []
        for j in range(PAIRS):
            q_t = q_ref[0, LANES * j:LANES * (j + 1), pl.ds(q0, tq)]
            key = lambda start, size, j=j: k_ref[0, start:start + size, LANES * j:LANES * (j + 1)]
            value_t = lambda start, size, j=j: vt_ref[0, 2 * j + half, :, start:start + size]
            for mi in range(2):
                mine = (row & (LANES // 2 - ROT_HALF)) == (2 * half + mi) * ROT_HALF
                chains.append((jnp.where(mine, q_t, zero), key, value_t))
        o = _attend_t(chains, _key_chunks(nk, nctx, DA_KEY_CHUNK))
        for j in range(PAIRS):
            d = o[2 * j] - lam * o[2 * j + 1]
            d = d * lax.rsqrt(jnp.mean(d * d, axis=0, keepdims=True) + RMS_EPS)
            both = jnp.concatenate([d, d], axis=0).T
            new = (both * g_ref[...] * (1.0 - lam_init)).astype(BF16)
            slab = (0, pl.ds(q0, tq), slice(LANES * j, LANES * (j + 1)))

            @pl.when(half == 0)
            def _():
                o_ref[slab] = new

            @pl.when(half == 1)
            def _():
                o_ref[slab] = jnp.where(lane >= DA_V, new, o_ref[slab])

    _for_query_blocks(attend, nt, nctx, tq)


def _for_query_blocks(attend, nt, nctx, tq):
    for t in range(nctx // tq):
        attend(t * tq, nctx)

    def latent_block(t, carry):
        attend(pl.multiple_of(nctx + t * tq, tq), nt)
        return carry

    lax.fori_loop(0, (nt - nctx) // tq, latent_block, 0)


def _da_attn(dq_t, dk, dv_t, dlam, gpair, lam_init, nctx):
    b, nt, _ = dk.shape
    tq = TOKEN_TILE
    return pl.pallas_call(
        functools.partial(_da_kernel, nt=nt, nctx=nctx, tq=tq),
        grid=(b, 2),
        in_specs=[
            pl.BlockSpec((1, DA_WIDTH, nt), lambda i, h: (i, 0, 0)),
            pl.BlockSpec((1, nt, DA_WIDTH), lambda i, h: (i, 0, 0)),
            pl.BlockSpec((1, DA_HEADS, VT_ROWS, nt), lambda i, h: (i, 0, 0, 0)),
            _const_spec(dlam.shape), _const_spec(gpair.shape), _const_spec(lam_init.shape),
        ],
        out_specs=pl.BlockSpec((1, nt, DA_WIDTH), lambda i, h: (i, 0, 0)),
        out_shape=jax.ShapeDtypeStruct((b, nt, DA_WIDTH), BF16),
        compiler_params=_params(("parallel", "arbitrary")),
        name="diff_attn",
    )(dq_t, dk, dv_t, dlam, gpair, lam_init)


def _mla_kernel(q_ref, k_ref, vt_ref, o_ref, *, nt, nctx, tq):
    def attend(q0, nk):
        chains = []
        for hh in range(MLA_HEADS):
            key = lambda start, size, hh=hh: k_ref[0, hh, start:start + size, :]
            value_t = lambda start, size, hh=hh: vt_ref[0, hh, :, start:start + size]
            chains.append((q_ref[0, hh, :, pl.ds(q0, tq)], key, value_t))
        o = _attend_t(chains, _key_chunks(nk, nctx, MLA_KEY_CHUNK))
        for j in range(MLA_HEADS // 2):
            o_ref[0, pl.ds(q0, tq), LANES * j:LANES * (j + 1)] = jnp.concatenate(o[2 * j:2 * j + 2], axis=0).T.astype(BF16)

    _for_query_blocks(attend, nt, nctx, tq)


def _mla_attn(mq_t, mk, mv_t, nctx):
    b, nh, nt, _ = mk.shape
    tq = TOKEN_TILE
    return pl.pallas_call(
        functools.partial(_mla_kernel, nt=nt, nctx=nctx, tq=tq),
        grid=(b,),
        in_specs=[
            pl.BlockSpec((1, nh, LANES, nt), lambda i: (i, 0, 0, 0)),
            pl.BlockSpec((1, nh, nt, LANES), lambda i: (i, 0, 0, 0)),
            pl.BlockSpec((1, nh, VT_ROWS, nt), lambda i: (i, 0, 0, 0)),
        ],
        out_specs=pl.BlockSpec((1, nt, nh * MLA_V), lambda i: (i, 0, 0)),
        out_shape=jax.ShapeDtypeStruct((b, nt, nh * MLA_V), BF16),
        compiler_params=_params(("parallel",)),
        name="mla_attn",
    )(mq_t, mk, mv_t)


def _layer_norm(z, g, b):
    mu = jnp.mean(z, axis=-1, keepdims=True)
    zc = z - mu
    var = jnp.mean(zc * zc, axis=-1, keepdims=True)
    return (zc * lax.rsqrt(var + LN_EPS)) * g + b


def _router_gates(logits, rb):
    scores = jax.nn.sigmoid(logits)
    sel = scores + rb
    lane = lax.broadcasted_iota(jnp.int32, logits.shape, 1)
    r = lane & (EXPERTS_PER_GROUP - 1)
    grp = (lane >> 2) & (N_GROUPS - 1)

    def in_group(x, k):
        return jnp.where(r >= k, pltpu.roll(x, k, 1), pltpu.roll(x, LANES - EXPERTS_PER_GROUP + k, 1))

    others = [in_group(sel, k) for k in (1, 2, 3)]
    pair_max = sel + jnp.maximum(jnp.maximum(others[0], others[1]), others[2])
    grp_score = jnp.maximum(jnp.maximum(pair_max, in_group(pair_max, 1)),
                            jnp.maximum(in_group(pair_max, 2), in_group(pair_max, 3)))
    in_best = None
    for k in (1, 2, 3):
        other = pltpu.roll(grp_score, EXPERTS_PER_GROUP * k, 1)
        wins = (grp_score > other) | ((grp_score == other) & (grp < k))
        in_best = wins if in_best is None else (in_best & wins)
    beaten = jnp.zeros(logits.shape, F32)
    for k, o in zip((1, 2, 3), others):
        beats = (o > sel) | ((o == sel) & (r >= k))
        beaten = beaten + jnp.where(beats, 1.0, 0.0)
    chosen = in_best & (beaten < 2.0)
    sc = jnp.where(chosen, scores, 0.0)
    tot = sc + in_group(sc, 1) + in_group(sc, 2) + in_group(sc, 3)
    return jnp.where(chosen, sc / tot, 0.0), in_best


ROW_BLOCK = 160
BF16_ROWS = 16


def _grouped_experts(v, gates, in_best, w1_ref, w3_ref, w2_ref, xs_ref, gs_ref, ys_ref):
    rows, d = v.shape
    per = EXPERTS_PER_GROUP * D_EXPERT
    lane = lax.broadcasted_iota(jnp.int32, (rows, LANES), 1)
    gsel = jnp.where(in_best & ((lane & (EXPERTS_PER_GROUP - 1)) == 0) & (lane < N_EXPERTS), 1.0, 0.0)
    gsel_b = gsel.astype(BF16)
    ri = lax.broadcasted_iota(jnp.int32, (rows, rows), 0)
    ci = lax.broadcasted_iota(jnp.int32, (rows, rows), 1)
    onehot = lambda cond: jnp.where(cond, 1.0, 0.0).astype(BF16)

    tot = jnp.sum(gsel, axis=0, keepdims=True)
    lane1 = lax.broadcasted_iota(jnp.int32, (1, LANES), 1)
    cnt = [jnp.sum(jnp.where(lane1 == EXPERTS_PER_GROUP * g, tot, 0.0)).astype(jnp.int32) for g in range(N_GROUPS)]
    off = [jnp.int32(0)]
    for g in range(1, N_GROUPS):
        off.append(off[-1] + cnt[g - 1])

    off_lane = sum(jnp.where(lane1 == EXPERTS_PER_GROUP * g, off[g].astype(F32), 0.0) for g in range(N_GROUPS))
    before = jnp.dot(onehot(ci < ri), gsel_b, preferred_element_type=F32)
    pos_col = jnp.sum(gsel * (before + off_lane), axis=1, keepdims=True)
    sub8 = lax.broadcasted_iota(jnp.int32, (SUBLANES, LANES), 0)
    lane8 = lax.broadcasted_iota(jnp.int32, (SUBLANES, LANES), 1)
    pick = jnp.where((lane8 == EXPERTS_PER_GROUP * sub8) & (sub8 < N_GROUPS), 1.0, 0.0).astype(BF16)
    gsel_t = lax.dot_general(pick, gsel_b, (((1,), (1,)), ((), ())), preferred_element_type=F32)
    before_t = jnp.dot(gsel_t.astype(BF16), onehot(ri < ci), preferred_element_type=F32)
    sub_col = lax.broadcasted_iota(jnp.int32, (SUBLANES, 1), 0)
    off_sub = sum(jnp.where(sub_col == g, off[g].astype(F32), 0.0) for g in range(N_GROUPS))
    pos_row = jnp.sum(gsel_t * (before_t + off_sub), axis=0, keepdims=True)
    perm = onehot(pos_row == ri.astype(F32))
    perm_t = onehot(pos_col == ci.astype(F32))

    xs_ref[...] = jnp.dot(perm, v, preferred_element_type=F32).astype(BF16)
    g_hi = gates.astype(BF16)
    g_lo = (gates - g_hi.astype(F32)).astype(BF16)
    gs_ref[...] = jnp.dot(perm, g_hi, preferred_element_type=F32) + jnp.dot(perm, g_lo, preferred_element_type=F32)
    ys_ref[...] = jnp.zeros(ys_ref.shape, F32)

    row_in_block = lax.broadcasted_iota(jnp.int32, (ROW_BLOCK, LANES), 0)
    for g in range(N_GROUPS):
        first = (off[g] // BF16_ROWS) * BF16_ROWS
        end = off[g] + cnt[g]
        for k in range(-(-rows // ROW_BLOCK)):
            lo = first + k * ROW_BLOCK

            @pl.when((lo < end) & (cnt[g] > 0))
            def _(lo=lo, g=g):
                st = pl.multiple_of(jnp.minimum(lo, rows - ROW_BLOCK), BF16_ROWS)
                xb = xs_ref[pl.ds(st, ROW_BLOCK), :]
                gb = jnp.where(row_in_block + st >= lo, gs_ref[pl.ds(st, ROW_BLOCK), :], 0.0)
                h1 = jnp.dot(xb, w1_ref[:, g * per:(g + 1) * per], preferred_element_type=F32)
                h3 = jnp.dot(xb, w3_ref[:, g * per:(g + 1) * per], preferred_element_type=F32)
                hh = (h1 * jax.nn.sigmoid(h1)) * h3
                parts = []
                for j in range(EXPERTS_PER_GROUP):
                    e = g * EXPERTS_PER_GROUP + j
                    parts.append((hh[:, j * D_EXPERT:(j + 1) * D_EXPERT] * gb[:, e:e + 1]).astype(BF16))
                ys_ref[pl.ds(st, ROW_BLOCK), :] += jnp.dot(jnp.concatenate(parts, axis=-1),
                                                           w2_ref[g * per:(g + 1) * per, :],
                                                           preferred_element_type=F32)

    return jnp.dot(perm_t, ys_ref[...].astype(BF16), preferred_element_type=F32)


def _post_kernel(xc_ref, xl_ref, mod_ref, lru_ref, da_ref, mla_ref, wo_ref, g1_ref, b1_ref, rw_ref, rb_ref, w1_ref,
                 w3_ref, w2_ref, g_ref, b_ref, o_ref, xs_ref, gs_ref, ys_ref, *, alpha, n_ctx_tiles):
    nb, tm, d = xl_ref.shape
    rows = nb * tm
    x = jnp.where(pl.program_id(1) < n_ctx_tiles, xc_ref[...], xl_ref[...]).reshape(rows, d)
    mod = mod_ref[:, 0]
    per_row = lambda k: jnp.broadcast_to(mod[:, k:k + 1], (nb, tm, d)).reshape(rows, d)
    a = jnp.concatenate([lru_ref[...], da_ref[...], mla_ref[...]], axis=-1).reshape(rows, -1)
    o = jnp.dot(a, wo_ref[...], preferred_element_type=F32)
    x1 = _layer_norm(alpha * x + per_row(2) * o, g1_ref[...], b1_ref[...])
    v = (x1 * (1.0 + per_row(4)) + per_row(3)).astype(BF16)
    gates, in_best = _router_gates(jnp.dot(v, rw_ref[...], preferred_element_type=F32), rb_ref[...])
    f = _grouped_experts(v, gates, in_best, w1_ref, w3_ref, w2_ref, xs_ref, gs_ref, ys_ref)
    o_ref[...] = _layer_norm(alpha * x1 + per_row(5) * f, g_ref[...], b_ref[...]).reshape(nb, tm, d)


POST_BATCH = 2


def _post(xc, xl, lat_shift, modt, lru_o, da_o, mla_o, wo, g1, b1, rw, rb, w1c, w3c, w2c, g2, b2, nctx, alpha,
          latent_only):
    b, nt, _ = lru_o.shape
    d = xl.shape[-1]
    tm = TOKEN_TILE
    nc = nctx // tm
    skip = nc if latent_only else 0
    nb = POST_BATCH if b % POST_BATCH == 0 else 1
    tok = lambda w: pl.BlockSpec((nb, tm, w), lambda i, t: (i, t + skip, 0))
    consts = (wo, g1, b1, rw, rb, w1c, w3c, w2c, g2, b2)
    return pl.pallas_call(
        functools.partial(_post_kernel, alpha=alpha, n_ctx_tiles=nc - skip),
        grid=(b // nb, nt // tm - skip),
        in_specs=_stream_specs(nb, tm, d, nc, lat_shift, skip) + [
            pl.BlockSpec((nb, 1, 8, d), lambda i, t: (i, jnp.where(t + skip >= nc, 1, 0), 0, 0)),
            tok(LRU_WIDTH), tok(da_o.shape[-1]), tok(mla_o.shape[-1]),
        ] + [_const_spec(c.shape) for c in consts],
        out_specs=pl.BlockSpec((nb, tm, d), lambda i, t: (i, t, 0)),
        out_shape=jax.ShapeDtypeStruct((b, nt - skip * tm, d), F32),
        scratch_shapes=[pltpu.VMEM((nb * tm, d), BF16), pltpu.VMEM((nb * tm, LANES), F32),
                        pltpu.VMEM((nb * tm, d), F32)],
        compiler_params=_params(("parallel", "parallel")),
        name="post",
    )(xc, xl, modt, lru_o, da_o, mla_o, *consts)


def _rotary_tables(n, nctx):
    rows = n // GRID_W
    row = jnp.repeat(jnp.arange(rows), GRID_W).astype(F32)
    col = jnp.tile(jnp.arange(GRID_W), rows).astype(F32)
    n_freq = DA_QK // 4
    inv = ROPE_THETA ** (-jnp.arange(n_freq, dtype=F32) / n_freq)
    ang = jnp.concatenate([row[:, None] * inv, col[:, None] * inv], axis=-1)
    ang = jnp.concatenate([jnp.zeros((nctx, DA_QK // 2), F32), ang], axis=0)
    c, s = jnp.cos(ang), jnp.sin(ang)
    cos_d = jnp.tile(c, (1, LANES // ROT_HALF))
    sin_d = jnp.concatenate([jnp.tile(-s, (1, 4)), jnp.tile(s, (1, 4))], axis=-1)
    one = jnp.ones_like(c)
    zero = jnp.zeros_like(c)
    cos_m = jnp.concatenate([one, one, c, one, one, one, c, one], axis=-1)
    sin_m = jnp.concatenate([zero, zero, -s, zero, zero, zero, s, zero], axis=-1)
    return cos_d, sin_d, cos_m, sin_m


def _rot_lanes(w, groups):
    lead = w.shape[:-1]
    w = w.reshape(lead + (groups, 2, ROT_HALF))
    return jnp.swapaxes(w, -3, -2).reshape(lead + (groups * 2 * ROT_HALF,))


def _mla_lanes(nope, rot):
    z = jnp.zeros(nope.shape[:-1] + (ROT_HALF,), nope.dtype)
    return jnp.concatenate([nope[..., :MLA_NOPE // 2], rot[..., :ROT_HALF], z,
                            nope[..., MLA_NOPE // 2:], rot[..., ROT_HALF:], z], axis=-1)


def _pack_in_weight(w_in):
    d = w_in.shape[0]
    n_lru = 2 * LRU_WIDTH
    n_lru_da = n_lru + 3 * DA_WIDTH
    n_rank = Q_RANK + KV_RANK
    wkr = w_in[:, n_lru_da + n_rank:n_lru_da + n_rank + MLA_ROPE]
    krp = _mla_lanes(jnp.zeros((d, MLA_NOPE), F32), wkr)
    wqk = w_in[:, n_lru:n_lru + 2 * DA_WIDTH].reshape(d, 2 * PAIRS, LANES)
    wqk = _rot_lanes(wqk, LANES // DA_QK).reshape(d, 2 * DA_WIDTH)
    return jnp.concatenate([w_in[:, n_lru_da:n_lru_da + n_rank], krp, wqk, w_in[:, n_lru + 2 * DA_WIDTH:n_lru_da],
                            w_in[:, :n_lru]], axis=-1).astype(BF16)


def _pack_uq(w_uq):
    r = w_uq.shape[0]
    w = w_uq.reshape(r, MLA_HEADS, MLA_NOPE + MLA_ROPE)
    return _mla_lanes(w[..., :MLA_NOPE], w[..., MLA_NOPE:]).reshape(r, MLA_HEADS * LANES).astype(BF16)


def _pack_ukv(w_ukv):
    r = w_ukv.shape[0]
    w = w_ukv.reshape(r, MLA_HEADS, MLA_NOPE + MLA_V)
    wk = _mla_lanes(w[..., :MLA_NOPE], jnp.zeros((r, MLA_HEADS, MLA_ROPE), F32)).reshape(r, MLA_HEADS * LANES)
    wv = w[..., MLA_NOPE:].reshape(r, MLA_HEADS * MLA_V)
    return wk.astype(BF16), wv.astype(BF16)


def _block_diag(w):
    nd, nb, bs, _ = w.shape
    eye = jnp.eye(nb, dtype=w.dtype)
    return jnp.einsum('dhij,hg->dhigj', w, eye).reshape(nd, nb * bs, nb * bs)


def kernel(x, c, ctx, c_ctx, w_mod, b_mod, w_in, w_out, conv_w, conv_b, lru_wa, lru_ba, lru_wi, lru_bi, lru_lambda, diff_lambda, diff_norm, mla_q_norm, mla_kv_norm, mla_w_uq, mla_w_ukv, ln1_g, ln1_b, ln2_g, ln2_b, router_w, router_b, exp_w1, exp_w3, exp_w2):
    bsz, n, d = x.shape
    nctx = ctx.shape[1]
    depth = w_mod.shape[0]
    alpha = (2 * depth) ** 0.25
    assert nctx % TOKEN_TILE == 0 and n % TOKEN_TILE == 0 and n % GRID_W == 0

    rows = -(-(bsz + 1) // SUBLANES) * SUBLANES
    cc = jnp.concatenate([c, c_ctx[None, :], jnp.zeros((rows - bsz - 1, d), F32)], axis=0)
    mod = _modulation(cc, w_mod, b_mod).reshape(depth, rows, 6, d)
    mod = jnp.pad(mod, ((0, 0), (0, 0), (0, 2), (0, 0)))
    mod_ctx = jnp.broadcast_to(mod[:, bsz][:, None], (depth, bsz, 8, d))
    modt = jnp.stack([mod_ctx, mod[:, :bsz]], axis=2)

    rot_tables = _rotary_tables(n, nctx)
    rw = jnp.tile(router_w, (1, LANES // N_EXPERTS)).astype(BF16)
    rb = jnp.tile(router_b, LANES // N_EXPERTS)[None, :].astype(F32)
    gpair = jnp.tile(diff_norm, (1, LANES // DA_V))

    xc, xl, lat_shift = ctx, x, nctx // TOKEN_TILE
    for l in range(depth):
        lam_init = jnp.full((1, 1), 0.8 - 0.6 * math.exp(-0.3 * l), F32)
        wuk, wuv = _pack_ukv(mla_w_ukv[l])
        lx, lg, dq_t, dk, dv_t, mq_t, mk, mv_t = _in_proj(
            xc, xl, lat_shift, nctx + n, modt[l], _pack_in_weight(w_in[l]), _pack_uq(mla_w_uq[l]), wuk, wuv,
            mla_q_norm[l][None, :], mla_kv_norm[l][None, :], rot_tables, nctx)
        lru_o = _lru(lx, lg, conv_w[l], conv_b[l][None, :], _block_diag(lru_wa[l]).astype(BF16),
                     _block_diag(lru_wi[l]).astype(BF16), lru_ba[l], lru_bi[l], lru_lambda[l], nctx)
        da_o = _da_attn(dq_t, dk, dv_t, diff_lambda[l], gpair[l][None, :], lam_init, nctx)
        mla_o = _mla_attn(mq_t, mk, mv_t, nctx)
        w1c = exp_w1[l].transpose(1, 0, 2).reshape(d, N_EXPERTS * D_EXPERT).astype(BF16)
        w3c = exp_w3[l].transpose(1, 0, 2).reshape(d, N_EXPERTS * D_EXPERT).astype(BF16)
        w2c = exp_w2[l].reshape(N_EXPERTS * D_EXPERT, d).astype(BF16)
        xa = _post(xc, xl, lat_shift, modt[l], lru_o, da_o, mla_o, w_out[l].astype(BF16), ln1_g[l][None, :],
                   ln1_b[l][None, :], rw, rb, w1c, w3c, w2c, ln2_g[l][None, :], ln2_b[l][None, :], nctx, alpha,
                   latent_only=(l == depth - 1))
        xc, xl, lat_shift = xa, xa, 0
    return xa
```

```python
import functools
import math

import jax
import jax.numpy as jnp
from jax import lax
from jax.experimental import pallas as pl
from jax.experimental.pallas import tpu as pltpu

F32 = jnp.float32
BF16 = jnp.bfloat16

GRID_W = 64
LRU_WIDTH = 256
LRU_BLOCKS = 4
CONV_W = 4
LRU_C = 8.0
DA_HEADS = 6
DA_QK = 32
DA_V = 2 * DA_QK
MLA_HEADS = 6
MLA_NOPE = 64
MLA_ROPE = 32
MLA_V = 64
Q_RANK = 256
KV_RANK = 128
MLA_SCALE = (MLA_NOPE + MLA_ROPE) ** -0.5
N_EXPERTS = 16
N_GROUPS = 4
EXPERTS_PER_GROUP = N_EXPERTS // N_GROUPS
D_EXPERT = 256
ROPE_THETA = 10000.0
LN_EPS = 1e-5
RMS_EPS = 1e-6

LANES = 128
SUBLANES = 8
TOKEN_TILE = 256
VMEM_LIMIT = 56 * 1024 * 1024

LOG2E = math.log2(math.e)
DA_QSCALE = DA_QK ** -0.5 * LOG2E
MLA_QSCALE = MLA_SCALE * LOG2E

DA_WIDTH = DA_HEADS * DA_V
C_CQ = 0
C_CKV = C_CQ + Q_RANK
C_KR = C_CKV + KV_RANK
C_DAQ = C_KR + LANES
C_DAK = C_DAQ + DA_WIDTH
C_DAV = C_DAK + DA_WIDTH
C_LRU = C_DAV + DA_WIDTH
C_END = C_LRU + 2 * LRU_WIDTH
PAIRS = DA_HEADS // 2
ROT_HALF = DA_QK // 2
VT_ROWS = DA_V + 16


def _params(sem):
    return pltpu.CompilerParams(dimension_semantics=sem, vmem_limit_bytes=VMEM_LIMIT)


def _const_spec(shape):
    nd = len(shape)
    return pl.BlockSpec(shape, lambda *_: (0,) * nd, pipeline_mode=pl.Buffered(1))


def _mod_kernel(c_ref, w_ref, b_ref, o_ref):
    c = c_ref[...]
    s = c * jax.nn.sigmoid(c)
    o_ref[0] = jnp.dot(s.astype(BF16), w_ref[0].astype(BF16), preferred_element_type=F32) + b_ref[0]


def _modulation(cc, w_mod, b_mod):
    depth, d, d6 = w_mod.shape
    r = cc.shape[0]
    tn = min(d6, 1536)
    return pl.pallas_call(
        _mod_kernel,
        grid=(depth, d6 // tn),
        in_specs=[
            pl.BlockSpec((r, d), lambda l, j: (0, 0)),
            pl.BlockSpec((1, d, tn), lambda l, j: (l, 0, j)),
            pl.BlockSpec((1, 1, tn), lambda l, j: (l, 0, j)),
        ],
        out_specs=pl.BlockSpec((1, r, tn), lambda l, j: (l, 0, j)),
        out_shape=jax.ShapeDtypeStruct((depth, r, d6), F32),
        compiler_params=_params(("parallel", "parallel")),
        name="modulation",
    )(cc, w_mod, b_mod.reshape(depth, 1, d6))


def _rotate(t, cosf, sinf):
    return t * cosf + pltpu.roll(t, LANES // 2, 1) * sinf


def _store_values_t(vt_ref, bi, v):
    rows = v.shape[0]
    ones = jnp.ones((VT_ROWS - DA_V, rows), BF16)
    for j in range(PAIRS):
        t = v[:, LANES * j:LANES * (j + 1)].T.astype(BF16)
        for k in range(2):
            vt_ref[bi, 2 * j + k, 0:DA_V, :] = t[DA_V * k:DA_V * (k + 1)]
            vt_ref[bi, 2 * j + k, DA_V:VT_ROWS, :] = ones


def _in_kernel(xc_ref, xl_ref, mod_ref, w1_ref, wuq_ref, wuk_ref, wuv_ref, qn_ref, kvn_ref, cd_ref, sd_ref, cm_ref,
               sm_ref, lx_ref, lg_ref, dq_ref, dk_ref, dv_ref, mq_ref, mk_ref, mv_ref, *, n_ctx_tiles):
    nb, tm, d = xl_ref.shape
    x = jnp.where(pl.program_id(1) < n_ctx_tiles, xc_ref[...], xl_ref[...])
    mod = mod_ref[:, 0]
    u = (x * (1.0 + mod[:, 1:2]) + mod[:, 0:1]).astype(BF16).reshape(nb * tm, d)
    proj = lambda lo, hi: jnp.dot(u, w1_ref[:, lo:hi], preferred_element_type=F32)
    rot = functools.partial(_rotate, cosf=cd_ref[...], sinf=sd_ref[...])
    rot_mla = functools.partial(_rotate, cosf=cm_ref[...], sinf=sm_ref[...])

    y_mla = proj(C_CQ, C_DAQ)
    y_da = proj(C_DAQ, C_DAV)

    cq = y_mla[:, C_CQ:C_CKV]
    ckv = y_mla[:, C_CKV:C_KR]
    krp = y_mla[:, C_KR:C_DAQ]
    qn = (cq * lax.rsqrt(jnp.mean(cq * cq, axis=-1, keepdims=True) + RMS_EPS)) * qn_ref[...]
    kvn = ((ckv * lax.rsqrt(jnp.mean(ckv * ckv, axis=-1, keepdims=True) + RMS_EPS)) * kvn_ref[...]).astype(BF16)
    q = jnp.dot(qn.astype(BF16), wuq_ref[...], preferred_element_type=F32)
    kn = jnp.dot(kvn, wuk_ref[...], preferred_element_type=F32)

    for bi in range(nb):
        rs = slice(bi * tm, (bi + 1) * tm)
        for j in range(PAIRS):
            t = y_da[rs, LANES * j:LANES * (j + 1)]
            dq_ref[bi, LANES * j:LANES * (j + 1), :] = (rot(t) * DA_QSCALE).T.astype(BF16)
            t = y_da[rs, DA_WIDTH + LANES * j:DA_WIDTH + LANES * (j + 1)]
            dk_ref[bi, :, LANES * j:LANES * (j + 1)] = rot(t).astype(BF16)

    y_rest = proj(C_DAV, C_END)
    mv = jnp.dot(kvn, wuv_ref[...], preferred_element_type=F32)

    for bi in range(nb):
        rs = slice(bi * tm, (bi + 1) * tm)
        kr = rot_mla(krp[rs])
        for h in range(MLA_HEADS):
            t = q[rs, LANES * h:LANES * (h + 1)]
            mq_ref[bi, h] = (rot_mla(t) * MLA_QSCALE).T.astype(BF16)
            mk_ref[bi, h] = (kn[rs, LANES * h:LANES * (h + 1)] + kr).astype(BF16)
        _store_values_t(dv_ref, bi, y_rest[rs, 0:DA_WIDTH])
        _store_values_t(mv_ref, bi, mv[rs])
    lx_ref[...] = y_rest[:, DA_WIDTH:DA_WIDTH + LRU_WIDTH].reshape(nb, tm, LRU_WIDTH)
    lg_ref[...] = y_rest[:, DA_WIDTH + LRU_WIDTH:DA_WIDTH + 2 * LRU_WIDTH].reshape(nb, tm, LRU_WIDTH)


IN_BATCH = 2


def _stream_specs(rows, tm, d, nc, lat_shift, skip=0):
    return [pl.BlockSpec((rows, tm, d), lambda i, t: (i, jnp.minimum(t + skip, nc - 1), 0)),
            pl.BlockSpec((rows, tm, d), lambda i, t: (i, jnp.maximum(t + skip - lat_shift, 0), 0))]


def _in_proj(xc, xl, lat_shift, nt, modt, w1, wuq, wuk, wuv, qnorm, kvnorm, rot_tables, nctx):
    b, _, d = xl.shape
    tm = TOKEN_TILE
    nc = nctx // tm
    nb = IN_BATCH if b % IN_BATCH == 0 else 1
    tok = lambda w: pl.BlockSpec((nb, tm, w), lambda i, t: (i, t, 0))
    head_t = lambda r: pl.BlockSpec((nb, DA_HEADS, r, tm), lambda i, t: (i, 0, 0, t))
    return pl.pallas_call(
        functools.partial(_in_kernel, n_ctx_tiles=nc),
        grid=(b // nb, nt // tm),
        in_specs=_stream_specs(nb, tm, d, nc, lat_shift) + [
            pl.BlockSpec((nb, 1, 8, d), lambda i, t: (i, jnp.where(t >= nc, 1, 0), 0, 0)),
            _const_spec(w1.shape), _const_spec(wuq.shape), _const_spec(wuk.shape), _const_spec(wuv.shape),
            _const_spec(qnorm.shape), _const_spec(kvnorm.shape),
        ] + [pl.BlockSpec((tm, LANES), lambda i, t: (t, 0))] * len(rot_tables),
        out_specs=[
            tok(LRU_WIDTH), tok(LRU_WIDTH),
            pl.BlockSpec((nb, DA_WIDTH, tm), lambda i, t: (i, 0, t)), tok(DA_WIDTH), head_t(VT_ROWS),
            head_t(LANES), pl.BlockSpec((nb, MLA_HEADS, tm, LANES), lambda i, t: (i, 0, t, 0)), head_t(VT_ROWS),
        ],
        out_shape=[
            jax.ShapeDtypeStruct((b, nt, LRU_WIDTH), F32),
            jax.ShapeDtypeStruct((b, nt, LRU_WIDTH), F32),
            jax.ShapeDtypeStruct((b, DA_WIDTH, nt), BF16),
            jax.ShapeDtypeStruct((b, nt, DA_WIDTH), BF16),
            jax.ShapeDtypeStruct((b, DA_HEADS, VT_ROWS, nt), BF16),
            jax.ShapeDtypeStruct((b, MLA_HEADS, LANES, nt), BF16),
            jax.ShapeDtypeStruct((b, MLA_HEADS, nt, LANES), BF16),
            jax.ShapeDtypeStruct((b, MLA_HEADS, VT_ROWS, nt), BF16),
        ],
        compiler_params=_params(("parallel", "parallel")),
        name="in_proj",
    )(xc, xl, modt, w1, wuq, wuk, wuv, qnorm, kvnorm, *rot_tables)


def _gelu_tanh(x):
    return 0.5 * x * (1.0 + jnp.tanh(math.sqrt(2.0 / math.pi) * (x + 0.044715 * (x * x * x))))


def _lru_kernel(x_ref, g_ref, cw_ref, cb_ref, wa_ref, wi_ref, ba_ref, bi_ref, lam_ref, o_ref,
                y_s, a_s, s_s, h_s, *, nt, nctx, chunk):
    w = LRU_WIDTH
    tiles = chunk // SUBLANES
    n_chunks = nt // chunk
    sub = lax.broadcasted_iota(jnp.int32, (tiles, SUBLANES, w), 1)
    tile_i = lax.broadcasted_iota(jnp.int32, (tiles, SUBLANES, w), 0)

    def conv_chunk(c, carry):
        r0 = pl.multiple_of(c * chunk, chunk)
        lo = pl.multiple_of(jnp.maximum(r0 - SUBLANES, 0), SUBLANES)
        hi = pl.multiple_of(jnp.minimum(r0 + chunk, nt - SUBLANES), SUBLANES)
        x3 = jnp.concatenate([x_ref[0, pl.ds(lo, SUBLANES), :], x_ref[0, pl.ds(r0, chunk), :],
                              x_ref[0, pl.ds(hi, SUBLANES), :]], axis=0).reshape(tiles + 2, SUBLANES, w)
        sh1 = pltpu.roll(x3, 1, 1)
        sh2 = pltpu.roll(x3, 2, 1)
        sh7 = pltpu.roll(x3, SUBLANES - 1, 1)
        pos = r0 + tile_i * SUBLANES + sub
        in_ctx = pos < nctx
        seg_pos = jnp.where(in_ctx, pos, pos - nctx)
        seg_last = jnp.where(in_ctx, nctx - 1, nt - nctx - 1)
        zero = jnp.zeros((tiles, SUBLANES, w), F32)
        xm2 = jnp.where(seg_pos >= 2, jnp.where(sub >= 2, sh2[1:-1], sh2[0:-2]), zero)
        xm1 = jnp.where(seg_pos >= 1, jnp.where(sub >= 1, sh1[1:-1], sh1[0:-2]), zero)
        xp1 = jnp.where(seg_pos < seg_last, jnp.where(sub < SUBLANES - 1, sh7[1:-1], sh7[2:]), zero)
        y = cb_ref[...] + xm2 * cw_ref[0:1] + xm1 * cw_ref[1:2] + x3[1:-1] * cw_ref[2:3] + xp1 * cw_ref[3:4]
        y_s[pl.ds(r0, chunk), :] = y.reshape(chunk, w)
        return carry

    lax.fori_loop(0, n_chunks, conv_chunk, 0)

    nctx_t = nctx // SUBLANES
    nt_t = nt // SUBLANES

    for d in range(2):
        nlam = -lam_ref[d:d + 1]
        softplus = jnp.maximum(nlam, 0.0) + jnp.log1p(jnp.exp(-jnp.abs(nlam)))
        c8 = -LRU_C * softplus

        def gate_chunk(c, carry, d=d, c8=c8):
            r0 = pl.multiple_of(c * chunk, chunk)
            y = y_s[pl.ds(r0, chunk), :]
            yb = y.astype(BF16)
            r = jax.nn.sigmoid(jnp.dot(yb, wa_ref[d], preferred_element_type=F32) + ba_ref[d:d + 1])
            i = jax.nn.sigmoid(jnp.dot(yb, wi_ref[d], preferred_element_type=F32) + bi_ref[d:d + 1])
            log_a = c8 * r
            a = jnp.exp(log_a)
            th = jnp.tanh(log_a)
            u = jnp.sqrt(-2.0 * th / (1.0 - th)) * (i * y)
            a3 = a.reshape(tiles, SUBLANES, w)
            u3 = u.reshape(tiles, SUBLANES, w)
            for sft in (1, 2, 4):
                if d == 0:
                    ok = sub >= sft
                    ash = pltpu.roll(a3, sft, 1)
                    ush = pltpu.roll(u3, sft, 1)
                else:
                    ok = sub < SUBLANES - sft
                    ash = pltpu.roll(a3, SUBLANES - sft, 1)
                    ush = pltpu.roll(u3, SUBLANES - sft, 1)
                u3 = jnp.where(ok, a3 * ush + u3, u3)
                a3 = jnp.where(ok, a3 * ash, a3)
            a_s[pl.ds(r0, chunk), :] = a3.reshape(chunk, w)
            s_s[pl.ds(r0, chunk), :] = u3.reshape(chunk, w)
            return carry

        lax.fori_loop(0, n_chunks, gate_chunk, 0, unroll=4)

        def carry_tile(j, hprev, d=d):
            if d == 0:
                t = j
            else:
                t = jnp.where(j < nctx_t, nctx_t - 1 - j, nt_t - 1 - (j - nctx_t))
            r0 = pl.multiple_of(t * SUBLANES, SUBLANES)
            h = a_s[pl.ds(r0, SUBLANES), :] * hprev + s_s[pl.ds(r0, SUBLANES), :]
            if d == 0:
                h_s[pl.ds(r0, SUBLANES), :] = h
                return h[SUBLANES - 1:SUBLANES]
            h_s[pl.ds(r0, SUBLANES), :] = h_s[pl.ds(r0, SUBLANES), :] + h
            return h[0:1]

        lax.fori_loop(0, nt_t, carry_tile, jnp.zeros((1, w), F32), unroll=4)

    def out_chunk(c, carry):
        r0 = pl.multiple_of(c * chunk, chunk)
        o_ref[0, pl.ds(r0, chunk), :] = (h_s[pl.ds(r0, chunk), :] * _gelu_tanh(g_ref[0, pl.ds(r0, chunk), :])).astype(BF16)
        return carry

    lax.fori_loop(0, n_chunks, out_chunk, 0)


def _lru(lx, lg, conv_w, conv_b, wa, wi, ba, bi, lam, nctx):
    b, nt, w = lx.shape
    chunk = TOKEN_TILE
    seq = pl.BlockSpec((1, nt, w), lambda i: (i, 0, 0))
    return pl.pallas_call(
        functools.partial(_lru_kernel, nt=nt, nctx=nctx, chunk=chunk),
        grid=(b,),
        in_specs=[seq, seq, _const_spec(conv_w.shape), _const_spec(conv_b.shape), _const_spec(wa.shape),
                  _const_spec(wi.shape), _const_spec(ba.shape), _const_spec(bi.shape), _const_spec(lam.shape)],
        out_specs=seq,
        out_shape=jax.ShapeDtypeStruct((b, nt, w), BF16),
        scratch_shapes=[pltpu.VMEM((nt, w), F32)] * 4,
        compiler_params=_params(("parallel",)),
        name="rglru",
    )(lx, lg, conv_w, conv_b, wa, wi, ba, bi, lam)


DA_KEY_CHUNK = 256
MLA_KEY_CHUNK = 256


def _key_chunks(nk, nctx, size):
    chunks = [(0, nctx)]
    chunks += [(s, min(size, nk - s)) for s in range(nctx, nk, size)]
    return chunks


def _attend_t(chains, chunks):
    def scores(n, ci):
        q_t, key, _ = chains[n]
        return jnp.dot(key(*chunks[ci]), q_t, preferred_element_type=F32).astype(BF16)

    s = [scores(n, 0) for n in range(len(chains))]
    state = [None] * len(chains)
    for ci in range(len(chunks)):
        for n, (_, _, value_t) in enumerate(chains):
            cm = jnp.max(s[n], axis=0, keepdims=True)
            if ci == 0:
                m_new = cm
            else:
                m_old, acc = state[n]
                m_new = jnp.maximum(m_old, cm)
            p = jnp.exp2(s[n] - m_new)
            if ci + 1 < len(chunks):
                s[n] = scores(n, ci + 1)
            pv = jnp.dot(value_t(*chunks[ci]), p, preferred_element_type=F32)
            if ci > 0:
                pv = acc * jnp.exp2(m_old.astype(F32) - m_new.astype(F32)) + pv
            state[n] = (m_new, pv)
    return [acc[0:DA_V] / acc[DA_V:DA_V + 1] for _, acc in state]


def _da_kernel(q_ref, k_ref, vt_ref, dl_ref, g_ref, li_ref, o_ref, *, nt, nctx, tq):
    half = pl.program_id(1)
    row = lax.broadcasted_iota(jnp.int32, (LANES, tq), 0)
    lane = lax.broadcasted_iota(jnp.int32, (tq, LANES), 1)
    dl = dl_ref[...]
    lam_init = li_ref[...]
    lam = (jnp.exp(jnp.sum(dl[0:1] * dl[1:2], axis=-1, keepdims=True))
           - jnp.exp(jnp.sum(dl[2:3] * dl[3:4], axis=-1, keepdims=True)) + lam_init)
    zero = jnp.zeros((LANES, tq), BF16)

    def attend(q0, nk):
        chains = []
        for j in range(PAIRS):
            q_t = q_ref[0, LANES * j:LANES * (j + 1), pl.ds(q0, tq)]
            key = lambda start, size, j=j: k_ref[0, start:start + size, LANES * j:LANES * (j + 1)]
            value_t = lambda start, size, j=j: vt_ref[0, 2 * j + half, :, start:start + size]
            for mi in range(2):
                mine = (row & (LANES // 2 - ROT_HALF)) == (2 * half + mi) * ROT_HALF
                chains.append((jnp.where(mine, q_t, zero), key, value_t))
        o = _attend_t(chains, _key_chunks(nk, nctx, DA_KEY_CHUNK))
        for j in range(PAIRS):
            d = o[2 * j] - lam * o[2 * j + 1]
            d = d * lax.rsqrt(jnp.mean(d * d, axis=0, keepdims=True) + RMS_EPS)
            both = jnp.concatenate([d, d], axis=0).T
            new = (both * g_ref[...] * (1.0 - lam_init)).astype(BF16)
            slab = (0, pl.ds(q0, tq), slice(LANES * j, LANES * (j + 1)))

            @pl.when(half == 0)
            def _():
                o_ref[slab] = new

            @pl.when(half == 1)
            def _():
                o_ref[slab] = jnp.where(lane >= DA_V, new, o_ref[slab])

    _for_query_blocks(attend, nt, nctx, tq)


def _for_query_blocks(attend, nt, nctx, tq):
    for t in range(nctx // tq):
        attend(t * tq, nctx)

    def latent_block(t, carry):
        attend(pl.multiple_of(nctx + t * tq, tq), nt)
        return carry

    lax.fori_loop(0, (nt - nctx) // tq, latent_block, 0)


def _da_attn(dq_t, dk, dv_t, dlam, gpair, lam_init, nctx):
    b, nt, _ = dk.shape
    tq = TOKEN_TILE
    return pl.pallas_call(
        functools.partial(_da_kernel, nt=nt, nctx=nctx, tq=tq),
        grid=(b, 2),
        in_specs=[
            pl.BlockSpec((1, DA_WIDTH, nt), lambda i, h: (i, 0, 0)),
            pl.BlockSpec((1, nt, DA_WIDTH), lambda i, h: (i, 0, 0)),
            pl.BlockSpec((1, DA_HEADS, VT_ROWS, nt), lambda i, h: (i, 0, 0, 0)),
            _const_spec(dlam.shape), _const_spec(gpair.shape), _const_spec(lam_init.shape),
        ],
        out_specs=pl.BlockSpec((1, nt, DA_WIDTH), lambda i, h: (i, 0, 0)),
        out_shape=jax.ShapeDtypeStruct((b, nt, DA_WIDTH), BF16),
        compiler_params=_params(("parallel", "arbitrary")),
        name="diff_attn",
    )(dq_t, dk, dv_t, dlam, gpair, lam_init)


def _mla_kernel(q_ref, k_ref, vt_ref, o_ref, *, nt, nctx, tq):
    def attend(q0, nk):
        chains = []
        for hh in range(MLA_HEADS):
            key = lambda start, size, hh=hh: k_ref[0, hh, start:start + size, :]
            value_t = lambda start, size, hh=hh: vt_ref[0, hh, :, start:start + size]
            chains.append((q_ref[0, hh, :, pl.ds(q0, tq)], key, value_t))
        o = _attend_t(chains, _key_chunks(nk, nctx, MLA_KEY_CHUNK))
        for j in range(MLA_HEADS // 2):
            o_ref[0, pl.ds(q0, tq), LANES * j:LANES * (j + 1)] = jnp.concatenate(o[2 * j:2 * j + 2], axis=0).T.astype(BF16)

    _for_query_blocks(attend, nt, nctx, tq)


def _mla_attn(mq_t, mk, mv_t, nctx):
    b, nh, nt, _ = mk.shape
    tq = TOKEN_TILE
    return pl.pallas_call(
        functools.partial(_mla_kernel, nt=nt, nctx=nctx, tq=tq),
        grid=(b,),
        in_specs=[
            pl.BlockSpec((1, nh, LANES, nt), lambda i: (i, 0, 0, 0)),
            pl.BlockSpec((1, nh, nt, LANES), lambda i: (i, 0, 0, 0)),
            pl.BlockSpec((1, nh, VT_ROWS, nt), lambda i: (i, 0, 0, 0)),
        ],
        out_specs=pl.BlockSpec((1, nt, nh * MLA_V), lambda i: (i, 0, 0)),
        out_shape=jax.ShapeDtypeStruct((b, nt, nh * MLA_V), BF16),
        compiler_params=_params(("parallel",)),
        name="mla_attn",
    )(mq_t, mk, mv_t)


def _layer_norm(z, g, b):
    mu = jnp.mean(z, axis=-1, keepdims=True)
    zc = z - mu
    var = jnp.mean(zc * zc, axis=-1, keepdims=True)
    return (zc * lax.rsqrt(var + LN_EPS)) * g + b


def _router_gates(logits, rb):
    scores = jax.nn.sigmoid(logits)
    sel = scores + rb
    lane = lax.broadcasted_iota(jnp.int32, logits.shape, 1)
    r = lane & (EXPERTS_PER_GROUP - 1)
    grp = (lane >> 2) & (N_GROUPS - 1)

    def in_group(x, k):
        return jnp.where(r >= k, pltpu.roll(x, k, 1), pltpu.roll(x, LANES - EXPERTS_PER_GROUP + k, 1))

    others = [in_group(sel, k) for k in (1, 2, 3)]
    pair_max = sel + jnp.maximum(jnp.maximum(others[0], others[1]), others[2])
    grp_score = jnp.maximum(jnp.maximum(pair_max, in_group(pair_max, 1)),
                            jnp.maximum(in_group(pair_max, 2), in_group(pair_max, 3)))
    in_best = None
    for k in (1, 2, 3):
        other = pltpu.roll(grp_score, EXPERTS_PER_GROUP * k, 1)
        wins = (grp_score > other) | ((grp_score == other) & (grp < k))
        in_best = wins if in_best is None else (in_best & wins)
    beaten = jnp.zeros(logits.shape, F32)
    for k, o in zip((1, 2, 3), others):
        beats = (o > sel) | ((o == sel) & (r >= k))
        beaten = beaten + jnp.where(beats, 1.0, 0.0)
    chosen = in_best & (beaten < 2.0)
    sc = jnp.where(chosen, scores, 0.0)
    tot = sc + in_group(sc, 1) + in_group(sc, 2) + in_group(sc, 3)
    return jnp.where(chosen, sc / tot, 0.0), in_best


ROW_BLOCK = 160
BF16_ROWS = 16


def _grouped_experts(v, gates, in_best, w1_ref, w3_ref, w2_ref, xs_ref, gs_ref, ys_ref):
    rows, d = v.shape
    per = EXPERTS_PER_GROUP * D_EXPERT
    lane = lax.broadcasted_iota(jnp.int32, (rows, LANES), 1)
    gsel = jnp.where(in_best & ((lane & (EXPERTS_PER_GROUP - 1)) == 0) & (lane < N_EXPERTS), 1.0, 0.0)
    gsel_b = gsel.astype(BF16)
    ri = lax.broadcasted_iota(jnp.int32, (rows, rows), 0)
    ci = lax.broadcasted_iota(jnp.int32, (rows, rows), 1)
    onehot = lambda cond: jnp.where(cond, 1.0, 0.0).astype(BF16)

    tot = jnp.sum(gsel, axis=0, keepdims=True)
    lane1 = lax.broadcasted_iota(jnp.int32, (1, LANES), 1)
    cnt = [jnp.sum(jnp.where(lane1 == EXPERTS_PER_GROUP * g, tot, 0.0)).astype(jnp.int32) for g in range(N_GROUPS)]
    off = [jnp.int32(0)]
    for g in range(1, N_GROUPS):
        off.append(off[-1] + cnt[g - 1])

    off_lane = sum(jnp.where(lane1 == EXPERTS_PER_GROUP * g, off[g].astype(F32), 0.0) for g in range(N_GROUPS))
    before = jnp.dot(onehot(ci < ri), gsel_b, preferred_element_type=F32)
    pos_col = jnp.sum(gsel * (before + off_lane), axis=1, keepdims=True)
    sub8 = lax.broadcasted_iota(jnp.int32, (SUBLANES, LANES), 0)
    lane8 = lax.broadcasted_iota(jnp.int32, (SUBLANES, LANES), 1)
    pick = jnp.where((lane8 == EXPERTS_PER_GROUP * sub8) & (sub8 < N_GROUPS), 1.0, 0.0).astype(BF16)
    gsel_t = lax.dot_general(pick, gsel_b, (((1,), (1,)), ((), ())), preferred_element_type=F32)
    before_t = jnp.dot(gsel_t.astype(BF16), onehot(ri < ci), preferred_element_type=F32)
    sub_col = lax.broadcasted_iota(jnp.int32, (SUBLANES, 1), 0)
    off_sub = sum(jnp.where(sub_col == g, off[g].astype(F32), 0.0) for g in range(N_GROUPS))
    pos_row = jnp.sum(gsel_t * (before_t + off_sub), axis=0, keepdims=True)
    perm = onehot(pos_row == ri.astype(F32))
    perm_t = onehot(pos_col == ci.astype(F32))

    xs_ref[...] = jnp.dot(perm, v, preferred_element_type=F32).astype(BF16)
    g_hi = gates.astype(BF16)
    g_lo = (gates - g_hi.astype(F32)).astype(BF16)
    gs_ref[...] = jnp.dot(perm, g_hi, preferred_element_type=F32) + jnp.dot(perm, g_lo, preferred_element_type=F32)
    ys_ref[...] = jnp.zeros(ys_ref.shape, F32)

    row_in_block = lax.broadcasted_iota(jnp.int32, (ROW_BLOCK, LANES), 0)
    for g in range(N_GROUPS):
        first = (off[g] // BF16_ROWS) * BF16_ROWS
        end = off[g] + cnt[g]
        for k in range(-(-rows // ROW_BLOCK)):
            lo = first + k * ROW_BLOCK

            @pl.when((lo < end) & (cnt[g] > 0))
            def _(lo=lo, g=g):
                st = pl.multiple_of(jnp.minimum(lo, rows - ROW_BLOCK), BF16_ROWS)
                xb = xs_ref[pl.ds(st, ROW_BLOCK), :]
                gb = jnp.where(row_in_block + st >= lo, gs_ref[pl.ds(st, ROW_BLOCK), :], 0.0)
                h1 = jnp.dot(xb, w1_ref[:, g * per:(g + 1) * per], preferred_element_type=F32)
                h3 = jnp.dot(xb, w3_ref[:, g * per:(g + 1) * per], preferred_element_type=F32)
                hh = (h1 * jax.nn.sigmoid(h1)) * h3
                parts = []
                for j in range(EXPERTS_PER_GROUP):
                    e = g * EXPERTS_PER_GROUP + j
                    parts.append((hh[:, j * D_EXPERT:(j + 1) * D_EXPERT] * gb[:, e:e + 1]).astype(BF16))
                ys_ref[pl.ds(st, ROW_BLOCK), :] += jnp.dot(jnp.concatenate(parts, axis=-1),
                                                           w2_ref[g * per:(g + 1) * per, :],
                                                           preferred_element_type=F32)

    return jnp.dot(perm_t, ys_ref[...].astype(BF16), preferred_element_type=F32)


def _post_kernel(xc_ref, xl_ref, mod_ref, lru_ref, da_ref, mla_ref, wo_ref, g1_ref, b1_ref, rw_ref, rb_ref, w1_ref,
                 w3_ref, w2_ref, g_ref, b_ref, o_ref, xs_ref, gs_ref, ys_ref, *, alpha, n_ctx_tiles):
    nb, tm, d = xl_ref.shape
    rows = nb * tm
    x = jnp.where(pl.program_id(1) < n_ctx_tiles, xc_ref[...], xl_ref[...]).reshape(rows, d)
    mod = mod_ref[:, 0]
    per_row = lambda k: jnp.broadcast_to(mod[:, k:k + 1], (nb, tm, d)).reshape(rows, d)
    a = jnp.concatenate([lru_ref[...], da_ref[...], mla_ref[...]], axis=-1).reshape(rows, -1)
    o = jnp.dot(a, wo_ref[...], preferred_element_type=F32)
    x1 = _layer_norm(alpha * x + per_row(2) * o, g1_ref[...], b1_ref[...])
    v = (x1 * (1.0 + per_row(4)) + per_row(3)).astype(BF16)
    gates, in_best = _router_gates(jnp.dot(v, rw_ref[...], preferred_element_type=F32), rb_ref[...])
    f = _grouped_experts(v, gates, in_best, w1_ref, w3_ref, w2_ref, xs_ref, gs_ref, ys_ref)
    o_ref[...] = _layer_norm(alpha * x1 + per_row(5) * f, g_ref[...], b_ref[...]).reshape(nb, tm, d)


POST_BATCH = 2


def _post(xc, xl, lat_shift, modt, lru_o, da_o, mla_o, wo, g1, b1, rw, rb, w1c, w3c, w2c, g2, b2, nctx, alpha,
          latent_only):
    b, nt, _ = lru_o.shape
    d = xl.shape[-1]
    tm = TOKEN_TILE
    nc = nctx // tm
    skip = nc if latent_only else 0
    nb = POST_BATCH if b % POST_BATCH == 0 else 1
    tok = lambda w: pl.BlockSpec((nb, tm, w), lambda i, t: (i, t + skip, 0))
    consts = (wo, g1, b1, rw, rb, w1c, w3c, w2c, g2, b2)
    return pl.pallas_call(
        functools.partial(_post_kernel, alpha=alpha, n_ctx_tiles=nc - skip),
        grid=(b // nb, nt // tm - skip),
        in_specs=_stream_specs(nb, tm, d, nc, lat_shift, skip) + [
            pl.BlockSpec((nb, 1, 8, d), lambda i, t: (i, jnp.where(t + skip >= nc, 1, 0), 0, 0)),
            tok(LRU_WIDTH), tok(da_o.shape[-1]), tok(mla_o.shape[-1]),
        ] + [_const_spec(c.shape) for c in consts],
        out_specs=pl.BlockSpec((nb, tm, d), lambda i, t: (i, t, 0)),
        out_shape=jax.ShapeDtypeStruct((b, nt - skip * tm, d), F32),
        scratch_shapes=[pltpu.VMEM((nb * tm, d), BF16), pltpu.VMEM((nb * tm, LANES), F32),
                        pltpu.VMEM((nb * tm, d), F32)],
        compiler_params=_params(("parallel", "parallel")),
        name="post",
    )(xc, xl, modt, lru_o, da_o, mla_o, *consts)


def _rotary_tables(n, nctx):
    rows = n // GRID_W
    row = jnp.repeat(jnp.arange(rows), GRID_W).astype(F32)
    col = jnp.tile(jnp.arange(GRID_W), rows).astype(F32)
    n_freq = DA_QK // 4
    inv = ROPE_THETA ** (-jnp.arange(n_freq, dtype=F32) / n_freq)
    ang = jnp.concatenate([row[:, None] * inv, col[:, None] * inv], axis=-1)
    ang = jnp.concatenate([jnp.zeros((nctx, DA_QK // 2), F32), ang], axis=0)
    c, s = jnp.cos(ang), jnp.sin(ang)
    cos_d = jnp.tile(c, (1, LANES // ROT_HALF))
    sin_d = jnp.concatenate([jnp.tile(-s, (1, 4)), jnp.tile(s, (1, 4))], axis=-1)
    one = jnp.ones_like(c)
    zero = jnp.zeros_like(c)
    cos_m = jnp.concatenate([one, one, c, one, one, one, c, one], axis=-1)
    sin_m = jnp.concatenate([zero, zero, -s, zero, zero, zero, s, zero], axis=-1)
    return cos_d, sin_d, cos_m, sin_m


def _rot_lanes(w, groups):
    lead = w.shape[:-1]
    w = w.reshape(lead + (groups, 2, ROT_HALF))
    return jnp.swapaxes(w, -3, -2).reshape(lead + (groups * 2 * ROT_HALF,))


def _mla_lanes(nope, rot):
    z = jnp.zeros(nope.shape[:-1] + (ROT_HALF,), nope.dtype)
    return jnp.concatenate([nope[..., :MLA_NOPE // 2], rot[..., :ROT_HALF], z,
                            nope[..., MLA_NOPE // 2:], rot[..., ROT_HALF:], z], axis=-1)


def _pack_in_weight(w_in):
    d = w_in.shape[0]
    n_lru = 2 * LRU_WIDTH
    n_lru_da = n_lru + 3 * DA_WIDTH
    n_rank = Q_RANK + KV_RANK
    wkr = w_in[:, n_lru_da + n_rank:n_lru_da + n_rank + MLA_ROPE]
    krp = _mla_lanes(jnp.zeros((d, MLA_NOPE), F32), wkr)
    wqk = w_in[:, n_lru:n_lru + 2 * DA_WIDTH].reshape(d, 2 * PAIRS, LANES)
    wqk = _rot_lanes(wqk, LANES // DA_QK).reshape(d, 2 * DA_WIDTH)
    return jnp.concatenate([w_in[:, n_lru_da:n_lru_da + n_rank], krp, wqk, w_in[:, n_lru + 2 * DA_WIDTH:n_lru_da],
                            w_in[:, :n_lru]], axis=-1).astype(BF16)


def _pack_uq(w_uq):
    r = w_uq.shape[0]
    w = w_uq.reshape(r, MLA_HEADS, MLA_NOPE + MLA_ROPE)
    return _mla_lanes(w[..., :MLA_NOPE], w[..., MLA_NOPE:]).reshape(r, MLA_HEADS * LANES).astype(BF16)


def _pack_ukv(w_ukv):
    r = w_ukv.shape[0]
    w = w_ukv.reshape(r, MLA_HEADS, MLA_NOPE + MLA_V)
    wk = _mla_lanes(w[..., :MLA_NOPE], jnp.zeros((r, MLA_HEADS, MLA_ROPE), F32)).reshape(r, MLA_HEADS * LANES)
    wv = w[..., MLA_NOPE:].reshape(r, MLA_HEADS * MLA_V)
    return wk.astype(BF16), wv.astype(BF16)


def _block_diag(w):
    nd, nb, bs, _ = w.shape
    eye = jnp.eye(nb, dtype=w.dtype)
    return jnp.einsum('dhij,hg->dhigj', w, eye).reshape(nd, nb * bs, nb * bs)


def kernel(x, c, ctx, c_ctx, w_mod, b_mod, w_in, w_out, conv_w, conv_b, lru_wa, lru_ba, lru_wi, lru_bi, lru_lambda, diff_lambda, diff_norm, mla_q_norm, mla_kv_norm, mla_w_uq, mla_w_ukv, ln1_g, ln1_b, ln2_g, ln2_b, router_w, router_b, exp_w1, exp_w3, exp_w2):
    bsz, n, d = x.shape
    nctx = ctx.shape[1]
    depth = w_mod.shape[0]
    alpha = (2 * depth) ** 0.25
    assert nctx % TOKEN_TILE == 0 and n % TOKEN_TILE == 0 and n % GRID_W == 0

    rows = -(-(bsz + 1) // SUBLANES) * SUBLANES
    cc = jnp.concatenate([c, c_ctx[None, :], jnp.zeros((rows - bsz - 1, d), F32)], axis=0)
    mod = _modulation(cc, w_mod, b_mod).reshape(depth, rows, 6, d)
    mod = jnp.pad(mod, ((0, 0), (0, 0), (0, 2), (0, 0)))
    mod_ctx = jnp.broadcast_to(mod[:, bsz][:, None], (depth, bsz, 8, d))
    modt = jnp.stack([mod_ctx, mod[:, :bsz]], axis=2)

    rot_tables = _rotary_tables(n, nctx)
    rw = jnp.tile(router_w, (1, LANES // N_EXPERTS)).astype(BF16)
    rb = jnp.tile(router_b, LANES // N_EXPERTS)[None, :].astype(F32)
    gpair = jnp.tile(diff_norm, (1, LANES // DA_V))

    xc, xl, lat_shift = ctx, x, nctx // TOKEN_TILE
    for l in range(depth):
        lam_init = jnp.full((1, 1), 0.8 - 0.6 * math.exp(-0.3 * l), F32)
        wuk, wuv = _pack_ukv(mla_w_ukv[l])
        lx, lg, dq_t, dk, dv_t, mq_t, mk, mv_t = _in_proj(
            xc, xl, lat_shift, nctx + n, modt[l], _pack_in_weight(w_in[l]), _pack_uq(mla_w_uq[l]), wuk, wuv,
            mla_q_norm[l][None, :], mla_kv_norm[l][None, :], rot_tables, nctx)
        lru_o = _lru(lx, lg, conv_w[l], conv_b[l][None, :], _block_diag(lru_wa[l]).astype(BF16),
                     _block_diag(lru_wi[l]).astype(BF16), lru_ba[l], lru_bi[l], lru_lambda[l], nctx)
        da_o = _da_attn(dq_t, dk, dv_t, diff_lambda[l], gpair[l][None, :], lam_init, nctx)
        mla_o = _mla_attn(mq_t, mk, mv_t, nctx)
        w1c = exp_w1[l].transpose(1, 0, 2).reshape(d, N_EXPERTS * D_EXPERT).astype(BF16)
        w3c = exp_w3[l].transpose(1, 0, 2).reshape(d, N_EXPERTS * D_EXPERT).astype(BF16)
        w2c = exp_w2[l].reshape(N_EXPERTS * D_EXPERT, d).astype(BF16)
        xa = _post(xc, xl, lat_shift, modt[l], lru_o, da_o, mla_o, w_out[l].astype(BF16), ln1_g[l][None, :],
                   ln1_b[l][None, :], rw, rb, w1c, w3c, w2c, ln2_g[l][None, :], ln2_b[l][None, :], nctx, alpha,
                   latent_only=(l == depth - 1))
        xc, xl, lat_shift = xa, xa, 0
    return xa
```

```python
import functools
import math

import jax
import jax.numpy as jnp
from jax import lax
from jax.experimental import pallas as pl
from jax.experimental.pallas import tpu as pltpu

F32 = jnp.float32
BF16 = jnp.bfloat16

GRID_W = 64
LRU_WIDTH = 256
LRU_BLOCKS = 4
CONV_W = 4
LRU_C = 8.0
DA_HEADS = 6
DA_QK = 32
DA_V = 2 * DA_QK
MLA_HEADS = 6
MLA_NOPE = 64
MLA_ROPE = 32
MLA_V = 64
Q_RANK = 256
KV_RANK = 128
MLA_SCALE = (MLA_NOPE + MLA_ROPE) ** -0.5
N_EXPERTS = 16
N_GROUPS = 4
EXPERTS_PER_GROUP = N_EXPERTS // N_GROUPS
D_EXPERT = 256
ROPE_THETA = 10000.0
LN_EPS = 1e-5
RMS_EPS = 1e-6

LANES = 128
SUBLANES = 8
TOKEN_TILE = 256
VMEM_LIMIT = 56 * 1024 * 1024

LOG2E = math.log2(math.e)
DA_QSCALE = DA_QK ** -0.5 * LOG2E
MLA_QSCALE = MLA_SCALE * LOG2E

DA_WIDTH = DA_HEADS * DA_V
C_CQ = 0
C_CKV = C_CQ + Q_RANK
C_KR = C_CKV + KV_RANK
C_DAQ = C_KR + LANES
C_DAK = C_DAQ + DA_WIDTH
C_DAV = C_DAK + DA_WIDTH
C_LRU = C_DAV + DA_WIDTH
C_END = C_LRU + 2 * LRU_WIDTH
PAIRS = DA_HEADS // 2
ROT_HALF = DA_QK // 2
DA_HALF_WIDTH = 2 * LANES
VT_ROWS = DA_V + 16


def _params(sem):
    return pltpu.CompilerParams(dimension_semantics=sem, vmem_limit_bytes=VMEM_LIMIT)


def _const_spec(shape):
    nd = len(shape)
    return pl.BlockSpec(shape, lambda *_: (0,) * nd, pipeline_mode=pl.Buffered(1))


def _mod_kernel(c_ref, w_ref, b_ref, o_ref):
    c = c_ref[...]
    s = c * jax.nn.sigmoid(c)
    o_ref[0] = jnp.dot(s.astype(BF16), w_ref[0].astype(BF16), preferred_element_type=F32) + b_ref[0]


def _modulation(cc, w_mod, b_mod):
    depth, d, d6 = w_mod.shape
    r = cc.shape[0]
    tn = min(d6, 1536)
    return pl.pallas_call(
        _mod_kernel,
        grid=(depth, d6 // tn),
        in_specs=[
            pl.BlockSpec((r, d), lambda l, j: (0, 0)),
            pl.BlockSpec((1, d, tn), lambda l, j: (l, 0, j)),
            pl.BlockSpec((1, 1, tn), lambda l, j: (l, 0, j)),
        ],
        out_specs=pl.BlockSpec((1, r, tn), lambda l, j: (l, 0, j)),
        out_shape=jax.ShapeDtypeStruct((depth, r, d6), F32),
        compiler_params=_params(("parallel", "parallel")),
        name="modulation",
    )(cc, w_mod, b_mod.reshape(depth, 1, d6))


def _rotate(t, cosf, sinf):
    return t * cosf + pltpu.roll(t, LANES // 2, 1) * sinf


def _store_values_t(vt_ref, bi, v):
    rows = v.shape[0]
    ones = jnp.ones((VT_ROWS - DA_V, rows), BF16)
    for j in range(PAIRS):
        t = v[:, LANES * j:LANES * (j + 1)].T.astype(BF16)
        for k in range(2):
            vt_ref[bi, 2 * j + k, 0:DA_V, :] = t[DA_V * k:DA_V * (k + 1)]
            vt_ref[bi, 2 * j + k, DA_V:VT_ROWS, :] = ones


def _in_kernel(xc_ref, xl_ref, mod_ref, w1_ref, wuq_ref, wuk_ref, wuv_ref, qn_ref, kvn_ref, cd_ref, sd_ref, cm_ref,
               sm_ref, lx_ref, lg_ref, dq_ref, dk_ref, dv_ref, mq_ref, mk_ref, mv_ref, *, n_ctx_tiles):
    nb, tm, d = xl_ref.shape
    x = jnp.where(pl.program_id(1) < n_ctx_tiles, xc_ref[...], xl_ref[...])
    mod = mod_ref[:, 0]
    u = (x * (1.0 + mod[:, 1:2]) + mod[:, 0:1]).astype(BF16).reshape(nb * tm, d)
    proj = lambda lo, hi: jnp.dot(u, w1_ref[:, lo:hi], preferred_element_type=F32)
    rot = functools.partial(_rotate, cosf=cd_ref[...], sinf=sd_ref[...])
    rot_mla = functools.partial(_rotate, cosf=cm_ref[...], sinf=sm_ref[...])

    y_mla = proj(C_CQ, C_DAQ)
    y_da = proj(C_DAQ, C_DAV)

    cq = y_mla[:, C_CQ:C_CKV]
    ckv = y_mla[:, C_CKV:C_KR]
    krp = y_mla[:, C_KR:C_DAQ]
    qn = (cq * lax.rsqrt(jnp.mean(cq * cq, axis=-1, keepdims=True) + RMS_EPS)) * qn_ref[...]
    kvn = ((ckv * lax.rsqrt(jnp.mean(ckv * ckv, axis=-1, keepdims=True) + RMS_EPS)) * kvn_ref[...]).astype(BF16)
    q = jnp.dot(qn.astype(BF16), wuq_ref[...], preferred_element_type=F32)
    kn = jnp.dot(kvn, wuk_ref[...], preferred_element_type=F32)

    for bi in range(nb):
        rs = slice(bi * tm, (bi + 1) * tm)
        for j in range(PAIRS):
            t = y_da[rs, LANES * j:LANES * (j + 1)]
            dq_ref[bi, LANES * j:LANES * (j + 1), :] = (rot(t) * DA_QSCALE).T.astype(BF16)
            t = y_da[rs, DA_WIDTH + LANES * j:DA_WIDTH + LANES * (j + 1)]
            dk_ref[bi, :, LANES * j:LANES * (j + 1)] = rot(t).astype(BF16)

    y_rest = proj(C_DAV, C_END)
    mv = jnp.dot(kvn, wuv_ref[...], preferred_element_type=F32)

    for bi in range(nb):
        rs = slice(bi * tm, (bi + 1) * tm)
        kr = rot_mla(krp[rs])
        for h in range(MLA_HEADS):
            t = q[rs, LANES * h:LANES * (h + 1)]
            mq_ref[bi, h] = (rot_mla(t) * MLA_QSCALE).T.astype(BF16)
            mk_ref[bi, h] = (kn[rs, LANES * h:LANES * (h + 1)] + kr).astype(BF16)
        _store_values_t(dv_ref, bi, y_rest[rs, 0:DA_WIDTH])
        _store_values_t(mv_ref, bi, mv[rs])
    lx_ref[...] = y_rest[:, DA_WIDTH:DA_WIDTH + LRU_WIDTH].reshape(nb, tm, LRU_WIDTH)
    lg_ref[...] = y_rest[:, DA_WIDTH + LRU_WIDTH:DA_WIDTH + 2 * LRU_WIDTH].reshape(nb, tm, LRU_WIDTH)


IN_BATCH = 2


def _stream_specs(rows, tm, d, nc, lat_shift, skip=0):
    return [pl.BlockSpec((rows, tm, d), lambda i, t: (i, jnp.minimum(t + skip, nc - 1), 0)),
            pl.BlockSpec((rows, tm, d), lambda i, t: (i, jnp.maximum(t + skip - lat_shift, 0), 0))]


def _in_proj(xc, xl, lat_shift, nt, modt, w1, wuq, wuk, wuv, qnorm, kvnorm, rot_tables, nctx):
    b, _, d = xl.shape
    tm = TOKEN_TILE
    nc = nctx // tm
    nb = IN_BATCH if b % IN_BATCH == 0 else 1
    tok = lambda w: pl.BlockSpec((nb, tm, w), lambda i, t: (i, t, 0))
    head_t = lambda r: pl.BlockSpec((nb, DA_HEADS, r, tm), lambda i, t: (i, 0, 0, t))
    return pl.pallas_call(
        functools.partial(_in_kernel, n_ctx_tiles=nc),
        grid=(b // nb, nt // tm),
        in_specs=_stream_specs(nb, tm, d, nc, lat_shift) + [
            pl.BlockSpec((nb, 1, 8, d), lambda i, t: (i, jnp.where(t >= nc, 1, 0), 0, 0)),
            _const_spec(w1.shape), _const_spec(wuq.shape), _const_spec(wuk.shape), _const_spec(wuv.shape),
            _const_spec(qnorm.shape), _const_spec(kvnorm.shape),
        ] + [pl.BlockSpec((tm, LANES), lambda i, t: (t, 0))] * len(rot_tables),
        out_specs=[
            tok(LRU_WIDTH), tok(LRU_WIDTH),
            pl.BlockSpec((nb, DA_WIDTH, tm), lambda i, t: (i, 0, t)), tok(DA_WIDTH), head_t(VT_ROWS),
            head_t(LANES), pl.BlockSpec((nb, MLA_HEADS, tm, LANES), lambda i, t: (i, 0, t, 0)), head_t(VT_ROWS),
        ],
        out_shape=[
            jax.ShapeDtypeStruct((b, nt, LRU_WIDTH), F32),
            jax.ShapeDtypeStruct((b, nt, LRU_WIDTH), F32),
            jax.ShapeDtypeStruct((b, DA_WIDTH, nt), BF16),
            jax.ShapeDtypeStruct((b, nt, DA_WIDTH), BF16),
            jax.ShapeDtypeStruct((b, DA_HEADS, VT_ROWS, nt), BF16),
            jax.ShapeDtypeStruct((b, MLA_HEADS, LANES, nt), BF16),
            jax.ShapeDtypeStruct((b, MLA_HEADS, nt, LANES), BF16),
            jax.ShapeDtypeStruct((b, MLA_HEADS, VT_ROWS, nt), BF16),
        ],
        compiler_params=_params(("parallel", "parallel")),
        name="in_proj",
    )(xc, xl, modt, w1, wuq, wuk, wuv, qnorm, kvnorm, *rot_tables)


def _gelu_tanh(x):
    return 0.5 * x * (1.0 + jnp.tanh(math.sqrt(2.0 / math.pi) * (x + 0.044715 * (x * x * x))))


def _lru_kernel(x_ref, g_ref, cw_ref, cb_ref, wa_ref, wi_ref, ba_ref, bi_ref, lam_ref, o_ref,
                y_s, a_s, s_s, h_s, *, nt, nctx, chunk):
    w = LRU_WIDTH
    tiles = chunk // SUBLANES
    n_chunks = nt // chunk
    sub = lax.broadcasted_iota(jnp.int32, (tiles, SUBLANES, w), 1)
    tile_i = lax.broadcasted_iota(jnp.int32, (tiles, SUBLANES, w), 0)

    def conv_chunk(c, carry):
        r0 = pl.multiple_of(c * chunk, chunk)
        lo = pl.multiple_of(jnp.maximum(r0 - SUBLANES, 0), SUBLANES)
        hi = pl.multiple_of(jnp.minimum(r0 + chunk, nt - SUBLANES), SUBLANES)
        x3 = jnp.concatenate([x_ref[0, pl.ds(lo, SUBLANES), :], x_ref[0, pl.ds(r0, chunk), :],
                              x_ref[0, pl.ds(hi, SUBLANES), :]], axis=0).reshape(tiles + 2, SUBLANES, w)
        sh1 = pltpu.roll(x3, 1, 1)
        sh2 = pltpu.roll(x3, 2, 1)
        sh7 = pltpu.roll(x3, SUBLANES - 1, 1)
        pos = r0 + tile_i * SUBLANES + sub
        in_ctx = pos < nctx
        seg_pos = jnp.where(in_ctx, pos, pos - nctx)
        seg_last = jnp.where(in_ctx, nctx - 1, nt - nctx - 1)
        zero = jnp.zeros((tiles, SUBLANES, w), F32)
        xm2 = jnp.where(seg_pos >= 2, jnp.where(sub >= 2, sh2[1:-1], sh2[0:-2]), zero)
        xm1 = jnp.where(seg_pos >= 1, jnp.where(sub >= 1, sh1[1:-1], sh1[0:-2]), zero)
        xp1 = jnp.where(seg_pos < seg_last, jnp.where(sub < SUBLANES - 1, sh7[1:-1], sh7[2:]), zero)
        y = cb_ref[...] + xm2 * cw_ref[0:1] + xm1 * cw_ref[1:2] + x3[1:-1] * cw_ref[2:3] + xp1 * cw_ref[3:4]
        y_s[pl.ds(r0, chunk), :] = y.reshape(chunk, w)
        return carry

    lax.fori_loop(0, n_chunks, conv_chunk, 0)

    nctx_t = nctx // SUBLANES
    nt_t = nt // SUBLANES

    for d in range(2):
        nlam = -lam_ref[d:d + 1]
        softplus = jnp.maximum(nlam, 0.0) + jnp.log1p(jnp.exp(-jnp.abs(nlam)))
        c8 = -LRU_C * softplus

        def gate_chunk(c, carry, d=d, c8=c8):
            r0 = pl.multiple_of(c * chunk, chunk)
            y = y_s[pl.ds(r0, chunk), :]
            yb = y.astype(BF16)
            r = jax.nn.sigmoid(jnp.dot(yb, wa_ref[d], preferred_element_type=F32) + ba_ref[d:d + 1])
            i = jax.nn.sigmoid(jnp.dot(yb, wi_ref[d], preferred_element_type=F32) + bi_ref[d:d + 1])
            log_a = c8 * r
            a = jnp.exp(log_a)
            th = jnp.tanh(log_a)
            u = jnp.sqrt(-2.0 * th / (1.0 - th)) * (i * y)
            a3 = a.reshape(tiles, SUBLANES, w)
            u3 = u.reshape(tiles, SUBLANES, w)
            for sft in (1, 2, 4):
                if d == 0:
                    ok = sub >= sft
                    ash = pltpu.roll(a3, sft, 1)
                    ush = pltpu.roll(u3, sft, 1)
                else:
                    ok = sub < SUBLANES - sft
                    ash = pltpu.roll(a3, SUBLANES - sft, 1)
                    ush = pltpu.roll(u3, SUBLANES - sft, 1)
                u3 = jnp.where(ok, a3 * ush + u3, u3)
                a3 = jnp.where(ok, a3 * ash, a3)
            a_s[pl.ds(r0, chunk), :] = a3.reshape(chunk, w)
            s_s[pl.ds(r0, chunk), :] = u3.reshape(chunk, w)
            return carry

        lax.fori_loop(0, n_chunks, gate_chunk, 0, unroll=4)

        def carry_tile(j, hprev, d=d):
            if d == 0:
                t = j
            else:
                t = jnp.where(j < nctx_t, nctx_t - 1 - j, nt_t - 1 - (j - nctx_t))
            r0 = pl.multiple_of(t * SUBLANES, SUBLANES)
            h = a_s[pl.ds(r0, SUBLANES), :] * hprev + s_s[pl.ds(r0, SUBLANES), :]
            if d == 0:
                h_s[pl.ds(r0, SUBLANES), :] = h
                return h[SUBLANES - 1:SUBLANES]
            h_s[pl.ds(r0, SUBLANES), :] = h_s[pl.ds(r0, SUBLANES), :] + h
            return h[0:1]

        lax.fori_loop(0, nt_t, carry_tile, jnp.zeros((1, w), F32), unroll=4)

    def out_chunk(c, carry):
        r0 = pl.multiple_of(c * chunk, chunk)
        o_ref[0, pl.ds(r0, chunk), :] = (h_s[pl.ds(r0, chunk), :] * _gelu_tanh(g_ref[0, pl.ds(r0, chunk), :])).astype(BF16)
        return carry

    lax.fori_loop(0, n_chunks, out_chunk, 0)


def _lru(lx, lg, conv_w, conv_b, wa, wi, ba, bi, lam, nctx):
    b, nt, w = lx.shape
    chunk = TOKEN_TILE
    seq = pl.BlockSpec((1, nt, w), lambda i: (i, 0, 0))
    return pl.pallas_call(
        functools.partial(_lru_kernel, nt=nt, nctx=nctx, chunk=chunk),
        grid=(b,),
        in_specs=[seq, seq, _const_spec(conv_w.shape), _const_spec(conv_b.shape), _const_spec(wa.shape),
                  _const_spec(wi.shape), _const_spec(ba.shape), _const_spec(bi.shape), _const_spec(lam.shape)],
        out_specs=seq,
        out_shape=jax.ShapeDtypeStruct((b, nt, w), BF16),
        scratch_shapes=[pltpu.VMEM((nt, w), F32)] * 4,
        compiler_params=_params(("parallel",)),
        name="rglru",
    )(lx, lg, conv_w, conv_b, wa, wi, ba, bi, lam)


DA_KEY_CHUNK = 256
MLA_KEY_CHUNK = 256


def _key_chunks(nk, nctx, size):
    chunks = [(0, nctx)]
    chunks += [(s, min(size, nk - s)) for s in range(nctx, nk, size)]
    return chunks


def _attend_t(chains, chunks):
    def scores(n, ci):
        q_t, key, _ = chains[n]
        return jnp.dot(key(*chunks[ci]), q_t, preferred_element_type=F32).astype(BF16)

    s = [scores(n, 0) for n in range(len(chains))]
    state = [None] * len(chains)
    for ci in range(len(chunks)):
        for n, (_, _, value_t) in enumerate(chains):
            cm = jnp.max(s[n], axis=0, keepdims=True)
            if ci == 0:
                m_new = cm
            else:
                m_old, acc = state[n]
                m_new = jnp.maximum(m_old, cm)
            p = jnp.exp2(s[n] - m_new)
            if ci + 1 < len(chunks):
                s[n] = scores(n, ci + 1)
            pv = jnp.dot(value_t(*chunks[ci]), p, preferred_element_type=F32)
            if ci > 0:
                pv = acc * jnp.exp2(m_old.astype(F32) - m_new.astype(F32)) + pv
            state[n] = (m_new, pv)
    return [acc[0:DA_V] / acc[DA_V:DA_V + 1] for _, acc in state]


def _da_kernel(q_ref, k_ref, vt_ref, dl_ref, g_ref, li_ref, o_ref, *, nt, nctx, tq):
    half = pl.program_id(1)
    row = lax.broadcasted_iota(jnp.int32, (LANES, tq), 0)
    dl = dl_ref[...]
    lam_init = li_ref[...]
    lam = (jnp.exp(jnp.sum(dl[0:1] * dl[1:2], axis=-1, keepdims=True))
           - jnp.exp(jnp.sum(dl[2:3] * dl[3:4], axis=-1, keepdims=True)) + lam_init)
    zero = jnp.zeros((LANES, tq), BF16)

    def attend(q0, nk):
        chains = []
        for j in range(PAIRS):
            q_t = q_ref[0, LANES * j:LANES * (j + 1), pl.ds(q0, tq)]
            key = lambda start, size, j=j: k_ref[0, start:start + size, LANES * j:LANES * (j + 1)]
            value_t = lambda start, size, j=j: vt_ref[0, 2 * j + half, :, start:start + size]
            for mi in range(2):
                mine = (row & (LANES // 2 - ROT_HALF)) == (2 * half + mi) * ROT_HALF
                chains.append((jnp.where(mine, q_t, zero), key, value_t))
        o = _attend_t(chains, _key_chunks(nk, nctx, DA_KEY_CHUNK))
        heads = []
        for j in range(PAIRS):
            d = o[2 * j] - lam * o[2 * j + 1]
            heads.append(d * lax.rsqrt(jnp.mean(d * d, axis=0, keepdims=True) + RMS_EPS))
        heads.append(jnp.zeros_like(heads[0]))
        for s in range(DA_HALF_WIDTH // LANES):
            out = jnp.concatenate(heads[2 * s:2 * s + 2], axis=0).T
            o_ref[0, pl.ds(q0, tq), LANES * s:LANES * (s + 1)] = (out * g_ref[...] * (1.0 - lam_init)).astype(BF16)

    _for_query_blocks(attend, nt, nctx, tq)


def _for_query_blocks(attend, nt, nctx, tq):
    for t in range(nctx // tq):
        attend(t * tq, nctx)

    def latent_block(t, carry):
        attend(pl.multiple_of(nctx + t * tq, tq), nt)
        return carry

    lax.fori_loop(0, (nt - nctx) // tq, latent_block, 0)


def _da_attn(dq_t, dk, dv_t, dlam, gpair, lam_init, nctx):
    b, nt, _ = dk.shape
    tq = TOKEN_TILE
    return pl.pallas_call(
        functools.partial(_da_kernel, nt=nt, nctx=nctx, tq=tq),
        grid=(b, 2),
        in_specs=[
            pl.BlockSpec((1, DA_WIDTH, nt), lambda i, h: (i, 0, 0)),
            pl.BlockSpec((1, nt, DA_WIDTH), lambda i, h: (i, 0, 0)),
            pl.BlockSpec((1, DA_HEADS, VT_ROWS, nt), lambda i, h: (i, 0, 0, 0)),
            _const_spec(dlam.shape), _const_spec(gpair.shape), _const_spec(lam_init.shape),
        ],
        out_specs=pl.BlockSpec((1, nt, DA_HALF_WIDTH), lambda i, h: (i, 0, h)),
        out_shape=jax.ShapeDtypeStruct((b, nt, 2 * DA_HALF_WIDTH), BF16),
        compiler_params=_params(("parallel", "parallel")),
        name="diff_attn",
    )(dq_t, dk, dv_t, dlam, gpair, lam_init)


def _mla_kernel(q_ref, k_ref, vt_ref, o_ref, *, nt, nctx, tq):
    def attend(q0, nk):
        chains = []
        for hh in range(MLA_HEADS):
            key = lambda start, size, hh=hh: k_ref[0, hh, start:start + size, :]
            value_t = lambda start, size, hh=hh: vt_ref[0, hh, :, start:start + size]
            chains.append((q_ref[0, hh, :, pl.ds(q0, tq)], key, value_t))
        o = _attend_t(chains, _key_chunks(nk, nctx, MLA_KEY_CHUNK))
        for j in range(MLA_HEADS // 2):
            o_ref[0, pl.ds(q0, tq), LANES * j:LANES * (j + 1)] = jnp.concatenate(o[2 * j:2 * j + 2], axis=0).T.astype(BF16)

    _for_query_blocks(attend, nt, nctx, tq)


def _mla_attn(mq_t, mk, mv_t, nctx):
    b, nh, nt, _ = mk.shape
    tq = TOKEN_TILE
    return pl.pallas_call(
        functools.partial(_mla_kernel, nt=nt, nctx=nctx, tq=tq),
        grid=(b,),
        in_specs=[
            pl.BlockSpec((1, nh, LANES, nt), lambda i: (i, 0, 0, 0)),
            pl.BlockSpec((1, nh, nt, LANES), lambda i: (i, 0, 0, 0)),
            pl.BlockSpec((1, nh, VT_ROWS, nt), lambda i: (i, 0, 0, 0)),
        ],
        out_specs=pl.BlockSpec((1, nt, nh * MLA_V), lambda i: (i, 0, 0)),
        out_shape=jax.ShapeDtypeStruct((b, nt, nh * MLA_V), BF16),
        compiler_params=_params(("parallel",)),
        name="mla_attn",
    )(mq_t, mk, mv_t)


def _layer_norm(z, g, b):
    mu = jnp.mean(z, axis=-1, keepdims=True)
    zc = z - mu
    var = jnp.mean(zc * zc, axis=-1, keepdims=True)
    return (zc * lax.rsqrt(var + LN_EPS)) * g + b


def _router_gates(logits, rb):
    scores = jax.nn.sigmoid(logits)
    sel = scores + rb
    lane = lax.broadcasted_iota(jnp.int32, logits.shape, 1)
    r = lane & (EXPERTS_PER_GROUP - 1)
    grp = (lane >> 2) & (N_GROUPS - 1)

    def in_group(x, k):
        return jnp.where(r >= k, pltpu.roll(x, k, 1), pltpu.roll(x, LANES - EXPERTS_PER_GROUP + k, 1))

    others = [in_group(sel, k) for k in (1, 2, 3)]
    pair_max = sel + jnp.maximum(jnp.maximum(others[0], others[1]), others[2])
    grp_score = jnp.maximum(jnp.maximum(pair_max, in_group(pair_max, 1)),
                            jnp.maximum(in_group(pair_max, 2), in_group(pair_max, 3)))
    in_best = None
    for k in (1, 2, 3):
        other = pltpu.roll(grp_score, EXPERTS_PER_GROUP * k, 1)
        wins = (grp_score > other) | ((grp_score == other) & (grp < k))
        in_best = wins if in_best is None else (in_best & wins)
    beaten = jnp.zeros(logits.shape, F32)
    for k, o in zip((1, 2, 3), others):
        beats = (o > sel) | ((o == sel) & (r >= k))
        beaten = beaten + jnp.where(beats, 1.0, 0.0)
    chosen = in_best & (beaten < 2.0)
    sc = jnp.where(chosen, scores, 0.0)
    tot = sc + in_group(sc, 1) + in_group(sc, 2) + in_group(sc, 3)
    return jnp.where(chosen, sc / tot, 0.0), in_best


ROW_BLOCK = 160
BF16_ROWS = 16


def _grouped_experts(v, gates, in_best, w1_ref, w3_ref, w2_ref, xs_ref, gs_ref, ys_ref):
    rows, d = v.shape
    per = EXPERTS_PER_GROUP * D_EXPERT
    lane = lax.broadcasted_iota(jnp.int32, (rows, LANES), 1)
    gsel = jnp.where(in_best & ((lane & (EXPERTS_PER_GROUP - 1)) == 0) & (lane < N_EXPERTS), 1.0, 0.0)
    gsel_b = gsel.astype(BF16)
    ri = lax.broadcasted_iota(jnp.int32, (rows, rows), 0)
    ci = lax.broadcasted_iota(jnp.int32, (rows, rows), 1)
    onehot = lambda cond: jnp.where(cond, 1.0, 0.0).astype(BF16)

    tot = jnp.sum(gsel, axis=0, keepdims=True)
    lane1 = lax.broadcasted_iota(jnp.int32, (1, LANES), 1)
    cnt = [jnp.sum(jnp.where(lane1 == EXPERTS_PER_GROUP * g, tot, 0.0)).astype(jnp.int32) for g in range(N_GROUPS)]
    off = [jnp.int32(0)]
    for g in range(1, N_GROUPS):
        off.append(off[-1] + cnt[g - 1])

    off_lane = sum(jnp.where(lane1 == EXPERTS_PER_GROUP * g, off[g].astype(F32), 0.0) for g in range(N_GROUPS))
    before = jnp.dot(onehot(ci < ri), gsel_b, preferred_element_type=F32)
    pos_col = jnp.sum(gsel * (before + off_lane), axis=1, keepdims=True)
    sub8 = lax.broadcasted_iota(jnp.int32, (SUBLANES, LANES), 0)
    lane8 = lax.broadcasted_iota(jnp.int32, (SUBLANES, LANES), 1)
    pick = jnp.where((lane8 == EXPERTS_PER_GROUP * sub8) & (sub8 < N_GROUPS), 1.0, 0.0).astype(BF16)
    gsel_t = lax.dot_general(pick, gsel_b, (((1,), (1,)), ((), ())), preferred_element_type=F32)
    before_t = jnp.dot(gsel_t.astype(BF16), onehot(ri < ci), preferred_element_type=F32)
    sub_col = lax.broadcasted_iota(jnp.int32, (SUBLANES, 1), 0)
    off_sub = sum(jnp.where(sub_col == g, off[g].astype(F32), 0.0) for g in range(N_GROUPS))
    pos_row = jnp.sum(gsel_t * (before_t + off_sub), axis=0, keepdims=True)
    perm = onehot(pos_row == ri.astype(F32))
    perm_t = onehot(pos_col == ci.astype(F32))

    xs_ref[...] = jnp.dot(perm, v, preferred_element_type=F32).astype(BF16)
    g_hi = gates.astype(BF16)
    g_lo = (gates - g_hi.astype(F32)).astype(BF16)
    gs_ref[...] = jnp.dot(perm, g_hi, preferred_element_type=F32) + jnp.dot(perm, g_lo, preferred_element_type=F32)
    ys_ref[...] = jnp.zeros(ys_ref.shape, F32)

    row_in_block = lax.broadcasted_iota(jnp.int32, (ROW_BLOCK, LANES), 0)
    for g in range(N_GROUPS):
        first = (off[g] // BF16_ROWS) * BF16_ROWS
        end = off[g] + cnt[g]
        for k in range(-(-rows // ROW_BLOCK)):
            lo = first + k * ROW_BLOCK

            @pl.when((lo < end) & (cnt[g] > 0))
            def _(lo=lo, g=g):
                st = pl.multiple_of(jnp.minimum(lo, rows - ROW_BLOCK), BF16_ROWS)
                xb = xs_ref[pl.ds(st, ROW_BLOCK), :]
                gb = jnp.where(row_in_block + st >= lo, gs_ref[pl.ds(st, ROW_BLOCK), :], 0.0)
                h1 = jnp.dot(xb, w1_ref[:, g * per:(g + 1) * per], preferred_element_type=F32)
                h3 = jnp.dot(xb, w3_ref[:, g * per:(g + 1) * per], preferred_element_type=F32)
                hh = (h1 * jax.nn.sigmoid(h1)) * h3
                parts = []
                for j in range(EXPERTS_PER_GROUP):
                    e = g * EXPERTS_PER_GROUP + j
                    parts.append((hh[:, j * D_EXPERT:(j + 1) * D_EXPERT] * gb[:, e:e + 1]).astype(BF16))
                ys_ref[pl.ds(st, ROW_BLOCK), :] += jnp.dot(jnp.concatenate(parts, axis=-1),
                                                           w2_ref[g * per:(g + 1) * per, :],
                                                           preferred_element_type=F32)

    return jnp.dot(perm_t, ys_ref[...].astype(BF16), preferred_element_type=F32)


def _post_kernel(xc_ref, xl_ref, mod_ref, lru_ref, da_ref, mla_ref, wo_ref, g1_ref, b1_ref, rw_ref, rb_ref, w1_ref,
                 w3_ref, w2_ref, g_ref, b_ref, o_ref, xs_ref, gs_ref, ys_ref, *, alpha, n_ctx_tiles):
    nb, tm, d = xl_ref.shape
    rows = nb * tm
    x = jnp.where(pl.program_id(1) < n_ctx_tiles, xc_ref[...], xl_ref[...]).reshape(rows, d)
    mod = mod_ref[:, 0]
    per_row = lambda k: jnp.broadcast_to(mod[:, k:k + 1], (nb, tm, d)).reshape(rows, d)
    a = jnp.concatenate([lru_ref[...], da_ref[...], mla_ref[...]], axis=-1).reshape(rows, -1)
    o = jnp.dot(a, wo_ref[...], preferred_element_type=F32)
    x1 = _layer_norm(alpha * x + per_row(2) * o, g1_ref[...], b1_ref[...])
    v = (x1 * (1.0 + per_row(4)) + per_row(3)).astype(BF16)
    gates, in_best = _router_gates(jnp.dot(v, rw_ref[...], preferred_element_type=F32), rb_ref[...])
    f = _grouped_experts(v, gates, in_best, w1_ref, w3_ref, w2_ref, xs_ref, gs_ref, ys_ref)
    o_ref[...] = _layer_norm(alpha * x1 + per_row(5) * f, g_ref[...], b_ref[...]).reshape(nb, tm, d)


POST_BATCH = 2


def _post(xc, xl, lat_shift, modt, lru_o, da_o, mla_o, wo, g1, b1, rw, rb, w1c, w3c, w2c, g2, b2, nctx, alpha,
          latent_only):
    b, nt, _ = lru_o.shape
    d = xl.shape[-1]
    tm = TOKEN_TILE
    nc = nctx // tm
    skip = nc if latent_only else 0
    nb = POST_BATCH if b % POST_BATCH == 0 else 1
    tok = lambda w: pl.BlockSpec((nb, tm, w), lambda i, t: (i, t + skip, 0))
    consts = (wo, g1, b1, rw, rb, w1c, w3c, w2c, g2, b2)
    return pl.pallas_call(
        functools.partial(_post_kernel, alpha=alpha, n_ctx_tiles=nc - skip),
        grid=(b // nb, nt // tm - skip),
        in_specs=_stream_specs(nb, tm, d, nc, lat_shift, skip) + [
            pl.BlockSpec((nb, 1, 8, d), lambda i, t: (i, jnp.where(t + skip >= nc, 1, 0), 0, 0)),
            tok(LRU_WIDTH), tok(da_o.shape[-1]), tok(mla_o.shape[-1]),
        ] + [_const_spec(c.shape) for c in consts],
        out_specs=pl.BlockSpec((nb, tm, d), lambda i, t: (i, t, 0)),
        out_shape=jax.ShapeDtypeStruct((b, nt - skip * tm, d), F32),
        scratch_shapes=[pltpu.VMEM((nb * tm, d), BF16), pltpu.VMEM((nb * tm, LANES), F32),
                        pltpu.VMEM((nb * tm, d), F32)],
        compiler_params=_params(("parallel", "parallel")),
        name="post",
    )(xc, xl, modt, lru_o, da_o, mla_o, *consts)


def _rotary_tables(n, nctx):
    rows = n // GRID_W
    row = jnp.repeat(jnp.arange(rows), GRID_W).astype(F32)
    col = jnp.tile(jnp.arange(GRID_W), rows).astype(F32)
    n_freq = DA_QK // 4
    inv = ROPE_THETA ** (-jnp.arange(n_freq, dtype=F32) / n_freq)
    ang = jnp.concatenate([row[:, None] * inv, col[:, None] * inv], axis=-1)
    ang = jnp.concatenate([jnp.zeros((nctx, DA_QK // 2), F32), ang], axis=0)
    c, s = jnp.cos(ang), jnp.sin(ang)
    cos_d = jnp.tile(c, (1, LANES // ROT_HALF))
    sin_d = jnp.concatenate([jnp.tile(-s, (1, 4)), jnp.tile(s, (1, 4))], axis=-1)
    one = jnp.ones_like(c)
    zero = jnp.zeros_like(c)
    cos_m = jnp.concatenate([one, one, c, one, one, one, c, one], axis=-1)
    sin_m = jnp.concatenate([zero, zero, -s, zero, zero, zero, s, zero], axis=-1)
    return cos_d, sin_d, cos_m, sin_m


def _rot_lanes(w, groups):
    lead = w.shape[:-1]
    w = w.reshape(lead + (groups, 2, ROT_HALF))
    return jnp.swapaxes(w, -3, -2).reshape(lead + (groups * 2 * ROT_HALF,))


def _mla_lanes(nope, rot):
    z = jnp.zeros(nope.shape[:-1] + (ROT_HALF,), nope.dtype)
    return jnp.concatenate([nope[..., :MLA_NOPE // 2], rot[..., :ROT_HALF], z,
                            nope[..., MLA_NOPE // 2:], rot[..., ROT_HALF:], z], axis=-1)


def _pack_in_weight(w_in):
    d = w_in.shape[0]
    n_lru = 2 * LRU_WIDTH
    n_lru_da = n_lru + 3 * DA_WIDTH
    n_rank = Q_RANK + KV_RANK
    wkr = w_in[:, n_lru_da + n_rank:n_lru_da + n_rank + MLA_ROPE]
    krp = _mla_lanes(jnp.zeros((d, MLA_NOPE), F32), wkr)
    wqk = w_in[:, n_lru:n_lru + 2 * DA_WIDTH].reshape(d, 2 * PAIRS, LANES)
    wqk = _rot_lanes(wqk, LANES // DA_QK).reshape(d, 2 * DA_WIDTH)
    return jnp.concatenate([w_in[:, n_lru_da:n_lru_da + n_rank], krp, wqk, w_in[:, n_lru + 2 * DA_WIDTH:n_lru_da],
                            w_in[:, :n_lru]], axis=-1).astype(BF16)


def _pack_uq(w_uq):
    r = w_uq.shape[0]
    w = w_uq.reshape(r, MLA_HEADS, MLA_NOPE + MLA_ROPE)
    return _mla_lanes(w[..., :MLA_NOPE], w[..., MLA_NOPE:]).reshape(r, MLA_HEADS * LANES).astype(BF16)


def _pack_ukv(w_ukv):
    r = w_ukv.shape[0]
    w = w_ukv.reshape(r, MLA_HEADS, MLA_NOPE + MLA_V)
    wk = _mla_lanes(w[..., :MLA_NOPE], jnp.zeros((r, MLA_HEADS, MLA_ROPE), F32)).reshape(r, MLA_HEADS * LANES)
    wv = w[..., MLA_NOPE:].reshape(r, MLA_HEADS * MLA_V)
    return wk.astype(BF16), wv.astype(BF16)


def _pack_out_weight(w_out):
    d = w_out.shape[1]
    w_da = w_out[LRU_WIDTH:LRU_WIDTH + DA_WIDTH].reshape(PAIRS, 2, DA_V, d)
    pad = jnp.zeros((DA_HALF_WIDTH - PAIRS * DA_V, d), w_out.dtype)
    halves = [jnp.concatenate([w_da[:, h].reshape(PAIRS * DA_V, d), pad], axis=0) for h in range(2)]
    return jnp.concatenate([w_out[:LRU_WIDTH]] + halves + [w_out[LRU_WIDTH + DA_WIDTH:]], axis=0).astype(BF16)


def _block_diag(w):
    nd, nb, bs, _ = w.shape
    eye = jnp.eye(nb, dtype=w.dtype)
    return jnp.einsum('dhij,hg->dhigj', w, eye).reshape(nd, nb * bs, nb * bs)


def kernel(x, c, ctx, c_ctx, w_mod, b_mod, w_in, w_out, conv_w, conv_b, lru_wa, lru_ba, lru_wi, lru_bi, lru_lambda, diff_lambda, diff_norm, mla_q_norm, mla_kv_norm, mla_w_uq, mla_w_ukv, ln1_g, ln1_b, ln2_g, ln2_b, router_w, router_b, exp_w1, exp_w3, exp_w2):
    bsz, n, d = x.shape
    nctx = ctx.shape[1]
    depth = w_mod.shape[0]
    alpha = (2 * depth) ** 0.25
    assert nctx % TOKEN_TILE == 0 and n % TOKEN_TILE == 0 and n % GRID_W == 0

    rows = -(-(bsz + 1) // SUBLANES) * SUBLANES
    cc = jnp.concatenate([c, c_ctx[None, :], jnp.zeros((rows - bsz - 1, d), F32)], axis=0)
    mod = _modulation(cc, w_mod, b_mod).reshape(depth, rows, 6, d)
    mod = jnp.pad(mod, ((0, 0), (0, 0), (0, 2), (0, 0)))
    mod_ctx = jnp.broadcast_to(mod[:, bsz][:, None], (depth, bsz, 8, d))
    modt = jnp.stack([mod_ctx, mod[:, :bsz]], axis=2)

    rot_tables = _rotary_tables(n, nctx)
    rw = jnp.tile(router_w, (1, LANES // N_EXPERTS)).astype(BF16)
    rb = jnp.tile(router_b, LANES // N_EXPERTS)[None, :].astype(F32)
    gpair = jnp.tile(diff_norm, (1, LANES // DA_V))

    xc, xl, lat_shift = ctx, x, nctx // TOKEN_TILE
    for l in range(depth):
        lam_init = jnp.full((1, 1), 0.8 - 0.6 * math.exp(-0.3 * l), F32)
        wuk, wuv = _pack_ukv(mla_w_ukv[l])
        lx, lg, dq_t, dk, dv_t, mq_t, mk, mv_t = _in_proj(
            xc, xl, lat_shift, nctx + n, modt[l], _pack_in_weight(w_in[l]), _pack_uq(mla_w_uq[l]), wuk, wuv,
            mla_q_norm[l][None, :], mla_kv_norm[l][None, :], rot_tables, nctx)
        lru_o = _lru(lx, lg, conv_w[l], conv_b[l][None, :], _block_diag(lru_wa[l]).astype(BF16),
                     _block_diag(lru_wi[l]).astype(BF16), lru_ba[l], lru_bi[l], lru_lambda[l], nctx)
        da_o = _da_attn(dq_t, dk, dv_t, diff_lambda[l], gpair[l][None, :], lam_init, nctx)
        mla_o = _mla_attn(mq_t, mk, mv_t, nctx)
        w1c = exp_w1[l].transpose(1, 0, 2).reshape(d, N_EXPERTS * D_EXPERT).astype(BF16)
        w3c = exp_w3[l].transpose(1, 0, 2).reshape(d, N_EXPERTS * D_EXPERT).astype(BF16)
        w2c = exp_w2[l].reshape(N_EXPERTS * D_EXPERT, d).astype(BF16)
        xa = _post(xc, xl, lat_shift, modt[l], lru_o, da_o, mla_o, _pack_out_weight(w_out[l]), ln1_g[l][None, :],
                   ln1_b[l][None, :], rw, rb, w1c, w3c, w2c, ln2_g[l][None, :], ln2_b[l][None, :], nctx, alpha,
                   latent_only=(l == depth - 1))
        xc, xl, lat_shift = xa, xa, 0
    return xa
```

```python
import functools
import math

import jax
import jax.numpy as jnp
from jax import lax
from jax.experimental import pallas as pl
from jax.experimental.pallas import tpu as pltpu

F32 = jnp.float32
BF16 = jnp.bfloat16

GRID_W = 64
LRU_WIDTH = 256
LRU_BLOCKS = 4
CONV_W = 4
LRU_C = 8.0
DA_HEADS = 6
DA_QK = 32
DA_V = 2 * DA_QK
MLA_HEADS = 6
MLA_NOPE = 64
MLA_ROPE = 32
MLA_V = 64
Q_RANK = 256
KV_RANK = 128
MLA_SCALE = (MLA_NOPE + MLA_ROPE) ** -0.5
N_EXPERTS = 16
N_GROUPS = 4
EXPERTS_PER_GROUP = N_EXPERTS // N_GROUPS
D_EXPERT = 256
ROPE_THETA = 10000.0
LN_EPS = 1e-5
RMS_EPS = 1e-6

LANES = 128
SUBLANES = 8
TOKEN_TILE = 256
VMEM_LIMIT = 56 * 1024 * 1024

LOG2E = math.log2(math.e)
DA_QSCALE = DA_QK ** -0.5 * LOG2E
MLA_QSCALE = MLA_SCALE * LOG2E

DA_WIDTH = DA_HEADS * DA_V
C_CQ = 0
C_CKV = C_CQ + Q_RANK
C_KR = C_CKV + KV_RANK
C_DAQ = C_KR + LANES
C_DAK = C_DAQ + DA_WIDTH
C_DAV = C_DAK + DA_WIDTH
C_LRU = C_DAV + DA_WIDTH
C_END = C_LRU + 2 * LRU_WIDTH
PAIRS = DA_HEADS // 2
ROT_HALF = DA_QK // 2
DA_HALF_WIDTH = 2 * LANES
VT_ROWS = DA_V + 16


def _params(sem):
    return pltpu.CompilerParams(dimension_semantics=sem, vmem_limit_bytes=VMEM_LIMIT)


def _const_spec(shape):
    nd = len(shape)
    return pl.BlockSpec(shape, lambda *_: (0,) * nd, pipeline_mode=pl.Buffered(1))


def _mod_kernel(c_ref, w_ref, b_ref, o_ref):
    c = c_ref[...]
    s = c * jax.nn.sigmoid(c)
    o_ref[0] = jnp.dot(s.astype(BF16), w_ref[0].astype(BF16), preferred_element_type=F32) + b_ref[0]


def _modulation(cc, w_mod, b_mod):
    depth, d, d6 = w_mod.shape
    r = cc.shape[0]
    tn = min(d6, 1536)
    return pl.pallas_call(
        _mod_kernel,
        grid=(depth, d6 // tn),
        in_specs=[
            pl.BlockSpec((r, d), lambda l, j: (0, 0)),
            pl.BlockSpec((1, d, tn), lambda l, j: (l, 0, j)),
            pl.BlockSpec((1, 1, tn), lambda l, j: (l, 0, j)),
        ],
        out_specs=pl.BlockSpec((1, r, tn), lambda l, j: (l, 0, j)),
        out_shape=jax.ShapeDtypeStruct((depth, r, d6), F32),
        compiler_params=_params(("parallel", "parallel")),
        name="modulation",
    )(cc, w_mod, b_mod.reshape(depth, 1, d6))


def _rotate(t, cosf, sinf):
    return t * cosf + pltpu.roll(t, LANES // 2, 1) * sinf


def _store_values_t(vt_ref, bi, v):
    rows = v.shape[0]
    ones = jnp.ones((VT_ROWS - DA_V, rows), BF16)
    for j in range(PAIRS):
        t = v[:, LANES * j:LANES * (j + 1)].T.astype(BF16)
        for k in range(2):
            vt_ref[bi, 2 * j + k, 0:DA_V, :] = t[DA_V * k:DA_V * (k + 1)]
            vt_ref[bi, 2 * j + k, DA_V:VT_ROWS, :] = ones


def _in_kernel(xc_ref, xl_ref, mod_ref, w1_ref, wuq_ref, wuk_ref, wuv_ref, qn_ref, kvn_ref, cd_ref, sd_ref, cm_ref,
               sm_ref, lx_ref, lg_ref, dq_ref, dk_ref, dv_ref, mq_ref, mk_ref, mv_ref, *, n_ctx_tiles):
    nb, tm, d = xl_ref.shape
    x = jnp.where(pl.program_id(1) < n_ctx_tiles, xc_ref[...], xl_ref[...])
    mod = mod_ref[:, 0]
    u = (x * (1.0 + mod[:, 1:2]) + mod[:, 0:1]).astype(BF16).reshape(nb * tm, d)
    proj = lambda lo, hi: jnp.dot(u, w1_ref[:, lo:hi], preferred_element_type=F32)
    rot = functools.partial(_rotate, cosf=cd_ref[...], sinf=sd_ref[...])
    rot_mla = functools.partial(_rotate, cosf=cm_ref[...], sinf=sm_ref[...])

    y_mla = proj(C_CQ, C_DAQ)
    y_da = proj(C_DAQ, C_DAV)

    cq = y_mla[:, C_CQ:C_CKV]
    ckv = y_mla[:, C_CKV:C_KR]
    krp = y_mla[:, C_KR:C_DAQ]
    qn = (cq * lax.rsqrt(jnp.mean(cq * cq, axis=-1, keepdims=True) + RMS_EPS)) * qn_ref[...]
    kvn = ((ckv * lax.rsqrt(jnp.mean(ckv * ckv, axis=-1, keepdims=True) + RMS_EPS)) * kvn_ref[...]).astype(BF16)
    q = jnp.dot(qn.astype(BF16), wuq_ref[...], preferred_element_type=F32)
    kn = jnp.dot(kvn, wuk_ref[...], preferred_element_type=F32)

    for bi in range(nb):
        rs = slice(bi * tm, (bi + 1) * tm)
        for j in range(PAIRS):
            t = y_da[rs, LANES * j:LANES * (j + 1)]
            dq_ref[bi, LANES * j:LANES * (j + 1), :] = (rot(t) * DA_QSCALE).T.astype(BF16)
            t = y_da[rs, DA_WIDTH + LANES * j:DA_WIDTH + LANES * (j + 1)]
            dk_ref[bi, :, LANES * j:LANES * (j + 1)] = rot(t).astype(BF16)

    y_rest = proj(C_DAV, C_END)
    mv = jnp.dot(kvn, wuv_ref[...], preferred_element_type=F32)

    for bi in range(nb):
        rs = slice(bi * tm, (bi + 1) * tm)
        kr = rot_mla(krp[rs])
        for h in range(MLA_HEADS):
            t = q[rs, LANES * h:LANES * (h + 1)]
            mq_ref[bi, h] = (rot_mla(t) * MLA_QSCALE).T.astype(BF16)
            mk_ref[bi, h] = (kn[rs, LANES * h:LANES * (h + 1)] + kr).astype(BF16)
        _store_values_t(dv_ref, bi, y_rest[rs, 0:DA_WIDTH])
        _store_values_t(mv_ref, bi, mv[rs])
    lx_ref[...] = y_rest[:, DA_WIDTH:DA_WIDTH + LRU_WIDTH].reshape(nb, tm, LRU_WIDTH)
    lg_ref[...] = y_rest[:, DA_WIDTH + LRU_WIDTH:DA_WIDTH + 2 * LRU_WIDTH].reshape(nb, tm, LRU_WIDTH)


IN_BATCH = 4


def _stream_specs(rows, tm, d, nc, lat_shift, skip=0):
    return [pl.BlockSpec((rows, tm, d), lambda i, t: (i, jnp.minimum(t + skip, nc - 1), 0)),
            pl.BlockSpec((rows, tm, d), lambda i, t: (i, jnp.maximum(t + skip - lat_shift, 0), 0))]


def _in_proj(xc, xl, lat_shift, nt, modt, w1, wuq, wuk, wuv, qnorm, kvnorm, rot_tables, nctx):
    b, _, d = xl.shape
    tm = TOKEN_TILE
    nc = nctx // tm
    nb = IN_BATCH if b % IN_BATCH == 0 else 1
    tok = lambda w: pl.BlockSpec((nb, tm, w), lambda i, t: (i, t, 0))
    head_t = lambda r: pl.BlockSpec((nb, DA_HEADS, r, tm), lambda i, t: (i, 0, 0, t))
    return pl.pallas_call(
        functools.partial(_in_kernel, n_ctx_tiles=nc),
        grid=(b // nb, nt // tm),
        in_specs=_stream_specs(nb, tm, d, nc, lat_shift) + [
            pl.BlockSpec((nb, 1, 8, d), lambda i, t: (i, jnp.where(t >= nc, 1, 0), 0, 0)),
            _const_spec(w1.shape), _const_spec(wuq.shape), _const_spec(wuk.shape), _const_spec(wuv.shape),
            _const_spec(qnorm.shape), _const_spec(kvnorm.shape),
        ] + [pl.BlockSpec((tm, LANES), lambda i, t: (t, 0))] * len(rot_tables),
        out_specs=[
            tok(LRU_WIDTH), tok(LRU_WIDTH),
            pl.BlockSpec((nb, DA_WIDTH, tm), lambda i, t: (i, 0, t)), tok(DA_WIDTH), head_t(VT_ROWS),
            head_t(LANES), pl.BlockSpec((nb, MLA_HEADS, tm, LANES), lambda i, t: (i, 0, t, 0)), head_t(VT_ROWS),
        ],
        out_shape=[
            jax.ShapeDtypeStruct((b, nt, LRU_WIDTH), F32),
            jax.ShapeDtypeStruct((b, nt, LRU_WIDTH), F32),
            jax.ShapeDtypeStruct((b, DA_WIDTH, nt), BF16),
            jax.ShapeDtypeStruct((b, nt, DA_WIDTH), BF16),
            jax.ShapeDtypeStruct((b, DA_HEADS, VT_ROWS, nt), BF16),
            jax.ShapeDtypeStruct((b, MLA_HEADS, LANES, nt), BF16),
            jax.ShapeDtypeStruct((b, MLA_HEADS, nt, LANES), BF16),
            jax.ShapeDtypeStruct((b, MLA_HEADS, VT_ROWS, nt), BF16),
        ],
        compiler_params=_params(("parallel", "parallel")),
        name="in_proj",
    )(xc, xl, modt, w1, wuq, wuk, wuv, qnorm, kvnorm, *rot_tables)


def _gelu_tanh(x):
    return 0.5 * x * (1.0 + jnp.tanh(math.sqrt(2.0 / math.pi) * (x + 0.044715 * (x * x * x))))


def _lru_kernel(x_ref, g_ref, cw_ref, cb_ref, wa_ref, wi_ref, ba_ref, bi_ref, lam_ref, o_ref,
                y_s, a_s, s_s, h_s, *, nt, nctx, chunk):
    w = LRU_WIDTH
    tiles = chunk // SUBLANES
    n_chunks = nt // chunk
    sub = lax.broadcasted_iota(jnp.int32, (tiles, SUBLANES, w), 1)
    tile_i = lax.broadcasted_iota(jnp.int32, (tiles, SUBLANES, w), 0)

    def conv_chunk(c, carry):
        r0 = pl.multiple_of(c * chunk, chunk)
        lo = pl.multiple_of(jnp.maximum(r0 - SUBLANES, 0), SUBLANES)
        hi = pl.multiple_of(jnp.minimum(r0 + chunk, nt - SUBLANES), SUBLANES)
        x3 = jnp.concatenate([x_ref[0, pl.ds(lo, SUBLANES), :], x_ref[0, pl.ds(r0, chunk), :],
                              x_ref[0, pl.ds(hi, SUBLANES), :]], axis=0).reshape(tiles + 2, SUBLANES, w)
        sh1 = pltpu.roll(x3, 1, 1)
        sh2 = pltpu.roll(x3, 2, 1)
        sh7 = pltpu.roll(x3, SUBLANES - 1, 1)
        pos = r0 + tile_i * SUBLANES + sub
        in_ctx = pos < nctx
        seg_pos = jnp.where(in_ctx, pos, pos - nctx)
        seg_last = jnp.where(in_ctx, nctx - 1, nt - nctx - 1)
        zero = jnp.zeros((tiles, SUBLANES, w), F32)
        xm2 = jnp.where(seg_pos >= 2, jnp.where(sub >= 2, sh2[1:-1], sh2[0:-2]), zero)
        xm1 = jnp.where(seg_pos >= 1, jnp.where(sub >= 1, sh1[1:-1], sh1[0:-2]), zero)
        xp1 = jnp.where(seg_pos < seg_last, jnp.where(sub < SUBLANES - 1, sh7[1:-1], sh7[2:]), zero)
        y = cb_ref[...] + xm2 * cw_ref[0:1] + xm1 * cw_ref[1:2] + x3[1:-1] * cw_ref[2:3] + xp1 * cw_ref[3:4]
        y_s[pl.ds(r0, chunk), :] = y.reshape(chunk, w)
        return carry

    lax.fori_loop(0, n_chunks, conv_chunk, 0)

    nctx_t = nctx // SUBLANES
    nt_t = nt // SUBLANES

    for d in range(2):
        nlam = -lam_ref[d:d + 1]
        softplus = jnp.maximum(nlam, 0.0) + jnp.log1p(jnp.exp(-jnp.abs(nlam)))
        c8 = -LRU_C * softplus

        def gate_chunk(c, carry, d=d, c8=c8):
            r0 = pl.multiple_of(c * chunk, chunk)
            y = y_s[pl.ds(r0, chunk), :]
            yb = y.astype(BF16)
            r = jax.nn.sigmoid(jnp.dot(yb, wa_ref[d], preferred_element_type=F32) + ba_ref[d:d + 1])
            i = jax.nn.sigmoid(jnp.dot(yb, wi_ref[d], preferred_element_type=F32) + bi_ref[d:d + 1])
            log_a = c8 * r
            a = jnp.exp(log_a)
            th = jnp.tanh(log_a)
            u = jnp.sqrt(-2.0 * th / (1.0 - th)) * (i * y)
            a3 = a.reshape(tiles, SUBLANES, w)
            u3 = u.reshape(tiles, SUBLANES, w)
            for sft in (1, 2, 4):
                if d == 0:
                    ok = sub >= sft
                    ash = pltpu.roll(a3, sft, 1)
                    ush = pltpu.roll(u3, sft, 1)
                else:
                    ok = sub < SUBLANES - sft
                    ash = pltpu.roll(a3, SUBLANES - sft, 1)
                    ush = pltpu.roll(u3, SUBLANES - sft, 1)
                u3 = jnp.where(ok, a3 * ush + u3, u3)
                a3 = jnp.where(ok, a3 * ash, a3)
            a_s[pl.ds(r0, chunk), :] = a3.reshape(chunk, w)
            s_s[pl.ds(r0, chunk), :] = u3.reshape(chunk, w)
            return carry

        lax.fori_loop(0, n_chunks, gate_chunk, 0, unroll=4)

        def carry_tile(j, hprev, d=d):
            if d == 0:
                t = j
            else:
                t = jnp.where(j < nctx_t, nctx_t - 1 - j, nt_t - 1 - (j - nctx_t))
            r0 = pl.multiple_of(t * SUBLANES, SUBLANES)
            h = a_s[pl.ds(r0, SUBLANES), :] * hprev + s_s[pl.ds(r0, SUBLANES), :]
            if d == 0:
                h_s[pl.ds(r0, SUBLANES), :] = h
                return h[SUBLANES - 1:SUBLANES]
            h_s[pl.ds(r0, SUBLANES), :] = h_s[pl.ds(r0, SUBLANES), :] + h
            return h[0:1]

        lax.fori_loop(0, nt_t, carry_tile, jnp.zeros((1, w), F32), unroll=4)

    def out_chunk(c, carry):
        r0 = pl.multiple_of(c * chunk, chunk)
        o_ref[0, pl.ds(r0, chunk), :] = (h_s[pl.ds(r0, chunk), :] * _gelu_tanh(g_ref[0, pl.ds(r0, chunk), :])).astype(BF16)
        return carry

    lax.fori_loop(0, n_chunks, out_chunk, 0)


def _lru(lx, lg, conv_w, conv_b, wa, wi, ba, bi, lam, nctx):
    b, nt, w = lx.shape
    chunk = TOKEN_TILE
    seq = pl.BlockSpec((1, nt, w), lambda i: (i, 0, 0))
    return pl.pallas_call(
        functools.partial(_lru_kernel, nt=nt, nctx=nctx, chunk=chunk),
        grid=(b,),
        in_specs=[seq, seq, _const_spec(conv_w.shape), _const_spec(conv_b.shape), _const_spec(wa.shape),
                  _const_spec(wi.shape), _const_spec(ba.shape), _const_spec(bi.shape), _const_spec(lam.shape)],
        out_specs=seq,
        out_shape=jax.ShapeDtypeStruct((b, nt, w), BF16),
        scratch_shapes=[pltpu.VMEM((nt, w), F32)] * 4,
        compiler_params=_params(("parallel",)),
        name="rglru",
    )(lx, lg, conv_w, conv_b, wa, wi, ba, bi, lam)


DA_KEY_CHUNK = 256
MLA_KEY_CHUNK = 256


def _key_chunks(nk, nctx, size):
    chunks = [(0, nctx)]
    chunks += [(s, min(size, nk - s)) for s in range(nctx, nk, size)]
    return chunks


def _attend_t(chains, chunks):
    def scores(n, ci):
        q_t, key, _ = chains[n]
        return jnp.dot(key(*chunks[ci]), q_t, preferred_element_type=F32).astype(BF16)

    s = [scores(n, 0) for n in range(len(chains))]
    state = [None] * len(chains)
    for ci in range(len(chunks)):
        for n, (_, _, value_t) in enumerate(chains):
            cm = jnp.max(s[n], axis=0, keepdims=True)
            if ci == 0:
                m_new = cm
            else:
                m_old, acc = state[n]
                m_new = jnp.maximum(m_old, cm)
            p = jnp.exp2(s[n] - m_new)
            if ci + 1 < len(chunks):
                s[n] = scores(n, ci + 1)
            pv = jnp.dot(value_t(*chunks[ci]), p, preferred_element_type=F32)
            if ci > 0:
                pv = acc * jnp.exp2(m_old.astype(F32) - m_new.astype(F32)) + pv
            state[n] = (m_new, pv)
    return [acc[0:DA_V] / acc[DA_V:DA_V + 1] for _, acc in state]


def _da_kernel(q_ref, k_ref, vt_ref, dl_ref, g_ref, li_ref, o_ref, *, nt, nctx, tq):
    half = pl.program_id(1)
    row = lax.broadcasted_iota(jnp.int32, (LANES, tq), 0)
    dl = dl_ref[...]
    lam_init = li_ref[...]
    lam = (jnp.exp(jnp.sum(dl[0:1] * dl[1:2], axis=-1, keepdims=True))
           - jnp.exp(jnp.sum(dl[2:3] * dl[3:4], axis=-1, keepdims=True)) + lam_init)
    zero = jnp.zeros((LANES, tq), BF16)

    def attend(q0, nk):
        chains = []
        for j in range(PAIRS):
            q_t = q_ref[0, LANES * j:LANES * (j + 1), pl.ds(q0, tq)]
            key = lambda start, size, j=j: k_ref[0, start:start + size, LANES * j:LANES * (j + 1)]
            value_t = lambda start, size, j=j: vt_ref[0, 2 * j + half, :, start:start + size]
            for mi in range(2):
                mine = (row & (LANES // 2 - ROT_HALF)) == (2 * half + mi) * ROT_HALF
                chains.append((jnp.where(mine, q_t, zero), key, value_t))
        o = _attend_t(chains, _key_chunks(nk, nctx, DA_KEY_CHUNK))
        heads = []
        for j in range(PAIRS):
            d = o[2 * j] - lam * o[2 * j + 1]
            heads.append(d * lax.rsqrt(jnp.mean(d * d, axis=0, keepdims=True) + RMS_EPS))
        heads.append(jnp.zeros_like(heads[0]))
        for s in range(DA_HALF_WIDTH // LANES):
            out = jnp.concatenate(heads[2 * s:2 * s + 2], axis=0).T
            o_ref[0, pl.ds(q0, tq), LANES * s:LANES * (s + 1)] = (out * g_ref[...] * (1.0 - lam_init)).astype(BF16)

    _for_query_blocks(attend, nt, nctx, tq)


def _for_query_blocks(attend, nt, nctx, tq):
    for t in range(nctx // tq):
        attend(t * tq, nctx)

    def latent_block(t, carry):
        attend(pl.multiple_of(nctx + t * tq, tq), nt)
        return carry

    lax.fori_loop(0, (nt - nctx) // tq, latent_block, 0)


def _da_attn(dq_t, dk, dv_t, dlam, gpair, lam_init, nctx):
    b, nt, _ = dk.shape
    tq = TOKEN_TILE
    return pl.pallas_call(
        functools.partial(_da_kernel, nt=nt, nctx=nctx, tq=tq),
        grid=(b, 2),
        in_specs=[
            pl.BlockSpec((1, DA_WIDTH, nt), lambda i, h: (i, 0, 0)),
            pl.BlockSpec((1, nt, DA_WIDTH), lambda i, h: (i, 0, 0)),
            pl.BlockSpec((1, DA_HEADS, VT_ROWS, nt), lambda i, h: (i, 0, 0, 0)),
            _const_spec(dlam.shape), _const_spec(gpair.shape), _const_spec(lam_init.shape),
        ],
        out_specs=pl.BlockSpec((1, nt, DA_HALF_WIDTH), lambda i, h: (i, 0, h)),
        out_shape=jax.ShapeDtypeStruct((b, nt, 2 * DA_HALF_WIDTH), BF16),
        compiler_params=_params(("parallel", "parallel")),
        name="diff_attn",
    )(dq_t, dk, dv_t, dlam, gpair, lam_init)


def _mla_kernel(q_ref, k_ref, vt_ref, o_ref, *, nt, nctx, tq):
    def attend(q0, nk):
        chains = []
        for hh in range(MLA_HEADS):
            key = lambda start, size, hh=hh: k_ref[0, hh, start:start + size, :]
            value_t = lambda start, size, hh=hh: vt_ref[0, hh, :, start:start + size]
            chains.append((q_ref[0, hh, :, pl.ds(q0, tq)], key, value_t))
        o = _attend_t(chains, _key_chunks(nk, nctx, MLA_KEY_CHUNK))
        for j in range(MLA_HEADS // 2):
            o_ref[0, pl.ds(q0, tq), LANES * j:LANES * (j + 1)] = jnp.concatenate(o[2 * j:2 * j + 2], axis=0).T.astype(BF16)

    _for_query_blocks(attend, nt, nctx, tq)


def _mla_attn(mq_t, mk, mv_t, nctx):
    b, nh, nt, _ = mk.shape
    tq = TOKEN_TILE
    return pl.pallas_call(
        functools.partial(_mla_kernel, nt=nt, nctx=nctx, tq=tq),
        grid=(b,),
        in_specs=[
            pl.BlockSpec((1, nh, LANES, nt), lambda i: (i, 0, 0, 0)),
            pl.BlockSpec((1, nh, nt, LANES), lambda i: (i, 0, 0, 0)),
            pl.BlockSpec((1, nh, VT_ROWS, nt), lambda i: (i, 0, 0, 0)),
        ],
        out_specs=pl.BlockSpec((1, nt, nh * MLA_V), lambda i: (i, 0, 0)),
        out_shape=jax.ShapeDtypeStruct((b, nt, nh * MLA_V), BF16),
        compiler_params=_params(("parallel",)),
        name="mla_attn",
    )(mq_t, mk, mv_t)


def _layer_norm(z, g, b):
    mu = jnp.mean(z, axis=-1, keepdims=True)
    zc = z - mu
    var = jnp.mean(zc * zc, axis=-1, keepdims=True)
    return (zc * lax.rsqrt(var + LN_EPS)) * g + b


def _router_gates(logits, rb):
    scores = jax.nn.sigmoid(logits)
    sel = scores + rb
    lane = lax.broadcasted_iota(jnp.int32, logits.shape, 1)
    r = lane & (EXPERTS_PER_GROUP - 1)
    grp = (lane >> 2) & (N_GROUPS - 1)

    def in_group(x, k):
        return jnp.where(r >= k, pltpu.roll(x, k, 1), pltpu.roll(x, LANES - EXPERTS_PER_GROUP + k, 1))

    others = [in_group(sel, k) for k in (1, 2, 3)]
    pair_max = sel + jnp.maximum(jnp.maximum(others[0], others[1]), others[2])
    grp_score = jnp.maximum(jnp.maximum(pair_max, in_group(pair_max, 1)),
                            jnp.maximum(in_group(pair_max, 2), in_group(pair_max, 3)))
    in_best = None
    for k in (1, 2, 3):
        other = pltpu.roll(grp_score, EXPERTS_PER_GROUP * k, 1)
        wins = (grp_score > other) | ((grp_score == other) & (grp < k))
        in_best = wins if in_best is None else (in_best & wins)
    beaten = jnp.zeros(logits.shape, F32)
    for k, o in zip((1, 2, 3), others):
        beats = (o > sel) | ((o == sel) & (r >= k))
        beaten = beaten + jnp.where(beats, 1.0, 0.0)
    chosen = in_best & (beaten < 2.0)
    sc = jnp.where(chosen, scores, 0.0)
    tot = sc + in_group(sc, 1) + in_group(sc, 2) + in_group(sc, 3)
    return jnp.where(chosen, sc / tot, 0.0), in_best


ROW_BLOCK = 160
BF16_ROWS = 16


def _grouped_experts(v, gates, in_best, w1_ref, w3_ref, w2_ref, xs_ref, gs_ref, ys_ref):
    rows, d = v.shape
    per = EXPERTS_PER_GROUP * D_EXPERT
    lane = lax.broadcasted_iota(jnp.int32, (rows, LANES), 1)
    gsel = jnp.where(in_best & ((lane & (EXPERTS_PER_GROUP - 1)) == 0) & (lane < N_EXPERTS), 1.0, 0.0)
    gsel_b = gsel.astype(BF16)
    ri = lax.broadcasted_iota(jnp.int32, (rows, rows), 0)
    ci = lax.broadcasted_iota(jnp.int32, (rows, rows), 1)
    onehot = lambda cond: jnp.where(cond, 1.0, 0.0).astype(BF16)

    tot = jnp.sum(gsel, axis=0, keepdims=True)
    lane1 = lax.broadcasted_iota(jnp.int32, (1, LANES), 1)
    cnt = [jnp.sum(jnp.where(lane1 == EXPERTS_PER_GROUP * g, tot, 0.0)).astype(jnp.int32) for g in range(N_GROUPS)]
    off = [jnp.int32(0)]
    for g in range(1, N_GROUPS):
        off.append(off[-1] + cnt[g - 1])

    off_lane = sum(jnp.where(lane1 == EXPERTS_PER_GROUP * g, off[g].astype(F32), 0.0) for g in range(N_GROUPS))
    before = jnp.dot(onehot(ci < ri), gsel_b, preferred_element_type=F32)
    pos_col = jnp.sum(gsel * (before + off_lane), axis=1, keepdims=True)
    sub8 = lax.broadcasted_iota(jnp.int32, (SUBLANES, LANES), 0)
    lane8 = lax.broadcasted_iota(jnp.int32, (SUBLANES, LANES), 1)
    pick = jnp.where((lane8 == EXPERTS_PER_GROUP * sub8) & (sub8 < N_GROUPS), 1.0, 0.0).astype(BF16)
    gsel_t = lax.dot_general(pick, gsel_b, (((1,), (1,)), ((), ())), preferred_element_type=F32)
    before_t = jnp.dot(gsel_t.astype(BF16), onehot(ri < ci), preferred_element_type=F32)
    sub_col = lax.broadcasted_iota(jnp.int32, (SUBLANES, 1), 0)
    off_sub = sum(jnp.where(sub_col == g, off[g].astype(F32), 0.0) for g in range(N_GROUPS))
    pos_row = jnp.sum(gsel_t * (before_t + off_sub), axis=0, keepdims=True)
    perm = onehot(pos_row == ri.astype(F32))
    perm_t = onehot(pos_col == ci.astype(F32))

    xs_ref[...] = jnp.dot(perm, v, preferred_element_type=F32).astype(BF16)
    g_hi = gates.astype(BF16)
    g_lo = (gates - g_hi.astype(F32)).astype(BF16)
    gs_ref[...] = jnp.dot(perm, g_hi, preferred_element_type=F32) + jnp.dot(perm, g_lo, preferred_element_type=F32)
    ys_ref[...] = jnp.zeros(ys_ref.shape, F32)

    row_in_block = lax.broadcasted_iota(jnp.int32, (ROW_BLOCK, LANES), 0)
    for g in range(N_GROUPS):
        first = (off[g] // BF16_ROWS) * BF16_ROWS
        end = off[g] + cnt[g]
        for k in range(-(-rows // ROW_BLOCK)):
            lo = first + k * ROW_BLOCK

            @pl.when((lo < end) & (cnt[g] > 0))
            def _(lo=lo, g=g):
                st = pl.multiple_of(jnp.minimum(lo, rows - ROW_BLOCK), BF16_ROWS)
                xb = xs_ref[pl.ds(st, ROW_BLOCK), :]
                gb = jnp.where(row_in_block + st >= lo, gs_ref[pl.ds(st, ROW_BLOCK), :], 0.0)
                h1 = jnp.dot(xb, w1_ref[:, g * per:(g + 1) * per], preferred_element_type=F32)
                h3 = jnp.dot(xb, w3_ref[:, g * per:(g + 1) * per], preferred_element_type=F32)
                hh = (h1 * jax.nn.sigmoid(h1)) * h3
                parts = []
                for j in range(EXPERTS_PER_GROUP):
                    e = g * EXPERTS_PER_GROUP + j
                    parts.append((hh[:, j * D_EXPERT:(j + 1) * D_EXPERT] * gb[:, e:e + 1]).astype(BF16))
                ys_ref[pl.ds(st, ROW_BLOCK), :] += jnp.dot(jnp.concatenate(parts, axis=-1),
                                                           w2_ref[g * per:(g + 1) * per, :],
                                                           preferred_element_type=F32)

    return jnp.dot(perm_t, ys_ref[...].astype(BF16), preferred_element_type=F32)


def _post_kernel(xc_ref, xl_ref, mod_ref, lru_ref, da_ref, mla_ref, wo_ref, g1_ref, b1_ref, rw_ref, rb_ref, w1_ref,
                 w3_ref, w2_ref, g_ref, b_ref, o_ref, xs_ref, gs_ref, ys_ref, *, alpha, n_ctx_tiles):
    nb, tm, d = xl_ref.shape
    rows = nb * tm
    x = jnp.where(pl.program_id(1) < n_ctx_tiles, xc_ref[...], xl_ref[...]).reshape(rows, d)
    mod = mod_ref[:, 0]
    per_row = lambda k: jnp.broadcast_to(mod[:, k:k + 1], (nb, tm, d)).reshape(rows, d)
    a = jnp.concatenate([lru_ref[...], da_ref[...], mla_ref[...]], axis=-1).reshape(rows, -1)
    o = jnp.dot(a, wo_ref[...], preferred_element_type=F32)
    x1 = _layer_norm(alpha * x + per_row(2) * o, g1_ref[...], b1_ref[...])
    v = (x1 * (1.0 + per_row(4)) + per_row(3)).astype(BF16)
    gates, in_best = _router_gates(jnp.dot(v, rw_ref[...], preferred_element_type=F32), rb_ref[...])
    f = _grouped_experts(v, gates, in_best, w1_ref, w3_ref, w2_ref, xs_ref, gs_ref, ys_ref)
    o_ref[...] = _layer_norm(alpha * x1 + per_row(5) * f, g_ref[...], b_ref[...]).reshape(nb, tm, d)


POST_BATCH = 2


def _post(xc, xl, lat_shift, modt, lru_o, da_o, mla_o, wo, g1, b1, rw, rb, w1c, w3c, w2c, g2, b2, nctx, alpha,
          latent_only):
    b, nt, _ = lru_o.shape
    d = xl.shape[-1]
    tm = TOKEN_TILE
    nc = nctx // tm
    skip = nc if latent_only else 0
    nb = POST_BATCH if b % POST_BATCH == 0 else 1
    tok = lambda w: pl.BlockSpec((nb, tm, w), lambda i, t: (i, t + skip, 0))
    consts = (wo, g1, b1, rw, rb, w1c, w3c, w2c, g2, b2)
    return pl.pallas_call(
        functools.partial(_post_kernel, alpha=alpha, n_ctx_tiles=nc - skip),
        grid=(b // nb, nt // tm - skip),
        in_specs=_stream_specs(nb, tm, d, nc, lat_shift, skip) + [
            pl.BlockSpec((nb, 1, 8, d), lambda i, t: (i, jnp.where(t + skip >= nc, 1, 0), 0, 0)),
            tok(LRU_WIDTH), tok(da_o.shape[-1]), tok(mla_o.shape[-1]),
        ] + [_const_spec(c.shape) for c in consts],
        out_specs=pl.BlockSpec((nb, tm, d), lambda i, t: (i, t, 0)),
        out_shape=jax.ShapeDtypeStruct((b, nt - skip * tm, d), F32),
        scratch_shapes=[pltpu.VMEM((nb * tm, d), BF16), pltpu.VMEM((nb * tm, LANES), F32),
                        pltpu.VMEM((nb * tm, d), F32)],
        compiler_params=_params(("parallel", "parallel")),
        name="post",
    )(xc, xl, modt, lru_o, da_o, mla_o, *consts)


def _rotary_tables(n, nctx):
    rows = n // GRID_W
    row = jnp.repeat(jnp.arange(rows), GRID_W).astype(F32)
    col = jnp.tile(jnp.arange(GRID_W), rows).astype(F32)
    n_freq = DA_QK // 4
    inv = ROPE_THETA ** (-jnp.arange(n_freq, dtype=F32) / n_freq)
    ang = jnp.concatenate([row[:, None] * inv, col[:, None] * inv], axis=-1)
    ang = jnp.concatenate([jnp.zeros((nctx, DA_QK // 2), F32), ang], axis=0)
    c, s = jnp.cos(ang), jnp.sin(ang)
    cos_d = jnp.tile(c, (1, LANES // ROT_HALF))
    sin_d = jnp.concatenate([jnp.tile(-s, (1, 4)), jnp.tile(s, (1, 4))], axis=-1)
    one = jnp.ones_like(c)
    zero = jnp.zeros_like(c)
    cos_m = jnp.concatenate([one, one, c, one, one, one, c, one], axis=-1)
    sin_m = jnp.concatenate([zero, zero, -s, zero, zero, zero, s, zero], axis=-1)
    return cos_d, sin_d, cos_m, sin_m


def _rot_lanes(w, groups):
    lead = w.shape[:-1]
    w = w.reshape(lead + (groups, 2, ROT_HALF))
    return jnp.swapaxes(w, -3, -2).reshape(lead + (groups * 2 * ROT_HALF,))


def _mla_lanes(nope, rot):
    z = jnp.zeros(nope.shape[:-1] + (ROT_HALF,), nope.dtype)
    return jnp.concatenate([nope[..., :MLA_NOPE // 2], rot[..., :ROT_HALF], z,
                            nope[..., MLA_NOPE // 2:], rot[..., ROT_HALF:], z], axis=-1)


def _pack_in_weight(w_in):
    d = w_in.shape[0]
    n_lru = 2 * LRU_WIDTH
    n_lru_da = n_lru + 3 * DA_WIDTH
    n_rank = Q_RANK + KV_RANK
    wkr = w_in[:, n_lru_da + n_rank:n_lru_da + n_rank + MLA_ROPE]
    krp = _mla_lanes(jnp.zeros((d, MLA_NOPE), F32), wkr)
    wqk = w_in[:, n_lru:n_lru + 2 * DA_WIDTH].reshape(d, 2 * PAIRS, LANES)
    wqk = _rot_lanes(wqk, LANES // DA_QK).reshape(d, 2 * DA_WIDTH)
    return jnp.concatenate([w_in[:, n_lru_da:n_lru_da + n_rank], krp, wqk, w_in[:, n_lru + 2 * DA_WIDTH:n_lru_da],
                            w_in[:, :n_lru]], axis=-1).astype(BF16)


def _pack_uq(w_uq):
    r = w_uq.shape[0]
    w = w_uq.reshape(r, MLA_HEADS, MLA_NOPE + MLA_ROPE)
    return _mla_lanes(w[..., :MLA_NOPE], w[..., MLA_NOPE:]).reshape(r, MLA_HEADS * LANES).astype(BF16)


def _pack_ukv(w_ukv):
    r = w_ukv.shape[0]
    w = w_ukv.reshape(r, MLA_HEADS, MLA_NOPE + MLA_V)
    wk = _mla_lanes(w[..., :MLA_NOPE], jnp.zeros((r, MLA_HEADS, MLA_ROPE), F32)).reshape(r, MLA_HEADS * LANES)
    wv = w[..., MLA_NOPE:].reshape(r, MLA_HEADS * MLA_V)
    return wk.astype(BF16), wv.astype(BF16)


def _pack_out_weight(w_out):
    d = w_out.shape[1]
    w_da = w_out[LRU_WIDTH:LRU_WIDTH + DA_WIDTH].reshape(PAIRS, 2, DA_V, d)
    pad = jnp.zeros((DA_HALF_WIDTH - PAIRS * DA_V, d), w_out.dtype)
    halves = [jnp.concatenate([w_da[:, h].reshape(PAIRS * DA_V, d), pad], axis=0) for h in range(2)]
    return jnp.concatenate([w_out[:LRU_WIDTH]] + halves + [w_out[LRU_WIDTH + DA_WIDTH:]], axis=0).astype(BF16)


def _block_diag(w):
    nd, nb, bs, _ = w.shape
    eye = jnp.eye(nb, dtype=w.dtype)
    return jnp.einsum('dhij,hg->dhigj', w, eye).reshape(nd, nb * bs, nb * bs)


def kernel(x, c, ctx, c_ctx, w_mod, b_mod, w_in, w_out, conv_w, conv_b, lru_wa, lru_ba, lru_wi, lru_bi, lru_lambda, diff_lambda, diff_norm, mla_q_norm, mla_kv_norm, mla_w_uq, mla_w_ukv, ln1_g, ln1_b, ln2_g, ln2_b, router_w, router_b, exp_w1, exp_w3, exp_w2):
    bsz, n, d = x.shape
    nctx = ctx.shape[1]
    depth = w_mod.shape[0]
    alpha = (2 * depth) ** 0.25
    assert nctx % TOKEN_TILE == 0 and n % TOKEN_TILE == 0 and n % GRID_W == 0

    rows = -(-(bsz + 1) // SUBLANES) * SUBLANES
    cc = jnp.concatenate([c, c_ctx[None, :], jnp.zeros((rows - bsz - 1, d), F32)], axis=0)
    mod = _modulation(cc, w_mod, b_mod).reshape(depth, rows, 6, d)
    mod = jnp.pad(mod, ((0, 0), (0, 0), (0, 2), (0, 0)))
    mod_ctx = jnp.broadcast_to(mod[:, bsz][:, None], (depth, bsz, 8, d))
    modt = jnp.stack([mod_ctx, mod[:, :bsz]], axis=2)

    rot_tables = _rotary_tables(n, nctx)
    rw = jnp.tile(router_w, (1, LANES // N_EXPERTS)).astype(BF16)
    rb = jnp.tile(router_b, LANES // N_EXPERTS)[None, :].astype(F32)
    gpair = jnp.tile(diff_norm, (1, LANES // DA_V))

    xc, xl, lat_shift = ctx, x, nctx // TOKEN_TILE
    for l in range(depth):
        lam_init = jnp.full((1, 1), 0.8 - 0.6 * math.exp(-0.3 * l), F32)
        wuk, wuv = _pack_ukv(mla_w_ukv[l])
        lx, lg, dq_t, dk, dv_t, mq_t, mk, mv_t = _in_proj(
            xc, xl, lat_shift, nctx + n, modt[l], _pack_in_weight(w_in[l]), _pack_uq(mla_w_uq[l]), wuk, wuv,
            mla_q_norm[l][None, :], mla_kv_norm[l][None, :], rot_tables, nctx)
        lru_o = _lru(lx, lg, conv_w[l], conv_b[l][None, :], _block_diag(lru_wa[l]).astype(BF16),
                     _block_diag(lru_wi[l]).astype(BF16), lru_ba[l], lru_bi[l], lru_lambda[l], nctx)
        da_o = _da_attn(dq_t, dk, dv_t, diff_lambda[l], gpair[l][None, :], lam_init, nctx)
        mla_o = _mla_attn(mq_t, mk, mv_t, nctx)
        w1c = exp_w1[l].transpose(1, 0, 2).reshape(d, N_EXPERTS * D_EXPERT).astype(BF16)
        w3c = exp_w3[l].transpose(1, 0, 2).reshape(d, N_EXPERTS * D_EXPERT).astype(BF16)
        w2c = exp_w2[l].reshape(N_EXPERTS * D_EXPERT, d).astype(BF16)
        xa = _post(xc, xl, lat_shift, modt[l], lru_o, da_o, mla_o, _pack_out_weight(w_out[l]), ln1_g[l][None, :],
                   ln1_b[l][None, :], rw, rb, w1c, w3c, w2c, ln2_g[l][None, :], ln2_b[l][None, :], nctx, alpha,
                   latent_only=(l == depth - 1))
        xc, xl, lat_shift = xa, xa, 0
    return xa
```

```python
import functools
import math

import jax
import jax.numpy as jnp
from jax import lax
from jax.experimental import pallas as pl
from jax.experimental.pallas import tpu as pltpu

F32 = jnp.float32
BF16 = jnp.bfloat16

GRID_W = 64
LRU_WIDTH = 256
LRU_BLOCKS = 4
CONV_W = 4
LRU_C = 8.0
DA_HEADS = 6
DA_QK = 32
DA_V = 2 * DA_QK
MLA_HEADS = 6
MLA_NOPE = 64
MLA_ROPE = 32
MLA_V = 64
Q_RANK = 256
KV_RANK = 128
MLA_SCALE = (MLA_NOPE + MLA_ROPE) ** -0.5
N_EXPERTS = 16
N_GROUPS = 4
EXPERTS_PER_GROUP = N_EXPERTS // N_GROUPS
D_EXPERT = 256
ROPE_THETA = 10000.0
LN_EPS = 1e-5
RMS_EPS = 1e-6

LANES = 128
SUBLANES = 8
TOKEN_TILE = 256
VMEM_LIMIT = 56 * 1024 * 1024

LOG2E = math.log2(math.e)
DA_QSCALE = DA_QK ** -0.5 * LOG2E
MLA_QSCALE = MLA_SCALE * LOG2E

DA_WIDTH = DA_HEADS * DA_V
C_CQ = 0
C_CKV = C_CQ + Q_RANK
C_KR = C_CKV + KV_RANK
C_DAQ = C_KR + LANES
C_DAK = C_DAQ + DA_WIDTH
C_DAV = C_DAK + DA_WIDTH
C_LRU = C_DAV + DA_WIDTH
C_END = C_LRU + 2 * LRU_WIDTH
PAIRS = DA_HEADS // 2
ROT_HALF = DA_QK // 2
DA_HALF_WIDTH = 2 * LANES
VT_ROWS = DA_V + 16


def _params(sem):
    return pltpu.CompilerParams(dimension_semantics=sem, vmem_limit_bytes=VMEM_LIMIT)


def _const_spec(shape):
    nd = len(shape)
    return pl.BlockSpec(shape, lambda *_: (0,) * nd, pipeline_mode=pl.Buffered(1))


def _mod_kernel(c_ref, w_ref, b_ref, o_ref):
    c = c_ref[...]
    s = c * jax.nn.sigmoid(c)
    o_ref[0] = jnp.dot(s.astype(BF16), w_ref[0].astype(BF16), preferred_element_type=F32) + b_ref[0]


def _modulation(cc, w_mod, b_mod):
    depth, d, d6 = w_mod.shape
    r = cc.shape[0]
    tn = min(d6, 1536)
    return pl.pallas_call(
        _mod_kernel,
        grid=(depth, d6 // tn),
        in_specs=[
            pl.BlockSpec((r, d), lambda l, j: (0, 0)),
            pl.BlockSpec((1, d, tn), lambda l, j: (l, 0, j)),
            pl.BlockSpec((1, 1, tn), lambda l, j: (l, 0, j)),
        ],
        out_specs=pl.BlockSpec((1, r, tn), lambda l, j: (l, 0, j)),
        out_shape=jax.ShapeDtypeStruct((depth, r, d6), F32),
        compiler_params=_params(("parallel", "parallel")),
        name="modulation",
    )(cc, w_mod, b_mod.reshape(depth, 1, d6))


def _rotate(t, cosf, sinf):
    return t * cosf + pltpu.roll(t, LANES // 2, 1) * sinf


def _store_values_t(vt_ref, bi, v):
    rows = v.shape[0]
    ones = jnp.ones((VT_ROWS - DA_V, rows), BF16)
    for j in range(PAIRS):
        t = v[:, LANES * j:LANES * (j + 1)].T.astype(BF16)
        for k in range(2):
            vt_ref[bi, 2 * j + k, 0:DA_V, :] = t[DA_V * k:DA_V * (k + 1)]
            vt_ref[bi, 2 * j + k, DA_V:VT_ROWS, :] = ones


def _in_kernel(xc_ref, xl_ref, mod_ref, w1_ref, wuq_ref, wuk_ref, wuv_ref, qn_ref, kvn_ref, cd_ref, sd_ref, cm_ref,
               sm_ref, lx_ref, lg_ref, dq_ref, dk_ref, dv_ref, mq_ref, mk_ref, mv_ref, *, n_ctx_tiles):
    nb, tm, d = xl_ref.shape
    x = jnp.where(pl.program_id(1) < n_ctx_tiles, xc_ref[...], xl_ref[...])
    mod = mod_ref[:, 0]
    u = (x * (1.0 + mod[:, 1:2]) + mod[:, 0:1]).astype(BF16).reshape(nb * tm, d)
    proj = lambda lo, hi: jnp.dot(u, w1_ref[:, lo:hi], preferred_element_type=F32)
    rot = functools.partial(_rotate, cosf=cd_ref[...], sinf=sd_ref[...])
    rot_mla = functools.partial(_rotate, cosf=cm_ref[...], sinf=sm_ref[...])

    y_mla = proj(C_CQ, C_DAQ)
    y_da = proj(C_DAQ, C_DAV)

    cq = y_mla[:, C_CQ:C_CKV]
    ckv = y_mla[:, C_CKV:C_KR]
    krp = y_mla[:, C_KR:C_DAQ]
    qn = (cq * lax.rsqrt(jnp.mean(cq * cq, axis=-1, keepdims=True) + RMS_EPS)) * qn_ref[...]
    kvn = ((ckv * lax.rsqrt(jnp.mean(ckv * ckv, axis=-1, keepdims=True) + RMS_EPS)) * kvn_ref[...]).astype(BF16)
    q = jnp.dot(qn.astype(BF16), wuq_ref[...], preferred_element_type=F32)
    kn = jnp.dot(kvn, wuk_ref[...], preferred_element_type=F32)

    for bi in range(nb):
        rs = slice(bi * tm, (bi + 1) * tm)
        for j in range(PAIRS):
            t = y_da[rs, LANES * j:LANES * (j + 1)]
            dq_ref[bi, LANES * j:LANES * (j + 1), :] = (rot(t) * DA_QSCALE).T.astype(BF16)
            t = y_da[rs, DA_WIDTH + LANES * j:DA_WIDTH + LANES * (j + 1)]
            dk_ref[bi, :, LANES * j:LANES * (j + 1)] = rot(t).astype(BF16)

    y_rest = proj(C_DAV, C_END)
    mv = jnp.dot(kvn, wuv_ref[...], preferred_element_type=F32)

    for bi in range(nb):
        rs = slice(bi * tm, (bi + 1) * tm)
        kr = rot_mla(krp[rs])
        for h in range(MLA_HEADS):
            t = q[rs, LANES * h:LANES * (h + 1)]
            mq_ref[bi, h] = (rot_mla(t) * MLA_QSCALE).T.astype(BF16)
            mk_ref[bi, h] = (kn[rs, LANES * h:LANES * (h + 1)] + kr).astype(BF16)
        _store_values_t(dv_ref, bi, y_rest[rs, 0:DA_WIDTH])
        _store_values_t(mv_ref, bi, mv[rs])
    lx_ref[...] = y_rest[:, DA_WIDTH:DA_WIDTH + LRU_WIDTH].reshape(nb, tm, LRU_WIDTH)
    lg_ref[...] = y_rest[:, DA_WIDTH + LRU_WIDTH:DA_WIDTH + 2 * LRU_WIDTH].reshape(nb, tm, LRU_WIDTH)


IN_BATCH = 4


def _stream_specs(rows, tm, d, nc, lat_shift, skip=0):
    return [pl.BlockSpec((rows, tm, d), lambda i, t: (i, jnp.minimum(t + skip, nc - 1), 0)),
            pl.BlockSpec((rows, tm, d), lambda i, t: (i, jnp.maximum(t + skip - lat_shift, 0), 0))]


def _in_proj(xc, xl, lat_shift, nt, modt, w1, wuq, wuk, wuv, qnorm, kvnorm, rot_tables, nctx):
    b, _, d = xl.shape
    tm = TOKEN_TILE
    nc = nctx // tm
    nb = IN_BATCH if b % IN_BATCH == 0 else 1
    tok = lambda w: pl.BlockSpec((nb, tm, w), lambda i, t: (i, t, 0))
    head_t = lambda r: pl.BlockSpec((nb, DA_HEADS, r, tm), lambda i, t: (i, 0, 0, t))
    return pl.pallas_call(
        functools.partial(_in_kernel, n_ctx_tiles=nc),
        grid=(b // nb, nt // tm),
        in_specs=_stream_specs(nb, tm, d, nc, lat_shift) + [
            pl.BlockSpec((nb, 1, 8, d), lambda i, t: (i, jnp.where(t >= nc, 1, 0), 0, 0)),
            _const_spec(w1.shape), _const_spec(wuq.shape), _const_spec(wuk.shape), _const_spec(wuv.shape),
            _const_spec(qnorm.shape), _const_spec(kvnorm.shape),
        ] + [pl.BlockSpec((tm, LANES), lambda i, t: (t, 0))] * len(rot_tables),
        out_specs=[
            tok(LRU_WIDTH), tok(LRU_WIDTH),
            pl.BlockSpec((nb, DA_WIDTH, tm), lambda i, t: (i, 0, t)), tok(DA_WIDTH), head_t(VT_ROWS),
            head_t(LANES), pl.BlockSpec((nb, MLA_HEADS, tm, LANES), lambda i, t: (i, 0, t, 0)), head_t(VT_ROWS),
        ],
        out_shape=[
            jax.ShapeDtypeStruct((b, nt, LRU_WIDTH), F32),
            jax.ShapeDtypeStruct((b, nt, LRU_WIDTH), F32),
            jax.ShapeDtypeStruct((b, DA_WIDTH, nt), BF16),
            jax.ShapeDtypeStruct((b, nt, DA_WIDTH), BF16),
            jax.ShapeDtypeStruct((b, DA_HEADS, VT_ROWS, nt), BF16),
            jax.ShapeDtypeStruct((b, MLA_HEADS, LANES, nt), BF16),
            jax.ShapeDtypeStruct((b, MLA_HEADS, nt, LANES), BF16),
            jax.ShapeDtypeStruct((b, MLA_HEADS, VT_ROWS, nt), BF16),
        ],
        compiler_params=_params(("parallel", "parallel")),
        name="in_proj",
    )(xc, xl, modt, w1, wuq, wuk, wuv, qnorm, kvnorm, *rot_tables)


def _gelu_tanh(x):
    return 0.5 * x * (1.0 + jnp.tanh(math.sqrt(2.0 / math.pi) * (x + 0.044715 * (x * x * x))))


def _lru_kernel(x_ref, g_ref, cw_ref, cb_ref, wa_ref, wi_ref, ba_ref, bi_ref, lam_ref, o_ref,
                y_s, a_s, s_s, h_s, hr_s, *, nt, nctx, chunk):
    w = LRU_WIDTH
    tiles = chunk // SUBLANES
    n_chunks = nt // chunk
    sub = lax.broadcasted_iota(jnp.int32, (tiles, SUBLANES, w), 1)
    tile_i = lax.broadcasted_iota(jnp.int32, (tiles, SUBLANES, w), 0)

    def conv_chunk(c, carry):
        r0 = pl.multiple_of(c * chunk, chunk)
        lo = pl.multiple_of(jnp.maximum(r0 - SUBLANES, 0), SUBLANES)
        hi = pl.multiple_of(jnp.minimum(r0 + chunk, nt - SUBLANES), SUBLANES)
        x3 = jnp.concatenate([x_ref[0, pl.ds(lo, SUBLANES), :], x_ref[0, pl.ds(r0, chunk), :],
                              x_ref[0, pl.ds(hi, SUBLANES), :]], axis=0).reshape(tiles + 2, SUBLANES, w)
        sh1 = pltpu.roll(x3, 1, 1)
        sh2 = pltpu.roll(x3, 2, 1)
        sh7 = pltpu.roll(x3, SUBLANES - 1, 1)
        pos = r0 + tile_i * SUBLANES + sub
        in_ctx = pos < nctx
        seg_pos = jnp.where(in_ctx, pos, pos - nctx)
        seg_last = jnp.where(in_ctx, nctx - 1, nt - nctx - 1)
        zero = jnp.zeros((tiles, SUBLANES, w), F32)
        xm2 = jnp.where(seg_pos >= 2, jnp.where(sub >= 2, sh2[1:-1], sh2[0:-2]), zero)
        xm1 = jnp.where(seg_pos >= 1, jnp.where(sub >= 1, sh1[1:-1], sh1[0:-2]), zero)
        xp1 = jnp.where(seg_pos < seg_last, jnp.where(sub < SUBLANES - 1, sh7[1:-1], sh7[2:]), zero)
        y = cb_ref[...] + xm2 * cw_ref[0:1] + xm1 * cw_ref[1:2] + x3[1:-1] * cw_ref[2:3] + xp1 * cw_ref[3:4]
        y_s[pl.ds(r0, chunk), :] = y.reshape(chunk, w)
        return carry

    lax.fori_loop(0, n_chunks, conv_chunk, 0)

    nctx_t = nctx // SUBLANES
    nt_t = nt // SUBLANES

    for d in range(2):
        nlam = -lam_ref[d:d + 1]
        softplus = jnp.maximum(nlam, 0.0) + jnp.log1p(jnp.exp(-jnp.abs(nlam)))
        c8 = -LRU_C * softplus

        def gate_chunk(c, carry, d=d, c8=c8):
            r0 = pl.multiple_of(c * chunk, chunk)
            y = y_s[pl.ds(r0, chunk), :]
            yb = y.astype(BF16)
            r = jax.nn.sigmoid(jnp.dot(yb, wa_ref[d], preferred_element_type=F32) + ba_ref[d:d + 1])
            i = jax.nn.sigmoid(jnp.dot(yb, wi_ref[d], preferred_element_type=F32) + bi_ref[d:d + 1])
            log_a = c8 * r
            a = jnp.exp(log_a)
            th = jnp.tanh(log_a)
            u = jnp.sqrt(-2.0 * th / (1.0 - th)) * (i * y)
            a3 = a.reshape(tiles, SUBLANES, w)
            u3 = u.reshape(tiles, SUBLANES, w)
            for sft in (1, 2, 4):
                if d == 0:
                    ok = sub >= sft
                    ash = pltpu.roll(a3, sft, 1)
                    ush = pltpu.roll(u3, sft, 1)
                else:
                    ok = sub < SUBLANES - sft
                    ash = pltpu.roll(a3, SUBLANES - sft, 1)
                    ush = pltpu.roll(u3, SUBLANES - sft, 1)
                u3 = jnp.where(ok, a3 * ush + u3, u3)
                a3 = jnp.where(ok, a3 * ash, a3)
            a_s[pl.ds(r0, chunk), :] = a3.reshape(chunk, w)
            s_s[pl.ds(r0, chunk), :] = u3.reshape(chunk, w)
            return carry

        lax.fori_loop(0, n_chunks, gate_chunk, 0, unroll=4)

        def carry_tile(j, hprev, d=d):
            if d == 0:
                t = j
            else:
                t = jnp.where(j < nctx_t, nctx_t - 1 - j, nt_t - 1 - (j - nctx_t))
            r0 = pl.multiple_of(t * SUBLANES, SUBLANES)
            h = a_s[pl.ds(r0, SUBLANES), :] * hprev + s_s[pl.ds(r0, SUBLANES), :]
            if d == 0:
                h_s[pl.ds(r0, SUBLANES), :] = h
                return h[SUBLANES - 1:SUBLANES]
            hr_s[pl.ds(r0, SUBLANES), :] = h
            return h[0:1]

        lax.fori_loop(0, nt_t, carry_tile, jnp.zeros((1, w), F32), unroll=4)

    def out_chunk(c, carry):
        r0 = pl.multiple_of(c * chunk, chunk)
        h = h_s[pl.ds(r0, chunk), :] + hr_s[pl.ds(r0, chunk), :]
        o_ref[0, pl.ds(r0, chunk), :] = (h * _gelu_tanh(g_ref[0, pl.ds(r0, chunk), :])).astype(BF16)
        return carry

    lax.fori_loop(0, n_chunks, out_chunk, 0)


def _lru(lx, lg, conv_w, conv_b, wa, wi, ba, bi, lam, nctx):
    b, nt, w = lx.shape
    chunk = TOKEN_TILE
    seq = pl.BlockSpec((1, nt, w), lambda i: (i, 0, 0))
    return pl.pallas_call(
        functools.partial(_lru_kernel, nt=nt, nctx=nctx, chunk=chunk),
        grid=(b,),
        in_specs=[seq, seq, _const_spec(conv_w.shape), _const_spec(conv_b.shape), _const_spec(wa.shape),
                  _const_spec(wi.shape), _const_spec(ba.shape), _const_spec(bi.shape), _const_spec(lam.shape)],
        out_specs=seq,
        out_shape=jax.ShapeDtypeStruct((b, nt, w), BF16),
        scratch_shapes=[pltpu.VMEM((nt, w), F32)] * 5,
        compiler_params=_params(("parallel",)),
        name="rglru",
    )(lx, lg, conv_w, conv_b, wa, wi, ba, bi, lam)


DA_KEY_CHUNK = 256
MLA_KEY_CHUNK = 256


def _key_chunks(nk, nctx, size):
    chunks = [(0, nctx)]
    chunks += [(s, min(size, nk - s)) for s in range(nctx, nk, size)]
    return chunks


def _attend_t(chains, chunks):
    def scores(n, ci):
        q_t, key, _ = chains[n]
        return jnp.dot(key(*chunks[ci]), q_t, preferred_element_type=F32).astype(BF16)

    s = [scores(n, 0) for n in range(len(chains))]
    state = [None] * len(chains)
    for ci in range(len(chunks)):
        for n, (_, _, value_t) in enumerate(chains):
            cm = jnp.max(s[n], axis=0, keepdims=True)
            if ci == 0:
                m_new = cm
            else:
                m_old, acc = state[n]
                m_new = jnp.maximum(m_old, cm)
            p = jnp.exp2(s[n] - m_new)
            if ci + 1 < len(chunks):
                s[n] = scores(n, ci + 1)
            pv = jnp.dot(value_t(*chunks[ci]), p, preferred_element_type=F32)
            if ci > 0:
                pv = acc * jnp.exp2(m_old.astype(F32) - m_new.astype(F32)) + pv
            state[n] = (m_new, pv)
    return [acc[0:DA_V] / acc[DA_V:DA_V + 1] for _, acc in state]


def _da_kernel(q_ref, k_ref, vt_ref, dl_ref, g_ref, li_ref, o_ref, *, nt, nctx, tq):
    half = pl.program_id(1)
    row = lax.broadcasted_iota(jnp.int32, (LANES, tq), 0)
    dl = dl_ref[...]
    lam_init = li_ref[...]
    lam = (jnp.exp(jnp.sum(dl[0:1] * dl[1:2], axis=-1, keepdims=True))
           - jnp.exp(jnp.sum(dl[2:3] * dl[3:4], axis=-1, keepdims=True)) + lam_init)
    zero = jnp.zeros((LANES, tq), BF16)

    def attend(q0, nk):
        chains = []
        for j in range(PAIRS):
            q_t = q_ref[0, LANES * j:LANES * (j + 1), pl.ds(q0, tq)]
            key = lambda start, size, j=j: k_ref[0, start:start + size, LANES * j:LANES * (j + 1)]
            value_t = lambda start, size, j=j: vt_ref[0, 2 * j + half, :, start:start + size]
            for mi in range(2):
                mine = (row & (LANES // 2 - ROT_HALF)) == (2 * half + mi) * ROT_HALF
                chains.append((jnp.where(mine, q_t, zero), key, value_t))
        o = _attend_t(chains, _key_chunks(nk, nctx, DA_KEY_CHUNK))
        heads = []
        for j in range(PAIRS):
            d = o[2 * j] - lam * o[2 * j + 1]
            heads.append(d * lax.rsqrt(jnp.mean(d * d, axis=0, keepdims=True) + RMS_EPS))
        heads.append(jnp.zeros_like(heads[0]))
        for s in range(DA_HALF_WIDTH // LANES):
            out = jnp.concatenate(heads[2 * s:2 * s + 2], axis=0).T
            o_ref[0, pl.ds(q0, tq), LANES * s:LANES * (s + 1)] = (out * g_ref[...] * (1.0 - lam_init)).astype(BF16)

    _for_query_blocks(attend, nt, nctx, tq)


def _for_query_blocks(attend, nt, nctx, tq):
    for t in range(nctx // tq):
        attend(t * tq, nctx)

    def latent_block(t, carry):
        attend(pl.multiple_of(nctx + t * tq, tq), nt)
        return carry

    lax.fori_loop(0, (nt - nctx) // tq, latent_block, 0)


def _da_attn(dq_t, dk, dv_t, dlam, gpair, lam_init, nctx):
    b, nt, _ = dk.shape
    tq = TOKEN_TILE
    return pl.pallas_call(
        functools.partial(_da_kernel, nt=nt, nctx=nctx, tq=tq),
        grid=(b, 2),
        in_specs=[
            pl.BlockSpec((1, DA_WIDTH, nt), lambda i, h: (i, 0, 0)),
            pl.BlockSpec((1, nt, DA_WIDTH), lambda i, h: (i, 0, 0)),
            pl.BlockSpec((1, DA_HEADS, VT_ROWS, nt), lambda i, h: (i, 0, 0, 0)),
            _const_spec(dlam.shape), _const_spec(gpair.shape), _const_spec(lam_init.shape),
        ],
        out_specs=pl.BlockSpec((1, nt, DA_HALF_WIDTH), lambda i, h: (i, 0, h)),
        out_shape=jax.ShapeDtypeStruct((b, nt, 2 * DA_HALF_WIDTH), BF16),
        compiler_params=_params(("parallel", "parallel")),
        name="diff_attn",
    )(dq_t, dk, dv_t, dlam, gpair, lam_init)


def _mla_kernel(q_ref, k_ref, vt_ref, o_ref, *, nt, nctx, tq):
    def attend(q0, nk):
        chains = []
        for hh in range(MLA_HEADS):
            key = lambda start, size, hh=hh: k_ref[0, hh, start:start + size, :]
            value_t = lambda start, size, hh=hh: vt_ref[0, hh, :, start:start + size]
            chains.append((q_ref[0, hh, :, pl.ds(q0, tq)], key, value_t))
        o = _attend_t(chains, _key_chunks(nk, nctx, MLA_KEY_CHUNK))
        for j in range(MLA_HEADS // 2):
            o_ref[0, pl.ds(q0, tq), LANES * j:LANES * (j + 1)] = jnp.concatenate(o[2 * j:2 * j + 2], axis=0).T.astype(BF16)

    _for_query_blocks(attend, nt, nctx, tq)


def _mla_attn(mq_t, mk, mv_t, nctx):
    b, nh, nt, _ = mk.shape
    tq = TOKEN_TILE
    return pl.pallas_call(
        functools.partial(_mla_kernel, nt=nt, nctx=nctx, tq=tq),
        grid=(b,),
        in_specs=[
            pl.BlockSpec((1, nh, LANES, nt), lambda i: (i, 0, 0, 0)),
            pl.BlockSpec((1, nh, nt, LANES), lambda i: (i, 0, 0, 0)),
            pl.BlockSpec((1, nh, VT_ROWS, nt), lambda i: (i, 0, 0, 0)),
        ],
        out_specs=pl.BlockSpec((1, nt, nh * MLA_V), lambda i: (i, 0, 0)),
        out_shape=jax.ShapeDtypeStruct((b, nt, nh * MLA_V), BF16),
        compiler_params=_params(("parallel",)),
        name="mla_attn",
    )(mq_t, mk, mv_t)


def _layer_norm(z, g, b):
    mu = jnp.mean(z, axis=-1, keepdims=True)
    zc = z - mu
    var = jnp.mean(zc * zc, axis=-1, keepdims=True)
    return (zc * lax.rsqrt(var + LN_EPS)) * g + b


def _router_gates(logits, rb):
    scores = jax.nn.sigmoid(logits)
    sel = scores + rb
    lane = lax.broadcasted_iota(jnp.int32, logits.shape, 1)
    r = lane & (EXPERTS_PER_GROUP - 1)
    grp = (lane >> 2) & (N_GROUPS - 1)

    def in_group(x, k):
        return jnp.where(r >= k, pltpu.roll(x, k, 1), pltpu.roll(x, LANES - EXPERTS_PER_GROUP + k, 1))

    others = [in_group(sel, k) for k in (1, 2, 3)]
    pair_max = sel + jnp.maximum(jnp.maximum(others[0], others[1]), others[2])
    grp_score = jnp.maximum(jnp.maximum(pair_max, in_group(pair_max, 1)),
                            jnp.maximum(in_group(pair_max, 2), in_group(pair_max, 3)))
    in_best = None
    for k in (1, 2, 3):
        other = pltpu.roll(grp_score, EXPERTS_PER_GROUP * k, 1)
        wins = (grp_score > other) | ((grp_score == other) & (grp < k))
        in_best = wins if in_best is None else (in_best & wins)
    beaten = jnp.zeros(logits.shape, F32)
    for k, o in zip((1, 2, 3), others):
        beats = (o > sel) | ((o == sel) & (r >= k))
        beaten = beaten + jnp.where(beats, 1.0, 0.0)
    chosen = in_best & (beaten < 2.0)
    sc = jnp.where(chosen, scores, 0.0)
    tot = sc + in_group(sc, 1) + in_group(sc, 2) + in_group(sc, 3)
    return jnp.where(chosen, sc / tot, 0.0), in_best


ROW_BLOCK = 160
BF16_ROWS = 16


def _grouped_experts(v, gates, in_best, w1_ref, w3_ref, w2_ref, xs_ref, gs_ref, ys_ref):
    rows, d = v.shape
    per = EXPERTS_PER_GROUP * D_EXPERT
    lane = lax.broadcasted_iota(jnp.int32, (rows, LANES), 1)
    gsel = jnp.where(in_best & ((lane & (EXPERTS_PER_GROUP - 1)) == 0) & (lane < N_EXPERTS), 1.0, 0.0)
    gsel_b = gsel.astype(BF16)
    ri = lax.broadcasted_iota(jnp.int32, (rows, rows), 0)
    ci = lax.broadcasted_iota(jnp.int32, (rows, rows), 1)
    onehot = lambda cond: jnp.where(cond, 1.0, 0.0).astype(BF16)

    tot = jnp.sum(gsel, axis=0, keepdims=True)
    lane1 = lax.broadcasted_iota(jnp.int32, (1, LANES), 1)
    cnt = [jnp.sum(jnp.where(lane1 == EXPERTS_PER_GROUP * g, tot, 0.0)).astype(jnp.int32) for g in range(N_GROUPS)]
    off = [jnp.int32(0)]
    for g in range(1, N_GROUPS):
        off.append(off[-1] + cnt[g - 1])

    off_lane = sum(jnp.where(lane1 == EXPERTS_PER_GROUP * g, off[g].astype(F32), 0.0) for g in range(N_GROUPS))
    before = jnp.dot(onehot(ci < ri), gsel_b, preferred_element_type=F32)
    pos_col = jnp.sum(gsel * (before + off_lane), axis=1, keepdims=True)
    sub8 = lax.broadcasted_iota(jnp.int32, (SUBLANES, LANES), 0)
    lane8 = lax.broadcasted_iota(jnp.int32, (SUBLANES, LANES), 1)
    pick = jnp.where((lane8 == EXPERTS_PER_GROUP * sub8) & (sub8 < N_GROUPS), 1.0, 0.0).astype(BF16)
    gsel_t = lax.dot_general(pick, gsel_b, (((1,), (1,)), ((), ())), preferred_element_type=F32)
    before_t = jnp.dot(gsel_t.astype(BF16), onehot(ri < ci), preferred_element_type=F32)
    sub_col = lax.broadcasted_iota(jnp.int32, (SUBLANES, 1), 0)
    off_sub = sum(jnp.where(sub_col == g, off[g].astype(F32), 0.0) for g in range(N_GROUPS))
    pos_row = jnp.sum(gsel_t * (before_t + off_sub), axis=0, keepdims=True)
    perm = onehot(pos_row == ri.astype(F32))
    perm_t = onehot(pos_col == ci.astype(F32))

    xs_ref[...] = jnp.dot(perm, v, preferred_element_type=F32).astype(BF16)
    g_hi = gates.astype(BF16)
    g_lo = (gates - g_hi.astype(F32)).astype(BF16)
    gs_ref[...] = jnp.dot(perm, g_hi, preferred_element_type=F32) + jnp.dot(perm, g_lo, preferred_element_type=F32)
    ys_ref[...] = jnp.zeros(ys_ref.shape, F32)

    row_in_block = lax.broadcasted_iota(jnp.int32, (ROW_BLOCK, LANES), 0)
    for g in range(N_GROUPS):
        first = (off[g] // BF16_ROWS) * BF16_ROWS
        end = off[g] + cnt[g]
        for k in range(-(-rows // ROW_BLOCK)):
            lo = first + k * ROW_BLOCK

            @pl.when((lo < end) & (cnt[g] > 0))
            def _(lo=lo, g=g):
                st = pl.multiple_of(jnp.minimum(lo, rows - ROW_BLOCK), BF16_ROWS)
                xb = xs_ref[pl.ds(st, ROW_BLOCK), :]
                gb = jnp.where(row_in_block + st >= lo, gs_ref[pl.ds(st, ROW_BLOCK), :], 0.0)
                h1 = jnp.dot(xb, w1_ref[:, g * per:(g + 1) * per], preferred_element_type=F32)
                h3 = jnp.dot(xb, w3_ref[:, g * per:(g + 1) * per], preferred_element_type=F32)
                hh = (h1 * jax.nn.sigmoid(h1)) * h3
                parts = []
                for j in range(EXPERTS_PER_GROUP):
                    e = g * EXPERTS_PER_GROUP + j
                    parts.append((hh[:, j * D_EXPERT:(j + 1) * D_EXPERT] * gb[:, e:e + 1]).astype(BF16))
                ys_ref[pl.ds(st, ROW_BLOCK), :] += jnp.dot(jnp.concatenate(parts, axis=-1),
                                                           w2_ref[g * per:(g + 1) * per, :],
                                                           preferred_element_type=F32)

    return jnp.dot(perm_t, ys_ref[...].astype(BF16), preferred_element_type=F32)


def _post_kernel(xc_ref, xl_ref, mod_ref, lru_ref, da_ref, mla_ref, wo_ref, g1_ref, b1_ref, rw_ref, rb_ref, w1_ref,
                 w3_ref, w2_ref, g_ref, b_ref, o_ref, xs_ref, gs_ref, ys_ref, *, alpha, n_ctx_tiles):
    nb, tm, d = xl_ref.shape
    rows = nb * tm
    x = jnp.where(pl.program_id(1) < n_ctx_tiles, xc_ref[...], xl_ref[...]).reshape(rows, d)
    mod = mod_ref[:, 0]
    per_row = lambda k: jnp.broadcast_to(mod[:, k:k + 1], (nb, tm, d)).reshape(rows, d)
    a = jnp.concatenate([lru_ref[...], da_ref[...], mla_ref[...]], axis=-1).reshape(rows, -1)
    o = jnp.dot(a, wo_ref[...], preferred_element_type=F32)
    x1 = _layer_norm(alpha * x + per_row(2) * o, g1_ref[...], b1_ref[...])
    v = (x1 * (1.0 + per_row(4)) + per_row(3)).astype(BF16)
    gates, in_best = _router_gates(jnp.dot(v, rw_ref[...], preferred_element_type=F32), rb_ref[...])
    f = _grouped_experts(v, gates, in_best, w1_ref, w3_ref, w2_ref, xs_ref, gs_ref, ys_ref)
    o_ref[...] = _layer_norm(alpha * x1 + per_row(5) * f, g_ref[...], b_ref[...]).reshape(nb, tm, d)


POST_BATCH = 2


def _post(xc, xl, lat_shift, modt, lru_o, da_o, mla_o, wo, g1, b1, rw, rb, w1c, w3c, w2c, g2, b2, nctx, alpha,
          latent_only):
    b, nt, _ = lru_o.shape
    d = xl.shape[-1]
    tm = TOKEN_TILE
    nc = nctx // tm
    skip = nc if latent_only else 0
    nb = POST_BATCH if b % POST_BATCH == 0 else 1
    tok = lambda w: pl.BlockSpec((nb, tm, w), lambda i, t: (i, t + skip, 0))
    consts = (wo, g1, b1, rw, rb, w1c, w3c, w2c, g2, b2)
    return pl.pallas_call(
        functools.partial(_post_kernel, alpha=alpha, n_ctx_tiles=nc - skip),
        grid=(b // nb, nt // tm - skip),
        in_specs=_stream_specs(nb, tm, d, nc, lat_shift, skip) + [
            pl.BlockSpec((nb, 1, 8, d), lambda i, t: (i, jnp.where(t + skip >= nc, 1, 0), 0, 0)),
            tok(LRU_WIDTH), tok(da_o.shape[-1]), tok(mla_o.shape[-1]),
        ] + [_const_spec(c.shape) for c in consts],
        out_specs=pl.BlockSpec((nb, tm, d), lambda i, t: (i, t, 0)),
        out_shape=jax.ShapeDtypeStruct((b, nt - skip * tm, d), F32),
        scratch_shapes=[pltpu.VMEM((nb * tm, d), BF16), pltpu.VMEM((nb * tm, LANES), F32),
                        pltpu.VMEM((nb * tm, d), F32)],
        compiler_params=_params(("parallel", "parallel")),
        name="post",
    )(xc, xl, modt, lru_o, da_o, mla_o, *consts)


def _rotary_tables(n, nctx):
    rows = n // GRID_W
    row = jnp.repeat(jnp.arange(rows), GRID_W).astype(F32)
    col = jnp.tile(jnp.arange(GRID_W), rows).astype(F32)
    n_freq = DA_QK // 4
    inv = ROPE_THETA ** (-jnp.arange(n_freq, dtype=F32) / n_freq)
    ang = jnp.concatenate([row[:, None] * inv, col[:, None] * inv], axis=-1)
    ang = jnp.concatenate([jnp.zeros((nctx, DA_QK // 2), F32), ang], axis=0)
    c, s = jnp.cos(ang), jnp.sin(ang)
    cos_d = jnp.tile(c, (1, LANES // ROT_HALF))
    sin_d = jnp.concatenate([jnp.tile(-s, (1, 4)), jnp.tile(s, (1, 4))], axis=-1)
    one = jnp.ones_like(c)
    zero = jnp.zeros_like(c)
    cos_m = jnp.concatenate([one, one, c, one, one, one, c, one], axis=-1)
    sin_m = jnp.concatenate([zero, zero, -s, zero, zero, zero, s, zero], axis=-1)
    return cos_d, sin_d, cos_m, sin_m


def _rot_lanes(w, groups):
    lead = w.shape[:-1]
    w = w.reshape(lead + (groups, 2, ROT_HALF))
    return jnp.swapaxes(w, -3, -2).reshape(lead + (groups * 2 * ROT_HALF,))


def _mla_lanes(nope, rot):
    z = jnp.zeros(nope.shape[:-1] + (ROT_HALF,), nope.dtype)
    return jnp.concatenate([nope[..., :MLA_NOPE // 2], rot[..., :ROT_HALF], z,
                            nope[..., MLA_NOPE // 2:], rot[..., ROT_HALF:], z], axis=-1)


def _pack_in_weight(w_in):
    d = w_in.shape[0]
    n_lru = 2 * LRU_WIDTH
    n_lru_da = n_lru + 3 * DA_WIDTH
    n_rank = Q_RANK + KV_RANK
    wkr = w_in[:, n_lru_da + n_rank:n_lru_da + n_rank + MLA_ROPE]
    krp = _mla_lanes(jnp.zeros((d, MLA_NOPE), F32), wkr)
    wqk = w_in[:, n_lru:n_lru + 2 * DA_WIDTH].reshape(d, 2 * PAIRS, LANES)
    wqk = _rot_lanes(wqk, LANES // DA_QK).reshape(d, 2 * DA_WIDTH)
    return jnp.concatenate([w_in[:, n_lru_da:n_lru_da + n_rank], krp, wqk, w_in[:, n_lru + 2 * DA_WIDTH:n_lru_da],
                            w_in[:, :n_lru]], axis=-1).astype(BF16)


def _pack_uq(w_uq):
    r = w_uq.shape[0]
    w = w_uq.reshape(r, MLA_HEADS, MLA_NOPE + MLA_ROPE)
    return _mla_lanes(w[..., :MLA_NOPE], w[..., MLA_NOPE:]).reshape(r, MLA_HEADS * LANES).astype(BF16)


def _pack_ukv(w_ukv):
    r = w_ukv.shape[0]
    w = w_ukv.reshape(r, MLA_HEADS, MLA_NOPE + MLA_V)
    wk = _mla_lanes(w[..., :MLA_NOPE], jnp.zeros((r, MLA_HEADS, MLA_ROPE), F32)).reshape(r, MLA_HEADS * LANES)
    wv = w[..., MLA_NOPE:].reshape(r, MLA_HEADS * MLA_V)
    return wk.astype(BF16), wv.astype(BF16)


def _pack_out_weight(w_out):
    d = w_out.shape[1]
    w_da = w_out[LRU_WIDTH:LRU_WIDTH + DA_WIDTH].reshape(PAIRS, 2, DA_V, d)
    pad = jnp.zeros((DA_HALF_WIDTH - PAIRS * DA_V, d), w_out.dtype)
    halves = [jnp.concatenate([w_da[:, h].reshape(PAIRS * DA_V, d), pad], axis=0) for h in range(2)]
    return jnp.concatenate([w_out[:LRU_WIDTH]] + halves + [w_out[LRU_WIDTH + DA_WIDTH:]], axis=0).astype(BF16)


def _block_diag(w):
    nd, nb, bs, _ = w.shape
    eye = jnp.eye(nb, dtype=w.dtype)
    return jnp.einsum('dhij,hg->dhigj', w, eye).reshape(nd, nb * bs, nb * bs)


def kernel(x, c, ctx, c_ctx, w_mod, b_mod, w_in, w_out, conv_w, conv_b, lru_wa, lru_ba, lru_wi, lru_bi, lru_lambda, diff_lambda, diff_norm, mla_q_norm, mla_kv_norm, mla_w_uq, mla_w_ukv, ln1_g, ln1_b, ln2_g, ln2_b, router_w, router_b, exp_w1, exp_w3, exp_w2):
    bsz, n, d = x.shape
    nctx = ctx.shape[1]
    depth = w_mod.shape[0]
    alpha = (2 * depth) ** 0.25
    assert nctx % TOKEN_TILE == 0 and n % TOKEN_TILE == 0 and n % GRID_W == 0

    rows = -(-(bsz + 1) // SUBLANES) * SUBLANES
    cc = jnp.concatenate([c, c_ctx[None, :], jnp.zeros((rows - bsz - 1, d), F32)], axis=0)
    mod = _modulation(cc, w_mod, b_mod).reshape(depth, rows, 6, d)
    mod = jnp.pad(mod, ((0, 0), (0, 0), (0, 2), (0, 0)))
    mod_ctx = jnp.broadcast_to(mod[:, bsz][:, None], (depth, bsz, 8, d))
    modt = jnp.stack([mod_ctx, mod[:, :bsz]], axis=2)

    rot_tables = _rotary_tables(n, nctx)
    rw = jnp.tile(router_w, (1, LANES // N_EXPERTS)).astype(BF16)
    rb = jnp.tile(router_b, LANES // N_EXPERTS)[None, :].astype(F32)
    gpair = jnp.tile(diff_norm, (1, LANES // DA_V))

    xc, xl, lat_shift = ctx, x, nctx // TOKEN_TILE
    for l in range(depth):
        lam_init = jnp.full((1, 1), 0.8 - 0.6 * math.exp(-0.3 * l), F32)
        wuk, wuv = _pack_ukv(mla_w_ukv[l])
        lx, lg, dq_t, dk, dv_t, mq_t, mk, mv_t = _in_proj(
            xc, xl, lat_shift, nctx + n, modt[l], _pack_in_weight(w_in[l]), _pack_uq(mla_w_uq[l]), wuk, wuv,
            mla_q_norm[l][None, :], mla_kv_norm[l][None, :], rot_tables, nctx)
        lru_o = _lru(lx, lg, conv_w[l], conv_b[l][None, :], _block_diag(lru_wa[l]).astype(BF16),
                     _block_diag(lru_wi[l]).astype(BF16), lru_ba[l], lru_bi[l], lru_lambda[l], nctx)
        da_o = _da_attn(dq_t, dk, dv_t, diff_lambda[l], gpair[l][None, :], lam_init, nctx)
        mla_o = _mla_attn(mq_t, mk, mv_t, nctx)
        w1c = exp_w1[l].transpose(1, 0, 2).reshape(d, N_EXPERTS * D_EXPERT).astype(BF16)
        w3c = exp_w3[l].transpose(1, 0, 2).reshape(d, N_EXPERTS * D_EXPERT).astype(BF16)
        w2c = exp_w2[l].reshape(N_EXPERTS * D_EXPERT, d).astype(BF16)
        xa = _post(xc, xl, lat_shift, modt[l], lru_o, da_o, mla_o, _pack_out_weight(w_out[l]), ln1_g[l][None, :],
                   ln1_b[l][None, :], rw, rb, w1c, w3c, w2c, ln2_g[l][None, :], ln2_b[l][None, :], nctx, alpha,
                   latent_only=(l == depth - 1))
        xc, xl, lat_shift = xa, xa, 0
    return xa
```

```python
import functools
import math

import jax
import jax.numpy as jnp
from jax import lax
from jax.experimental import pallas as pl
from jax.experimental.pallas import tpu as pltpu

F32 = jnp.float32
BF16 = jnp.bfloat16

GRID_W = 64
LRU_WIDTH = 256
LRU_BLOCKS = 4
CONV_W = 4
LRU_C = 8.0
DA_HEADS = 6
DA_QK = 32
DA_V = 2 * DA_QK
MLA_HEADS = 6
MLA_NOPE = 64
MLA_ROPE = 32
MLA_V = 64
Q_RANK = 256
KV_RANK = 128
MLA_SCALE = (MLA_NOPE + MLA_ROPE) ** -0.5
N_EXPERTS = 16
N_GROUPS = 4
EXPERTS_PER_GROUP = N_EXPERTS // N_GROUPS
D_EXPERT = 256
ROPE_THETA = 10000.0
LN_EPS = 1e-5
RMS_EPS = 1e-6

LANES = 128
SUBLANES = 8
TOKEN_TILE = 256
VMEM_LIMIT = 56 * 1024 * 1024

LOG2E = math.log2(math.e)
DA_QSCALE = DA_QK ** -0.5 * LOG2E
MLA_QSCALE = MLA_SCALE * LOG2E

DA_WIDTH = DA_HEADS * DA_V
C_CQ = 0
C_CKV = C_CQ + Q_RANK
C_KR = C_CKV + KV_RANK
C_DAQ = C_KR + LANES
C_DAK = C_DAQ + DA_WIDTH
C_DAV = C_DAK + DA_WIDTH
C_LRU = C_DAV + DA_WIDTH
C_END = C_LRU + 2 * LRU_WIDTH
PAIRS = DA_HEADS // 2
ROT_HALF = DA_QK // 2
DA_HALF_WIDTH = 2 * LANES
VT_ROWS = DA_V + 16


def _params(sem):
    return pltpu.CompilerParams(dimension_semantics=sem, vmem_limit_bytes=VMEM_LIMIT)


def _const_spec(shape):
    nd = len(shape)
    return pl.BlockSpec(shape, lambda *_: (0,) * nd, pipeline_mode=pl.Buffered(1))


def _mod_kernel(c_ref, w_ref, b_ref, o_ref):
    c = c_ref[...]
    s = c * jax.nn.sigmoid(c)
    o_ref[0] = jnp.dot(s.astype(BF16), w_ref[0].astype(BF16), preferred_element_type=F32) + b_ref[0]


def _modulation(cc, w_mod, b_mod):
    depth, d, d6 = w_mod.shape
    r = cc.shape[0]
    tn = min(d6, 1536)
    return pl.pallas_call(
        _mod_kernel,
        grid=(depth, d6 // tn),
        in_specs=[
            pl.BlockSpec((r, d), lambda l, j: (0, 0)),
            pl.BlockSpec((1, d, tn), lambda l, j: (l, 0, j)),
            pl.BlockSpec((1, 1, tn), lambda l, j: (l, 0, j)),
        ],
        out_specs=pl.BlockSpec((1, r, tn), lambda l, j: (l, 0, j)),
        out_shape=jax.ShapeDtypeStruct((depth, r, d6), F32),
        compiler_params=_params(("parallel", "parallel")),
        name="modulation",
    )(cc, w_mod, b_mod.reshape(depth, 1, d6))


def _rotate(t, cosf, sinf):
    return t * cosf + pltpu.roll(t, LANES // 2, 1) * sinf


def _store_values_t(vt_ref, bi, v):
    rows = v.shape[0]
    ones = jnp.ones((VT_ROWS - DA_V, rows), BF16)
    for j in range(PAIRS):
        t = v[:, LANES * j:LANES * (j + 1)].T.astype(BF16)
        for k in range(2):
            vt_ref[bi, 2 * j + k, 0:DA_V, :] = t[DA_V * k:DA_V * (k + 1)]
            vt_ref[bi, 2 * j + k, DA_V:VT_ROWS, :] = ones


def _in_kernel(xc_ref, xl_ref, mod_ref, w1_ref, wuq_ref, wuk_ref, wuv_ref, qn_ref, kvn_ref, cd_ref, sd_ref, cm_ref,
               sm_ref, lx_ref, lg_ref, dq_ref, dk_ref, dv_ref, mq_ref, mk_ref, mv_ref, *, n_ctx_tiles):
    nb, tm, d = xl_ref.shape
    x = jnp.where(pl.program_id(1) < n_ctx_tiles, xc_ref[...], xl_ref[...])
    mod = mod_ref[:, 0]
    u = (x * (1.0 + mod[:, 1:2]) + mod[:, 0:1]).astype(BF16).reshape(nb * tm, d)
    proj = lambda lo, hi: jnp.dot(u, w1_ref[:, lo:hi], preferred_element_type=F32)
    rot = functools.partial(_rotate, cosf=cd_ref[...], sinf=sd_ref[...])
    rot_mla = functools.partial(_rotate, cosf=cm_ref[...], sinf=sm_ref[...])

    y_mla = proj(C_CQ, C_DAQ)
    y_da = proj(C_DAQ, C_DAV)

    cq = y_mla[:, C_CQ:C_CKV]
    ckv = y_mla[:, C_CKV:C_KR]
    krp = y_mla[:, C_KR:C_DAQ]
    qn = (cq * lax.rsqrt(jnp.mean(cq * cq, axis=-1, keepdims=True) + RMS_EPS)) * qn_ref[...]
    kvn = ((ckv * lax.rsqrt(jnp.mean(ckv * ckv, axis=-1, keepdims=True) + RMS_EPS)) * kvn_ref[...]).astype(BF16)
    q = jnp.dot(qn.astype(BF16), wuq_ref[...], preferred_element_type=F32)
    kn = jnp.dot(kvn, wuk_ref[...], preferred_element_type=F32)

    for bi in range(nb):
        rs = slice(bi * tm, (bi + 1) * tm)
        for j in range(PAIRS):
            t = y_da[rs, LANES * j:LANES * (j + 1)]
            dq_ref[bi, LANES * j:LANES * (j + 1), :] = (rot(t) * DA_QSCALE).T.astype(BF16)
            t = y_da[rs, DA_WIDTH + LANES * j:DA_WIDTH + LANES * (j + 1)]
            dk_ref[bi, :, LANES * j:LANES * (j + 1)] = rot(t).astype(BF16)

    y_rest = proj(C_DAV, C_END)
    mv = jnp.dot(kvn, wuv_ref[...], preferred_element_type=F32)

    for bi in range(nb):
        rs = slice(bi * tm, (bi + 1) * tm)
        kr = rot_mla(krp[rs])
        for h in range(MLA_HEADS):
            t = q[rs, LANES * h:LANES * (h + 1)]
            mq_ref[bi, h] = (rot_mla(t) * MLA_QSCALE).T.astype(BF16)
            mk_ref[bi, h] = (kn[rs, LANES * h:LANES * (h + 1)] + kr).astype(BF16)
        _store_values_t(dv_ref, bi, y_rest[rs, 0:DA_WIDTH])
        _store_values_t(mv_ref, bi, mv[rs])
    lx_ref[...] = y_rest[:, DA_WIDTH:DA_WIDTH + LRU_WIDTH].reshape(nb, tm, LRU_WIDTH)
    lg_ref[...] = y_rest[:, DA_WIDTH + LRU_WIDTH:DA_WIDTH + 2 * LRU_WIDTH].reshape(nb, tm, LRU_WIDTH)


IN_BATCH = 4


def _stream_specs(rows, tm, d, nc, lat_shift, skip=0):
    return [pl.BlockSpec((rows, tm, d), lambda i, t: (i, jnp.minimum(t + skip, nc - 1), 0)),
            pl.BlockSpec((rows, tm, d), lambda i, t: (i, jnp.maximum(t + skip - lat_shift, 0), 0))]


def _in_proj(xc, xl, lat_shift, nt, modt, w1, wuq, wuk, wuv, qnorm, kvnorm, rot_tables, nctx):
    b, _, d = xl.shape
    tm = TOKEN_TILE
    nc = nctx // tm
    nb = IN_BATCH if b % IN_BATCH == 0 else 1
    tok = lambda w: pl.BlockSpec((nb, tm, w), lambda i, t: (i, t, 0))
    head_t = lambda r: pl.BlockSpec((nb, DA_HEADS, r, tm), lambda i, t: (i, 0, 0, t))
    return pl.pallas_call(
        functools.partial(_in_kernel, n_ctx_tiles=nc),
        grid=(b // nb, nt // tm),
        in_specs=_stream_specs(nb, tm, d, nc, lat_shift) + [
            pl.BlockSpec((nb, 1, 8, d), lambda i, t: (i, jnp.where(t >= nc, 1, 0), 0, 0)),
            _const_spec(w1.shape), _const_spec(wuq.shape), _const_spec(wuk.shape), _const_spec(wuv.shape),
            _const_spec(qnorm.shape), _const_spec(kvnorm.shape),
        ] + [pl.BlockSpec((tm, LANES), lambda i, t: (t, 0))] * len(rot_tables),
        out_specs=[
            tok(LRU_WIDTH), tok(LRU_WIDTH),
            pl.BlockSpec((nb, DA_WIDTH, tm), lambda i, t: (i, 0, t)), tok(DA_WIDTH), head_t(VT_ROWS),
            head_t(LANES), pl.BlockSpec((nb, MLA_HEADS, tm, LANES), lambda i, t: (i, 0, t, 0)), head_t(VT_ROWS),
        ],
        out_shape=[
            jax.ShapeDtypeStruct((b, nt, LRU_WIDTH), F32),
            jax.ShapeDtypeStruct((b, nt, LRU_WIDTH), F32),
            jax.ShapeDtypeStruct((b, DA_WIDTH, nt), BF16),
            jax.ShapeDtypeStruct((b, nt, DA_WIDTH), BF16),
            jax.ShapeDtypeStruct((b, DA_HEADS, VT_ROWS, nt), BF16),
            jax.ShapeDtypeStruct((b, MLA_HEADS, LANES, nt), BF16),
            jax.ShapeDtypeStruct((b, MLA_HEADS, nt, LANES), BF16),
            jax.ShapeDtypeStruct((b, MLA_HEADS, VT_ROWS, nt), BF16),
        ],
        compiler_params=_params(("parallel", "parallel")),
        name="in_proj",
    )(xc, xl, modt, w1, wuq, wuk, wuv, qnorm, kvnorm, *rot_tables)


def _gelu_tanh(x):
    return 0.5 * x * (1.0 + jnp.tanh(math.sqrt(2.0 / math.pi) * (x + 0.044715 * (x * x * x))))


def _lru_kernel(x_ref, g_ref, cw_ref, cb_ref, wa_ref, wi_ref, ba_ref, bi_ref, lam_ref, o_ref,
                y_s, a_s, s_s, h_s, *, nt, nctx, chunk):
    w = LRU_WIDTH
    tiles = chunk // SUBLANES
    n_chunks = nt // chunk
    sub = lax.broadcasted_iota(jnp.int32, (tiles, SUBLANES, w), 1)
    tile_i = lax.broadcasted_iota(jnp.int32, (tiles, SUBLANES, w), 0)

    def conv_chunk(c, carry):
        r0 = pl.multiple_of(c * chunk, chunk)
        lo = pl.multiple_of(jnp.maximum(r0 - SUBLANES, 0), SUBLANES)
        hi = pl.multiple_of(jnp.minimum(r0 + chunk, nt - SUBLANES), SUBLANES)
        x3 = jnp.concatenate([x_ref[0, pl.ds(lo, SUBLANES), :], x_ref[0, pl.ds(r0, chunk), :],
                              x_ref[0, pl.ds(hi, SUBLANES), :]], axis=0).reshape(tiles + 2, SUBLANES, w)
        sh1 = pltpu.roll(x3, 1, 1)
        sh2 = pltpu.roll(x3, 2, 1)
        sh7 = pltpu.roll(x3, SUBLANES - 1, 1)
        pos = r0 + tile_i * SUBLANES + sub
        in_ctx = pos < nctx
        seg_pos = jnp.where(in_ctx, pos, pos - nctx)
        seg_last = jnp.where(in_ctx, nctx - 1, nt - nctx - 1)
        zero = jnp.zeros((tiles, SUBLANES, w), F32)
        xm2 = jnp.where(seg_pos >= 2, jnp.where(sub >= 2, sh2[1:-1], sh2[0:-2]), zero)
        xm1 = jnp.where(seg_pos >= 1, jnp.where(sub >= 1, sh1[1:-1], sh1[0:-2]), zero)
        xp1 = jnp.where(seg_pos < seg_last, jnp.where(sub < SUBLANES - 1, sh7[1:-1], sh7[2:]), zero)
        y = cb_ref[...] + xm2 * cw_ref[0:1] + xm1 * cw_ref[1:2] + x3[1:-1] * cw_ref[2:3] + xp1 * cw_ref[3:4]
        y_s[pl.ds(r0, chunk), :] = y.reshape(chunk, w)
        return carry

    lax.fori_loop(0, n_chunks, conv_chunk, 0)

    nctx_t = nctx // SUBLANES
    nt_t = nt // SUBLANES

    for d in range(2):
        nlam = -lam_ref[d:d + 1]
        softplus = jnp.maximum(nlam, 0.0) + jnp.log1p(jnp.exp(-jnp.abs(nlam)))
        c8 = -LRU_C * softplus

        def gate_chunk(c, carry, d=d, c8=c8):
            r0 = pl.multiple_of(c * chunk, chunk)
            y = y_s[pl.ds(r0, chunk), :]
            yb = y.astype(BF16)
            r = jax.nn.sigmoid(jnp.dot(yb, wa_ref[d], preferred_element_type=F32) + ba_ref[d:d + 1])
            i = jax.nn.sigmoid(jnp.dot(yb, wi_ref[d], preferred_element_type=F32) + bi_ref[d:d + 1])
            log_a = c8 * r
            a = jnp.exp(log_a)
            th = jnp.tanh(log_a)
            u = jnp.sqrt(-2.0 * th / (1.0 - th)) * (i * y)
            a3 = a.reshape(tiles, SUBLANES, w)
            u3 = u.reshape(tiles, SUBLANES, w)
            for sft in (1, 2, 4):
                if d == 0:
                    ok = sub >= sft
                    ash = pltpu.roll(a3, sft, 1)
                    ush = pltpu.roll(u3, sft, 1)
                else:
                    ok = sub < SUBLANES - sft
                    ash = pltpu.roll(a3, SUBLANES - sft, 1)
                    ush = pltpu.roll(u3, SUBLANES - sft, 1)
                u3 = jnp.where(ok, a3 * ush + u3, u3)
                a3 = jnp.where(ok, a3 * ash, a3)
            a_s[pl.ds(r0, chunk), :] = a3.reshape(chunk, w)
            s_s[pl.ds(r0, chunk), :] = u3.reshape(chunk, w)
            return carry

        lax.fori_loop(0, n_chunks, gate_chunk, 0, unroll=4)

        def carry_tile(j, hprev, d=d):
            if d == 0:
                t = j
            else:
                t = jnp.where(j < nctx_t, nctx_t - 1 - j, nt_t - 1 - (j - nctx_t))
            r0 = pl.multiple_of(t * SUBLANES, SUBLANES)
            h = a_s[pl.ds(r0, SUBLANES), :] * hprev + s_s[pl.ds(r0, SUBLANES), :]
            if d == 0:
                h_s[pl.ds(r0, SUBLANES), :] = h
                return h[SUBLANES - 1:SUBLANES]
            h_s[pl.ds(r0, SUBLANES), :] = h_s[pl.ds(r0, SUBLANES), :] + h
            return h[0:1]

        lax.fori_loop(0, nt_t, carry_tile, jnp.zeros((1, w), F32), unroll=4)

    def out_chunk(c, carry):
        r0 = pl.multiple_of(c * chunk, chunk)
        o_ref[0, pl.ds(r0, chunk), :] = (h_s[pl.ds(r0, chunk), :] * _gelu_tanh(g_ref[0, pl.ds(r0, chunk), :])).astype(BF16)
        return carry

    lax.fori_loop(0, n_chunks, out_chunk, 0)


def _lru(lx, lg, conv_w, conv_b, wa, wi, ba, bi, lam, nctx):
    b, nt, w = lx.shape
    chunk = TOKEN_TILE
    seq = pl.BlockSpec((1, nt, w), lambda i: (i, 0, 0))
    return pl.pallas_call(
        functools.partial(_lru_kernel, nt=nt, nctx=nctx, chunk=chunk),
        grid=(b,),
        in_specs=[seq, seq, _const_spec(conv_w.shape), _const_spec(conv_b.shape), _const_spec(wa.shape),
                  _const_spec(wi.shape), _const_spec(ba.shape), _const_spec(bi.shape), _const_spec(lam.shape)],
        out_specs=seq,
        out_shape=jax.ShapeDtypeStruct((b, nt, w), BF16),
        scratch_shapes=[pltpu.VMEM((nt, w), F32)] * 4,
        compiler_params=_params(("parallel",)),
        name="rglru",
    )(lx, lg, conv_w, conv_b, wa, wi, ba, bi, lam)


DA_KEY_CHUNK = 256
MLA_KEY_CHUNK = 256


def _key_chunks(nk, nctx, size):
    chunks = [(0, nctx)]
    chunks += [(s, min(size, nk - s)) for s in range(nctx, nk, size)]
    return chunks


def _attend_t(chains, chunks):
    def scores(n, ci):
        q_t, key, _ = chains[n]
        return jnp.dot(key(*chunks[ci]), q_t, preferred_element_type=F32).astype(BF16)

    s = [scores(n, 0) for n in range(len(chains))]
    state = [None] * len(chains)
    for ci in range(len(chunks)):
        for n, (_, _, value_t) in enumerate(chains):
            cm = jnp.max(s[n], axis=0, keepdims=True)
            if ci == 0:
                m_new = cm
            else:
                m_old, acc = state[n]
                m_new = jnp.maximum(m_old, cm)
            p = jnp.exp2(s[n] - m_new)
            if ci + 1 < len(chunks):
                s[n] = scores(n, ci + 1)
            pv = jnp.dot(value_t(*chunks[ci]), p, preferred_element_type=F32)
            if ci > 0:
                pv = acc * jnp.exp2(m_old.astype(F32) - m_new.astype(F32)) + pv
            state[n] = (m_new, pv)
    return [acc[0:DA_V] / acc[DA_V:DA_V + 1] for _, acc in state]


def _da_kernel(q_ref, k_ref, vt_ref, dl_ref, g_ref, li_ref, o_ref, *, nt, nctx, tq):
    half = pl.program_id(1)
    row = lax.broadcasted_iota(jnp.int32, (LANES, tq), 0)
    dl = dl_ref[...]
    lam_init = li_ref[...]
    lam = (jnp.exp(jnp.sum(dl[0:1] * dl[1:2], axis=-1, keepdims=True))
           - jnp.exp(jnp.sum(dl[2:3] * dl[3:4], axis=-1, keepdims=True)) + lam_init)
    zero = jnp.zeros((LANES, tq), BF16)

    def attend(q0, nk):
        chains = []
        for j in range(PAIRS):
            q_t = q_ref[0, LANES * j:LANES * (j + 1), pl.ds(q0, tq)]
            key = lambda start, size, j=j: k_ref[0, start:start + size, LANES * j:LANES * (j + 1)]
            value_t = lambda start, size, j=j: vt_ref[0, 2 * j + half, :, start:start + size]
            for mi in range(2):
                mine = (row & (LANES // 2 - ROT_HALF)) == (2 * half + mi) * ROT_HALF
                chains.append((jnp.where(mine, q_t, zero), key, value_t))
        o = _attend_t(chains, _key_chunks(nk, nctx, DA_KEY_CHUNK))
        heads = []
        for j in range(PAIRS):
            d = o[2 * j] - lam * o[2 * j + 1]
            heads.append(d * lax.rsqrt(jnp.mean(d * d, axis=0, keepdims=True) + RMS_EPS))
        heads.append(jnp.zeros_like(heads[0]))
        for s in range(DA_HALF_WIDTH // LANES):
            out = jnp.concatenate(heads[2 * s:2 * s + 2], axis=0).T
            o_ref[0, pl.ds(q0, tq), LANES * s:LANES * (s + 1)] = (out * g_ref[...] * (1.0 - lam_init)).astype(BF16)

    _for_query_blocks(attend, nt, nctx, tq)


def _for_query_blocks(attend, nt, nctx, tq):
    for t in range(nctx // tq):
        attend(t * tq, nctx)

    def latent_block(t, carry):
        attend(pl.multiple_of(nctx + t * tq, tq), nt)
        return carry

    lax.fori_loop(0, (nt - nctx) // tq, latent_block, 0)


def _da_attn(dq_t, dk, dv_t, dlam, gpair, lam_init, nctx):
    b, nt, _ = dk.shape
    tq = TOKEN_TILE
    return pl.pallas_call(
        functools.partial(_da_kernel, nt=nt, nctx=nctx, tq=tq),
        grid=(b, 2),
        in_specs=[
            pl.BlockSpec((1, DA_WIDTH, nt), lambda i, h: (i, 0, 0)),
            pl.BlockSpec((1, nt, DA_WIDTH), lambda i, h: (i, 0, 0)),
            pl.BlockSpec((1, DA_HEADS, VT_ROWS, nt), lambda i, h: (i, 0, 0, 0)),
            _const_spec(dlam.shape), _const_spec(gpair.shape), _const_spec(lam_init.shape),
        ],
        out_specs=pl.BlockSpec((1, nt, DA_HALF_WIDTH), lambda i, h: (i, 0, h)),
        out_shape=jax.ShapeDtypeStruct((b, nt, 2 * DA_HALF_WIDTH), BF16),
        compiler_params=_params(("parallel", "parallel")),
        name="diff_attn",
    )(dq_t, dk, dv_t, dlam, gpair, lam_init)


def _mla_kernel(q_ref, k_ref, vt_ref, o_ref, *, nt, nctx, tq):
    def attend(q0, nk):
        chains = []
        for hh in range(MLA_HEADS):
            key = lambda start, size, hh=hh: k_ref[0, hh, start:start + size, :]
            value_t = lambda start, size, hh=hh: vt_ref[0, hh, :, start:start + size]
            chains.append((q_ref[0, hh, :, pl.ds(q0, tq)], key, value_t))
        o = _attend_t(chains, _key_chunks(nk, nctx, MLA_KEY_CHUNK))
        for j in range(MLA_HEADS // 2):
            o_ref[0, pl.ds(q0, tq), LANES * j:LANES * (j + 1)] = jnp.concatenate(o[2 * j:2 * j + 2], axis=0).T.astype(BF16)

    _for_query_blocks(attend, nt, nctx, tq)


def _mla_attn(mq_t, mk, mv_t, nctx):
    b, nh, nt, _ = mk.shape
    tq = TOKEN_TILE
    return pl.pallas_call(
        functools.partial(_mla_kernel, nt=nt, nctx=nctx, tq=tq),
        grid=(b,),
        in_specs=[
            pl.BlockSpec((1, nh, LANES, nt), lambda i: (i, 0, 0, 0)),
            pl.BlockSpec((1, nh, nt, LANES), lambda i: (i, 0, 0, 0)),
            pl.BlockSpec((1, nh, VT_ROWS, nt), lambda i: (i, 0, 0, 0)),
        ],
        out_specs=pl.BlockSpec((1, nt, nh * MLA_V), lambda i: (i, 0, 0)),
        out_shape=jax.ShapeDtypeStruct((b, nt, nh * MLA_V), BF16),
        compiler_params=_params(("parallel",)),
        name="mla_attn",
    )(mq_t, mk, mv_t)


def _layer_norm(z, g, b):
    mu = jnp.mean(z, axis=-1, keepdims=True)
    zc = z - mu
    var = jnp.mean(zc * zc, axis=-1, keepdims=True)
    return (zc * lax.rsqrt(var + LN_EPS)) * g + b


def _router_gates(logits, rb):
    scores = jax.nn.sigmoid(logits)
    sel = scores + rb
    lane = lax.broadcasted_iota(jnp.int32, logits.shape, 1)
    r = lane & (EXPERTS_PER_GROUP - 1)
    grp = (lane >> 2) & (N_GROUPS - 1)

    def in_group(x, k):
        return jnp.where(r >= k, pltpu.roll(x, k, 1), pltpu.roll(x, LANES - EXPERTS_PER_GROUP + k, 1))

    others = [in_group(sel, k) for k in (1, 2, 3)]
    pair_max = sel + jnp.maximum(jnp.maximum(others[0], others[1]), others[2])
    grp_score = jnp.maximum(jnp.maximum(pair_max, in_group(pair_max, 1)),
                            jnp.maximum(in_group(pair_max, 2), in_group(pair_max, 3)))
    in_best = None
    for k in (1, 2, 3):
        other = pltpu.roll(grp_score, EXPERTS_PER_GROUP * k, 1)
        wins = (grp_score > other) | ((grp_score == other) & (grp < k))
        in_best = wins if in_best is None else (in_best & wins)
    beaten = jnp.zeros(logits.shape, F32)
    for k, o in zip((1, 2, 3), others):
        beats = (o > sel) | ((o == sel) & (r >= k))
        beaten = beaten + jnp.where(beats, 1.0, 0.0)
    chosen = in_best & (beaten < 2.0)
    sc = jnp.where(chosen, scores, 0.0)
    tot = sc + in_group(sc, 1) + in_group(sc, 2) + in_group(sc, 3)
    return jnp.where(chosen, sc / tot, 0.0), in_best


ROW_BLOCK = 160
BF16_ROWS = 16


def _grouped_experts(v, gates, in_best, w1_ref, w3_ref, w2_ref, xs_ref, gs_ref, ys_ref):
    rows, d = v.shape
    per = EXPERTS_PER_GROUP * D_EXPERT
    lane = lax.broadcasted_iota(jnp.int32, (rows, LANES), 1)
    gsel = jnp.where(in_best & ((lane & (EXPERTS_PER_GROUP - 1)) == 0) & (lane < N_EXPERTS), 1.0, 0.0)
    gsel_b = gsel.astype(BF16)
    ri = lax.broadcasted_iota(jnp.int32, (rows, rows), 0)
    ci = lax.broadcasted_iota(jnp.int32, (rows, rows), 1)
    onehot = lambda cond: jnp.where(cond, 1.0, 0.0).astype(BF16)

    tot = jnp.sum(gsel, axis=0, keepdims=True)
    lane1 = lax.broadcasted_iota(jnp.int32, (1, LANES), 1)
    cnt = [jnp.sum(jnp.where(lane1 == EXPERTS_PER_GROUP * g, tot, 0.0)).astype(jnp.int32) for g in range(N_GROUPS)]
    off = [jnp.int32(0)]
    for g in range(1, N_GROUPS):
        off.append(off[-1] + cnt[g - 1])

    off_lane = sum(jnp.where(lane1 == EXPERTS_PER_GROUP * g, off[g].astype(F32), 0.0) for g in range(N_GROUPS))
    before = jnp.dot(onehot(ci < ri), gsel_b, preferred_element_type=F32)
    pos_col = jnp.sum(gsel * (before + off_lane), axis=1, keepdims=True)
    sub8 = lax.broadcasted_iota(jnp.int32, (SUBLANES, LANES), 0)
    lane8 = lax.broadcasted_iota(jnp.int32, (SUBLANES, LANES), 1)
    pick = jnp.where((lane8 == EXPERTS_PER_GROUP * sub8) & (sub8 < N_GROUPS), 1.0, 0.0).astype(BF16)
    gsel_t = lax.dot_general(pick, gsel_b, (((1,), (1,)), ((), ())), preferred_element_type=F32)
    before_t = jnp.dot(gsel_t.astype(BF16), onehot(ri < ci), preferred_element_type=F32)
    sub_col = lax.broadcasted_iota(jnp.int32, (SUBLANES, 1), 0)
    off_sub = sum(jnp.where(sub_col == g, off[g].astype(F32), 0.0) for g in range(N_GROUPS))
    pos_row = jnp.sum(gsel_t * (before_t + off_sub), axis=0, keepdims=True)
    perm = onehot(pos_row == ri.astype(F32))
    perm_t = onehot(pos_col == ci.astype(F32))

    xs_ref[...] = jnp.dot(perm, v, preferred_element_type=F32).astype(BF16)
    g_hi = gates.astype(BF16)
    g_lo = (gates - g_hi.astype(F32)).astype(BF16)
    gs_ref[...] = jnp.dot(perm, g_hi, preferred_element_type=F32) + jnp.dot(perm, g_lo, preferred_element_type=F32)
    ys_ref[...] = jnp.zeros(ys_ref.shape, F32)

    row_in_block = lax.broadcasted_iota(jnp.int32, (ROW_BLOCK, LANES), 0)
    for g in range(N_GROUPS):
        first = (off[g] // BF16_ROWS) * BF16_ROWS
        end = off[g] + cnt[g]
        for k in range(-(-rows // ROW_BLOCK)):
            lo = first + k * ROW_BLOCK

            @pl.when((lo < end) & (cnt[g] > 0))
            def _(lo=lo, g=g):
                st = pl.multiple_of(jnp.minimum(lo, rows - ROW_BLOCK), BF16_ROWS)
                xb = xs_ref[pl.ds(st, ROW_BLOCK), :]
                gb = jnp.where(row_in_block + st >= lo, gs_ref[pl.ds(st, ROW_BLOCK), :], 0.0)
                h1 = jnp.dot(xb, w1_ref[:, g * per:(g + 1) * per], preferred_element_type=F32)
                h3 = jnp.dot(xb, w3_ref[:, g * per:(g + 1) * per], preferred_element_type=F32)
                hh = (h1 * jax.nn.sigmoid(h1)) * h3
                parts = []
                for j in range(EXPERTS_PER_GROUP):
                    e = g * EXPERTS_PER_GROUP + j
                    parts.append((hh[:, j * D_EXPERT:(j + 1) * D_EXPERT] * gb[:, e:e + 1]).astype(BF16))
                ys_ref[pl.ds(st, ROW_BLOCK), :] += jnp.dot(jnp.concatenate(parts, axis=-1),
                                                           w2_ref[g * per:(g + 1) * per, :],
                                                           preferred_element_type=F32)

    return jnp.dot(perm_t, ys_ref[...].astype(BF16), preferred_element_type=F32)


def _post_kernel(xc_ref, xl_ref, mod_ref, lru_ref, da_ref, mla_ref, wo_ref, g1_ref, b1_ref, rw_ref, rb_ref, w1_ref,
                 w3_ref, w2_ref, g_ref, b_ref, o_ref, xs_ref, gs_ref, ys_ref, *, alpha, n_ctx_tiles):
    nb, tm, d = xl_ref.shape
    rows = nb * tm
    x = jnp.where(pl.program_id(1) < n_ctx_tiles, xc_ref[...], xl_ref[...]).reshape(rows, d)
    mod = mod_ref[:, 0]
    per_row = lambda k: jnp.broadcast_to(mod[:, k:k + 1], (nb, tm, d)).reshape(rows, d)
    a = jnp.concatenate([lru_ref[...], da_ref[...], mla_ref[...]], axis=-1).reshape(rows, -1)
    o = jnp.dot(a, wo_ref[...], preferred_element_type=F32)
    x1 = _layer_norm(alpha * x + per_row(2) * o, g1_ref[...], b1_ref[...])
    v = (x1 * (1.0 + per_row(4)) + per_row(3)).astype(BF16)
    gates, in_best = _router_gates(jnp.dot(v, rw_ref[...], preferred_element_type=F32), rb_ref[...])
    f = _grouped_experts(v, gates, in_best, w1_ref, w3_ref, w2_ref, xs_ref, gs_ref, ys_ref)
    o_ref[...] = _layer_norm(alpha * x1 + per_row(5) * f, g_ref[...], b_ref[...]).reshape(nb, tm, d)


POST_BATCH = 2


def _post(xc, xl, lat_shift, modt, lru_o, da_o, mla_o, wo, g1, b1, rw, rb, w1c, w3c, w2c, g2, b2, nctx, alpha,
          latent_only):
    b, nt, _ = lru_o.shape
    d = xl.shape[-1]
    tm = TOKEN_TILE
    nc = nctx // tm
    skip = nc if latent_only else 0
    nb = POST_BATCH if b % POST_BATCH == 0 else 1
    tok = lambda w: pl.BlockSpec((nb, tm, w), lambda i, t: (i, t + skip, 0))
    consts = (wo, g1, b1, rw, rb, w1c, w3c, w2c, g2, b2)
    return pl.pallas_call(
        functools.partial(_post_kernel, alpha=alpha, n_ctx_tiles=nc - skip),
        grid=(b // nb, nt // tm - skip),
        in_specs=_stream_specs(nb, tm, d, nc, lat_shift, skip) + [
            pl.BlockSpec((nb, 1, 8, d), lambda i, t: (i, jnp.where(t + skip >= nc, 1, 0), 0, 0)),
            tok(LRU_WIDTH), tok(da_o.shape[-1]), tok(mla_o.shape[-1]),
        ] + [_const_spec(c.shape) for c in consts],
        out_specs=pl.BlockSpec((nb, tm, d), lambda i, t: (i, t, 0)),
        out_shape=jax.ShapeDtypeStruct((b, nt - skip * tm, d), F32),
        scratch_shapes=[pltpu.VMEM((nb * tm, d), BF16), pltpu.VMEM((nb * tm, LANES), F32),
                        pltpu.VMEM((nb * tm, d), F32)],
        compiler_params=_params(("parallel", "parallel")),
        name="post",
    )(xc, xl, modt, lru_o, da_o, mla_o, *consts)


def _rotary_tables(n, nctx):
    rows = n // GRID_W
    row = jnp.repeat(jnp.arange(rows), GRID_W).astype(F32)
    col = jnp.tile(jnp.arange(GRID_W), rows).astype(F32)
    n_freq = DA_QK // 4
    inv = ROPE_THETA ** (-jnp.arange(n_freq, dtype=F32) / n_freq)
    ang = jnp.concatenate([row[:, None] * inv, col[:, None] * inv], axis=-1)
    ang = jnp.concatenate([jnp.zeros((nctx, DA_QK // 2), F32), ang], axis=0)
    c, s = jnp.cos(ang), jnp.sin(ang)
    cos_d = jnp.tile(c, (1, LANES // ROT_HALF))
    sin_d = jnp.concatenate([jnp.tile(-s, (1, 4)), jnp.tile(s, (1, 4))], axis=-1)
    one = jnp.ones_like(c)
    zero = jnp.zeros_like(c)
    cos_m = jnp.concatenate([one, one, c, one, one, one, c, one], axis=-1)
    sin_m = jnp.concatenate([zero, zero, -s, zero, zero, zero, s, zero], axis=-1)
    return cos_d, sin_d, cos_m, sin_m


def _rot_lanes(w, groups):
    lead = w.shape[:-1]
    w = w.reshape(lead + (groups, 2, ROT_HALF))
    return jnp.swapaxes(w, -3, -2).reshape(lead + (groups * 2 * ROT_HALF,))


def _mla_lanes(nope, rot):
    z = jnp.zeros(nope.shape[:-1] + (ROT_HALF,), nope.dtype)
    return jnp.concatenate([nope[..., :MLA_NOPE // 2], rot[..., :ROT_HALF], z,
                            nope[..., MLA_NOPE // 2:], rot[..., ROT_HALF:], z], axis=-1)


def _pack_in_weight(w_in):
    d = w_in.shape[0]
    n_lru = 2 * LRU_WIDTH
    n_lru_da = n_lru + 3 * DA_WIDTH
    n_rank = Q_RANK + KV_RANK
    wkr = w_in[:, n_lru_da + n_rank:n_lru_da + n_rank + MLA_ROPE]
    krp = _mla_lanes(jnp.zeros((d, MLA_NOPE), F32), wkr)
    wqk = w_in[:, n_lru:n_lru + 2 * DA_WIDTH].reshape(d, 2 * PAIRS, LANES)
    wqk = _rot_lanes(wqk, LANES // DA_QK).reshape(d, 2 * DA_WIDTH)
    return jnp.concatenate([w_in[:, n_lru_da:n_lru_da + n_rank], krp, wqk, w_in[:, n_lru + 2 * DA_WIDTH:n_lru_da],
                            w_in[:, :n_lru]], axis=-1).astype(BF16)


def _pack_uq(w_uq):
    r = w_uq.shape[0]
    w = w_uq.reshape(r, MLA_HEADS, MLA_NOPE + MLA_ROPE)
    return _mla_lanes(w[..., :MLA_NOPE], w[..., MLA_NOPE:]).reshape(r, MLA_HEADS * LANES).astype(BF16)


def _pack_ukv(w_ukv):
    r = w_ukv.shape[0]
    w = w_ukv.reshape(r, MLA_HEADS, MLA_NOPE + MLA_V)
    wk = _mla_lanes(w[..., :MLA_NOPE], jnp.zeros((r, MLA_HEADS, MLA_ROPE), F32)).reshape(r, MLA_HEADS * LANES)
    wv = w[..., MLA_NOPE:].reshape(r, MLA_HEADS * MLA_V)
    return wk.astype(BF16), wv.astype(BF16)


def _pack_out_weight(w_out):
    d = w_out.shape[1]
    w_da = w_out[LRU_WIDTH:LRU_WIDTH + DA_WIDTH].reshape(PAIRS, 2, DA_V, d)
    pad = jnp.zeros((DA_HALF_WIDTH - PAIRS * DA_V, d), w_out.dtype)
    halves = [jnp.concatenate([w_da[:, h].reshape(PAIRS * DA_V, d), pad], axis=0) for h in range(2)]
    return jnp.concatenate([w_out[:LRU_WIDTH]] + halves + [w_out[LRU_WIDTH + DA_WIDTH:]], axis=0).astype(BF16)


def _block_diag(w):
    nd, nb, bs, _ = w.shape
    eye = jnp.eye(nb, dtype=w.dtype)
    return jnp.einsum('dhij,hg->dhigj', w, eye).reshape(nd, nb * bs, nb * bs)


def kernel(x, c, ctx, c_ctx, w_mod, b_mod, w_in, w_out, conv_w, conv_b, lru_wa, lru_ba, lru_wi, lru_bi, lru_lambda, diff_lambda, diff_norm, mla_q_norm, mla_kv_norm, mla_w_uq, mla_w_ukv, ln1_g, ln1_b, ln2_g, ln2_b, router_w, router_b, exp_w1, exp_w3, exp_w2):
    bsz, n, d = x.shape
    nctx = ctx.shape[1]
    depth = w_mod.shape[0]
    alpha = (2 * depth) ** 0.25
    assert nctx % TOKEN_TILE == 0 and n % TOKEN_TILE == 0 and n % GRID_W == 0

    rows = -(-(bsz + 1) // SUBLANES) * SUBLANES
    cc = jnp.concatenate([c, c_ctx[None, :], jnp.zeros((rows - bsz - 1, d), F32)], axis=0)
    mod = _modulation(cc, w_mod, b_mod).reshape(depth, rows, 6, d)
    mod = jnp.pad(mod, ((0, 0), (0, 0), (0, 2), (0, 0)))
    mod_ctx = jnp.broadcast_to(mod[:, bsz][:, None], (depth, bsz, 8, d))
    modt = jnp.stack([mod_ctx, mod[:, :bsz]], axis=2)

    rot_tables = _rotary_tables(n, nctx)
    rw = jnp.tile(router_w, (1, LANES // N_EXPERTS)).astype(BF16)
    rb = jnp.tile(router_b, LANES // N_EXPERTS)[None, :].astype(F32)
    gpair = jnp.tile(diff_norm, (1, LANES // DA_V))

    xc, xl, lat_shift = ctx, x, nctx // TOKEN_TILE
    for l in range(depth):
        lam_init = jnp.full((1, 1), 0.8 - 0.6 * math.exp(-0.3 * l), F32)
        wuk, wuv = _pack_ukv(mla_w_ukv[l])
        lx, lg, dq_t, dk, dv_t, mq_t, mk, mv_t = _in_proj(
            xc, xl, lat_shift, nctx + n, modt[l], _pack_in_weight(w_in[l]), _pack_uq(mla_w_uq[l]), wuk, wuv,
            mla_q_norm[l][None, :], mla_kv_norm[l][None, :], rot_tables, nctx)
        lru_o = _lru(lx, lg, conv_w[l], conv_b[l][None, :], _block_diag(lru_wa[l]).astype(BF16),
                     _block_diag(lru_wi[l]).astype(BF16), lru_ba[l], lru_bi[l], lru_lambda[l], nctx)
        da_o = _da_attn(dq_t, dk, dv_t, diff_lambda[l], gpair[l][None, :], lam_init, nctx)
        mla_o = _mla_attn(mq_t, mk, mv_t, nctx)
        w1c = exp_w1[l].transpose(1, 0, 2).reshape(d, N_EXPERTS * D_EXPERT).astype(BF16)
        w3c = exp_w3[l].transpose(1, 0, 2).reshape(d, N_EXPERTS * D_EXPERT).astype(BF16)
        w2c = exp_w2[l].reshape(N_EXPERTS * D_EXPERT, d).astype(BF16)
        xa = _post(xc, xl, lat_shift, modt[l], lru_o, da_o, mla_o, _pack_out_weight(w_out[l]), ln1_g[l][None, :],
                   ln1_b[l][None, :], rw, rb, w1c, w3c, w2c, ln2_g[l][None, :], ln2_b[l][None, :], nctx, alpha,
                   latent_only=(l == depth - 1))
        xc, xl, lat_shift = xa, xa, 0
    return xa
```
